```python
import math
import numpy as np
import jax
import jax.numpy as jnp
from jax import lax

D_MODEL = 2048
BATCH = 8
SEQ = 2048
DEPTH = 2

HEAD_DIM = 128
NSA_HEADS = D_MODEL // (2 * HEAD_DIM)
NSA_KV_HEADS = 2
NSA_GROUP = NSA_HEADS // NSA_KV_HEADS
DIFF_VDIM = 2 * HEAD_DIM
DIFF_HEADS = D_MODEL // (2 * DIFF_VDIM)
CMP_LEN = 32
CMP_STRIDE = 16
CMP_HIDDEN = 256
SLC_LEN = 64
SLC_TOPK = 16
WINDOW = 512
D_FF = 5632
Q_BLOCK = 128
ROPE_THETA = 10000.0
EPS = 1e-6
NEG = -1e30

SPLIT_SIZES = (NSA_HEADS * HEAD_DIM,
               NSA_KV_HEADS * HEAD_DIM, NSA_KV_HEADS * HEAD_DIM,
               NSA_KV_HEADS * HEAD_DIM, NSA_KV_HEADS * HEAD_DIM,
               NSA_KV_HEADS * HEAD_DIM, NSA_KV_HEADS * HEAD_DIM,
               3 * NSA_HEADS,
               2 * DIFF_HEADS * HEAD_DIM, 2 * DIFF_HEADS * HEAD_DIM,
               DIFF_HEADS * DIFF_VDIM)
IN_WIDTH = sum(SPLIT_SIZES)

kernel_name = 'hybrid_nsa_diff_macaron'


def _rmsnorm(x, g):
    xf = x.astype(jnp.float32)
    y = xf * lax.rsqrt(jnp.mean(xf * xf, axis=-1, keepdims=True) + EPS)
    return (y * g.astype(jnp.float32)).astype(x.dtype)


def _swiglu(h, w_gate, w_up, w_down):
    return (jax.nn.silu(h @ w_gate) * (h @ w_up)) @ w_down


def _rope_tables(T):
    inv = 1.0 / (ROPE_THETA ** (jnp.arange(0, HEAD_DIM, 2, dtype=jnp.float32) / HEAD_DIM))
    ang = jnp.arange(T, dtype=jnp.float32)[:, None] * inv[None, :]
    ang = jnp.concatenate([ang, ang], axis=-1)
    return jnp.cos(ang), jnp.sin(ang)


def _rope(x, cos, sin):
    x1, x2 = jnp.split(x, 2, axis=-1)
    rot = jnp.concatenate([-x2, x1], axis=-1)
    return (x.astype(jnp.float32) * cos + rot.astype(jnp.float32) * sin).astype(x.dtype)


def _heads(t, n):
    B, T, _ = t.shape
    return t.reshape(B, T, n, -1).transpose(0, 2, 1, 3)


def _masked_softmax(s, mask):
    s = jnp.where(mask, s.astype(jnp.float32), NEG)
    return jax.nn.softmax(s, axis=-1) * mask


def _block_overlap(n_cmp, n_slc):
    c0 = np.arange(n_cmp) * CMP_STRIDE
    s0 = np.arange(n_slc) * SLC_LEN
    lo = np.maximum(c0[:, None], s0[None, :])
    hi = np.minimum(c0[:, None] + CMP_LEN, s0[None, :] + SLC_LEN)
    return (np.clip(hi - lo, 0, None) / CMP_LEN).astype(np.float32)


def _nsa(q, k_cmp, v_cmp, k_slc, v_slc, k_win, v_win, gates,
         pos_k, pos_v, wk1, wk2, wv1, wv2):
    B, G, R, T, dk = q.shape
    scale = dk ** -0.5
    n_cmp = (T - CMP_LEN) // CMP_STRIDE + 1
    n_slc = T // SLC_LEN
    n_sel = min(SLC_TOPK, n_slc)
    n_qb = T // Q_BLOCK
    t_pos = jnp.arange(T)

    win_idx = jnp.arange(n_cmp)[:, None] * CMP_STRIDE + jnp.arange(CMP_LEN)[None, :]

    def compress(kv, pos, w1, w2):
        blocks = kv[:, :, win_idx, :] + pos
        flat = blocks.reshape(B, G, n_cmp, CMP_LEN * dk)
        return jax.nn.silu(flat @ w1) @ w2

    kc = compress(k_cmp, pos_k, wk1, wk2)
    vc = compress(v_cmp, pos_v, wv1, wv2)
    s_c = jnp.einsum('bgrtd,bgnd->bgrtn', q, kc) * scale
    c_end = jnp.arange(n_cmp) * CMP_STRIDE + CMP_LEN - 1
    c_mask = c_end[None, :] <= t_pos[:, None]
    p_c = _masked_softmax(s_c, c_mask)
    o_cmp = jnp.einsum('bgrtn,bgnd->bgrtd', p_c.astype(vc.dtype), vc)

    overlap = jnp.asarray(_block_overlap(n_cmp, n_slc), dtype=jnp.float32)
    imp = jnp.einsum('bgrtn,ns->bgts', p_c, overlap)
    blk = jnp.arange(n_slc)
    cur = t_pos // SLC_LEN
    forced = (blk[None, :] == 0) | (blk[None, :] == cur[:, None]) | (blk[None, :] == cur[:, None] - 1)
    causal_blk = blk[None, :] * SLC_LEN <= t_pos[:, None]
    imp = jnp.where(forced, 1e4, imp)
    imp = jnp.where(causal_blk, imp, -1.0)
    top_val, top_idx = lax.top_k(imp, n_sel)
    top_ok = top_val >= 0.0

    kb = k_slc.reshape(B, G, n_slc, SLC_LEN, dk)
    vb = v_slc.reshape(B, G, n_slc, SLC_LEN, dk)

    def sel_block(i):
        b = i // n_qb
        s0 = (i % n_qb) * Q_BLOCK
        qi = lax.dynamic_slice(q, (b, 0, 0, s0, 0), (1, G, R, Q_BLOCK, dk))[0]
        ii = lax.dynamic_slice(top_idx, (b, 0, s0, 0), (1, G, Q_BLOCK, n_sel))[0]
        ok = lax.dynamic_slice(top_ok, (b, 0, s0, 0), (1, G, Q_BLOCK, n_sel))[0]
        kbb = lax.dynamic_index_in_dim(kb, b, 0, keepdims=False)
        vbb = lax.dynamic_index_in_dim(vb, b, 0, keepdims=False)
        ks = jax.vmap(lambda kg, ig: kg[ig])(kbb, ii)
        vs = jax.vmap(lambda vg, ig: vg[ig])(vbb, ii)
        tq = s0 + jnp.arange(Q_BLOCK)
        kpos = ii[..., None] * SLC_LEN + jnp.arange(SLC_LEN)
        mask = ok[..., None] & (kpos <= tq[None, :, None, None])
        mask = mask.reshape(G, 1, Q_BLOCK, n_sel * SLC_LEN)
        s = jnp.einsum('grqd,gqnld->grqnl', qi, ks) * scale
        p = _masked_softmax(s.reshape(G, R, Q_BLOCK, n_sel * SLC_LEN), mask)
        p = p.reshape(G, R, Q_BLOCK, n_sel, SLC_LEN)
        return jnp.einsum('grqnl,gqnld->grqd', p.astype(vs.dtype), vs)

    o = lax.map(sel_block, jnp.arange(B * n_qb))
    o_slc = o.reshape(B, n_qb, G, R, Q_BLOCK, dk).transpose(0, 2, 3, 1, 4, 5).reshape(B, G, R, T, dk)

    kp = jnp.pad(k_win, ((0, 0), (0, 0), (WINDOW, 0), (0, 0)))
    vp = jnp.pad(v_win, ((0, 0), (0, 0), (WINDOW, 0), (0, 0)))
    span = WINDOW + Q_BLOCK

    def win_block(j):
        s0 = j * Q_BLOCK
        qi = lax.dynamic_slice_in_dim(q, s0, Q_BLOCK, axis=3)
        ki = lax.dynamic_slice_in_dim(kp, s0, span, axis=2)
        vi = lax.dynamic_slice_in_dim(vp, s0, span, axis=2)
        tq = s0 + jnp.arange(Q_BLOCK)
        tk = s0 - WINDOW + jnp.arange(span)
        dist = tq[:, None] - tk[None, :]
        mask = (tk[None, :] >= 0) & (dist >= 0) & (dist < WINDOW)
        s = jnp.einsum('bgrqd,bgkd->bgrqk', qi, ki) * scale
        p = _masked_softmax(s, mask)
        return jnp.einsum('bgrqk,bgkd->bgrqd', p.astype(vi.dtype), vi)

    o = lax.map(win_block, jnp.arange(n_qb))
    o_win = o.transpose(1, 2, 3, 0, 4, 5).reshape(B, G, R, T, dk)

    g = gates.reshape(B, T, G, R, 3).transpose(0, 2, 3, 1, 4)
    out = g[..., 0:1] * o_cmp + g[..., 1:2] * o_slc + g[..., 2:3] * o_win
    return out.astype(q.dtype).transpose(0, 3, 1, 2, 4).reshape(B, T, G * R * dk)


def _diff_attn(q, k, v, lam_q1, lam_k1, lam_q2, lam_k2, subln_g, lambda_init):
    B, Hd, _, T, dk = q.shape
    scale = dk ** -0.5
    n_qb = T // Q_BLOCK
    f32 = jnp.float32
    lam = (jnp.exp(jnp.sum(lam_q1.astype(f32) * lam_k1.astype(f32)))
           - jnp.exp(jnp.sum(lam_q2.astype(f32) * lam_k2.astype(f32))) + lambda_init)
    t_k = jnp.arange(T)

    def blk(j):
        s0 = j * Q_BLOCK
        qi = lax.dynamic_slice_in_dim(q, s0, Q_BLOCK, axis=3)
        s = jnp.einsum('bhcqd,bhckd->bhcqk', qi, k) * scale
        mask = t_k[None, :] <= (s0 + jnp.arange(Q_BLOCK))[:, None]
        p = _masked_softmax(s, mask)
        a = p[:, :, 0] - lam * p[:, :, 1]
        return jnp.einsum('bhqk,bhkd->bhqd', a.astype(v.dtype), v)

    o = lax.map(blk, jnp.arange(n_qb))
    o = o.transpose(1, 2, 0, 3, 4).reshape(B, Hd, T, 2 * dk)
    o = (_rmsnorm(o, subln_g) * (1.0 - lambda_init)).astype(v.dtype)
    return o.transpose(0, 2, 1, 3).reshape(B, T, Hd * 2 * dk)


def _token_mix(h, w_in, pos_k, pos_v, wk1, wk2, wv1, wv2,
               lam_q1, lam_k1, lam_q2, lam_k2, subln_g, w_out, lambda_init, cos, sin):
    B, T, _ = h.shape
    proj = h @ w_in
    cuts = [int(c) for c in np.cumsum(SPLIT_SIZES)[:-1]]
    (q_n, kc, vc, ks, vs, kw, vw, g, q_d, k_d, v_d) = jnp.split(proj, cuts, axis=-1)
    G, R = NSA_KV_HEADS, NSA_GROUP
    qn = _rope(_heads(q_n, NSA_HEADS), cos, sin).reshape(B, G, R, T, HEAD_DIM)
    kc = _rope(_heads(kc, G), cos, sin)
    ks = _rope(_heads(ks, G), cos, sin)
    kw = _rope(_heads(kw, G), cos, sin)
    gates = jax.nn.sigmoid(g.astype(jnp.float32)).reshape(B, T, NSA_HEADS, 3)
    o_nsa = _nsa(qn, kc, _heads(vc, G), ks, _heads(vs, G), kw, _heads(vw, G), gates,
                 pos_k, pos_v, wk1, wk2, wv1, wv2)
    qd = _rope(_heads(q_d, 2 * DIFF_HEADS), cos, sin).reshape(B, DIFF_HEADS, 2, T, HEAD_DIM)
    kd = _rope(_heads(k_d, 2 * DIFF_HEADS), cos, sin).reshape(B, DIFF_HEADS, 2, T, HEAD_DIM)
    vd = _heads(v_d, DIFF_HEADS)
    o_diff = _diff_attn(qd, kd, vd, lam_q1, lam_k1, lam_q2, lam_k2, subln_g, lambda_init)
    return jnp.concatenate([o_nsa, o_diff], axis=-1) @ w_out


def setup_inputs(seed: int = 0) -> dict:
    key = jax.random.key(seed)
    ks = jax.random.split(key, 24)
    f32 = jnp.float32

    def nrm(k, shape, scale):
        return jax.random.normal(k, shape, f32) * scale

    def gain(k, shape):
        return 1.0 + 0.05 * jax.random.normal(k, shape, f32)

    L, D = DEPTH, D_MODEL
    cin = CMP_LEN * HEAD_DIM
    return {
        'x': nrm(ks[0], (BATCH, SEQ, D), 1.0),
        'ffn1_norm': gain(ks[1], (L, D)),
        'ffn1_w_gate': nrm(ks[2], (L, D, D_FF), D ** -0.5),
        'ffn1_w_up': nrm(ks[3], (L, D, D_FF), D ** -0.5),
        'ffn1_w_down': nrm(ks[4], (L, D_FF, D), D_FF ** -0.5),
        'mix_norm': gain(ks[5], (L, D)),
        'w_in': nrm(ks[6], (L, D, IN_WIDTH), D ** -0.5),
        'cmp_pos_k': nrm(ks[7], (L, CMP_LEN, HEAD_DIM), 0.1),
        'cmp_pos_v': nrm(ks[8], (L, CMP_LEN, HEAD_DIM), 0.1),
        'cmp_wk1': nrm(ks[9], (L, cin, CMP_HIDDEN), cin ** -0.5),
        'cmp_wk2': nrm(ks[10], (L, CMP_HIDDEN, HEAD_DIM), CMP_HIDDEN ** -0.5),
        'cmp_wv1': nrm(ks[11], (L, cin, CMP_HIDDEN), cin ** -0.5),
        'cmp_wv2': nrm(ks[12], (L, CMP_HIDDEN, HEAD_DIM), CMP_HIDDEN ** -0.5),
        'lam_q1': nrm(ks[13], (L, HEAD_DIM), 0.1),
        'lam_k1': nrm(ks[14], (L, HEAD_DIM), 0.1),
        'lam_q2': nrm(ks[15], (L, HEAD_DIM), 0.1),
        'lam_k2': nrm(ks[16], (L, HEAD_DIM), 0.1),
        'diff_subln': gain(ks[17], (L, DIFF_VDIM)),
        'w_out': nrm(ks[18], (L, D, D), D ** -0.5),
        'ffn2_norm': gain(ks[19], (L, D)),
        'ffn2_w_gate': nrm(ks[20], (L, D, D_FF), D ** -0.5),
        'ffn2_w_up': nrm(ks[21], (L, D, D_FF), D ** -0.5),
        'ffn2_w_down': nrm(ks[22], (L, D_FF, D), D_FF ** -0.5),
        'final_norm': gain(ks[23], (D,)),
    }


def reference(x, ffn1_norm, ffn1_w_gate, ffn1_w_up, ffn1_w_down, mix_norm, w_in,
              cmp_pos_k, cmp_pos_v, cmp_wk1, cmp_wk2, cmp_wv1, cmp_wv2,
              lam_q1, lam_k1, lam_q2, lam_k2, diff_subln, w_out,
              ffn2_norm, ffn2_w_gate, ffn2_w_up, ffn2_w_down, final_norm):
    T = x.shape[1]
    cos, sin = _rope_tables(T)
    for l in range(DEPTH):
        lambda_init = 0.8 - 0.6 * math.exp(-0.3 * l)
        x = x + 0.5 * _swiglu(_rmsnorm(x, ffn1_norm[l]), ffn1_w_gate[l], ffn1_w_up[l], ffn1_w_down[l])
        h = _rmsnorm(x, mix_norm[l])
        x = x + _token_mix(h, w_in[l], cmp_pos_k[l], cmp_pos_v[l], cmp_wk1[l], cmp_wk2[l],
                           cmp_wv1[l], cmp_wv2[l], lam_q1[l], lam_k1[l], lam_q2[l], lam_k2[l],
                           diff_subln[l], w_out[l], lambda_init, cos, sin)
        x = x + 0.5 * _swiglu(_rmsnorm(x, ffn2_norm[l]), ffn2_w_gate[l], ffn2_w_up[l], ffn2_w_down[l])
    return _rmsnorm(x, final_norm)
```

```python
import functools
import math

import numpy as np
import jax
import jax.numpy as jnp
from jax import lax
from jax.experimental import pallas as pl
from jax.experimental.pallas import tpu as pltpu

HEAD_DIM = 128
NSA_KV_HEADS = 2
CMP_LEN = 32
CMP_STRIDE = 16
SLC_LEN = 64
SLC_TOPK = 16
WINDOW = 512
Q_BLOCK = 128
ROPE_THETA = 10000.0
EPS = 1e-6
NEG = -1e30
LANES = 128
VMEM_LIMIT = 56 * 1024 * 1024

F32 = jnp.float32
BF16 = jnp.bfloat16


def _largest_divisor(n, cap):
    for d in range(min(n, cap), 0, -1):
        if n % d == 0:
            return d
    return 1


def _row_tile(n, cap):
    for d in range(min(n, cap), 7, -1):
        if n % d == 0 and d % 8 == 0:
            return d
    return n


def _rms(x, g):
    return x * lax.rsqrt(jnp.mean(x * x, axis=-1, keepdims=True) + EPS) * g


def _dot(a, b):
    return jnp.dot(a, b, preferred_element_type=F32)


def _dot_nt(a, b):
    return lax.dot_general(a, b, (((1,), (1,)), ((), ())), preferred_element_type=F32)


def _params(*sem):
    return pltpu.CompilerParams(dimension_semantics=sem, vmem_limit_bytes=VMEM_LIMIT)


def _ffn_kernel(x_ref, g_ref, wg_ref, wu_ref, wd_ref, fg_ref, o_ref, h_ref, *, final_norm):
    j = pl.program_id(1)

    @pl.when(j == 0)
    def _():
        x = x_ref[...]
        h_ref[...] = _rms(x, g_ref[...]).astype(BF16)
        o_ref[...] = x

    h = h_ref[...]
    a = _dot(h, wg_ref[...])
    b = _dot(h, wu_ref[...])
    act = (a * jax.nn.sigmoid(a) * b * 0.5).astype(BF16)
    o_ref[...] += _dot(act, wd_ref[...])

    if final_norm:
        @pl.when(j == pl.num_programs(1) - 1)
        def _():
            o_ref[...] = _rms(o_ref[...], fg_ref[...])


def _ffn(x, g, wg, wu, wd, fg, final_norm):
    n, d = x.shape
    f = wg.shape[1]
    tm = _row_tile(n, 512)
    tf = LANES * _largest_divisor(f // LANES, 4)
    return pl.pallas_call(
        functools.partial(_ffn_kernel, final_norm=final_norm),
        grid=(n // tm, f // tf),
        in_specs=[
            pl.BlockSpec((tm, d), lambda i, j: (i, 0)),
            pl.BlockSpec((1, d), lambda i, j: (0, 0)),
            pl.BlockSpec((d, tf), lambda i, j: (0, j)),
            pl.BlockSpec((d, tf), lambda i, j: (0, j)),
            pl.BlockSpec((tf, d), lambda i, j: (j, 0)),
            pl.BlockSpec((1, d), lambda i, j: (0, 0)),
        ],
        out_specs=pl.BlockSpec((tm, d), lambda i, j: (i, 0)),
        out_shape=jax.ShapeDtypeStruct((n, d), F32),
        scratch_shapes=[pltpu.VMEM((tm, d), BF16)],
        compiler_params=_params("parallel", "arbitrary"),
        name="ffn",
    )(x, g, wg, wu, wd, fg)


def _proj_kernel(x_ref, g_ref, w_ref, wgate_ref, cos_ref, sin_ref, o_ref, gate_ref, h_ref,
                 *, groups_per_tile, n_rope_groups):
    j = pl.program_id(1)

    @pl.when(j == 0)
    def _():
        h = _rms(x_ref[...], g_ref[...]).astype(BF16)
        h_ref[...] = h
        gate_ref[...] = _dot(h, wgate_ref[...])

    acc = _dot(h_ref[...], w_ref[...])
    cos = cos_ref[...]
    sin = sin_ref[...]
    for gi in range(groups_per_tile):
        is_rope = (j * groups_per_tile + gi) < n_rope_groups
        c = jnp.where(is_rope, cos, 1.0)
        s = jnp.where(is_rope, sin, 0.0)
        xg = acc[:, gi * LANES:(gi + 1) * LANES]
        y = xg * c + pltpu.roll(xg, HEAD_DIM // 2, axis=1) * s
        o_ref[:, gi * LANES:(gi + 1) * LANES] = y.astype(BF16)


def _project(x, g, w_main, w_gate, cos, sin_signed, n_rope_groups):
    n, d = x.shape
    t = cos.shape[0]
    width = w_main.shape[1]
    gw = w_gate.shape[1]
    n_groups = width // LANES
    gpt = _largest_divisor(n_groups, 11)
    tn = gpt * LANES
    tm = _row_tile(t, 512)
    t_blocks = t // tm
    return pl.pallas_call(
        functools.partial(_proj_kernel, groups_per_tile=gpt, n_rope_groups=n_rope_groups),
        grid=(n // tm, width // tn),
        in_specs=[
            pl.BlockSpec((tm, d), lambda i, j: (i, 0)),
            pl.BlockSpec((1, d), lambda i, j: (0, 0)),
            pl.BlockSpec((d, tn), lambda i, j: (0, j)),
            pl.BlockSpec((d, gw), lambda i, j: (0, 0)),
            pl.BlockSpec((tm, HEAD_DIM), lambda i, j: (i % t_blocks, 0)),
            pl.BlockSpec((tm, HEAD_DIM), lambda i, j: (i % t_blocks, 0)),
        ],
        out_specs=[
            pl.BlockSpec((tm, tn), lambda i, j: (i, j)),
            pl.BlockSpec((tm, gw), lambda i, j: (i, 0)),
        ],
        out_shape=[
            jax.ShapeDtypeStruct((n, width), BF16),
            jax.ShapeDtypeStruct((n, gw), F32),
        ],
        scratch_shapes=[pltpu.VMEM((tm, d), BF16)],
        compiler_params=_params("parallel", "arbitrary"),
        name="in_proj",
    )(x, g, w_main, w_gate, cos, sin_signed)


def _cmp_kernel(ck_ref, cv_ref, pk_ref, pv_ref, wk1_ref, wk2_ref, wv1_ref, wv2_ref, ok_ref, ov_ref):
    def one(c_ref, p_ref, w1_ref, w2_ref, o_ref):
        c = c_ref[0, 0]
        half = c.shape[1]
        n_chunks = c.shape[0]
        a = _dot(c, w1_ref[0:half, :])
        b = _dot(c, w1_ref[half:2 * half, :])
        p = jnp.broadcast_to(p_ref[...], (8, 2 * half))
        bias = _dot(p, w1_ref[...])[0:1]
        pre = a + pltpu.roll(b, n_chunks - 1, axis=0) + bias
        hid = (pre * jax.nn.sigmoid(pre)).astype(BF16)
        o_ref[0, 0] = _dot(hid, w2_ref[...]).astype(BF16)

    one(ck_ref, pk_ref, wk1_ref, wk2_ref, ok_ref)
    one(cv_ref, pv_ref, wv1_ref, wv2_ref, ov_ref)


def _compress(ck, cv, pk, pv, wk1, wk2, wv1, wv2):
    b, g, nc, cw = ck.shape
    hid = wk1.shape[1]
    dk = wk2.shape[1]
    chunk_spec = pl.BlockSpec((1, 1, nc, cw), lambda i, j: (i, j, 0, 0))
    full = lambda shape: pl.BlockSpec(shape, lambda i, j: tuple(0 for _ in shape))
    out_spec = pl.BlockSpec((1, 1, nc, dk), lambda i, j: (i, j, 0, 0))
    return pl.pallas_call(
        _cmp_kernel,
        grid=(b, g),
        in_specs=[chunk_spec, chunk_spec, full((1, 2 * cw)), full((1, 2 * cw)),
                  full((2 * cw, hid)), full((hid, dk)), full((2 * cw, hid)), full((hid, dk))],
        out_specs=[out_spec, out_spec],
        out_shape=[jax.ShapeDtypeStruct((b, g, nc, dk), BF16)] * 2,
        compiler_params=_params("parallel", "parallel"),
        name="nsa_compress",
    )(ck, cv, pk, pv, wk1, wk2, wv1, wv2)


def _nsa_kernel(q_ref, ks_ref, vs_ref, kw_ref, vw_ref, kc_ref, vc_ref, gate_ref, ovl_ref, exp_ref,
                o_ref, *, seq, rep, n_slc, n_sel):
    j = pl.program_id(2)
    s0 = j * Q_BLOCK
    scale = HEAD_DIM ** -0.5
    q = q_ref[0]
    qs = [q[:, r * HEAD_DIM:(r + 1) * HEAD_DIM] for r in range(rep)]
    t_col = s0 + lax.broadcasted_iota(jnp.int32, (Q_BLOCK, 1), 0)

    kc = kc_ref[0, 0]
    vc = vc_ref[0, 0]
    n_rows = kc.shape[0]
    n_idx = lax.broadcasted_iota(jnp.int32, (Q_BLOCK, n_rows), 1)
    c_mask = (n_idx * CMP_STRIDE + (CMP_LEN - 1)) <= t_col
    p_sum = jnp.zeros((Q_BLOCK, n_rows), F32)
    o_cmp = []
    for r in range(rep):
        s = jnp.where(c_mask, _dot_nt(qs[r], kc) * scale, NEG)
        m = jnp.max(s, axis=-1, keepdims=True)
        e = jnp.where(c_mask, jnp.exp(s - m), 0.0)
        l = jnp.sum(e, axis=-1, keepdims=True)
        p = e / jnp.where(l > 0.0, l, 1.0)
        p_sum = p_sum + p
        o_cmp.append(_dot(p.astype(BF16), vc))

    hi = p_sum.astype(BF16)
    lo = (p_sum - hi.astype(F32)).astype(BF16)
    ovl = ovl_ref[...]
    imp = _dot(hi, ovl) + _dot(lo, ovl)
    blk = lax.broadcasted_iota(jnp.int32, (Q_BLOCK, LANES), 1)
    cur = t_col // SLC_LEN
    forced = (blk == 0) | (blk == cur) | (blk == cur - 1)
    imp = jnp.where(forced, 1e4, imp)
    imp = jnp.where(blk * SLC_LEN <= t_col, imp, -1.0)
    rank = jnp.zeros((Q_BLOCK, LANES), F32)
    for sp in range(n_slc):
        col = imp[:, sp:sp + 1]
        ge = jnp.where(col >= imp, 1.0, 0.0)
        gt = jnp.where(col > imp, 1.0, 0.0)
        rank = rank + jnp.where(blk > sp, ge, gt)
    sel = jnp.where(rank < float(n_sel), 1.0, 0.0).astype(BF16)
    key_sel = _dot(sel, exp_ref[...])

    key = lax.broadcasted_iota(jnp.int32, (Q_BLOCK, seq), 1)
    bias = jnp.where((key_sel > 0.5) & (key <= t_col), 0.0, NEG)
    ks = ks_ref[0]
    vs = vs_ref[0]
    o_slc = []
    for r in range(rep):
        s = _dot_nt(qs[r], ks) * scale + bias
        m = jnp.max(s, axis=-1, keepdims=True)
        e = jnp.exp(s - m)
        l = jnp.sum(e, axis=-1, keepdims=True)
        o_slc.append(_dot(e.astype(BF16), vs) / l)

    span = WINDOW + Q_BLOCK
    start = pl.multiple_of(jnp.maximum(s0 - WINDOW, 0), Q_BLOCK)
    kw = kw_ref[0, pl.ds(start, span), :]
    vw = vw_ref[0, pl.ds(start, span), :]
    dist = t_col - (start + lax.broadcasted_iota(jnp.int32, (Q_BLOCK, span), 1))
    wbias = jnp.where((dist >= 0) & (dist < WINDOW), 0.0, NEG)
    o_win = []
    for r in range(rep):
        s = _dot_nt(qs[r], kw) * scale + wbias
        m = jnp.max(s, axis=-1, keepdims=True)
        e = jnp.exp(s - m)
        l = jnp.sum(e, axis=-1, keepdims=True)
        o_win.append(_dot(e.astype(BF16), vw) / l)

    gt = jax.nn.sigmoid(gate_ref[0])
    for r in range(rep):
        out = (gt[:, 3 * r:3 * r + 1] * o_cmp[r] + gt[:, 3 * r + 1:3 * r + 2] * o_slc[r]
               + gt[:, 3 * r + 2:3 * r + 3] * o_win[r])
        o_ref[0, :, r * HEAD_DIM:(r + 1) * HEAD_DIM] = out.astype(BF16)


def _block_overlap(n_rows, n_cmp, n_slc):
    c0 = np.arange(n_cmp) * CMP_STRIDE
    s0 = np.arange(n_slc) * SLC_LEN
    lo = np.maximum(c0[:, None], s0[None, :])
    hi = np.minimum(c0[:, None] + CMP_LEN, s0[None, :] + SLC_LEN)
    out = np.zeros((n_rows, LANES), np.float32)
    out[:n_cmp, :n_slc] = np.clip(hi - lo, 0, None) / CMP_LEN
    return out


def _nsa_attention(proj, gates, kc, vc, off, batch, seq, rep):
    g = NSA_KV_HEADS
    n_qb = seq // Q_BLOCK
    n_slc = seq // SLC_LEN
    n_sel = min(SLC_TOPK, n_slc)
    n_rows = kc.shape[2]
    n_cmp = (seq - CMP_LEN) // CMP_STRIDE + 1
    qw = rep * HEAD_DIM
    ovl = jnp.asarray(_block_overlap(n_rows, n_cmp, n_slc), BF16)
    expand = np.zeros((LANES, seq), np.float32)
    expand[np.arange(seq) // SLC_LEN, np.arange(seq)] = 1.0
    expand = jnp.asarray(expand, BF16)

    def kv_spec(name):
        base = off[name] // HEAD_DIM
        return pl.BlockSpec((1, seq, HEAD_DIM), lambda b, h, j: (b, 0, base + h))

    cmp_spec = pl.BlockSpec((1, 1, n_rows, HEAD_DIM), lambda b, h, j: (b, h, 0, 0))
    q_base = off["q_n"] // qw
    return pl.pallas_call(
        functools.partial(_nsa_kernel, seq=seq, rep=rep, n_slc=n_slc, n_sel=n_sel),
        grid=(batch, g, n_qb),
        in_specs=[
            pl.BlockSpec((1, Q_BLOCK, qw), lambda b, h, j: (b, j, q_base + h)),
            kv_spec("ks"), kv_spec("vs"), kv_spec("kw"), kv_spec("vw"),
            cmp_spec, cmp_spec,
            pl.BlockSpec((1, Q_BLOCK, LANES), lambda b, h, j: (b, j, h)),
            pl.BlockSpec((n_rows, LANES), lambda b, h, j: (0, 0)),
            pl.BlockSpec((LANES, seq), lambda b, h, j: (0, 0)),
        ],
        out_specs=pl.BlockSpec((1, Q_BLOCK, qw), lambda b, h, j: (b, j, h)),
        out_shape=jax.ShapeDtypeStruct((batch, seq, g * qw), BF16),
        compiler_params=_params("parallel", "parallel", "arbitrary"),
        name="nsa_attention",
    )(proj, proj, proj, proj, proj, kc, vc, gates, ovl, expand)


def _diff_kernel(q_ref, k_ref, v_ref, lam_ref, sg_ref, o_ref, *, seq, tq, lambda_init):
    j = pl.program_id(2)
    scale = HEAD_DIM ** -0.5
    lv = lam_ref[...]
    lam = (jnp.exp(jnp.sum(lv[0:1] * lv[1:2], axis=-1, keepdims=True))
           - jnp.exp(jnp.sum(lv[2:3] * lv[3:4], axis=-1, keepdims=True)) + lambda_init)
    q = q_ref[0]
    k = k_ref[0]
    v = v_ref[0]
    t_col = j * tq + lax.broadcasted_iota(jnp.int32, (tq, 1), 0)
    key = lax.broadcasted_iota(jnp.int32, (tq, seq), 1)
    bias = jnp.where(key <= t_col, 0.0, NEG)

    def softmax(c):
        s = _dot_nt(q[:, c * HEAD_DIM:(c + 1) * HEAD_DIM], k[:, c * HEAD_DIM:(c + 1) * HEAD_DIM]) * scale + bias
        m = jnp.max(s, axis=-1, keepdims=True)
        e = jnp.exp(s - m)
        return e / jnp.sum(e, axis=-1, keepdims=True)

    a = (softmax(0) - lam * softmax(1)).astype(BF16)
    o = _dot(a, v)
    o_ref[0] = (_rms(o, sg_ref[...]) * (1.0 - lambda_init)).astype(BF16)


def _diff_attention(proj, lam, subln, off, batch, seq, heads, lambda_init):
    vw = 2 * HEAD_DIM
    tq = _row_tile(seq, 256)
    qb, kb, vb = off["q_d"] // vw, off["k_d"] // vw, off["v_d"] // vw
    return pl.pallas_call(
        functools.partial(_diff_kernel, seq=seq, tq=tq, lambda_init=lambda_init),
        grid=(batch, heads, seq // tq),
        in_specs=[
            pl.BlockSpec((1, tq, vw), lambda b, h, j: (b, j, qb + h)),
            pl.BlockSpec((1, seq, vw), lambda b, h, j: (b, 0, kb + h)),
            pl.BlockSpec((1, seq, vw), lambda b, h, j: (b, 0, vb + h)),
            pl.BlockSpec((4, HEAD_DIM), lambda b, h, j: (0, 0)),
            pl.BlockSpec((1, vw), lambda b, h, j: (0, 0)),
        ],
        out_specs=pl.BlockSpec((1, tq, vw), lambda b, h, j: (b, j, h)),
        out_shape=jax.ShapeDtypeStruct((batch, seq, heads * vw), BF16),
        compiler_params=_params("parallel", "parallel", "arbitrary"),
        name="diff_attention",
    )(proj, proj, proj, lam, subln)


def _out_kernel(x_ref, a_ref, b_ref, wa_ref, wb_ref, o_ref):
    o_ref[...] = x_ref[...] + _dot(a_ref[...], wa_ref[...]) + _dot(b_ref[...], wb_ref[...])


def _out_project(x, a, b, wa, wb):
    n, d = x.shape
    ka, kb = a.shape[1], b.shape[1]
    tm = _row_tile(n, 512)
    tn = LANES * _largest_divisor(d // LANES, 8)
    return pl.pallas_call(
        _out_kernel,
        grid=(n // tm, d // tn),
        in_specs=[
            pl.BlockSpec((tm, tn), lambda i, j: (i, j)),
            pl.BlockSpec((tm, ka), lambda i, j: (i, 0)),
            pl.BlockSpec((tm, kb), lambda i, j: (i, 0)),
            pl.BlockSpec((ka, tn), lambda i, j: (0, j)),
            pl.BlockSpec((kb, tn), lambda i, j: (0, j)),
        ],
        out_specs=pl.BlockSpec((tm, tn), lambda i, j: (i, j)),
        out_shape=jax.ShapeDtypeStruct((n, d), F32),
        compiler_params=_params("parallel", "arbitrary"),
        name="out_proj",
    )(x, a, b, wa, wb)


def _rope_tables(t):
    inv = 1.0 / (ROPE_THETA ** (jnp.arange(0, HEAD_DIM, 2, dtype=F32) / HEAD_DIM))
    ang = jnp.arange(t, dtype=F32)[:, None] * inv[None, :]
    ang = jnp.concatenate([ang, ang], axis=-1)
    sign = jnp.concatenate([-jnp.ones((HEAD_DIM // 2,), F32), jnp.ones((HEAD_DIM // 2,), F32)])
    return jnp.cos(ang), jnp.sin(ang) * sign[None, :]


def _layout(d_model):
    nsa_heads = d_model // (2 * HEAD_DIM)
    diff_heads = d_model // (4 * HEAD_DIM)
    kv = NSA_KV_HEADS * HEAD_DIM
    sizes = dict(q_n=nsa_heads * HEAD_DIM, kc=kv, vc=kv, ks=kv, vs=kv, kw=kv, vw=kv, gates=3 * nsa_heads,
                 q_d=2 * diff_heads * HEAD_DIM, k_d=2 * diff_heads * HEAD_DIM, v_d=diff_heads * 2 * HEAD_DIM)
    orig, o = {}, 0
    for name in ("q_n", "kc", "vc", "ks", "vs", "kw", "vw", "gates", "q_d", "k_d", "v_d"):
        orig[name] = o
        o += sizes[name]
    rope_names = ("q_n", "kc", "ks", "kw", "q_d", "k_d")
    plain_names = ("vc", "vs", "vw", "v_d")
    new, o = {}, 0
    for name in rope_names + plain_names:
        new[name] = o
        o += sizes[name]
    n_rope_groups = sum(sizes[nm] for nm in rope_names) // LANES
    return sizes, orig, new, rope_names + plain_names, n_rope_groups, nsa_heads, diff_heads


def kernel(x, ffn1_norm, ffn1_w_gate, ffn1_w_up, ffn1_w_down, mix_norm, w_in, cmp_pos_k, cmp_pos_v, cmp_wk1, cmp_wk2, cmp_wv1, cmp_wv2, lam_q1, lam_k1, lam_q2, lam_k2, diff_subln, w_out, ffn2_norm, ffn2_w_gate, ffn2_w_up, ffn2_w_down, final_norm):
    batch, seq, d_model = x.shape
    depth = ffn1_norm.shape[0]
    n = batch * seq
    sizes, orig, off, order, n_rope_groups, nsa_heads, diff_heads = _layout(d_model)
    g = NSA_KV_HEADS
    rep = nsa_heads // g
    cos, sin_signed = _rope_tables(seq)
    fg = final_norm.reshape(1, d_model)
    n_chunks = seq // CMP_STRIDE

    def chunked(cols):
        c = cols.reshape(batch, n_chunks, CMP_STRIDE, g, HEAD_DIM).transpose(0, 3, 1, 2, 4)
        return c.reshape(batch, g, n_chunks, CMP_STRIDE * HEAD_DIM)

    xf = x.reshape(n, d_model)
    for l in range(depth):
        lambda_init = 0.8 - 0.6 * math.exp(-0.3 * l)
        xf = _ffn(xf, ffn1_norm[l].reshape(1, d_model), ffn1_w_gate[l].astype(BF16), ffn1_w_up[l].astype(BF16),
                  ffn1_w_down[l].astype(BF16), fg, False)

        wl = w_in[l]
        w_main = jnp.concatenate([wl[:, orig[nm]:orig[nm] + sizes[nm]] for nm in order], axis=1).astype(BF16)
        gate_tiles = []
        for h in range(g):
            cols = wl[:, orig["gates"] + h * 3 * rep: orig["gates"] + (h + 1) * 3 * rep]
            gate_tiles.append(jnp.pad(cols, ((0, 0), (0, LANES - 3 * rep))))
        w_gate = jnp.concatenate(gate_tiles, axis=1).astype(BF16)
        proj, gates = _project(xf, mix_norm[l].reshape(1, d_model), w_main, w_gate, cos, sin_signed, n_rope_groups)
        proj3 = proj.reshape(batch, seq, -1)
        gates3 = gates.reshape(batch, seq, -1)

        kv = sizes["kc"]
        ck = chunked(proj3[:, :, off["kc"]:off["kc"] + kv])
        cv = chunked(proj3[:, :, off["vc"]:off["vc"] + kv])
        kc, vc = _compress(ck, cv,
                           cmp_pos_k[l].reshape(1, -1).astype(BF16), cmp_pos_v[l].reshape(1, -1).astype(BF16),
                           cmp_wk1[l].astype(BF16), cmp_wk2[l].astype(BF16),
                           cmp_wv1[l].astype(BF16), cmp_wv2[l].astype(BF16))
        o_nsa = _nsa_attention(proj3, gates3, kc, vc, off, batch, seq, rep)
        lam = jnp.stack([lam_q1[l], lam_k1[l], lam_q2[l], lam_k2[l]])
        o_diff = _diff_attention(proj3, lam, diff_subln[l].reshape(1, -1), off, batch, seq, diff_heads, lambda_init)

        half = sizes["q_n"]
        wo = w_out[l].astype(BF16)
        xf = _out_project(xf, o_nsa.reshape(n, -1), o_diff.reshape(n, -1), wo[:half], wo[half:])

        xf = _ffn(xf, ffn2_norm[l].reshape(1, d_model), ffn2_w_gate[l].astype(BF16), ffn2_w_up[l].astype(BF16),
                  ffn2_w_down[l].astype(BF16), fg, l == depth - 1)
    return xf.reshape(batch, seq, d_model)
```

```python
import functools
import math

import numpy as np
import jax
import jax.numpy as jnp
from jax import lax
from jax.experimental import pallas as pl
from jax.experimental.pallas import tpu as pltpu

HEAD_DIM = 128
NSA_KV_HEADS = 2
CMP_LEN = 32
CMP_STRIDE = 16
SLC_LEN = 64
SLC_TOPK = 16
WINDOW = 512
Q_BLOCK = 128
ROPE_THETA = 10000.0
EPS = 1e-6
NEG = -1e30
LOG2E = 1.4426950408889634
SLC_VARIANT_LEN = 512
LANES = 128
VMEM_LIMIT = 56 * 1024 * 1024

F32 = jnp.float32
BF16 = jnp.bfloat16


def _largest_divisor(n, cap):
    for d in range(min(n, cap), 0, -1):
        if n % d == 0:
            return d
    return 1


def _row_tile(n, cap):
    for d in range(min(n, cap), 7, -1):
        if n % d == 0 and d % 8 == 0:
            return d
    return n


def _rms(x, g):
    return x * lax.rsqrt(jnp.mean(x * x, axis=-1, keepdims=True) + EPS) * g


def _dot(a, b):
    return jnp.dot(a, b, preferred_element_type=F32)


def _dot_nt(a, b):
    return lax.dot_general(a, b, (((1,), (1,)), ((), ())), preferred_element_type=F32)


def _params(*sem):
    return pltpu.CompilerParams(dimension_semantics=sem, vmem_limit_bytes=VMEM_LIMIT)


def _ffn_kernel(x_ref, g_ref, wg_ref, wu_ref, wd_ref, fg_ref, o_ref, h_ref, *, final_norm):
    j = pl.program_id(1)

    @pl.when(j == 0)
    def _():
        x = x_ref[...]
        h_ref[...] = _rms(x, g_ref[...]).astype(BF16)
        o_ref[...] = x

    h = h_ref[...]
    a = _dot(h, wg_ref[...])
    b = _dot(h, wu_ref[...])
    act = (a * jax.nn.sigmoid(a) * b * 0.5).astype(BF16)
    o_ref[...] += _dot(act, wd_ref[...])

    if final_norm:
        @pl.when(j == pl.num_programs(1) - 1)
        def _():
            o_ref[...] = _rms(o_ref[...], fg_ref[...])


def _ffn(x, g, wg, wu, wd, fg, final_norm):
    n, d = x.shape
    f = wg.shape[1]
    tm = _row_tile(n, 512)
    tf = LANES * _largest_divisor(f // LANES, 4)
    return pl.pallas_call(
        functools.partial(_ffn_kernel, final_norm=final_norm),
        grid=(n // tm, f // tf),
        in_specs=[
            pl.BlockSpec((tm, d), lambda i, j: (i, 0)),
            pl.BlockSpec((1, d), lambda i, j: (0, 0)),
            pl.BlockSpec((d, tf), lambda i, j: (0, j)),
            pl.BlockSpec((d, tf), lambda i, j: (0, j)),
            pl.BlockSpec((tf, d), lambda i, j: (j, 0)),
            pl.BlockSpec((1, d), lambda i, j: (0, 0)),
        ],
        out_specs=pl.BlockSpec((tm, d), lambda i, j: (i, 0)),
        out_shape=jax.ShapeDtypeStruct((n, d), F32),
        scratch_shapes=[pltpu.VMEM((tm, d), BF16)],
        compiler_params=_params("parallel", "arbitrary"),
        name="ffn",
    )(x, g, wg, wu, wd, fg)


def _proj_kernel(x_ref, g_ref, w_ref, wgate_ref, cos_ref, sin_ref, o_ref, gate_ref, h_ref,
                 *, groups_per_tile, n_rope_groups):
    j = pl.program_id(1)

    @pl.when(j == 0)
    def _():
        h = _rms(x_ref[...], g_ref[...]).astype(BF16)
        h_ref[...] = h
        gate_ref[...] = _dot(h, wgate_ref[...])

    acc = _dot(h_ref[...], w_ref[...])
    cos = cos_ref[...]
    sin = sin_ref[...]
    for gi in range(groups_per_tile):
        is_rope = (j * groups_per_tile + gi) < n_rope_groups
        c = jnp.where(is_rope, cos, 1.0)
        s = jnp.where(is_rope, sin, 0.0)
        xg = acc[:, gi * LANES:(gi + 1) * LANES]
        y = xg * c + pltpu.roll(xg, HEAD_DIM // 2, axis=1) * s
        o_ref[:, gi * LANES:(gi + 1) * LANES] = y.astype(BF16)


def _project(x, g, w_main, w_gate, cos, sin_signed, n_rope_groups):
    n, d = x.shape
    t = cos.shape[0]
    width = w_main.shape[1]
    gw = w_gate.shape[1]
    n_groups = width // LANES
    gpt = _largest_divisor(n_groups, 11)
    tn = gpt * LANES
    tm = _row_tile(t, 512)
    t_blocks = t // tm
    return pl.pallas_call(
        functools.partial(_proj_kernel, groups_per_tile=gpt, n_rope_groups=n_rope_groups),
        grid=(n // tm, width // tn),
        in_specs=[
            pl.BlockSpec((tm, d), lambda i, j: (i, 0)),
            pl.BlockSpec((1, d), lambda i, j: (0, 0)),
            pl.BlockSpec((d, tn), lambda i, j: (0, j)),
            pl.BlockSpec((d, gw), lambda i, j: (0, 0)),
            pl.BlockSpec((tm, HEAD_DIM), lambda i, j: (i % t_blocks, 0)),
            pl.BlockSpec((tm, HEAD_DIM), lambda i, j: (i % t_blocks, 0)),
        ],
        out_specs=[
            pl.BlockSpec((tm, tn), lambda i, j: (i, j)),
            pl.BlockSpec((tm, gw), lambda i, j: (i, 0)),
        ],
        out_shape=[
            jax.ShapeDtypeStruct((n, width), BF16),
            jax.ShapeDtypeStruct((n, gw), F32),
        ],
        scratch_shapes=[pltpu.VMEM((tm, d), BF16)],
        compiler_params=_params("parallel", "arbitrary"),
        name="in_proj",
    )(x, g, w_main, w_gate, cos, sin_signed)


def _cmp_kernel(ck_ref, cv_ref, pk_ref, pv_ref, wk1_ref, wk2_ref, wv1_ref, wv2_ref, ok_ref, ov_ref):
    def one(c_ref, p_ref, w1_ref, w2_ref, o_ref):
        c = c_ref[0, 0]
        half = c.shape[1]
        n_chunks = c.shape[0]
        a = _dot(c, w1_ref[0:half, :])
        b = _dot(c, w1_ref[half:2 * half, :])
        p = jnp.broadcast_to(p_ref[...], (8, 2 * half))
        bias = _dot(p, w1_ref[...])[0:1]
        pre = a + pltpu.roll(b, n_chunks - 1, axis=0) + bias
        hid = (pre * jax.nn.sigmoid(pre)).astype(BF16)
        o_ref[0, 0] = _dot(hid, w2_ref[...]).astype(BF16)

    one(ck_ref, pk_ref, wk1_ref, wk2_ref, ok_ref)
    one(cv_ref, pv_ref, wv1_ref, wv2_ref, ov_ref)


def _compress(ck, cv, pk, pv, wk1, wk2, wv1, wv2):
    b, g, nc, cw = ck.shape
    hid = wk1.shape[1]
    dk = wk2.shape[1]
    chunk_spec = pl.BlockSpec((1, 1, nc, cw), lambda i, j: (i, j, 0, 0))
    full = lambda shape: pl.BlockSpec(shape, lambda i, j: tuple(0 for _ in shape))
    out_spec = pl.BlockSpec((1, 1, nc, dk), lambda i, j: (i, j, 0, 0))
    return pl.pallas_call(
        _cmp_kernel,
        grid=(b, g),
        in_specs=[chunk_spec, chunk_spec, full((1, 2 * cw)), full((1, 2 * cw)),
                  full((2 * cw, hid)), full((hid, dk)), full((2 * cw, hid)), full((hid, dk))],
        out_specs=[out_spec, out_spec],
        out_shape=[jax.ShapeDtypeStruct((b, g, nc, dk), BF16)] * 2,
        compiler_params=_params("parallel", "parallel"),
        name="nsa_compress",
    )(ck, cv, pk, pv, wk1, wk2, wv1, wv2)


def _nsa_kernel(q_ref, ks_ref, vs_ref, kw_ref, vw_ref, kc_ref, vc_ref, gate_ref, ovl_ref, exp_ref,
                o_ref, slc_ref, *, seq, rep, n_slc, n_sel, var_len):
    j = pl.program_id(2)
    s0 = j * Q_BLOCK
    c = HEAD_DIM ** -0.5 * LOG2E
    q = q_ref[0]
    q4 = jnp.concatenate([q[:, r * HEAD_DIM:(r + 1) * HEAD_DIM] for r in range(rep)], axis=0)
    t_col = s0 + lax.broadcasted_iota(jnp.int32, (Q_BLOCK, 1), 0)
    rows = lambda a, r: a[r * Q_BLOCK:(r + 1) * Q_BLOCK]

    def masked_attention(s4, bias, v):
        es, ls = [], []
        for r in range(rep):
            t = rows(s4, r) + bias
            e = jnp.exp2(t - jnp.max(t, axis=-1, keepdims=True))
            ls.append(jnp.sum(e, axis=-1, keepdims=True))
            es.append(e.astype(BF16))
        return _dot(jnp.concatenate(es, axis=0), v) / jnp.concatenate(ls, axis=0)

    kc = kc_ref[0, 0]
    vc = vc_ref[0, 0]
    n_rows = kc.shape[0]
    n_idx = lax.broadcasted_iota(jnp.int32, (Q_BLOCK, n_rows), 1)
    c_mask = (n_idx * CMP_STRIDE + (CMP_LEN - 1)) <= t_col
    s4 = _dot_nt(q4, kc) * c
    p_sum = jnp.zeros((Q_BLOCK, n_rows), F32)
    ps = []
    for r in range(rep):
        t = jnp.where(c_mask, rows(s4, r), NEG)
        e = jnp.where(c_mask, jnp.exp2(t - jnp.max(t, axis=-1, keepdims=True)), 0.0)
        l = jnp.sum(e, axis=-1, keepdims=True)
        p = e / jnp.where(l > 0.0, l, 1.0)
        p_sum = p_sum + p
        ps.append(p.astype(BF16))
    o_cmp = _dot(jnp.concatenate(ps, axis=0), vc)

    hi = p_sum.astype(BF16)
    lo = (p_sum - hi.astype(F32)).astype(BF16)
    ovl = ovl_ref[...]
    imp = _dot(hi, ovl) + _dot(lo, ovl)
    n_pad = -(-n_slc // 8) * 8
    imp_t = imp.T[0:n_pad]
    blk = lax.broadcasted_iota(jnp.int32, (n_pad, Q_BLOCK), 0)
    t_row = s0 + lax.broadcasted_iota(jnp.int32, (1, Q_BLOCK), 1)
    cur = t_row // SLC_LEN
    forced = (blk == 0) | (blk == cur) | (blk == cur - 1)
    imp_t = jnp.where(forced, 1e4, imp_t)
    imp_t = jnp.where(blk * SLC_LEN <= t_row, imp_t, -1.0)
    rank = jnp.zeros((n_pad, Q_BLOCK), F32)
    for sp in range(n_slc):
        row = imp_t[sp:sp + 1, :]
        ge = jnp.where(row >= imp_t, 1.0, 0.0)
        gt = jnp.where(row > imp_t, 1.0, 0.0)
        rank = rank + jnp.where(blk > sp, ge, gt)
    sel_t = jnp.where(rank < float(n_sel), 1.0, 0.0)
    sel_t = jnp.concatenate([sel_t, jnp.zeros((LANES - n_pad, Q_BLOCK), F32)], axis=0)
    sel = sel_t.T.astype(BF16)

    q_per_var = var_len // Q_BLOCK
    for v in range(seq // var_len):
        kv_len = (v + 1) * var_len

        @pl.when(j // q_per_var == v)
        def _(kv_len=kv_len):
            key_sel = _dot(sel, exp_ref[:, 0:kv_len])
            key = lax.broadcasted_iota(jnp.int32, (Q_BLOCK, kv_len), 1)
            bias = jnp.where((key_sel > 0.5) & (key <= t_col), 0.0, NEG)
            s4 = _dot_nt(q4, ks_ref[0, 0:kv_len, :]) * c
            slc_ref[...] = masked_attention(s4, bias, vs_ref[0, 0:kv_len, :])

    span = WINDOW + Q_BLOCK
    start = pl.multiple_of(jnp.maximum(s0 - WINDOW, 0), Q_BLOCK)
    dist = t_col - (start + lax.broadcasted_iota(jnp.int32, (Q_BLOCK, span), 1))
    wbias = jnp.where((dist >= 0) & (dist < WINDOW), 0.0, NEG)
    s4 = _dot_nt(q4, kw_ref[0, pl.ds(start, span), :]) * c
    o_win = masked_attention(s4, wbias, vw_ref[0, pl.ds(start, span), :])

    gt = jax.nn.sigmoid(gate_ref[0])
    o_slc = slc_ref[...]
    for r in range(rep):
        out = (gt[:, 3 * r:3 * r + 1] * rows(o_cmp, r) + gt[:, 3 * r + 1:3 * r + 2] * rows(o_slc, r)
               + gt[:, 3 * r + 2:3 * r + 3] * rows(o_win, r))
        o_ref[0, :, r * HEAD_DIM:(r + 1) * HEAD_DIM] = out.astype(BF16)


def _block_overlap(n_rows, n_cmp, n_slc):
    c0 = np.arange(n_cmp) * CMP_STRIDE
    s0 = np.arange(n_slc) * SLC_LEN
    lo = np.maximum(c0[:, None], s0[None, :])
    hi = np.minimum(c0[:, None] + CMP_LEN, s0[None, :] + SLC_LEN)
    out = np.zeros((n_rows, LANES), np.float32)
    out[:n_cmp, :n_slc] = np.clip(hi - lo, 0, None) / CMP_LEN
    return out


def _nsa_attention(proj, gates, kc, vc, off, batch, seq, rep):
    g = NSA_KV_HEADS
    n_qb = seq // Q_BLOCK
    n_slc = seq // SLC_LEN
    n_sel = min(SLC_TOPK, n_slc)
    n_rows = kc.shape[2]
    n_cmp = (seq - CMP_LEN) // CMP_STRIDE + 1
    qw = rep * HEAD_DIM
    var_len = SLC_VARIANT_LEN if seq % SLC_VARIANT_LEN == 0 else seq
    ovl = jnp.asarray(_block_overlap(n_rows, n_cmp, n_slc), BF16)
    expand = np.zeros((LANES, seq), np.float32)
    expand[np.arange(seq) // SLC_LEN, np.arange(seq)] = 1.0
    expand = jnp.asarray(expand, BF16)

    def kv_spec(name):
        base = off[name] // HEAD_DIM
        return pl.BlockSpec((1, seq, HEAD_DIM), lambda b, h, j: (b, 0, base + h))

    cmp_spec = pl.BlockSpec((1, 1, n_rows, HEAD_DIM), lambda b, h, j: (b, h, 0, 0))
    q_base = off["q_n"] // qw
    return pl.pallas_call(
        functools.partial(_nsa_kernel, seq=seq, rep=rep, n_slc=n_slc, n_sel=n_sel, var_len=var_len),
        grid=(batch, g, n_qb),
        scratch_shapes=[pltpu.VMEM((rep * Q_BLOCK, HEAD_DIM), F32)],
        in_specs=[
            pl.BlockSpec((1, Q_BLOCK, qw), lambda b, h, j: (b, j, q_base + h)),
            kv_spec("ks"), kv_spec("vs"), kv_spec("kw"), kv_spec("vw"),
            cmp_spec, cmp_spec,
            pl.BlockSpec((1, Q_BLOCK, LANES), lambda b, h, j: (b, j, h)),
            pl.BlockSpec((n_rows, LANES), lambda b, h, j: (0, 0)),
            pl.BlockSpec((LANES, seq), lambda b, h, j: (0, 0)),
        ],
        out_specs=pl.BlockSpec((1, Q_BLOCK, qw), lambda b, h, j: (b, j, h)),
        out_shape=jax.ShapeDtypeStruct((batch, seq, g * qw), BF16),
        compiler_params=_params("parallel", "parallel", "arbitrary"),
        name="nsa_attention",
    )(proj, proj, proj, proj, proj, kc, vc, gates, ovl, expand)


def _diff_kernel(q_ref, k_ref, v_ref, lam_ref, sg_ref, o_ref, *, seq, tq, lambda_init):
    j = pl.program_id(2)
    c = HEAD_DIM ** -0.5 * LOG2E
    lv = lam_ref[...]
    lam = (jnp.exp(jnp.sum(lv[0:1] * lv[1:2], axis=-1, keepdims=True))
           - jnp.exp(jnp.sum(lv[2:3] * lv[3:4], axis=-1, keepdims=True)) + lambda_init)
    q = q_ref[0]
    diag_bias = jnp.where(lax.broadcasted_iota(jnp.int32, (tq, tq), 1) <= lax.broadcasted_iota(jnp.int32, (tq, tq), 0),
                          0.0, NEG)

    def attend(cmap, past):
        lo, hi = cmap * HEAD_DIM, (cmap + 1) * HEAD_DIM
        qc = q[:, lo:hi]
        t_d = _dot_nt(qc, k_ref[0, past:past + tq, lo:hi]) * c + diag_bias
        m = jnp.max(t_d, axis=-1, keepdims=True)
        if past:
            t_p = _dot_nt(qc, k_ref[0, 0:past, lo:hi]) * c
            m = jnp.maximum(m, jnp.max(t_p, axis=-1, keepdims=True))
        e_d = jnp.exp2(t_d - m)
        l = jnp.sum(e_d, axis=-1, keepdims=True)
        o = _dot(e_d.astype(BF16), v_ref[0, past:past + tq, :])
        if past:
            e_p = jnp.exp2(t_p - m)
            l = l + jnp.sum(e_p, axis=-1, keepdims=True)
            o = o + _dot(e_p.astype(BF16), v_ref[0, 0:past, :])
        return o / l

    for v in range(seq // tq):
        @pl.when(j == v)
        def _(past=v * tq):
            o = attend(0, past) - lam * attend(1, past)
            o_ref[0] = (_rms(o, sg_ref[...]) * (1.0 - lambda_init)).astype(BF16)


def _diff_attention(proj, lam, subln, off, batch, seq, heads, lambda_init):
    vw = 2 * HEAD_DIM
    tq = _row_tile(seq, 256)
    qb, kb, vb = off["q_d"] // vw, off["k_d"] // vw, off["v_d"] // vw
    return pl.pallas_call(
        functools.partial(_diff_kernel, seq=seq, tq=tq, lambda_init=lambda_init),
        grid=(batch, heads, seq // tq),
        in_specs=[
            pl.BlockSpec((1, tq, vw), lambda b, h, j: (b, j, qb + h)),
            pl.BlockSpec((1, seq, vw), lambda b, h, j: (b, 0, kb + h)),
            pl.BlockSpec((1, seq, vw), lambda b, h, j: (b, 0, vb + h)),
            pl.BlockSpec((4, HEAD_DIM), lambda b, h, j: (0, 0)),
            pl.BlockSpec((1, vw), lambda b, h, j: (0, 0)),
        ],
        out_specs=pl.BlockSpec((1, tq, vw), lambda b, h, j: (b, j, h)),
        out_shape=jax.ShapeDtypeStruct((batch, seq, heads * vw), BF16),
        compiler_params=_params("parallel", "parallel", "arbitrary"),
        name="diff_attention",
    )(proj, proj, proj, lam, subln)


def _out_kernel(x_ref, a_ref, b_ref, wa_ref, wb_ref, o_ref):
    o_ref[...] = x_ref[...] + _dot(a_ref[...], wa_ref[...]) + _dot(b_ref[...], wb_ref[...])


def _out_project(x, a, b, wa, wb):
    n, d = x.shape
    ka, kb = a.shape[1], b.shape[1]
    tm = _row_tile(n, 512)
    tn = LANES * _largest_divisor(d // LANES, 8)
    return pl.pallas_call(
        _out_kernel,
        grid=(n // tm, d // tn),
        in_specs=[
            pl.BlockSpec((tm, tn), lambda i, j: (i, j)),
            pl.BlockSpec((tm, ka), lambda i, j: (i, 0)),
            pl.BlockSpec((tm, kb), lambda i, j: (i, 0)),
            pl.BlockSpec((ka, tn), lambda i, j: (0, j)),
            pl.BlockSpec((kb, tn), lambda i, j: (0, j)),
        ],
        out_specs=pl.BlockSpec((tm, tn), lambda i, j: (i, j)),
        out_shape=jax.ShapeDtypeStruct((n, d), F32),
        compiler_params=_params("parallel", "arbitrary"),
        name="out_proj",
    )(x, a, b, wa, wb)


def _rope_tables(t):
    inv = 1.0 / (ROPE_THETA ** (jnp.arange(0, HEAD_DIM, 2, dtype=F32) / HEAD_DIM))
    ang = jnp.arange(t, dtype=F32)[:, None] * inv[None, :]
    ang = jnp.concatenate([ang, ang], axis=-1)
    sign = jnp.concatenate([-jnp.ones((HEAD_DIM // 2,), F32), jnp.ones((HEAD_DIM // 2,), F32)])
    return jnp.cos(ang), jnp.sin(ang) * sign[None, :]


def _layout(d_model):
    nsa_heads = d_model // (2 * HEAD_DIM)
    diff_heads = d_model // (4 * HEAD_DIM)
    kv = NSA_KV_HEADS * HEAD_DIM
    sizes = dict(q_n=nsa_heads * HEAD_DIM, kc=kv, vc=kv, ks=kv, vs=kv, kw=kv, vw=kv, gates=3 * nsa_heads,
                 q_d=2 * diff_heads * HEAD_DIM, k_d=2 * diff_heads * HEAD_DIM, v_d=diff_heads * 2 * HEAD_DIM)
    orig, o = {}, 0
    for name in ("q_n", "kc", "vc", "ks", "vs", "kw", "vw", "gates", "q_d", "k_d", "v_d"):
        orig[name] = o
        o += sizes[name]
    rope_names = ("q_n", "kc", "ks", "kw", "q_d", "k_d")
    plain_names = ("vc", "vs", "vw", "v_d")
    new, o = {}, 0
    for name in rope_names + plain_names:
        new[name] = o
        o += sizes[name]
    n_rope_groups = sum(sizes[nm] for nm in rope_names) // LANES
    return sizes, orig, new, rope_names + plain_names, n_rope_groups, nsa_heads, diff_heads


def kernel(x, ffn1_norm, ffn1_w_gate, ffn1_w_up, ffn1_w_down, mix_norm, w_in, cmp_pos_k, cmp_pos_v, cmp_wk1, cmp_wk2, cmp_wv1, cmp_wv2, lam_q1, lam_k1, lam_q2, lam_k2, diff_subln, w_out, ffn2_norm, ffn2_w_gate, ffn2_w_up, ffn2_w_down, final_norm):
    batch, seq, d_model = x.shape
    depth = ffn1_norm.shape[0]
    n = batch * seq
    sizes, orig, off, order, n_rope_groups, nsa_heads, diff_heads = _layout(d_model)
    g = NSA_KV_HEADS
    rep = nsa_heads // g
    cos, sin_signed = _rope_tables(seq)
    fg = final_norm.reshape(1, d_model)
    n_chunks = seq // CMP_STRIDE

    def chunked(cols):
        c = cols.reshape(batch, n_chunks, CMP_STRIDE, g, HEAD_DIM).transpose(0, 3, 1, 2, 4)
        return c.reshape(batch, g, n_chunks, CMP_STRIDE * HEAD_DIM)

    xf = x.reshape(n, d_model)
    for l in range(depth):
        lambda_init = 0.8 - 0.6 * math.exp(-0.3 * l)
        xf = _ffn(xf, ffn1_norm[l].reshape(1, d_model), ffn1_w_gate[l].astype(BF16), ffn1_w_up[l].astype(BF16),
                  ffn1_w_down[l].astype(BF16), fg, False)

        wl = w_in[l]
        w_main = jnp.concatenate([wl[:, orig[nm]:orig[nm] + sizes[nm]] for nm in order], axis=1).astype(BF16)
        gate_tiles = []
        for h in range(g):
            cols = wl[:, orig["gates"] + h * 3 * rep: orig["gates"] + (h + 1) * 3 * rep]
            gate_tiles.append(jnp.pad(cols, ((0, 0), (0, LANES - 3 * rep))))
        w_gate = jnp.concatenate(gate_tiles, axis=1).astype(BF16)
        proj, gates = _project(xf, mix_norm[l].reshape(1, d_model), w_main, w_gate, cos, sin_signed, n_rope_groups)
        proj3 = proj.reshape(batch, seq, -1)
        gates3 = gates.reshape(batch, seq, -1)

        kv = sizes["kc"]
        ck = chunked(proj3[:, :, off["kc"]:off["kc"] + kv])
        cv = chunked(proj3[:, :, off["vc"]:off["vc"] + kv])
        kc, vc = _compress(ck, cv,
                           cmp_pos_k[l].reshape(1, -1).astype(BF16), cmp_pos_v[l].reshape(1, -1).astype(BF16),
                           cmp_wk1[l].astype(BF16), cmp_wk2[l].astype(BF16),
                           cmp_wv1[l].astype(BF16), cmp_wv2[l].astype(BF16))
        o_nsa = _nsa_attention(proj3, gates3, kc, vc, off, batch, seq, rep)
        lam = jnp.stack([lam_q1[l], lam_k1[l], lam_q2[l], lam_k2[l]])
        o_diff = _diff_attention(proj3, lam, diff_subln[l].reshape(1, -1), off, batch, seq, diff_heads, lambda_init)

        half = sizes["q_n"]
        wo = w_out[l].astype(BF16)
        xf = _out_project(xf, o_nsa.reshape(n, -1), o_diff.reshape(n, -1), wo[:half], wo[half:])

        xf = _ffn(xf, ffn2_norm[l].reshape(1, d_model), ffn2_w_gate[l].astype(BF16), ffn2_w_up[l].astype(BF16),
                  ffn2_w_down[l].astype(BF16), fg, l == depth - 1)
    return xf.reshape(batch, seq, d_model)
```

```python
import functools
import math

import numpy as np
import jax
import jax.numpy as jnp
from jax import lax
from jax.experimental import pallas as pl
from jax.experimental.pallas import tpu as pltpu

HEAD_DIM = 128
NSA_KV_HEADS = 2
CMP_LEN = 32
CMP_STRIDE = 16
SLC_LEN = 64
SLC_TOPK = 16
WINDOW = 512
Q_BLOCK = 128
ROPE_THETA = 10000.0
EPS = 1e-6
NEG = -1e30
LOG2E = 1.4426950408889634
SLC_VARIANT_LEN = 512
LANES = 128
VMEM_LIMIT = 58 * 1024 * 1024

F32 = jnp.float32
BF16 = jnp.bfloat16


def _largest_divisor(n, cap):
    for d in range(min(n, cap), 0, -1):
        if n % d == 0:
            return d
    return 1


def _row_tile(n, cap):
    for d in range(min(n, cap), 7, -1):
        if n % d == 0 and d % 8 == 0:
            return d
    return n


def _rms(x, g):
    return x * lax.rsqrt(jnp.mean(x * x, axis=-1, keepdims=True) + EPS) * g


def _dot(a, b):
    return jnp.dot(a, b, preferred_element_type=F32)


def _dot_nt(a, b):
    return lax.dot_general(a, b, (((1,), (1,)), ((), ())), preferred_element_type=F32)


def _params(*sem):
    return pltpu.CompilerParams(dimension_semantics=sem, vmem_limit_bytes=VMEM_LIMIT)


def _ffn_kernel(*refs, final_norm, cast_next):
    if cast_next:
        (x_ref, g_ref, wg_ref, wu_ref, wd_ref, fg_ref, ng_ref, nu_ref, nd_ref,
         o_ref, og_ref, ou_ref, od_ref, h_ref) = refs
        og_ref[...] = ng_ref[...].astype(BF16)
        ou_ref[...] = nu_ref[...].astype(BF16)
        od_ref[...] = nd_ref[...].astype(BF16)
    else:
        x_ref, g_ref, wg_ref, wu_ref, wd_ref, fg_ref, o_ref, h_ref = refs
    j = pl.program_id(1)

    @pl.when(j == 0)
    def _():
        x = x_ref[...]
        h_ref[...] = _rms(x, g_ref[...]).astype(BF16)
        o_ref[...] = x

    h = h_ref[...]
    a = _dot(h, wg_ref[...])
    b = _dot(h, wu_ref[...])
    act = (a * jax.nn.sigmoid(a) * b * 0.5).astype(BF16)
    o_ref[...] += _dot(act, wd_ref[...])

    if final_norm:
        @pl.when(j == pl.num_programs(1) - 1)
        def _():
            o_ref[...] = _rms(o_ref[...], fg_ref[...])


def _ffn(x, g, wg, wu, wd, fg, final_norm, next_weights=None):
    n, d = x.shape
    f = wg.shape[1]
    tm = _row_tile(n, 1024)
    tf = LANES * _largest_divisor(f // LANES, 4)
    n_i, n_j = n // tm, f // tf
    in_specs = [
        pl.BlockSpec((tm, d), lambda i, j: (i, 0)),
        pl.BlockSpec((1, d), lambda i, j: (0, 0)),
        pl.BlockSpec((d, tf), lambda i, j: (0, j)),
        pl.BlockSpec((d, tf), lambda i, j: (0, j)),
        pl.BlockSpec((tf, d), lambda i, j: (j, 0)),
        pl.BlockSpec((1, d), lambda i, j: (0, 0)),
    ]
    out_specs = [pl.BlockSpec((tm, d), lambda i, j: (i, 0))]
    out_shape = [jax.ShapeDtypeStruct((n, d), F32)]
    args = [x, g, wg, wu, wd, fg]
    if next_weights is not None:
        stacked, layer = next_weights
        dr, fr = d // n_i, tf // n_i
        assert dr * n_i == d and fr * n_i == tf and dr % 16 == 0 and fr % 16 == 0
        assert all(w.shape[1:] == s for w, s in zip(stacked, ((d, f), (d, f), (f, d))))
        in_specs += [
            pl.BlockSpec((None, dr, tf), lambda i, j: (layer, i, j)),
            pl.BlockSpec((None, dr, tf), lambda i, j: (layer, i, j)),
            pl.BlockSpec((None, fr, d), lambda i, j: (layer, j * n_i + i, 0)),
        ]
        out_specs += [
            pl.BlockSpec((dr, tf), lambda i, j: (i, j)),
            pl.BlockSpec((dr, tf), lambda i, j: (i, j)),
            pl.BlockSpec((fr, d), lambda i, j: (j * n_i + i, 0)),
        ]
        out_shape += [jax.ShapeDtypeStruct(w.shape[1:], BF16) for w in stacked]
        args += list(stacked)
    outs = pl.pallas_call(
        functools.partial(_ffn_kernel, final_norm=final_norm, cast_next=next_weights is not None),
        grid=(n_i, n_j),
        in_specs=in_specs,
        out_specs=out_specs,
        out_shape=out_shape,
        scratch_shapes=[pltpu.VMEM((tm, d), BF16)],
        compiler_params=_params("parallel", "arbitrary"),
        name="ffn",
    )(*args)
    return outs[0], tuple(outs[1:])


def _proj_kernel(x_ref, g_ref, w_ref, wgate_ref, cos_ref, sin_ref, o_ref, gate_ref, h_ref,
                 *, groups_per_tile, n_rope_groups):
    j = pl.program_id(1)

    @pl.when(j == 0)
    def _():
        h = _rms(x_ref[...], g_ref[...]).astype(BF16)
        h_ref[...] = h
        gate_ref[...] = _dot(h, wgate_ref[...])

    acc = _dot(h_ref[...], w_ref[...])
    cos = cos_ref[...]
    sin = sin_ref[...]
    for gi in range(groups_per_tile):
        is_rope = (j * groups_per_tile + gi) < n_rope_groups
        c = jnp.where(is_rope, cos, 1.0)
        s = jnp.where(is_rope, sin, 0.0)
        xg = acc[:, gi * LANES:(gi + 1) * LANES]
        y = xg * c + pltpu.roll(xg, HEAD_DIM // 2, axis=1) * s
        o_ref[:, gi * LANES:(gi + 1) * LANES] = y.astype(BF16)


def _project(x, g, w_main, w_gate, cos, sin_signed, n_rope_groups):
    n, d = x.shape
    t = cos.shape[0]
    width = w_main.shape[1]
    gw = w_gate.shape[1]
    n_groups = width // LANES
    gpt = _largest_divisor(n_groups, 11)
    tn = gpt * LANES
    tm = _row_tile(t, 1024)
    t_blocks = t // tm
    return pl.pallas_call(
        functools.partial(_proj_kernel, groups_per_tile=gpt, n_rope_groups=n_rope_groups),
        grid=(n // tm, width // tn),
        in_specs=[
            pl.BlockSpec((tm, d), lambda i, j: (i, 0)),
            pl.BlockSpec((1, d), lambda i, j: (0, 0)),
            pl.BlockSpec((d, tn), lambda i, j: (0, j)),
            pl.BlockSpec((d, gw), lambda i, j: (0, 0)),
            pl.BlockSpec((tm, HEAD_DIM), lambda i, j: (i % t_blocks, 0)),
            pl.BlockSpec((tm, HEAD_DIM), lambda i, j: (i % t_blocks, 0)),
        ],
        out_specs=[
            pl.BlockSpec((tm, tn), lambda i, j: (i, j)),
            pl.BlockSpec((tm, gw), lambda i, j: (i, 0)),
        ],
        out_shape=[
            jax.ShapeDtypeStruct((n, width), BF16),
            jax.ShapeDtypeStruct((n, gw), F32),
        ],
        scratch_shapes=[pltpu.VMEM((tm, d), BF16)],
        compiler_params=_params("parallel", "arbitrary"),
        name="in_proj",
    )(x, g, w_main, w_gate, cos, sin_signed)


def _cmp_kernel(ck_ref, cv_ref, pk_ref, pv_ref, wk1_ref, wk2_ref, wv1_ref, wv2_ref, ok_ref, ov_ref):
    def one(c_ref, p_ref, w1_ref, w2_ref, o_ref):
        c = c_ref[0, 0]
        half = c.shape[1]
        n_chunks = c.shape[0]
        a = _dot(c, w1_ref[0:half, :])
        b = _dot(c, w1_ref[half:2 * half, :])
        p = jnp.broadcast_to(p_ref[...], (8, 2 * half))
        bias = _dot(p, w1_ref[...])[0:1]
        pre = a + pltpu.roll(b, n_chunks - 1, axis=0) + bias
        hid = (pre * jax.nn.sigmoid(pre)).astype(BF16)
        o_ref[0, 0] = _dot(hid, w2_ref[...]).astype(BF16)

    one(ck_ref, pk_ref, wk1_ref, wk2_ref, ok_ref)
    one(cv_ref, pv_ref, wv1_ref, wv2_ref, ov_ref)


def _compress(ck, cv, pk, pv, wk1, wk2, wv1, wv2):
    b, g, nc, cw = ck.shape
    hid = wk1.shape[1]
    dk = wk2.shape[1]
    chunk_spec = pl.BlockSpec((1, 1, nc, cw), lambda i, j: (i, j, 0, 0))
    full = lambda shape: pl.BlockSpec(shape, lambda i, j: tuple(0 for _ in shape))
    out_spec = pl.BlockSpec((1, 1, nc, dk), lambda i, j: (i, j, 0, 0))
    return pl.pallas_call(
        _cmp_kernel,
        grid=(b, g),
        in_specs=[chunk_spec, chunk_spec, full((1, 2 * cw)), full((1, 2 * cw)),
                  full((2 * cw, hid)), full((hid, dk)), full((2 * cw, hid)), full((hid, dk))],
        out_specs=[out_spec, out_spec],
        out_shape=[jax.ShapeDtypeStruct((b, g, nc, dk), BF16)] * 2,
        compiler_params=_params("parallel", "parallel"),
        name="nsa_compress",
    )(ck, cv, pk, pv, wk1, wk2, wv1, wv2)


def _nsa_kernel(q_ref, ks_ref, vs_ref, kw_ref, vw_ref, kc_ref, vc_ref, gate_ref, ovl_ref, exp_ref,
                o_ref, slc_ref, *, seq, rep, n_slc, n_sel, var_len):
    j = pl.program_id(2)
    s0 = j * Q_BLOCK
    c = HEAD_DIM ** -0.5 * LOG2E
    q = q_ref[0]
    q4 = jnp.concatenate([q[:, r * HEAD_DIM:(r + 1) * HEAD_DIM] for r in range(rep)], axis=0)
    t_col = s0 + lax.broadcasted_iota(jnp.int32, (Q_BLOCK, 1), 0)
    rows = lambda a, r: a[r * Q_BLOCK:(r + 1) * Q_BLOCK]

    def masked_attention(s4, bias, v):
        es, ls = [], []
        for r in range(rep):
            t = rows(s4, r) + bias
            e = jnp.exp2(t - jnp.max(t, axis=-1, keepdims=True))
            ls.append(jnp.sum(e, axis=-1, keepdims=True))
            es.append(e.astype(BF16))
        return _dot(jnp.concatenate(es, axis=0), v) / jnp.concatenate(ls, axis=0)

    kc = kc_ref[0, 0]
    vc = vc_ref[0, 0]
    n_rows = kc.shape[0]
    n_idx = lax.broadcasted_iota(jnp.int32, (Q_BLOCK, n_rows), 1)
    c_mask = (n_idx * CMP_STRIDE + (CMP_LEN - 1)) <= t_col
    s4 = _dot_nt(q4, kc) * c
    p_sum = jnp.zeros((Q_BLOCK, n_rows), F32)
    ps = []
    for r in range(rep):
        t = jnp.where(c_mask, rows(s4, r), NEG)
        e = jnp.where(c_mask, jnp.exp2(t - jnp.max(t, axis=-1, keepdims=True)), 0.0)
        l = jnp.sum(e, axis=-1, keepdims=True)
        p = e / jnp.where(l > 0.0, l, 1.0)
        p_sum = p_sum + p
        ps.append(p.astype(BF16))
    o_cmp = _dot(jnp.concatenate(ps, axis=0), vc)

    hi = p_sum.astype(BF16)
    lo = (p_sum - hi.astype(F32)).astype(BF16)
    ovl = ovl_ref[...]
    imp = _dot(hi, ovl) + _dot(lo, ovl)
    n_pad = -(-n_slc // 8) * 8
    imp_t = imp.T[0:n_pad]
    blk = lax.broadcasted_iota(jnp.int32, (n_pad, Q_BLOCK), 0)
    t_row = s0 + lax.broadcasted_iota(jnp.int32, (1, Q_BLOCK), 1)
    cur = t_row // SLC_LEN
    forced = (blk == 0) | (blk == cur) | (blk == cur - 1)
    imp_t = jnp.where(forced, 1e4, imp_t)
    imp_t = jnp.where(blk * SLC_LEN <= t_row, imp_t, -1.0)
    rank = jnp.zeros((n_pad, Q_BLOCK), F32)
    for sp in range(n_slc):
        row = imp_t[sp:sp + 1, :]
        ge = jnp.where(row >= imp_t, 1.0, 0.0)
        gt = jnp.where(row > imp_t, 1.0, 0.0)
        rank = rank + jnp.where(blk > sp, ge, gt)
    sel_t = jnp.where(rank < float(n_sel), 1.0, 0.0)
    sel_t = jnp.concatenate([sel_t, jnp.zeros((LANES - n_pad, Q_BLOCK), F32)], axis=0)
    sel = sel_t.T.astype(BF16)

    q_per_var = var_len // Q_BLOCK
    for v in range(seq // var_len):
        kv_len = (v + 1) * var_len

        @pl.when(j // q_per_var == v)
        def _(kv_len=kv_len):
            key_sel = _dot(sel, exp_ref[:, 0:kv_len])
            key = lax.broadcasted_iota(jnp.int32, (Q_BLOCK, kv_len), 1)
            bias = jnp.where((key_sel > 0.5) & (key <= t_col), 0.0, NEG)
            s4 = _dot_nt(q4, ks_ref[0, 0:kv_len, :]) * c
            slc_ref[...] = masked_attention(s4, bias, vs_ref[0, 0:kv_len, :])

    span = WINDOW + Q_BLOCK
    start = pl.multiple_of(jnp.maximum(s0 - WINDOW, 0), Q_BLOCK)
    dist = t_col - (start + lax.broadcasted_iota(jnp.int32, (Q_BLOCK, span), 1))
    wbias = jnp.where((dist >= 0) & (dist < WINDOW), 0.0, NEG)
    s4 = _dot_nt(q4, kw_ref[0, pl.ds(start, span), :]) * c
    o_win = masked_attention(s4, wbias, vw_ref[0, pl.ds(start, span), :])

    gt = jax.nn.sigmoid(gate_ref[0])
    o_slc = slc_ref[...]
    for r in range(rep):
        out = (gt[:, 3 * r:3 * r + 1] * rows(o_cmp, r) + gt[:, 3 * r + 1:3 * r + 2] * rows(o_slc, r)
               + gt[:, 3 * r + 2:3 * r + 3] * rows(o_win, r))
        o_ref[0, :, r * HEAD_DIM:(r + 1) * HEAD_DIM] = out.astype(BF16)


def _block_overlap(n_rows, n_cmp, n_slc):
    c0 = np.arange(n_cmp) * CMP_STRIDE
    s0 = np.arange(n_slc) * SLC_LEN
    lo = np.maximum(c0[:, None], s0[None, :])
    hi = np.minimum(c0[:, None] + CMP_LEN, s0[None, :] + SLC_LEN)
    out = np.zeros((n_rows, LANES), np.float32)
    out[:n_cmp, :n_slc] = np.clip(hi - lo, 0, None) / CMP_LEN
    return out


def _nsa_attention(proj, gates, kc, vc, off, batch, seq, rep):
    g = NSA_KV_HEADS
    n_qb = seq // Q_BLOCK
    n_slc = seq // SLC_LEN
    n_sel = min(SLC_TOPK, n_slc)
    n_rows = kc.shape[2]
    n_cmp = (seq - CMP_LEN) // CMP_STRIDE + 1
    qw = rep * HEAD_DIM
    var_len = SLC_VARIANT_LEN if seq % SLC_VARIANT_LEN == 0 else seq
    ovl = jnp.asarray(_block_overlap(n_rows, n_cmp, n_slc), BF16)
    expand = np.zeros((LANES, seq), np.float32)
    expand[np.arange(seq) // SLC_LEN, np.arange(seq)] = 1.0
    expand = jnp.asarray(expand, BF16)

    def kv_spec(name):
        base = off[name] // HEAD_DIM
        return pl.BlockSpec((1, seq, HEAD_DIM), lambda b, h, j: (b, 0, base + h))

    cmp_spec = pl.BlockSpec((1, 1, n_rows, HEAD_DIM), lambda b, h, j: (b, h, 0, 0))
    q_base = off["q_n"] // qw
    return pl.pallas_call(
        functools.partial(_nsa_kernel, seq=seq, rep=rep, n_slc=n_slc, n_sel=n_sel, var_len=var_len),
        grid=(batch, g, n_qb),
        scratch_shapes=[pltpu.VMEM((rep * Q_BLOCK, HEAD_DIM), F32)],
        in_specs=[
            pl.BlockSpec((1, Q_BLOCK, qw), lambda b, h, j: (b, j, q_base + h)),
            kv_spec("ks"), kv_spec("vs"), kv_spec("kw"), kv_spec("vw"),
            cmp_spec, cmp_spec,
            pl.BlockSpec((1, Q_BLOCK, LANES), lambda b, h, j: (b, j, h)),
            pl.BlockSpec((n_rows, LANES), lambda b, h, j: (0, 0)),
            pl.BlockSpec((LANES, seq), lambda b, h, j: (0, 0)),
        ],
        out_specs=pl.BlockSpec((1, Q_BLOCK, qw), lambda b, h, j: (b, j, h)),
        out_shape=jax.ShapeDtypeStruct((batch, seq, g * qw), BF16),
        compiler_params=_params("parallel", "parallel", "arbitrary"),
        name="nsa_attention",
    )(proj, proj, proj, proj, proj, kc, vc, gates, ovl, expand)


def _diff_kernel(q_ref, k_ref, v_ref, lam_ref, sg_ref, o_ref, *, seq, tq, lambda_init):
    j = pl.program_id(2)
    c = HEAD_DIM ** -0.5 * LOG2E
    lv = lam_ref[...]
    lam = (jnp.exp(jnp.sum(lv[0:1] * lv[1:2], axis=-1, keepdims=True))
           - jnp.exp(jnp.sum(lv[2:3] * lv[3:4], axis=-1, keepdims=True)) + lambda_init)
    q = q_ref[0]
    diag_bias = jnp.where(lax.broadcasted_iota(jnp.int32, (tq, tq), 1) <= lax.broadcasted_iota(jnp.int32, (tq, tq), 0),
                          0.0, NEG)

    def attend(cmap, past):
        lo, hi = cmap * HEAD_DIM, (cmap + 1) * HEAD_DIM
        qc = q[:, lo:hi]
        t_d = _dot_nt(qc, k_ref[0, past:past + tq, lo:hi]) * c + diag_bias
        m = jnp.max(t_d, axis=-1, keepdims=True)
        if past:
            t_p = _dot_nt(qc, k_ref[0, 0:past, lo:hi]) * c
            m = jnp.maximum(m, jnp.max(t_p, axis=-1, keepdims=True))
        e_d = jnp.exp2(t_d - m)
        l = jnp.sum(e_d, axis=-1, keepdims=True)
        o = _dot(e_d.astype(BF16), v_ref[0, past:past + tq, :])
        if past:
            e_p = jnp.exp2(t_p - m)
            l = l + jnp.sum(e_p, axis=-1, keepdims=True)
            o = o + _dot(e_p.astype(BF16), v_ref[0, 0:past, :])
        return o / l

    for v in range(seq // tq):
        @pl.when(j == v)
        def _(past=v * tq):
            o = attend(0, past) - lam * attend(1, past)
            o_ref[0] = (_rms(o, sg_ref[...]) * (1.0 - lambda_init)).astype(BF16)


def _diff_attention(proj, lam, subln, off, batch, seq, heads, lambda_init):
    vw = 2 * HEAD_DIM
    tq = _row_tile(seq, 256)
    qb, kb, vb = off["q_d"] // vw, off["k_d"] // vw, off["v_d"] // vw
    return pl.pallas_call(
        functools.partial(_diff_kernel, seq=seq, tq=tq, lambda_init=lambda_init),
        grid=(batch, heads, seq // tq),
        in_specs=[
            pl.BlockSpec((1, tq, vw), lambda b, h, j: (b, j, qb + h)),
            pl.BlockSpec((1, seq, vw), lambda b, h, j: (b, 0, kb + h)),
            pl.BlockSpec((1, seq, vw), lambda b, h, j: (b, 0, vb + h)),
            pl.BlockSpec((4, HEAD_DIM), lambda b, h, j: (0, 0)),
            pl.BlockSpec((1, vw), lambda b, h, j: (0, 0)),
        ],
        out_specs=pl.BlockSpec((1, tq, vw), lambda b, h, j: (b, j, h)),
        out_shape=jax.ShapeDtypeStruct((batch, seq, heads * vw), BF16),
        compiler_params=_params("parallel", "parallel", "arbitrary"),
        name="diff_attention",
    )(proj, proj, proj, lam, subln)


def _out_kernel(x_ref, a_ref, b_ref, wa_ref, wb_ref, o_ref):
    o_ref[...] = x_ref[...] + _dot(a_ref[...], wa_ref[...]) + _dot(b_ref[...], wb_ref[...])


def _out_project(x, a, b, wa, wb):
    n, d = x.shape
    ka, kb = a.shape[1], b.shape[1]
    tm = _row_tile(n, 512)
    tn = LANES * _largest_divisor(d // LANES, 16)
    return pl.pallas_call(
        _out_kernel,
        grid=(n // tm, d // tn),
        in_specs=[
            pl.BlockSpec((tm, tn), lambda i, j: (i, j)),
            pl.BlockSpec((tm, ka), lambda i, j: (i, 0)),
            pl.BlockSpec((tm, kb), lambda i, j: (i, 0)),
            pl.BlockSpec((ka, tn), lambda i, j: (0, j)),
            pl.BlockSpec((kb, tn), lambda i, j: (0, j)),
        ],
        out_specs=pl.BlockSpec((tm, tn), lambda i, j: (i, j)),
        out_shape=jax.ShapeDtypeStruct((n, d), F32),
        compiler_params=_params("parallel", "arbitrary"),
        name="out_proj",
    )(x, a, b, wa, wb)


def _rope_tables(t):
    inv = 1.0 / (ROPE_THETA ** (jnp.arange(0, HEAD_DIM, 2, dtype=F32) / HEAD_DIM))
    ang = jnp.arange(t, dtype=F32)[:, None] * inv[None, :]
    ang = jnp.concatenate([ang, ang], axis=-1)
    sign = jnp.concatenate([-jnp.ones((HEAD_DIM // 2,), F32), jnp.ones((HEAD_DIM // 2,), F32)])
    return jnp.cos(ang), jnp.sin(ang) * sign[None, :]


def _layout(d_model):
    nsa_heads = d_model // (2 * HEAD_DIM)
    diff_heads = d_model // (4 * HEAD_DIM)
    kv = NSA_KV_HEADS * HEAD_DIM
    sizes = dict(q_n=nsa_heads * HEAD_DIM, kc=kv, vc=kv, ks=kv, vs=kv, kw=kv, vw=kv, gates=3 * nsa_heads,
                 q_d=2 * diff_heads * HEAD_DIM, k_d=2 * diff_heads * HEAD_DIM, v_d=diff_heads * 2 * HEAD_DIM)
    orig, o = {}, 0
    for name in ("q_n", "kc", "vc", "ks", "vs", "kw", "vw", "gates", "q_d", "k_d", "v_d"):
        orig[name] = o
        o += sizes[name]
    rope_names = ("q_n", "kc", "ks", "kw", "q_d", "k_d")
    plain_names = ("vc", "vs", "vw", "v_d")
    new, o = {}, 0
    for name in rope_names + plain_names:
        new[name] = o
        o += sizes[name]
    n_rope_groups = sum(sizes[nm] for nm in rope_names) // LANES
    return sizes, orig, new, rope_names + plain_names, n_rope_groups, nsa_heads, diff_heads


def kernel(x, ffn1_norm, ffn1_w_gate, ffn1_w_up, ffn1_w_down, mix_norm, w_in, cmp_pos_k, cmp_pos_v, cmp_wk1, cmp_wk2, cmp_wv1, cmp_wv2, lam_q1, lam_k1, lam_q2, lam_k2, diff_subln, w_out, ffn2_norm, ffn2_w_gate, ffn2_w_up, ffn2_w_down, final_norm):
    batch, seq, d_model = x.shape
    depth = ffn1_norm.shape[0]
    n = batch * seq
    sizes, orig, off, order, n_rope_groups, nsa_heads, diff_heads = _layout(d_model)
    g = NSA_KV_HEADS
    rep = nsa_heads // g
    cos, sin_signed = _rope_tables(seq)
    fg = final_norm.reshape(1, d_model)
    n_chunks = seq // CMP_STRIDE

    def chunked(cols):
        c = cols.reshape(batch, n_chunks, CMP_STRIDE, g, HEAD_DIM).transpose(0, 3, 1, 2, 4)
        return c.reshape(batch, g, n_chunks, CMP_STRIDE * HEAD_DIM)

    ffn1_stacked = (ffn1_w_gate, ffn1_w_up, ffn1_w_down)
    ffn2_stacked = (ffn2_w_gate, ffn2_w_up, ffn2_w_down)
    w_ffn = tuple(w[0].astype(BF16) for w in ffn1_stacked)

    xf = x.reshape(n, d_model)
    for l in range(depth):
        lambda_init = 0.8 - 0.6 * math.exp(-0.3 * l)
        xf, w_ffn = _ffn(xf, ffn1_norm[l].reshape(1, d_model), *w_ffn, fg, False, (ffn2_stacked, l))

        wl = w_in[l]
        w_main = jnp.concatenate([wl[:, orig[nm]:orig[nm] + sizes[nm]] for nm in order], axis=1).astype(BF16)
        gate_tiles = []
        for h in range(g):
            cols = wl[:, orig["gates"] + h * 3 * rep: orig["gates"] + (h + 1) * 3 * rep]
            gate_tiles.append(jnp.pad(cols, ((0, 0), (0, LANES - 3 * rep))))
        w_gate = jnp.concatenate(gate_tiles, axis=1).astype(BF16)
        proj, gates = _project(xf, mix_norm[l].reshape(1, d_model), w_main, w_gate, cos, sin_signed, n_rope_groups)
        proj3 = proj.reshape(batch, seq, -1)
        gates3 = gates.reshape(batch, seq, -1)

        kv = sizes["kc"]
        ck = chunked(proj3[:, :, off["kc"]:off["kc"] + kv])
        cv = chunked(proj3[:, :, off["vc"]:off["vc"] + kv])
        kc, vc = _compress(ck, cv,
                           cmp_pos_k[l].reshape(1, -1).astype(BF16), cmp_pos_v[l].reshape(1, -1).astype(BF16),
                           cmp_wk1[l].astype(BF16), cmp_wk2[l].astype(BF16),
                           cmp_wv1[l].astype(BF16), cmp_wv2[l].astype(BF16))
        o_nsa = _nsa_attention(proj3, gates3, kc, vc, off, batch, seq, rep)
        lam = jnp.stack([lam_q1[l], lam_k1[l], lam_q2[l], lam_k2[l]])
        o_diff = _diff_attention(proj3, lam, diff_subln[l].reshape(1, -1), off, batch, seq, diff_heads, lambda_init)

        half = sizes["q_n"]
        wo = w_out[l].astype(BF16)
        xf = _out_project(xf, o_nsa.reshape(n, -1), o_diff.reshape(n, -1), wo[:half], wo[half:])

        last = l == depth - 1
        xf, w_ffn = _ffn(xf, ffn2_norm[l].reshape(1, d_model), *w_ffn, fg, last,
                         None if last else (ffn1_stacked, l + 1))
    return xf.reshape(batch, seq, d_model)
```

```python
import functools
import math

import numpy as np
import jax
import jax.numpy as jnp
from jax import lax
from jax.experimental import pallas as pl
from jax.experimental.pallas import tpu as pltpu

HEAD_DIM = 128
NSA_KV_HEADS = 2
CMP_LEN = 32
CMP_STRIDE = 16
SLC_LEN = 64
SLC_TOPK = 16
WINDOW = 512
Q_BLOCK = 128
ROPE_THETA = 10000.0
EPS = 1e-6
NEG = -1e30
MASK_FLOOR = -1e20
LOG2E = 1.4426950408889634
QK_SCALE = HEAD_DIM ** -0.5 * LOG2E
SLC_VARIANT_LEN = 512
LANES = 128
VMEM_LIMIT = 58 * 1024 * 1024

F32 = jnp.float32
BF16 = jnp.bfloat16


def _largest_divisor(n, cap):
    for d in range(min(n, cap), 0, -1):
        if n % d == 0:
            return d
    return 1


def _row_tile(n, cap):
    for d in range(min(n, cap), 7, -1):
        if n % d == 0 and d % 8 == 0:
            return d
    return n


def _rms(x, g):
    return x * lax.rsqrt(jnp.mean(x * x, axis=-1, keepdims=True) + EPS) * g


def _dot(a, b):
    return jnp.dot(a, b, preferred_element_type=F32)


def _dot_nt(a, b):
    return lax.dot_general(a, b, (((1,), (1,)), ((), ())), preferred_element_type=F32)


def _params(*sem):
    return pltpu.CompilerParams(dimension_semantics=sem, vmem_limit_bytes=VMEM_LIMIT)


def _ffn_kernel(*refs, final_norm, cast_next):
    if cast_next:
        (x_ref, g_ref, wg_ref, wu_ref, wd_ref, fg_ref, ng_ref, nu_ref, nd_ref,
         o_ref, og_ref, ou_ref, od_ref, h_ref) = refs
        og_ref[...] = ng_ref[...].astype(BF16)
        ou_ref[...] = nu_ref[...].astype(BF16)
        od_ref[...] = nd_ref[...].astype(BF16)
    else:
        x_ref, g_ref, wg_ref, wu_ref, wd_ref, fg_ref, o_ref, h_ref = refs
    j = pl.program_id(1)

    @pl.when(j == 0)
    def _():
        x = x_ref[...]
        h_ref[...] = _rms(x, g_ref[...]).astype(BF16)
        o_ref[...] = x

    h = h_ref[...]
    a = _dot(h, wg_ref[...])
    b = _dot(h, wu_ref[...])
    act = (a * jax.nn.sigmoid(a) * b * 0.5).astype(BF16)
    o_ref[...] += _dot(act, wd_ref[...])

    if final_norm:
        @pl.when(j == pl.num_programs(1) - 1)
        def _():
            o_ref[...] = _rms(o_ref[...], fg_ref[...])


def _ffn(x, g, wg, wu, wd, fg, final_norm, next_weights=None):
    n, d = x.shape
    f = wg.shape[1]
    tm = _row_tile(n, 1024)
    tf = LANES * _largest_divisor(f // LANES, 4)
    n_i, n_j = n // tm, f // tf
    in_specs = [
        pl.BlockSpec((tm, d), lambda i, j: (i, 0)),
        pl.BlockSpec((1, d), lambda i, j: (0, 0)),
        pl.BlockSpec((d, tf), lambda i, j: (0, j)),
        pl.BlockSpec((d, tf), lambda i, j: (0, j)),
        pl.BlockSpec((tf, d), lambda i, j: (j, 0)),
        pl.BlockSpec((1, d), lambda i, j: (0, 0)),
    ]
    out_specs = [pl.BlockSpec((tm, d), lambda i, j: (i, 0))]
    out_shape = [jax.ShapeDtypeStruct((n, d), F32)]
    args = [x, g, wg, wu, wd, fg]
    if next_weights is not None:
        stacked, layer = next_weights
        dr, fr = d // n_i, tf // n_i
        assert dr * n_i == d and fr * n_i == tf and dr % 16 == 0 and fr % 16 == 0
        assert all(w.shape[1:] == s for w, s in zip(stacked, ((d, f), (d, f), (f, d))))
        in_specs += [
            pl.BlockSpec((None, dr, tf), lambda i, j: (layer, i, j)),
            pl.BlockSpec((None, dr, tf), lambda i, j: (layer, i, j)),
            pl.BlockSpec((None, fr, d), lambda i, j: (layer, j * n_i + i, 0)),
        ]
        out_specs += [
            pl.BlockSpec((dr, tf), lambda i, j: (i, j)),
            pl.BlockSpec((dr, tf), lambda i, j: (i, j)),
            pl.BlockSpec((fr, d), lambda i, j: (j * n_i + i, 0)),
        ]
        out_shape += [jax.ShapeDtypeStruct(w.shape[1:], BF16) for w in stacked]
        args += list(stacked)
    outs = pl.pallas_call(
        functools.partial(_ffn_kernel, final_norm=final_norm, cast_next=next_weights is not None),
        grid=(n_i, n_j),
        in_specs=in_specs,
        out_specs=out_specs,
        out_shape=out_shape,
        scratch_shapes=[pltpu.VMEM((tm, d), BF16)],
        compiler_params=_params("parallel", "arbitrary"),
        name="ffn",
    )(*args)
    return outs[0], tuple(outs[1:])


def _proj_kernel(x_ref, g_ref, w_ref, wgate_ref, cos_ref, sin_ref, o_ref, gate_ref, h_ref,
                 *, groups_per_tile, n_rope_groups, query_groups):
    j = pl.program_id(1)

    @pl.when(j == 0)
    def _():
        h = _rms(x_ref[...], g_ref[...]).astype(BF16)
        h_ref[...] = h
        gate_ref[...] = _dot(h, wgate_ref[...])

    acc = _dot(h_ref[...], w_ref[...])
    cos = cos_ref[...]
    sin = sin_ref[...]
    for gi in range(groups_per_tile):
        group = j * groups_per_tile + gi
        is_rope = group < n_rope_groups
        is_query = functools.reduce(jnp.logical_or, [(group >= lo) & (group < hi) for lo, hi in query_groups])
        c = jnp.where(is_rope, cos, 1.0)
        s = jnp.where(is_rope, sin, 0.0)
        xg = acc[:, gi * LANES:(gi + 1) * LANES]
        y = xg * c + pltpu.roll(xg, HEAD_DIM // 2, axis=1) * s
        y = y * jnp.where(is_query, QK_SCALE, 1.0)
        o_ref[:, gi * LANES:(gi + 1) * LANES] = y.astype(BF16)


def _project(x, g, w_main, w_gate, cos, sin_signed, n_rope_groups, query_groups):
    n, d = x.shape
    t = cos.shape[0]
    width = w_main.shape[1]
    gw = w_gate.shape[1]
    n_groups = width // LANES
    gpt = _largest_divisor(n_groups, 11)
    tn = gpt * LANES
    tm = _row_tile(t, 1024)
    t_blocks = t // tm
    return pl.pallas_call(
        functools.partial(_proj_kernel, groups_per_tile=gpt, n_rope_groups=n_rope_groups,
                          query_groups=query_groups),
        grid=(n // tm, width // tn),
        in_specs=[
            pl.BlockSpec((tm, d), lambda i, j: (i, 0)),
            pl.BlockSpec((1, d), lambda i, j: (0, 0)),
            pl.BlockSpec((d, tn), lambda i, j: (0, j)),
            pl.BlockSpec((d, gw), lambda i, j: (0, 0)),
            pl.BlockSpec((tm, HEAD_DIM), lambda i, j: (i % t_blocks, 0)),
            pl.BlockSpec((tm, HEAD_DIM), lambda i, j: (i % t_blocks, 0)),
        ],
        out_specs=[
            pl.BlockSpec((tm, tn), lambda i, j: (i, j)),
            pl.BlockSpec((tm, gw), lambda i, j: (i, 0)),
        ],
        out_shape=[
            jax.ShapeDtypeStruct((n, width), BF16),
            jax.ShapeDtypeStruct((n, gw), F32),
        ],
        scratch_shapes=[pltpu.VMEM((tm, d), BF16)],
        compiler_params=_params("parallel", "arbitrary"),
        name="in_proj",
    )(x, g, w_main, w_gate, cos, sin_signed)


def _cmp_kernel(ck_ref, cv_ref, pk_ref, pv_ref, wk1_ref, wk2_ref, wv1_ref, wv2_ref, ok_ref, ov_ref):
    def one(c_ref, p_ref, w1_ref, w2_ref, o_ref):
        c = c_ref[0, 0]
        half = c.shape[1]
        n_chunks = c.shape[0]
        a = _dot(c, w1_ref[0:half, :])
        b = _dot(c, w1_ref[half:2 * half, :])
        p = jnp.broadcast_to(p_ref[...], (8, 2 * half))
        bias = _dot(p, w1_ref[...])[0:1]
        pre = a + pltpu.roll(b, n_chunks - 1, axis=0) + bias
        hid = (pre * jax.nn.sigmoid(pre)).astype(BF16)
        o_ref[0, 0] = _dot(hid, w2_ref[...]).astype(BF16)

    one(ck_ref, pk_ref, wk1_ref, wk2_ref, ok_ref)
    one(cv_ref, pv_ref, wv1_ref, wv2_ref, ov_ref)


def _compress(ck, cv, pk, pv, wk1, wk2, wv1, wv2):
    b, g, nc, cw = ck.shape
    hid = wk1.shape[1]
    dk = wk2.shape[1]
    chunk_spec = pl.BlockSpec((1, 1, nc, cw), lambda i, j: (i, j, 0, 0))
    full = lambda shape: pl.BlockSpec(shape, lambda i, j: tuple(0 for _ in shape))
    out_spec = pl.BlockSpec((1, 1, nc, dk), lambda i, j: (i, j, 0, 0))
    return pl.pallas_call(
        _cmp_kernel,
        grid=(b, g),
        in_specs=[chunk_spec, chunk_spec, full((1, 2 * cw)), full((1, 2 * cw)),
                  full((2 * cw, hid)), full((hid, dk)), full((2 * cw, hid)), full((hid, dk))],
        out_specs=[out_spec, out_spec],
        out_shape=[jax.ShapeDtypeStruct((b, g, nc, dk), BF16)] * 2,
        compiler_params=_params("parallel", "parallel"),
        name="nsa_compress",
    )(ck, cv, pk, pv, wk1, wk2, wv1, wv2)


def _transpose_bf16(a):
    return a.astype(F32).T.astype(BF16)


def _softmax_piece(st, vt, elem_bias=None, blk_bias=None):
    nk, nc = st.shape
    if elem_bias is not None:
        st = st + elem_bias
    nb = 1 if blk_bias is None else blk_bias.shape[0]
    s4 = st.reshape(nb, nk // nb // 8, 8, nc)
    bm = jnp.max(s4, axis=1)
    if blk_bias is not None:
        bm = bm + blk_bias[:, None, :]
    m = jnp.max(jnp.max(bm, axis=0), axis=0, keepdims=True)
    m = jnp.maximum(m, MASK_FLOOR)
    shift = -m if blk_bias is None else blk_bias - m
    e = jnp.exp2(s4 + shift[:, None, None, :])
    l = jnp.sum(jnp.sum(jnp.sum(e, axis=0), axis=0), axis=0, keepdims=True)
    return m, l, _dot(vt, e.reshape(nk, nc).astype(BF16))


def _merge_pieces(pieces):
    m = functools.reduce(jnp.maximum, [p[0] for p in pieces])
    ws = [jnp.exp2(p[0] - m) for p in pieces]
    l = sum(w * p[1] for w, p in zip(ws, pieces))
    o = sum(w * p[2] for w, p in zip(ws, pieces))
    return o / l


def _nsa_kernel(q_ref, ks_ref, vs_ref, kw_ref, vw_ref, kc_ref, vc_ref, gate_ref, ovl_ref,
                o_ref, vst_ref, vwt_ref, *, seq, rep, n_slc, n_sel, var_len):
    j = pl.program_id(2)
    s0 = j * Q_BLOCK

    @pl.when(j == 0)
    def _():
        for i in range(seq // Q_BLOCK):
            vst_ref[i] = _transpose_bf16(vs_ref[0, i * Q_BLOCK:(i + 1) * Q_BLOCK, :])
            vwt_ref[i] = _transpose_bf16(vw_ref[0, i * Q_BLOCK:(i + 1) * Q_BLOCK, :])

    q = q_ref[0]
    q4 = jnp.concatenate([q[:, r * HEAD_DIM:(r + 1) * HEAD_DIM] for r in range(rep)], axis=0)
    t_row = s0 + lax.broadcasted_iota(jnp.int32, (1, Q_BLOCK), 1)
    head = lambda a, r: a[:, r * Q_BLOCK:(r + 1) * Q_BLOCK]

    kc = kc_ref[0, 0]
    n_rows = kc.shape[0]
    n_idx = lax.broadcasted_iota(jnp.int32, (n_rows, Q_BLOCK), 0)
    c_mask = (n_idx * CMP_STRIDE + (CMP_LEN - 1)) <= t_row
    st = _dot_nt(kc, q4)
    p_sum = jnp.zeros((n_rows, Q_BLOCK), F32)
    ps = []
    for r in range(rep):
        t = jnp.where(c_mask, head(st, r), NEG)
        e = jnp.where(c_mask, jnp.exp2(t - jnp.max(t, axis=0, keepdims=True)), 0.0)
        l = jnp.sum(e, axis=0, keepdims=True)
        p = e / jnp.where(l > 0.0, l, 1.0)
        p_sum = p_sum + p
        ps.append(p.astype(BF16))
    o_cmp = _dot(_transpose_bf16(vc_ref[0, 0]), jnp.concatenate(ps, axis=1))

    hi = p_sum.astype(BF16)
    lo = (p_sum - hi.astype(F32)).astype(BF16)
    ovl = ovl_ref[...]
    n_pad = -(-n_slc // 8) * 8
    imp = (_dot(ovl, hi) + _dot(ovl, lo))[0:n_pad]
    blk = lax.broadcasted_iota(jnp.int32, (n_pad, Q_BLOCK), 0)
    cur = t_row // SLC_LEN
    forced = (blk == 0) | (blk == cur) | (blk == cur - 1)
    blk_causal = blk * SLC_LEN <= t_row
    imp = jnp.where(forced, 1e4, imp)
    imp = jnp.where(blk_causal, imp, -1.0)
    rank = jnp.zeros((n_pad, Q_BLOCK), F32)
    for sp in range(n_slc):
        row = imp[sp:sp + 1, :]
        ge = jnp.where(row >= imp, 1.0, 0.0)
        gt = jnp.where(row > imp, 1.0, 0.0)
        rank = rank + jnp.where(blk > sp, ge, gt)
    blocks_per_q = Q_BLOCK // SLC_LEN
    blk_bias = jnp.where((rank < float(n_sel)) & (blk < j * blocks_per_q), 0.0, NEG)
    blk_bias = jnp.concatenate([blk_bias] * rep, axis=1)

    nc = rep * Q_BLOCK
    kk = lax.broadcasted_iota(jnp.int32, (Q_BLOCK, Q_BLOCK), 0)
    qq = lax.broadcasted_iota(jnp.int32, (Q_BLOCK, Q_BLOCK), 1)
    tri = jnp.concatenate([jnp.where(kk <= qq, 0.0, NEG)] * rep, axis=1)
    anti = jnp.concatenate([jnp.where(kk > qq, 0.0, NEG)] * rep, axis=1)
    valid_row = lambda ok: jnp.where(ok, jnp.zeros((1, nc), F32), jnp.full((1, nc), NEG, F32))
    n_back = WINDOW // Q_BLOCK
    far = jnp.maximum(j - n_back, 0)
    mid = jnp.maximum(j - (n_back - 1), 0)
    mid_bias = jnp.concatenate([valid_row(mid + i < j) for i in range(n_back - 1)], axis=0)
    rows_of = lambda ref, blk0, n: ref[0, pl.ds(pl.multiple_of(blk0 * Q_BLOCK, Q_BLOCK), n * Q_BLOCK), :]
    tiles_of = lambda ref, blk0, n: jnp.concatenate([ref[blk0 + i] for i in range(n)], axis=1)
    q_per_var = var_len // Q_BLOCK
    blk_per_var = var_len // SLC_LEN

    def branches(n_chunks):
        specs = [
            (lambda: _dot_nt(rows_of(kw_ref, j, 1), q4), lambda: vwt_ref[j], tri, None),
            (lambda: _dot_nt(rows_of(kw_ref, far, 1), q4), lambda: vwt_ref[far], anti, valid_row(j >= n_back)),
            (lambda: _dot_nt(rows_of(kw_ref, mid, n_back - 1), q4), lambda: tiles_of(vwt_ref, mid, n_back - 1),
             None, mid_bias),
            (lambda: _dot_nt(rows_of(ks_ref, j, 1), q4), lambda: vst_ref[j], tri, None),
        ]
        for ci in range(n_chunks):
            specs.append((lambda ci=ci: _dot_nt(ks_ref[0, ci * var_len:(ci + 1) * var_len, :], q4),
                          lambda ci=ci: tiles_of(vst_ref, ci * q_per_var, q_per_var),
                          None, blk_bias[ci * blk_per_var:(ci + 1) * blk_per_var]))
        pieces = []
        st_next = specs[0][0]()
        for i, (_, vt, elem_bias, piece_bias) in enumerate(specs):
            st = st_next
            if i + 1 < len(specs):
                st_next = specs[i + 1][0]()
            pieces.append(_softmax_piece(st, vt(), elem_bias, piece_bias))
        return _merge_pieces(pieces[3:]), _merge_pieces(pieces[:3])

    gt = jax.nn.sigmoid(gate_ref[0]).T
    for v in range(seq // var_len):
        @pl.when(j // q_per_var == v)
        def _(n_chunks=v + 1):
            o_slc, o_win = branches(n_chunks)
            for r in range(rep):
                out = (gt[3 * r:3 * r + 1] * head(o_cmp, r) + gt[3 * r + 1:3 * r + 2] * head(o_slc, r)
                       + gt[3 * r + 2:3 * r + 3] * head(o_win, r))
                o_ref[0, :, r * HEAD_DIM:(r + 1) * HEAD_DIM] = out.T.astype(BF16)


def _block_overlap(n_rows, n_cmp, n_slc):
    c0 = np.arange(n_cmp) * CMP_STRIDE
    s0 = np.arange(n_slc) * SLC_LEN
    lo = np.maximum(c0[None, :], s0[:, None])
    hi = np.minimum(c0[None, :] + CMP_LEN, s0[:, None] + SLC_LEN)
    out = np.zeros((LANES, n_rows), np.float32)
    out[:n_slc, :n_cmp] = np.clip(hi - lo, 0, None) / CMP_LEN
    return out


def _nsa_attention(proj, gates, kc, vc, off, batch, seq, rep):
    g = NSA_KV_HEADS
    n_qb = seq // Q_BLOCK
    n_slc = seq // SLC_LEN
    n_sel = min(SLC_TOPK, n_slc)
    n_rows = kc.shape[2]
    n_cmp = (seq - CMP_LEN) // CMP_STRIDE + 1
    qw = rep * HEAD_DIM
    var_len = SLC_VARIANT_LEN if seq % SLC_VARIANT_LEN == 0 else seq
    ovl = jnp.asarray(_block_overlap(n_rows, n_cmp, n_slc), BF16)

    def kv_spec(name):
        base = off[name] // HEAD_DIM
        return pl.BlockSpec((1, seq, HEAD_DIM), lambda b, h, j: (b, 0, base + h))

    cmp_spec = pl.BlockSpec((1, 1, n_rows, HEAD_DIM), lambda b, h, j: (b, h, 0, 0))
    q_base = off["q_n"] // qw
    return pl.pallas_call(
        functools.partial(_nsa_kernel, seq=seq, rep=rep, n_slc=n_slc, n_sel=n_sel, var_len=var_len),
        grid=(batch, g, n_qb),
        scratch_shapes=[
            pltpu.VMEM((seq // Q_BLOCK, HEAD_DIM, Q_BLOCK), BF16),
            pltpu.VMEM((seq // Q_BLOCK, HEAD_DIM, Q_BLOCK), BF16),
        ],
        in_specs=[
            pl.BlockSpec((1, Q_BLOCK, qw), lambda b, h, j: (b, j, q_base + h)),
            kv_spec("ks"), kv_spec("vs"), kv_spec("kw"), kv_spec("vw"),
            cmp_spec, cmp_spec,
            pl.BlockSpec((1, Q_BLOCK, LANES), lambda b, h, j: (b, j, h)),
            pl.BlockSpec((LANES, n_rows), lambda b, h, j: (0, 0)),
        ],
        out_specs=pl.BlockSpec((1, Q_BLOCK, qw), lambda b, h, j: (b, j, h)),
        out_shape=jax.ShapeDtypeStruct((batch, seq, g * qw), BF16),
        compiler_params=_params("parallel", "parallel", "arbitrary"),
        name="nsa_attention",
    )(proj, proj, proj, proj, proj, kc, vc, gates, ovl)


def _diff_kernel(q_ref, k_ref, v_ref, lam_ref, sg_ref, o_ref, vt_ref, *, seq, tq, chunk, lambda_init):
    j = pl.program_id(2)

    @pl.when(j == 0)
    def _():
        for i in range(seq // tq):
            vt_ref[i] = _transpose_bf16(v_ref[0, i * tq:(i + 1) * tq, :])

    lv = lam_ref[...]
    lam = (jnp.exp(jnp.sum(lv[0:1] * lv[1:2], axis=-1, keepdims=True))
           - jnp.exp(jnp.sum(lv[2:3] * lv[3:4], axis=-1, keepdims=True)) + lambda_init)
    q = q_ref[0]
    qs = [q[:, c * HEAD_DIM:(c + 1) * HEAD_DIM] for c in range(2)]
    tri = jnp.where(lax.broadcasted_iota(jnp.int32, (tq, tq), 0) <= lax.broadcasted_iota(jnp.int32, (tq, tq), 1),
                    0.0, NEG)

    def attend(v):
        spans = [(v * tq, tq, tri)]
        c0 = 0
        while c0 < v * tq:
            n = min(chunk, v * tq - c0)
            spans.append((c0, n, None))
            c0 += n
        specs = [(c, s) for s in spans for c in range(2)]
        scores = lambda c, s: _dot_nt(k_ref[0, s[0]:s[0] + s[1], c * HEAD_DIM:(c + 1) * HEAD_DIM], qs[c])
        pieces = ([], [])
        st_next = scores(*specs[0])
        for i, (c, s) in enumerate(specs):
            st = st_next
            if i + 1 < len(specs):
                st_next = scores(*specs[i + 1])
            vt = jnp.concatenate([vt_ref[s[0] // tq + t] for t in range(s[1] // tq)], axis=1)
            pieces[c].append(_softmax_piece(st, vt, s[2]))
        return _merge_pieces(pieces[0]) - lam * _merge_pieces(pieces[1])

    for v in range(seq // tq):
        @pl.when(j == v)
        def _(v=v):
            o = attend(v).T
            o_ref[0] = (_rms(o, sg_ref[...]) * (1.0 - lambda_init)).astype(BF16)


def _diff_attention(proj, lam, subln, off, batch, seq, heads, lambda_init):
    vw = 2 * HEAD_DIM
    tq = _row_tile(seq, 256)
    qb, kb, vb = off["q_d"] // vw, off["k_d"] // vw, off["v_d"] // vw
    chunk = 2 * tq
    return pl.pallas_call(
        functools.partial(_diff_kernel, seq=seq, tq=tq, chunk=chunk, lambda_init=lambda_init),
        grid=(batch, heads, seq // tq),
        scratch_shapes=[pltpu.VMEM((seq // tq, vw, tq), BF16)],
        in_specs=[
            pl.BlockSpec((1, tq, vw), lambda b, h, j: (b, j, qb + h)),
            pl.BlockSpec((1, seq, vw), lambda b, h, j: (b, 0, kb + h)),
            pl.BlockSpec((1, seq, vw), lambda b, h, j: (b, 0, vb + h)),
            pl.BlockSpec((4, HEAD_DIM), lambda b, h, j: (0, 0)),
            pl.BlockSpec((1, vw), lambda b, h, j: (0, 0)),
        ],
        out_specs=pl.BlockSpec((1, tq, vw), lambda b, h, j: (b, j, h)),
        out_shape=jax.ShapeDtypeStruct((batch, seq, heads * vw), BF16),
        compiler_params=_params("parallel", "parallel", "arbitrary"),
        name="diff_attention",
    )(proj, proj, proj, lam, subln)


def _out_kernel(x_ref, a_ref, b_ref, wa_ref, wb_ref, o_ref):
    o_ref[...] = x_ref[...] + _dot(a_ref[...], wa_ref[...]) + _dot(b_ref[...], wb_ref[...])


def _out_project(x, a, b, wa, wb):
    n, d = x.shape
    ka, kb = a.shape[1], b.shape[1]
    tm = _row_tile(n, 512)
    tn = LANES * _largest_divisor(d // LANES, 16)
    return pl.pallas_call(
        _out_kernel,
        grid=(n // tm, d // tn),
        in_specs=[
            pl.BlockSpec((tm, tn), lambda i, j: (i, j)),
            pl.BlockSpec((tm, ka), lambda i, j: (i, 0)),
            pl.BlockSpec((tm, kb), lambda i, j: (i, 0)),
            pl.BlockSpec((ka, tn), lambda i, j: (0, j)),
            pl.BlockSpec((kb, tn), lambda i, j: (0, j)),
        ],
        out_specs=pl.BlockSpec((tm, tn), lambda i, j: (i, j)),
        out_shape=jax.ShapeDtypeStruct((n, d), F32),
        compiler_params=_params("parallel", "arbitrary"),
        name="out_proj",
    )(x, a, b, wa, wb)


def _rope_tables(t):
    inv = 1.0 / (ROPE_THETA ** (jnp.arange(0, HEAD_DIM, 2, dtype=F32) / HEAD_DIM))
    ang = jnp.arange(t, dtype=F32)[:, None] * inv[None, :]
    ang = jnp.concatenate([ang, ang], axis=-1)
    sign = jnp.concatenate([-jnp.ones((HEAD_DIM // 2,), F32), jnp.ones((HEAD_DIM // 2,), F32)])
    return jnp.cos(ang), jnp.sin(ang) * sign[None, :]


def _layout(d_model):
    nsa_heads = d_model // (2 * HEAD_DIM)
    diff_heads = d_model // (4 * HEAD_DIM)
    kv = NSA_KV_HEADS * HEAD_DIM
    sizes = dict(q_n=nsa_heads * HEAD_DIM, kc=kv, vc=kv, ks=kv, vs=kv, kw=kv, vw=kv, gates=3 * nsa_heads,
                 q_d=2 * diff_heads * HEAD_DIM, k_d=2 * diff_heads * HEAD_DIM, v_d=diff_heads * 2 * HEAD_DIM)
    orig, o = {}, 0
    for name in ("q_n", "kc", "vc", "ks", "vs", "kw", "vw", "gates", "q_d", "k_d", "v_d"):
        orig[name] = o
        o += sizes[name]
    rope_names = ("q_n", "kc", "ks", "kw", "q_d", "k_d")
    plain_names = ("vc", "vs", "vw", "v_d")
    new, o = {}, 0
    for name in rope_names + plain_names:
        new[name] = o
        o += sizes[name]
    n_rope_groups = sum(sizes[nm] for nm in rope_names) // LANES
    return sizes, orig, new, rope_names + plain_names, n_rope_groups, nsa_heads, diff_heads


def kernel(x, ffn1_norm, ffn1_w_gate, ffn1_w_up, ffn1_w_down, mix_norm, w_in, cmp_pos_k, cmp_pos_v, cmp_wk1, cmp_wk2, cmp_wv1, cmp_wv2, lam_q1, lam_k1, lam_q2, lam_k2, diff_subln, w_out, ffn2_norm, ffn2_w_gate, ffn2_w_up, ffn2_w_down, final_norm):
    batch, seq, d_model = x.shape
    depth = ffn1_norm.shape[0]
    n = batch * seq
    sizes, orig, off, order, n_rope_groups, nsa_heads, diff_heads = _layout(d_model)
    g = NSA_KV_HEADS
    rep = nsa_heads // g
    cos, sin_signed = _rope_tables(seq)
    fg = final_norm.reshape(1, d_model)
    n_chunks = seq // CMP_STRIDE

    def chunked(cols):
        c = cols.reshape(batch, n_chunks, CMP_STRIDE, g, HEAD_DIM).transpose(0, 3, 1, 2, 4)
        return c.reshape(batch, g, n_chunks, CMP_STRIDE * HEAD_DIM)

    ffn1_stacked = (ffn1_w_gate, ffn1_w_up, ffn1_w_down)
    ffn2_stacked = (ffn2_w_gate, ffn2_w_up, ffn2_w_down)
    w_ffn = tuple(w[0].astype(BF16) for w in ffn1_stacked)

    xf = x.reshape(n, d_model)
    for l in range(depth):
        lambda_init = 0.8 - 0.6 * math.exp(-0.3 * l)
        xf, w_ffn = _ffn(xf, ffn1_norm[l].reshape(1, d_model), *w_ffn, fg, False, (ffn2_stacked, l))

        wl = w_in[l]
        w_main = jnp.concatenate([wl[:, orig[nm]:orig[nm] + sizes[nm]] for nm in order], axis=1).astype(BF16)
        gate_tiles = []
        for h in range(g):
            cols = wl[:, orig["gates"] + h * 3 * rep: orig["gates"] + (h + 1) * 3 * rep]
            gate_tiles.append(jnp.pad(cols, ((0, 0), (0, LANES - 3 * rep))))
        w_gate = jnp.concatenate(gate_tiles, axis=1).astype(BF16)
        query_groups = tuple((off[nm] // LANES, (off[nm] + sizes[nm]) // LANES) for nm in ("q_n", "q_d"))
        proj, gates = _project(xf, mix_norm[l].reshape(1, d_model), w_main, w_gate, cos, sin_signed, n_rope_groups,
                               query_groups)
        proj3 = proj.reshape(batch, seq, -1)
        gates3 = gates.reshape(batch, seq, -1)

        kv = sizes["kc"]
        ck = chunked(proj3[:, :, off["kc"]:off["kc"] + kv])
        cv = chunked(proj3[:, :, off["vc"]:off["vc"] + kv])
        kc, vc = _compress(ck, cv,
                           cmp_pos_k[l].reshape(1, -1).astype(BF16), cmp_pos_v[l].reshape(1, -1).astype(BF16),
                           cmp_wk1[l].astype(BF16), cmp_wk2[l].astype(BF16),
                           cmp_wv1[l].astype(BF16), cmp_wv2[l].astype(BF16))
        o_nsa = _nsa_attention(proj3, gates3, kc, vc, off, batch, seq, rep)
        lam = jnp.stack([lam_q1[l], lam_k1[l], lam_q2[l], lam_k2[l]])
        o_diff = _diff_attention(proj3, lam, diff_subln[l].reshape(1, -1), off, batch, seq, diff_heads, lambda_init)

        half = sizes["q_n"]
        wo = w_out[l].astype(BF16)
        xf = _out_project(xf, o_nsa.reshape(n, -1), o_diff.reshape(n, -1), wo[:half], wo[half:])

        last = l == depth - 1
        xf, w_ffn = _ffn(xf, ffn2_norm[l].reshape(1, d_model), *w_ffn, fg, last,
                         None if last else (ffn1_stacked, l + 1))
    return xf.reshape(batch, seq, d_model)
```

```python
import functools
import math

import numpy as np
import jax
import jax.numpy as jnp
from jax import lax
from jax.experimental import pallas as pl
from jax.experimental.pallas import tpu as pltpu

HEAD_DIM = 128
NSA_KV_HEADS = 2
CMP_LEN = 32
CMP_STRIDE = 16
SLC_LEN = 64
SLC_TOPK = 16
WINDOW = 512
Q_BLOCK = 128
ROPE_THETA = 10000.0
EPS = 1e-6
NEG = -1e30
MASK_FLOOR = -1e20
LOG2E = 1.4426950408889634
QK_SCALE = HEAD_DIM ** -0.5 * LOG2E
SLC_VARIANT_LEN = 512
LANES = 128
VMEM_LIMIT = 58 * 1024 * 1024

F32 = jnp.float32
BF16 = jnp.bfloat16


def _largest_divisor(n, cap):
    for d in range(min(n, cap), 0, -1):
        if n % d == 0:
            return d
    return 1


def _row_tile(n, cap):
    for d in range(min(n, cap), 7, -1):
        if n % d == 0 and d % 8 == 0:
            return d
    return n


def _rms(x, g):
    return x * lax.rsqrt(jnp.mean(x * x, axis=-1, keepdims=True) + EPS) * g


def _dot(a, b):
    return jnp.dot(a, b, preferred_element_type=F32)


def _dot_nt(a, b):
    return lax.dot_general(a, b, (((1,), (1,)), ((), ())), preferred_element_type=F32)


def _params(*sem):
    return pltpu.CompilerParams(dimension_semantics=sem, vmem_limit_bytes=VMEM_LIMIT)


def _ffn_kernel(*refs, final_norm, cast_next):
    if cast_next:
        (x_ref, g_ref, wg_ref, wu_ref, wd_ref, fg_ref, ng_ref, nu_ref, nd_ref,
         o_ref, og_ref, ou_ref, od_ref, h_ref) = refs
        og_ref[...] = ng_ref[...].astype(BF16)
        ou_ref[...] = nu_ref[...].astype(BF16)
        od_ref[...] = nd_ref[...].astype(BF16)
    else:
        x_ref, g_ref, wg_ref, wu_ref, wd_ref, fg_ref, o_ref, h_ref = refs
    j = pl.program_id(1)

    @pl.when(j == 0)
    def _():
        x = x_ref[...]
        h_ref[...] = _rms(x, g_ref[...]).astype(BF16)
        o_ref[...] = x

    h = h_ref[...]
    a = _dot(h, wg_ref[...])
    b = _dot(h, wu_ref[...])
    act = (a * jax.nn.sigmoid(a) * b * 0.5).astype(BF16)
    o_ref[...] += _dot(act, wd_ref[...])

    if final_norm:
        @pl.when(j == pl.num_programs(1) - 1)
        def _():
            o_ref[...] = _rms(o_ref[...], fg_ref[...])


def _ffn(x, g, wg, wu, wd, fg, final_norm, next_weights=None):
    n, d = x.shape
    f = wg.shape[1]
    tm = _row_tile(n, 1024)
    tf = LANES * _largest_divisor(f // LANES, 4)
    n_i, n_j = n // tm, f // tf
    in_specs = [
        pl.BlockSpec((tm, d), lambda i, j: (i, 0)),
        pl.BlockSpec((1, d), lambda i, j: (0, 0)),
        pl.BlockSpec((d, tf), lambda i, j: (0, j)),
        pl.BlockSpec((d, tf), lambda i, j: (0, j)),
        pl.BlockSpec((tf, d), lambda i, j: (j, 0)),
        pl.BlockSpec((1, d), lambda i, j: (0, 0)),
    ]
    out_specs = [pl.BlockSpec((tm, d), lambda i, j: (i, 0))]
    out_shape = [jax.ShapeDtypeStruct((n, d), F32)]
    args = [x, g, wg, wu, wd, fg]
    if next_weights is not None:
        stacked, layer = next_weights
        dr, fr = d // n_i, tf // n_i
        assert dr * n_i == d and fr * n_i == tf and dr % 16 == 0 and fr % 16 == 0
        assert all(w.shape[1:] == s for w, s in zip(stacked, ((d, f), (d, f), (f, d))))
        in_specs += [
            pl.BlockSpec((None, dr, tf), lambda i, j: (layer, i, j)),
            pl.BlockSpec((None, dr, tf), lambda i, j: (layer, i, j)),
            pl.BlockSpec((None, fr, d), lambda i, j: (layer, j * n_i + i, 0)),
        ]
        out_specs += [
            pl.BlockSpec((dr, tf), lambda i, j: (i, j)),
            pl.BlockSpec((dr, tf), lambda i, j: (i, j)),
            pl.BlockSpec((fr, d), lambda i, j: (j * n_i + i, 0)),
        ]
        out_shape += [jax.ShapeDtypeStruct(w.shape[1:], BF16) for w in stacked]
        args += list(stacked)
    outs = pl.pallas_call(
        functools.partial(_ffn_kernel, final_norm=final_norm, cast_next=next_weights is not None),
        grid=(n_i, n_j),
        in_specs=in_specs,
        out_specs=out_specs,
        out_shape=out_shape,
        scratch_shapes=[pltpu.VMEM((tm, d), BF16)],
        compiler_params=_params("parallel", "arbitrary"),
        name="ffn",
    )(*args)
    return outs[0], tuple(outs[1:])


def _proj_kernel(x_ref, g_ref, w_ref, wgate_ref, cos_ref, sin_ref, o_ref, gate_ref, h_ref,
                 *, groups_per_tile, n_rope_groups, query_groups):
    j = pl.program_id(1)

    @pl.when(j == 0)
    def _():
        h = _rms(x_ref[...], g_ref[...]).astype(BF16)
        h_ref[...] = h
        gate_ref[...] = _dot(h, wgate_ref[...])

    acc = _dot(h_ref[...], w_ref[...])
    cos = cos_ref[...]
    sin = sin_ref[...]
    for gi in range(groups_per_tile):
        group = j * groups_per_tile + gi
        is_rope = group < n_rope_groups
        is_query = functools.reduce(jnp.logical_or, [(group >= lo) & (group < hi) for lo, hi in query_groups])
        c = jnp.where(is_rope, cos, 1.0)
        s = jnp.where(is_rope, sin, 0.0)
        xg = acc[:, gi * LANES:(gi + 1) * LANES]
        y = xg * c + pltpu.roll(xg, HEAD_DIM // 2, axis=1) * s
        y = y * jnp.where(is_query, QK_SCALE, 1.0)
        o_ref[:, gi * LANES:(gi + 1) * LANES] = y.astype(BF16)


def _project(x, g, w_main, w_gate, cos, sin_signed, n_rope_groups, query_groups):
    n, d = x.shape
    t = cos.shape[0]
    width = w_main.shape[1]
    gw = w_gate.shape[1]
    n_groups = width // LANES
    gpt = _largest_divisor(n_groups, 11)
    tn = gpt * LANES
    tm = _row_tile(t, 1024)
    t_blocks = t // tm
    return pl.pallas_call(
        functools.partial(_proj_kernel, groups_per_tile=gpt, n_rope_groups=n_rope_groups,
                          query_groups=query_groups),
        grid=(n // tm, width // tn),
        in_specs=[
            pl.BlockSpec((tm, d), lambda i, j: (i, 0)),
            pl.BlockSpec((1, d), lambda i, j: (0, 0)),
            pl.BlockSpec((d, tn), lambda i, j: (0, j)),
            pl.BlockSpec((d, gw), lambda i, j: (0, 0)),
            pl.BlockSpec((tm, HEAD_DIM), lambda i, j: (i % t_blocks, 0)),
            pl.BlockSpec((tm, HEAD_DIM), lambda i, j: (i % t_blocks, 0)),
        ],
        out_specs=[
            pl.BlockSpec((tm, tn), lambda i, j: (i, j)),
            pl.BlockSpec((tm, gw), lambda i, j: (i, 0)),
        ],
        out_shape=[
            jax.ShapeDtypeStruct((n, width), BF16),
            jax.ShapeDtypeStruct((n, gw), F32),
        ],
        scratch_shapes=[pltpu.VMEM((tm, d), BF16)],
        compiler_params=_params("parallel", "arbitrary"),
        name="in_proj",
    )(x, g, w_main, w_gate, cos, sin_signed)


def _cmp_kernel(k_ref, v_ref, pk_ref, pv_ref, wk1_ref, wk2_ref, wv1_ref, wv2_ref, ok_ref, ov_ref, x_ref):
    seq = k_ref.shape[1]
    n_chunks = seq // CMP_STRIDE

    def one(kv_ref, p_ref, w1_ref, w2_ref, o_ref):
        x_ref[...] = kv_ref[0].astype(F32)
        a = jnp.zeros((n_chunks, w1_ref.shape[1]), F32)
        b = jnp.zeros((n_chunks, w1_ref.shape[1]), F32)
        for l in range(CMP_STRIDE):
            rows = x_ref[pl.ds(l, n_chunks, stride=CMP_STRIDE), :].astype(BF16)
            a = a + _dot(rows, w1_ref[l * HEAD_DIM:(l + 1) * HEAD_DIM, :])
            b = b + _dot(rows, w1_ref[(CMP_STRIDE + l) * HEAD_DIM:(CMP_STRIDE + l + 1) * HEAD_DIM, :])
        p = jnp.broadcast_to(p_ref[...], (8, CMP_LEN * HEAD_DIM))
        bias = _dot(p, w1_ref[...])[0:1]
        pre = a + pltpu.roll(b, n_chunks - 1, axis=0) + bias
        hid = (pre * jax.nn.sigmoid(pre)).astype(BF16)
        o_ref[0, 0] = _dot(hid, w2_ref[...]).astype(BF16)

    one(k_ref, pk_ref, wk1_ref, wk2_ref, ok_ref)
    one(v_ref, pv_ref, wv1_ref, wv2_ref, ov_ref)


def _compress(proj, off, batch, seq, pk, pv, wk1, wk2, wv1, wv2):
    g = NSA_KV_HEADS
    nc = seq // CMP_STRIDE
    hid = wk1.shape[1]
    dk = wk2.shape[1]
    assert CMP_LEN == 2 * CMP_STRIDE and dk == HEAD_DIM
    kb, vb = off["kc"] // HEAD_DIM, off["vc"] // HEAD_DIM
    full = lambda shape: pl.BlockSpec(shape, lambda i, j: tuple(0 for _ in shape))
    out_spec = pl.BlockSpec((1, 1, nc, dk), lambda i, j: (i, j, 0, 0))
    return pl.pallas_call(
        _cmp_kernel,
        grid=(batch, g),
        in_specs=[pl.BlockSpec((1, seq, HEAD_DIM), lambda i, j: (i, 0, kb + j)),
                  pl.BlockSpec((1, seq, HEAD_DIM), lambda i, j: (i, 0, vb + j)),
                  full((1, CMP_LEN * dk)), full((1, CMP_LEN * dk)),
                  full((CMP_LEN * dk, hid)), full((hid, dk)), full((CMP_LEN * dk, hid)), full((hid, dk))],
        out_specs=[out_spec, out_spec],
        out_shape=[jax.ShapeDtypeStruct((batch, g, nc, dk), BF16)] * 2,
        scratch_shapes=[pltpu.VMEM((seq, HEAD_DIM), F32)],
        compiler_params=_params("parallel", "parallel"),
        name="nsa_compress",
    )(proj, proj, pk, pv, wk1, wk2, wv1, wv2)


ONES_ROWS = 16


def _transposed_values(v, ones_rows):
    vt = v.astype(F32).T
    if ones_rows:
        vt = jnp.concatenate([vt, jnp.ones((ones_rows, vt.shape[1]), F32)], axis=0)
    return vt.astype(BF16)


def _softmax_piece(st, vt, dv, elem_bias=None, blk_bias=None):
    nk, nc = st.shape
    if elem_bias is not None:
        st = st + elem_bias
    nb = 1 if blk_bias is None else blk_bias.shape[0]
    s4 = st.reshape(nb, nk // nb // 8, 8, nc)
    bm = jnp.max(s4, axis=1)
    if blk_bias is not None:
        bm = bm + blk_bias[:, None, :]
    m = jnp.max(jnp.max(bm, axis=0), axis=0, keepdims=True)
    m = jnp.maximum(m, MASK_FLOOR)
    shift = -m if blk_bias is None else blk_bias - m
    e = jnp.exp2(s4 + shift[:, None, None, :])
    o = _dot(vt, e.reshape(nk, nc).astype(BF16))
    if vt.shape[0] > dv:
        return m, o[dv:dv + 1], o[0:dv]
    return m, jnp.sum(jnp.sum(jnp.sum(e, axis=0), axis=0), axis=0, keepdims=True), o


def _merge_pieces(pieces):
    m = functools.reduce(jnp.maximum, [p[0] for p in pieces])
    ws = [jnp.exp2(p[0] - m) for p in pieces]
    l = sum(w * p[1] for w, p in zip(ws, pieces))
    o = sum(w * p[2] for w, p in zip(ws, pieces))
    return o / l


def _nsa_kernel(q_ref, ks_ref, vs_ref, kw_ref, vw_ref, kc_ref, vc_ref, gate_ref, ovl_ref,
                o_ref, vst_ref, vwt_ref, *, seq, rep, n_slc, n_sel, var_len):
    j = pl.program_id(2)
    s0 = j * Q_BLOCK

    @pl.when(j == 0)
    def _():
        for i in range(seq // Q_BLOCK):
            vst_ref[i] = _transposed_values(vs_ref[0, i * Q_BLOCK:(i + 1) * Q_BLOCK, :], ONES_ROWS)
            vwt_ref[i] = _transposed_values(vw_ref[0, i * Q_BLOCK:(i + 1) * Q_BLOCK, :], ONES_ROWS)

    q = q_ref[0]
    q4 = jnp.concatenate([q[:, r * HEAD_DIM:(r + 1) * HEAD_DIM] for r in range(rep)], axis=0)
    t_row = s0 + lax.broadcasted_iota(jnp.int32, (1, Q_BLOCK), 1)
    head = lambda a, r: a[:, r * Q_BLOCK:(r + 1) * Q_BLOCK]

    kc = kc_ref[0, 0]
    n_rows = kc.shape[0]
    n_idx = lax.broadcasted_iota(jnp.int32, (n_rows, Q_BLOCK), 0)
    c_mask = (n_idx * CMP_STRIDE + (CMP_LEN - 1)) <= t_row
    st = _dot_nt(kc, q4)
    p_sum = jnp.zeros((n_rows, Q_BLOCK), F32)
    ps = []
    for r in range(rep):
        t = jnp.where(c_mask, head(st, r), NEG)
        e = jnp.where(c_mask, jnp.exp2(t - jnp.max(t, axis=0, keepdims=True)), 0.0)
        l = jnp.sum(e, axis=0, keepdims=True)
        p = e / jnp.where(l > 0.0, l, 1.0)
        p_sum = p_sum + p
        ps.append(p.astype(BF16))
    o_cmp = _dot(vc_ref[0, 0].astype(F32).T.astype(BF16), jnp.concatenate(ps, axis=1))

    hi = p_sum.astype(BF16)
    lo = (p_sum - hi.astype(F32)).astype(BF16)
    ovl = ovl_ref[...]
    n_pad = -(-n_slc // 8) * 8
    imp = (_dot(ovl, hi) + _dot(ovl, lo))[0:n_pad]
    blk = lax.broadcasted_iota(jnp.int32, (n_pad, Q_BLOCK), 0)
    cur = t_row // SLC_LEN
    forced = (blk == 0) | (blk == cur) | (blk == cur - 1)
    blk_causal = blk * SLC_LEN <= t_row
    imp = jnp.where(forced, 1e4, imp)
    imp = jnp.where(blk_causal, imp, -1.0)
    rank = jnp.zeros((n_pad, Q_BLOCK), F32)
    for sp in range(n_slc):
        row = imp[sp:sp + 1, :]
        ge = jnp.where(row >= imp, 1.0, 0.0)
        gt = jnp.where(row > imp, 1.0, 0.0)
        rank = rank + jnp.where(blk > sp, ge, gt)
    blocks_per_q = Q_BLOCK // SLC_LEN
    blk_bias = jnp.where((rank < float(n_sel)) & (blk < j * blocks_per_q), 0.0, NEG)
    blk_bias = jnp.concatenate([blk_bias] * rep, axis=1)

    nc = rep * Q_BLOCK
    kk = lax.broadcasted_iota(jnp.int32, (Q_BLOCK, Q_BLOCK), 0)
    qq = lax.broadcasted_iota(jnp.int32, (Q_BLOCK, Q_BLOCK), 1)
    tri = jnp.concatenate([jnp.where(kk <= qq, 0.0, NEG)] * rep, axis=1)
    anti = jnp.concatenate([jnp.where(kk > qq, 0.0, NEG)] * rep, axis=1)
    valid_row = lambda ok: jnp.where(ok, jnp.zeros((1, nc), F32), jnp.full((1, nc), NEG, F32))
    n_back = WINDOW // Q_BLOCK
    far = jnp.maximum(j - n_back, 0)
    mid = jnp.maximum(j - (n_back - 1), 0)
    mid_bias = jnp.concatenate([valid_row(mid + i < j) for i in range(n_back - 1)], axis=0)
    rows_of = lambda ref, blk0, n: ref[0, pl.ds(pl.multiple_of(blk0 * Q_BLOCK, Q_BLOCK), n * Q_BLOCK), :]
    tiles_of = lambda ref, blk0, n: jnp.concatenate([ref[blk0 + i] for i in range(n)], axis=1)
    q_per_var = var_len // Q_BLOCK
    blk_per_var = var_len // SLC_LEN

    def branches(n_chunks):
        specs = [
            (lambda: _dot_nt(rows_of(kw_ref, j, 1), q4), lambda: vwt_ref[j], tri, None),
            (lambda: _dot_nt(rows_of(kw_ref, far, 1), q4), lambda: vwt_ref[far], anti, valid_row(j >= n_back)),
            (lambda: _dot_nt(rows_of(kw_ref, mid, n_back - 1), q4), lambda: tiles_of(vwt_ref, mid, n_back - 1),
             None, mid_bias),
            (lambda: _dot_nt(rows_of(ks_ref, j, 1), q4), lambda: vst_ref[j], tri, None),
        ]
        for ci in range(n_chunks):
            specs.append((lambda ci=ci: _dot_nt(ks_ref[0, ci * var_len:(ci + 1) * var_len, :], q4),
                          lambda ci=ci: tiles_of(vst_ref, ci * q_per_var, q_per_var),
                          None, blk_bias[ci * blk_per_var:(ci + 1) * blk_per_var]))
        pieces = []
        ahead = 2
        sts = [spec[0]() for spec in specs[:ahead]]
        for i, (_, vt, elem_bias, piece_bias) in enumerate(specs):
            if i + ahead < len(specs):
                sts.append(specs[i + ahead][0]())
            pieces.append(_softmax_piece(sts[i], vt(), HEAD_DIM, elem_bias, piece_bias))
        return _merge_pieces(pieces[3:]), _merge_pieces(pieces[:3])

    gt = jax.nn.sigmoid(gate_ref[0]).T
    for v in range(seq // var_len):
        @pl.when(j // q_per_var == v)
        def _(n_chunks=v + 1):
            o_slc, o_win = branches(n_chunks)
            for r in range(rep):
                out = (gt[3 * r:3 * r + 1] * head(o_cmp, r) + gt[3 * r + 1:3 * r + 2] * head(o_slc, r)
                       + gt[3 * r + 2:3 * r + 3] * head(o_win, r))
                o_ref[0, :, r * HEAD_DIM:(r + 1) * HEAD_DIM] = out.T.astype(BF16)


def _block_overlap(n_rows, n_cmp, n_slc):
    c0 = np.arange(n_cmp) * CMP_STRIDE
    s0 = np.arange(n_slc) * SLC_LEN
    lo = np.maximum(c0[None, :], s0[:, None])
    hi = np.minimum(c0[None, :] + CMP_LEN, s0[:, None] + SLC_LEN)
    out = np.zeros((LANES, n_rows), np.float32)
    out[:n_slc, :n_cmp] = np.clip(hi - lo, 0, None) / CMP_LEN
    return out


def _nsa_attention(proj, gates, kc, vc, off, batch, seq, rep):
    g = NSA_KV_HEADS
    n_qb = seq // Q_BLOCK
    n_slc = seq // SLC_LEN
    n_sel = min(SLC_TOPK, n_slc)
    n_rows = kc.shape[2]
    n_cmp = (seq - CMP_LEN) // CMP_STRIDE + 1
    qw = rep * HEAD_DIM
    var_len = SLC_VARIANT_LEN if seq % SLC_VARIANT_LEN == 0 else seq
    ovl = jnp.asarray(_block_overlap(n_rows, n_cmp, n_slc), BF16)

    def kv_spec(name):
        base = off[name] // HEAD_DIM
        return pl.BlockSpec((1, seq, HEAD_DIM), lambda b, h, j: (b, 0, base + h))

    cmp_spec = pl.BlockSpec((1, 1, n_rows, HEAD_DIM), lambda b, h, j: (b, h, 0, 0))
    q_base = off["q_n"] // qw
    return pl.pallas_call(
        functools.partial(_nsa_kernel, seq=seq, rep=rep, n_slc=n_slc, n_sel=n_sel, var_len=var_len),
        grid=(batch, g, n_qb),
        scratch_shapes=[
            pltpu.VMEM((seq // Q_BLOCK, HEAD_DIM + ONES_ROWS, Q_BLOCK), BF16),
            pltpu.VMEM((seq // Q_BLOCK, HEAD_DIM + ONES_ROWS, Q_BLOCK), BF16),
        ],
        in_specs=[
            pl.BlockSpec((1, Q_BLOCK, qw), lambda b, h, j: (b, j, q_base + h)),
            kv_spec("ks"), kv_spec("vs"), kv_spec("kw"), kv_spec("vw"),
            cmp_spec, cmp_spec,
            pl.BlockSpec((1, Q_BLOCK, LANES), lambda b, h, j: (b, j, h)),
            pl.BlockSpec((LANES, n_rows), lambda b, h, j: (0, 0)),
        ],
        out_specs=pl.BlockSpec((1, Q_BLOCK, qw), lambda b, h, j: (b, j, h)),
        out_shape=jax.ShapeDtypeStruct((batch, seq, g * qw), BF16),
        compiler_params=_params("parallel", "parallel", "arbitrary"),
        name="nsa_attention",
    )(proj, proj, proj, proj, proj, kc, vc, gates, ovl)


def _diff_kernel(q_ref, k_ref, v_ref, lam_ref, sg_ref, o_ref, vt_ref, *, seq, tq, chunk, lambda_init):
    j = pl.program_id(2)

    @pl.when(j == 0)
    def _():
        for i in range(seq // tq):
            vt_ref[i] = _transposed_values(v_ref[0, i * tq:(i + 1) * tq, :], 0)

    lv = lam_ref[...]
    lam = (jnp.exp(jnp.sum(lv[0:1] * lv[1:2], axis=-1, keepdims=True))
           - jnp.exp(jnp.sum(lv[2:3] * lv[3:4], axis=-1, keepdims=True)) + lambda_init)
    q = q_ref[0]
    qs = [q[:, c * HEAD_DIM:(c + 1) * HEAD_DIM] for c in range(2)]
    tri = jnp.where(lax.broadcasted_iota(jnp.int32, (tq, tq), 0) <= lax.broadcasted_iota(jnp.int32, (tq, tq), 1),
                    0.0, NEG)

    def attend(v):
        spans = [(v * tq, tq, tri)]
        c0 = 0
        while c0 < v * tq:
            n = min(chunk, v * tq - c0)
            spans.append((c0, n, None))
            c0 += n
        specs = [(c, s) for s in spans for c in range(2)]
        scores = lambda c, s: _dot_nt(k_ref[0, s[0]:s[0] + s[1], c * HEAD_DIM:(c + 1) * HEAD_DIM], qs[c])
        pieces = ([], [])
        ahead = 2
        sts = [scores(*spec) for spec in specs[:ahead]]
        for i, (c, s) in enumerate(specs):
            if i + ahead < len(specs):
                sts.append(scores(*specs[i + ahead]))
            vt = jnp.concatenate([vt_ref[s[0] // tq + t] for t in range(s[1] // tq)], axis=1)
            pieces[c].append(_softmax_piece(sts[i], vt, 2 * HEAD_DIM, s[2]))
        return _merge_pieces(pieces[0]) - lam * _merge_pieces(pieces[1])

    for v in range(seq // tq):
        @pl.when(j == v)
        def _(v=v):
            o = attend(v).T
            o_ref[0] = (_rms(o, sg_ref[...]) * (1.0 - lambda_init)).astype(BF16)


def _diff_attention(proj, lam, subln, off, batch, seq, heads, lambda_init):
    vw = 2 * HEAD_DIM
    tq = _row_tile(seq, 512)
    qb, kb, vb = off["q_d"] // vw, off["k_d"] // vw, off["v_d"] // vw
    chunk = tq
    return pl.pallas_call(
        functools.partial(_diff_kernel, seq=seq, tq=tq, chunk=chunk, lambda_init=lambda_init),
        grid=(batch, heads, seq // tq),
        scratch_shapes=[pltpu.VMEM((seq // tq, vw, tq), BF16)],
        in_specs=[
            pl.BlockSpec((1, tq, vw), lambda b, h, j: (b, j, qb + h)),
            pl.BlockSpec((1, seq, vw), lambda b, h, j: (b, 0, kb + h)),
            pl.BlockSpec((1, seq, vw), lambda b, h, j: (b, 0, vb + h)),
            pl.BlockSpec((4, HEAD_DIM), lambda b, h, j: (0, 0)),
            pl.BlockSpec((1, vw), lambda b, h, j: (0, 0)),
        ],
        out_specs=pl.BlockSpec((1, tq, vw), lambda b, h, j: (b, j, h)),
        out_shape=jax.ShapeDtypeStruct((batch, seq, heads * vw), BF16),
        compiler_params=_params("parallel", "parallel", "arbitrary"),
        name="diff_attention",
    )(proj, proj, proj, lam, subln)


def _out_kernel(x_ref, a_ref, b_ref, wa_ref, wb_ref, o_ref):
    o_ref[...] = x_ref[...] + _dot(a_ref[...], wa_ref[...]) + _dot(b_ref[...], wb_ref[...])


def _out_project(x, a, b, wa, wb):
    n, d = x.shape
    ka, kb = a.shape[1], b.shape[1]
    tm = _row_tile(n, 512)
    tn = LANES * _largest_divisor(d // LANES, 16)
    return pl.pallas_call(
        _out_kernel,
        grid=(n // tm, d // tn),
        in_specs=[
            pl.BlockSpec((tm, tn), lambda i, j: (i, j)),
            pl.BlockSpec((tm, ka), lambda i, j: (i, 0)),
            pl.BlockSpec((tm, kb), lambda i, j: (i, 0)),
            pl.BlockSpec((ka, tn), lambda i, j: (0, j)),
            pl.BlockSpec((kb, tn), lambda i, j: (0, j)),
        ],
        out_specs=pl.BlockSpec((tm, tn), lambda i, j: (i, j)),
        out_shape=jax.ShapeDtypeStruct((n, d), F32),
        compiler_params=_params("parallel", "arbitrary"),
        name="out_proj",
    )(x, a, b, wa, wb)


def _rope_tables(t):
    inv = 1.0 / (ROPE_THETA ** (jnp.arange(0, HEAD_DIM, 2, dtype=F32) / HEAD_DIM))
    ang = jnp.arange(t, dtype=F32)[:, None] * inv[None, :]
    ang = jnp.concatenate([ang, ang], axis=-1)
    sign = jnp.concatenate([-jnp.ones((HEAD_DIM // 2,), F32), jnp.ones((HEAD_DIM // 2,), F32)])
    return jnp.cos(ang), jnp.sin(ang) * sign[None, :]


def _layout(d_model):
    nsa_heads = d_model // (2 * HEAD_DIM)
    diff_heads = d_model // (4 * HEAD_DIM)
    kv = NSA_KV_HEADS * HEAD_DIM
    sizes = dict(q_n=nsa_heads * HEAD_DIM, kc=kv, vc=kv, ks=kv, vs=kv, kw=kv, vw=kv, gates=3 * nsa_heads,
                 q_d=2 * diff_heads * HEAD_DIM, k_d=2 * diff_heads * HEAD_DIM, v_d=diff_heads * 2 * HEAD_DIM)
    orig, o = {}, 0
    for name in ("q_n", "kc", "vc", "ks", "vs", "kw", "vw", "gates", "q_d", "k_d", "v_d"):
        orig[name] = o
        o += sizes[name]
    rope_names = ("q_n", "kc", "ks", "kw", "q_d", "k_d")
    plain_names = ("vc", "vs", "vw", "v_d")
    new, o = {}, 0
    for name in rope_names + plain_names:
        new[name] = o
        o += sizes[name]
    n_rope_groups = sum(sizes[nm] for nm in rope_names) // LANES
    return sizes, orig, new, rope_names + plain_names, n_rope_groups, nsa_heads, diff_heads


def kernel(x, ffn1_norm, ffn1_w_gate, ffn1_w_up, ffn1_w_down, mix_norm, w_in, cmp_pos_k, cmp_pos_v, cmp_wk1, cmp_wk2, cmp_wv1, cmp_wv2, lam_q1, lam_k1, lam_q2, lam_k2, diff_subln, w_out, ffn2_norm, ffn2_w_gate, ffn2_w_up, ffn2_w_down, final_norm):
    batch, seq, d_model = x.shape
    depth = ffn1_norm.shape[0]
    n = batch * seq
    sizes, orig, off, order, n_rope_groups, nsa_heads, diff_heads = _layout(d_model)
    g = NSA_KV_HEADS
    rep = nsa_heads // g
    cos, sin_signed = _rope_tables(seq)
    fg = final_norm.reshape(1, d_model)
    ffn1_stacked = (ffn1_w_gate, ffn1_w_up, ffn1_w_down)
    ffn2_stacked = (ffn2_w_gate, ffn2_w_up, ffn2_w_down)
    w_ffn = tuple(w[0].astype(BF16) for w in ffn1_stacked)

    xf = x.reshape(n, d_model)
    for l in range(depth):
        lambda_init = 0.8 - 0.6 * math.exp(-0.3 * l)
        xf, w_ffn = _ffn(xf, ffn1_norm[l].reshape(1, d_model), *w_ffn, fg, False, (ffn2_stacked, l))

        wl = w_in[l]
        w_main = jnp.concatenate([wl[:, orig[nm]:orig[nm] + sizes[nm]] for nm in order], axis=1).astype(BF16)
        gate_tiles = []
        for h in range(g):
            cols = wl[:, orig["gates"] + h * 3 * rep: orig["gates"] + (h + 1) * 3 * rep]
            gate_tiles.append(jnp.pad(cols, ((0, 0), (0, LANES - 3 * rep))))
        w_gate = jnp.concatenate(gate_tiles, axis=1).astype(BF16)
        query_groups = tuple((off[nm] // LANES, (off[nm] + sizes[nm]) // LANES) for nm in ("q_n", "q_d"))
        proj, gates = _project(xf, mix_norm[l].reshape(1, d_model), w_main, w_gate, cos, sin_signed, n_rope_groups,
                               query_groups)
        proj3 = proj.reshape(batch, seq, -1)
        gates3 = gates.reshape(batch, seq, -1)

        kc, vc = _compress(proj3, off, batch, seq,
                           cmp_pos_k[l].reshape(1, -1).astype(BF16), cmp_pos_v[l].reshape(1, -1).astype(BF16),
                           cmp_wk1[l].astype(BF16), cmp_wk2[l].astype(BF16),
                           cmp_wv1[l].astype(BF16), cmp_wv2[l].astype(BF16))
        o_nsa = _nsa_attention(proj3, gates3, kc, vc, off, batch, seq, rep)
        lam = jnp.stack([lam_q1[l], lam_k1[l], lam_q2[l], lam_k2[l]])
        o_diff = _diff_attention(proj3, lam, diff_subln[l].reshape(1, -1), off, batch, seq, diff_heads, lambda_init)

        half = sizes["q_n"]
        wo = w_out[l].astype(BF16)
        xf = _out_project(xf, o_nsa.reshape(n, -1), o_diff.reshape(n, -1), wo[:half], wo[half:])

        last = l == depth - 1
        xf, w_ffn = _ffn(xf, ffn2_norm[l].reshape(1, d_model), *w_ffn, fg, last,
                         None if last else (ffn1_stacked, l + 1))
    return xf.reshape(batch, seq, d_model)
```

```python
import functools
import math

import numpy as np
import jax
import jax.numpy as jnp
from jax import lax
from jax.experimental import pallas as pl
from jax.experimental.pallas import tpu as pltpu

HEAD_DIM = 128
NSA_KV_HEADS = 2
CMP_LEN = 32
CMP_STRIDE = 16
SLC_LEN = 64
SLC_TOPK = 16
WINDOW = 512
Q_BLOCK = 128
ROPE_THETA = 10000.0
EPS = 1e-6
NEG = -1e30
MASK_FLOOR = -1e20
LOG2E = 1.4426950408889634
QK_SCALE = HEAD_DIM ** -0.5 * LOG2E
SLC_VARIANT_LEN = 512
NSA_BLOCKS_PER_STEP = 4
LANES = 128
VMEM_LIMIT = 58 * 1024 * 1024

F32 = jnp.float32
BF16 = jnp.bfloat16


def _largest_divisor(n, cap):
    for d in range(min(n, cap), 0, -1):
        if n % d == 0:
            return d
    return 1


def _row_tile(n, cap):
    for d in range(min(n, cap), 7, -1):
        if n % d == 0 and d % 8 == 0:
            return d
    return n


def _rms(x, g):
    return x * lax.rsqrt(jnp.mean(x * x, axis=-1, keepdims=True) + EPS) * g


def _dot(a, b):
    return jnp.dot(a, b, preferred_element_type=F32)


def _dot_nt(a, b):
    return lax.dot_general(a, b, (((1,), (1,)), ((), ())), preferred_element_type=F32)


def _params(*sem):
    return pltpu.CompilerParams(dimension_semantics=sem, vmem_limit_bytes=VMEM_LIMIT)


def _ffn_kernel(*refs, final_norm, cast_next):
    if cast_next:
        (x_ref, g_ref, wg_ref, wu_ref, wd_ref, fg_ref, ng_ref, nu_ref, nd_ref,
         o_ref, og_ref, ou_ref, od_ref, h_ref) = refs
        og_ref[...] = ng_ref[...].astype(BF16)
        ou_ref[...] = nu_ref[...].astype(BF16)
        od_ref[...] = nd_ref[...].astype(BF16)
    else:
        x_ref, g_ref, wg_ref, wu_ref, wd_ref, fg_ref, o_ref, h_ref = refs
    j = pl.program_id(1)

    @pl.when(j == 0)
    def _():
        x = x_ref[...]
        h_ref[...] = _rms(x, g_ref[...]).astype(BF16)
        o_ref[...] = x

    h = h_ref[...]
    a = _dot(h, wg_ref[...])
    b = _dot(h, wu_ref[...])
    act = (a * jax.nn.sigmoid(a) * b * 0.5).astype(BF16)
    o_ref[...] += _dot(act, wd_ref[...])

    if final_norm:
        @pl.when(j == pl.num_programs(1) - 1)
        def _():
            o_ref[...] = _rms(o_ref[...], fg_ref[...])


def _ffn(x, g, wg, wu, wd, fg, final_norm, next_weights=None):
    n, d = x.shape
    f = wg.shape[1]
    tm = _row_tile(n, 1024)
    tf = LANES * _largest_divisor(f // LANES, 4)
    n_i, n_j = n // tm, f // tf
    in_specs = [
        pl.BlockSpec((tm, d), lambda i, j: (i, 0)),
        pl.BlockSpec((1, d), lambda i, j: (0, 0)),
        pl.BlockSpec((d, tf), lambda i, j: (0, j)),
        pl.BlockSpec((d, tf), lambda i, j: (0, j)),
        pl.BlockSpec((tf, d), lambda i, j: (j, 0)),
        pl.BlockSpec((1, d), lambda i, j: (0, 0)),
    ]
    out_specs = [pl.BlockSpec((tm, d), lambda i, j: (i, 0))]
    out_shape = [jax.ShapeDtypeStruct((n, d), F32)]
    args = [x, g, wg, wu, wd, fg]
    if next_weights is not None:
        stacked, layer = next_weights
        dr, fr = d // n_i, tf // n_i
        assert dr * n_i == d and fr * n_i == tf and dr % 16 == 0 and fr % 16 == 0
        assert all(w.shape[1:] == s for w, s in zip(stacked, ((d, f), (d, f), (f, d))))
        in_specs += [
            pl.BlockSpec((None, dr, tf), lambda i, j: (layer, i, j)),
            pl.BlockSpec((None, dr, tf), lambda i, j: (layer, i, j)),
            pl.BlockSpec((None, fr, d), lambda i, j: (layer, j * n_i + i, 0)),
        ]
        out_specs += [
            pl.BlockSpec((dr, tf), lambda i, j: (i, j)),
            pl.BlockSpec((dr, tf), lambda i, j: (i, j)),
            pl.BlockSpec((fr, d), lambda i, j: (j * n_i + i, 0)),
        ]
        out_shape += [jax.ShapeDtypeStruct(w.shape[1:], BF16) for w in stacked]
        args += list(stacked)
    outs = pl.pallas_call(
        functools.partial(_ffn_kernel, final_norm=final_norm, cast_next=next_weights is not None),
        grid=(n_i, n_j),
        in_specs=in_specs,
        out_specs=out_specs,
        out_shape=out_shape,
        scratch_shapes=[pltpu.VMEM((tm, d), BF16)],
        compiler_params=_params("parallel", "arbitrary"),
        name="ffn",
    )(*args)
    return outs[0], tuple(outs[1:])


def _proj_kernel(x_ref, g_ref, w_ref, wgate_ref, cos_ref, sin_ref, o_ref, gate_ref, h_ref,
                 *, groups_per_tile, n_rope_groups, query_groups):
    j = pl.program_id(1)

    @pl.when(j == 0)
    def _():
        h = _rms(x_ref[...], g_ref[...]).astype(BF16)
        h_ref[...] = h
        gate_ref[...] = _dot(h, wgate_ref[...])

    acc = _dot(h_ref[...], w_ref[...])
    cos = cos_ref[...]
    sin = sin_ref[...]
    for gi in range(groups_per_tile):
        group = j * groups_per_tile + gi
        is_rope = group < n_rope_groups
        is_query = functools.reduce(jnp.logical_or, [(group >= lo) & (group < hi) for lo, hi in query_groups])
        c = jnp.where(is_rope, cos, 1.0)
        s = jnp.where(is_rope, sin, 0.0)
        xg = acc[:, gi * LANES:(gi + 1) * LANES]
        y = xg * c + pltpu.roll(xg, HEAD_DIM // 2, axis=1) * s
        y = y * jnp.where(is_query, QK_SCALE, 1.0)
        o_ref[:, gi * LANES:(gi + 1) * LANES] = y.astype(BF16)


def _project(x, g, w_main, w_gate, cos, sin_signed, n_rope_groups, query_groups):
    n, d = x.shape
    t = cos.shape[0]
    width = w_main.shape[1]
    gw = w_gate.shape[1]
    n_groups = width // LANES
    gpt = _largest_divisor(n_groups, 11)
    tn = gpt * LANES
    tm = _row_tile(t, 1024)
    t_blocks = t // tm
    return pl.pallas_call(
        functools.partial(_proj_kernel, groups_per_tile=gpt, n_rope_groups=n_rope_groups,
                          query_groups=query_groups),
        grid=(n // tm, width // tn),
        in_specs=[
            pl.BlockSpec((tm, d), lambda i, j: (i, 0)),
            pl.BlockSpec((1, d), lambda i, j: (0, 0)),
            pl.BlockSpec((d, tn), lambda i, j: (0, j)),
            pl.BlockSpec((d, gw), lambda i, j: (0, 0)),
            pl.BlockSpec((tm, HEAD_DIM), lambda i, j: (i % t_blocks, 0)),
            pl.BlockSpec((tm, HEAD_DIM), lambda i, j: (i % t_blocks, 0)),
        ],
        out_specs=[
            pl.BlockSpec((tm, tn), lambda i, j: (i, j)),
            pl.BlockSpec((tm, gw), lambda i, j: (i, 0)),
        ],
        out_shape=[
            jax.ShapeDtypeStruct((n, width), BF16),
            jax.ShapeDtypeStruct((n, gw), F32),
        ],
        scratch_shapes=[pltpu.VMEM((tm, d), BF16)],
        compiler_params=_params("parallel", "arbitrary"),
        name="in_proj",
    )(x, g, w_main, w_gate, cos, sin_signed)


def _cmp_kernel(k_ref, v_ref, pk_ref, pv_ref, wk1_ref, wk2_ref, wv1_ref, wv2_ref, ok_ref, ov_ref, x_ref):
    seq = k_ref.shape[1]
    n_chunks = seq // CMP_STRIDE

    def one(kv_ref, p_ref, w1_ref, w2_ref, o_ref):
        x_ref[...] = kv_ref[0].astype(F32)
        a = jnp.zeros((n_chunks, w1_ref.shape[1]), F32)
        b = jnp.zeros((n_chunks, w1_ref.shape[1]), F32)
        for l in range(CMP_STRIDE):
            rows = x_ref[pl.ds(l, n_chunks, stride=CMP_STRIDE), :].astype(BF16)
            a = a + _dot(rows, w1_ref[l * HEAD_DIM:(l + 1) * HEAD_DIM, :])
            b = b + _dot(rows, w1_ref[(CMP_STRIDE + l) * HEAD_DIM:(CMP_STRIDE + l + 1) * HEAD_DIM, :])
        p = jnp.broadcast_to(p_ref[...], (8, CMP_LEN * HEAD_DIM))
        bias = _dot(p, w1_ref[...])[0:1]
        pre = a + pltpu.roll(b, n_chunks - 1, axis=0) + bias
        hid = (pre * jax.nn.sigmoid(pre)).astype(BF16)
        o_ref[0, 0] = _dot(hid, w2_ref[...]).astype(BF16)

    one(k_ref, pk_ref, wk1_ref, wk2_ref, ok_ref)
    one(v_ref, pv_ref, wv1_ref, wv2_ref, ov_ref)


def _compress(proj, off, batch, seq, pk, pv, wk1, wk2, wv1, wv2):
    g = NSA_KV_HEADS
    nc = seq // CMP_STRIDE
    hid = wk1.shape[1]
    dk = wk2.shape[1]
    assert CMP_LEN == 2 * CMP_STRIDE and dk == HEAD_DIM
    kb, vb = off["kc"] // HEAD_DIM, off["vc"] // HEAD_DIM
    full = lambda shape: pl.BlockSpec(shape, lambda i, j: tuple(0 for _ in shape))
    out_spec = pl.BlockSpec((1, 1, nc, dk), lambda i, j: (i, j, 0, 0))
    return pl.pallas_call(
        _cmp_kernel,
        grid=(batch, g),
        in_specs=[pl.BlockSpec((1, seq, HEAD_DIM), lambda i, j: (i, 0, kb + j)),
                  pl.BlockSpec((1, seq, HEAD_DIM), lambda i, j: (i, 0, vb + j)),
                  full((1, CMP_LEN * dk)), full((1, CMP_LEN * dk)),
                  full((CMP_LEN * dk, hid)), full((hid, dk)), full((CMP_LEN * dk, hid)), full((hid, dk))],
        out_specs=[out_spec, out_spec],
        out_shape=[jax.ShapeDtypeStruct((batch, g, nc, dk), BF16)] * 2,
        scratch_shapes=[pltpu.VMEM((seq, HEAD_DIM), F32)],
        compiler_params=_params("parallel", "parallel"),
        name="nsa_compress",
    )(proj, proj, pk, pv, wk1, wk2, wv1, wv2)


ONES_ROWS = 16


def _transposed_values(v, ones_rows):
    vt = v.astype(F32).T
    if ones_rows:
        vt = jnp.concatenate([vt, jnp.ones((ones_rows, vt.shape[1]), F32)], axis=0)
    return vt.astype(BF16)


def _softmax_piece(st, vt, dv, elem_bias=None, blk_bias=None):
    nk, nc = st.shape
    if elem_bias is not None:
        st = st + elem_bias
    nb = 1 if blk_bias is None else blk_bias.shape[0]
    s4 = st.reshape(nb, nk // nb // 8, 8, nc)
    bm = jnp.max(s4, axis=1)
    if blk_bias is not None:
        bm = bm + blk_bias[:, None, :]
    m = jnp.max(jnp.max(bm, axis=0), axis=0, keepdims=True)
    m = jnp.maximum(m, MASK_FLOOR)
    shift = -m if blk_bias is None else blk_bias - m
    e = jnp.exp2(s4 + shift[:, None, None, :])
    o = _dot(vt, e.reshape(nk, nc).astype(BF16))
    if vt.shape[0] > dv:
        return m, o[dv:dv + 1], o[0:dv]
    return m, jnp.sum(jnp.sum(jnp.sum(e, axis=0), axis=0), axis=0, keepdims=True), o


def _merge_pieces(pieces):
    m = functools.reduce(jnp.maximum, [p[0] for p in pieces])
    ws = [jnp.exp2(p[0] - m) for p in pieces]
    l = sum(w * p[1] for w, p in zip(ws, pieces))
    o = sum(w * p[2] for w, p in zip(ws, pieces))
    return o / l


def _interleave(gens):
    results = [None] * len(gens)
    live = list(range(len(gens)))
    while live:
        for i in list(live):
            try:
                next(gens[i])
            except StopIteration as stop:
                results[i] = stop.value
                live.remove(i)
    return results


def _nsa_block(j, q, gates, ks_ref, kw_ref, kc_ref, vc_ref, ovl_ref, vst_ref, vwt_ref,
               *, rep, n_slc, n_sel, var_len):
    s0 = j * Q_BLOCK
    q4 = jnp.concatenate([q[:, r * HEAD_DIM:(r + 1) * HEAD_DIM] for r in range(rep)], axis=0)
    t_row = s0 + lax.broadcasted_iota(jnp.int32, (1, Q_BLOCK), 1)
    head = lambda a, r: a[:, r * Q_BLOCK:(r + 1) * Q_BLOCK]

    nc = rep * Q_BLOCK
    kk = lax.broadcasted_iota(jnp.int32, (Q_BLOCK, Q_BLOCK), 0)
    qq = lax.broadcasted_iota(jnp.int32, (Q_BLOCK, Q_BLOCK), 1)
    tri = jnp.concatenate([jnp.where(kk <= qq, 0.0, NEG)] * rep, axis=1)
    anti = jnp.concatenate([jnp.where(kk > qq, 0.0, NEG)] * rep, axis=1)
    valid_row = lambda ok: jnp.where(ok, jnp.zeros((1, nc), F32), jnp.full((1, nc), NEG, F32))
    n_back = WINDOW // Q_BLOCK
    far = jnp.maximum(j - n_back, 0)
    mid = jnp.maximum(j - (n_back - 1), 0)
    mid_bias = jnp.concatenate([valid_row(mid + i < j) for i in range(n_back - 1)], axis=0)
    rows_of = lambda ref, blk0, n: ref[0, pl.ds(pl.multiple_of(blk0 * Q_BLOCK, Q_BLOCK), n * Q_BLOCK), :]
    tiles_of = lambda ref, blk0, n: jnp.concatenate([ref[blk0 + i] for i in range(n)], axis=1)
    early = [
        (rows_of(kw_ref, j, 1), lambda: vwt_ref[j], tri, None),
        (rows_of(kw_ref, far, 1), lambda: vwt_ref[far], anti, valid_row(j >= n_back)),
        (rows_of(kw_ref, mid, n_back - 1), lambda: tiles_of(vwt_ref, mid, n_back - 1), None, mid_bias),
        (rows_of(ks_ref, j, 1), lambda: vst_ref[j], tri, None),
    ]
    early_scores = [_dot_nt(k, q4) for k, _, _, _ in early]
    first_scores = _dot_nt(ks_ref[0, 0:var_len, :], q4)
    early_piece = lambda i: _softmax_piece(early_scores[i], early[i][1](), HEAD_DIM, early[i][2], early[i][3])

    kc = kc_ref[0, 0]
    n_rows = kc.shape[0]
    n_idx = lax.broadcasted_iota(jnp.int32, (n_rows, Q_BLOCK), 0)
    c_mask = (n_idx * CMP_STRIDE + (CMP_LEN - 1)) <= t_row
    st = _dot_nt(kc, q4)
    yield
    win_pieces = [early_piece(0), early_piece(1)]
    p_sum = jnp.zeros((n_rows, Q_BLOCK), F32)
    ps = []
    for r in range(rep):
        t = jnp.where(c_mask, head(st, r), NEG)
        e = jnp.where(c_mask, jnp.exp2(t - jnp.max(t, axis=0, keepdims=True)), 0.0)
        l = jnp.sum(e, axis=0, keepdims=True)
        p = e / jnp.where(l > 0.0, l, 1.0)
        p_sum = p_sum + p
        ps.append(p.astype(BF16))
    o_cmp = _dot(vc_ref[0, 0].astype(F32).T.astype(BF16), jnp.concatenate(ps, axis=1))

    hi = p_sum.astype(BF16)
    lo = (p_sum - hi.astype(F32)).astype(BF16)
    ovl = ovl_ref[...]
    n_pad = -(-n_slc // 8) * 8
    imp = (_dot(ovl, hi) + _dot(ovl, lo))[0:n_pad]
    yield
    win_pieces.append(early_piece(2))
    slc_diag = early_piece(3)
    blk = lax.broadcasted_iota(jnp.int32, (n_pad, Q_BLOCK), 0)
    cur = t_row // SLC_LEN
    forced = (blk == 0) | (blk == cur) | (blk == cur - 1)
    blk_causal = blk * SLC_LEN <= t_row
    imp = jnp.where(forced, 1e4, imp)
    imp = jnp.where(blk_causal, imp, -1.0)
    rank = jnp.zeros((n_pad, Q_BLOCK), F32)
    for sp in range(n_slc):
        row = imp[sp:sp + 1, :]
        ge = jnp.where(row >= imp, 1.0, 0.0)
        gt = jnp.where(row > imp, 1.0, 0.0)
        rank = rank + jnp.where(blk > sp, ge, gt)
    blocks_per_q = Q_BLOCK // SLC_LEN
    blk_bias = jnp.where((rank < float(n_sel)) & (blk < j * blocks_per_q), 0.0, NEG)
    blk_bias = jnp.concatenate([blk_bias] * rep, axis=1)
    yield

    gt = jax.nn.sigmoid(gates).T
    gate = lambda branch: jnp.concatenate([jnp.broadcast_to(gt[3 * r + branch:3 * r + branch + 1], (HEAD_DIM, Q_BLOCK))
                                           for r in range(rep)], axis=1)
    partial = gate(0) * o_cmp + gate(2) * _merge_pieces(win_pieces)
    q_per_var = var_len // Q_BLOCK
    blk_per_var = var_len // SLC_LEN
    scores = lambda ci: _dot_nt(ks_ref[0, ci * var_len:(ci + 1) * var_len, :], q4)
    chunk_piece = lambda ci, st: _softmax_piece(st, tiles_of(vst_ref, ci * q_per_var, q_per_var), HEAD_DIM,
                                                None, blk_bias[ci * blk_per_var:(ci + 1) * blk_per_var])
    first_chunk = chunk_piece(0, first_scores)
    return dict(partial=partial, g_slc=gate(1), pieces=[slc_diag, first_chunk], scores=scores, chunk_piece=chunk_piece)


def _nsa_selected(ctx, n_chunks):
    pieces = list(ctx["pieces"])
    ahead = 2
    sts = {ci: ctx["scores"](ci) for ci in range(1, min(1 + ahead, n_chunks))}
    for ci in range(1, n_chunks):
        if ci + ahead < n_chunks:
            sts[ci + ahead] = ctx["scores"](ci + ahead)
        pieces.append(ctx["chunk_piece"](ci, sts[ci]))
        yield
    return ctx["partial"] + ctx["g_slc"] * _merge_pieces(pieces)


def _nsa_kernel(q_ref, ks_ref, vs_ref, kw_ref, vw_ref, kc_ref, vc_ref, gate_ref, ovl_ref,
                o_ref, vst_ref, vwt_ref, *, seq, rep, n_slc, n_sel, var_len, blocks):
    jj = pl.program_id(2)

    @pl.when(jj == 0)
    def _():
        for i in range(seq // Q_BLOCK):
            vst_ref[i] = _transposed_values(vs_ref[0, i * Q_BLOCK:(i + 1) * Q_BLOCK, :], ONES_ROWS)
            vwt_ref[i] = _transposed_values(vw_ref[0, i * Q_BLOCK:(i + 1) * Q_BLOCK, :], ONES_ROWS)

    rows = lambda h: slice(h * Q_BLOCK, (h + 1) * Q_BLOCK)
    ctxs = _interleave([
        _nsa_block(jj * blocks + h, q_ref[0, rows(h), :], gate_ref[0, rows(h), :], ks_ref, kw_ref, kc_ref, vc_ref,
                   ovl_ref, vst_ref, vwt_ref, rep=rep, n_slc=n_slc, n_sel=n_sel, var_len=var_len)
        for h in range(blocks)])

    steps_per_var = var_len // Q_BLOCK // blocks
    for v in range(seq // var_len):
        @pl.when(jj // steps_per_var == v)
        def _(n_chunks=v + 1):
            outs = _interleave([_nsa_selected(ctx, n_chunks) for ctx in ctxs])
            for h, out in enumerate(outs):
                for r in range(rep):
                    o_ref[0, rows(h), r * HEAD_DIM:(r + 1) * HEAD_DIM] = (
                        out[:, r * Q_BLOCK:(r + 1) * Q_BLOCK].T.astype(BF16))


def _block_overlap(n_rows, n_cmp, n_slc):
    c0 = np.arange(n_cmp) * CMP_STRIDE
    s0 = np.arange(n_slc) * SLC_LEN
    lo = np.maximum(c0[None, :], s0[:, None])
    hi = np.minimum(c0[None, :] + CMP_LEN, s0[:, None] + SLC_LEN)
    out = np.zeros((LANES, n_rows), np.float32)
    out[:n_slc, :n_cmp] = np.clip(hi - lo, 0, None) / CMP_LEN
    return out


def _nsa_attention(proj, gates, kc, vc, off, batch, seq, rep):
    g = NSA_KV_HEADS
    n_qb = seq // Q_BLOCK
    n_slc = seq // SLC_LEN
    n_sel = min(SLC_TOPK, n_slc)
    n_rows = kc.shape[2]
    n_cmp = (seq - CMP_LEN) // CMP_STRIDE + 1
    qw = rep * HEAD_DIM
    var_len = SLC_VARIANT_LEN if seq % SLC_VARIANT_LEN == 0 else seq
    blocks = NSA_BLOCKS_PER_STEP
    assert (var_len // Q_BLOCK) % blocks == 0
    ovl = jnp.asarray(_block_overlap(n_rows, n_cmp, n_slc), BF16)

    def kv_spec(name):
        base = off[name] // HEAD_DIM
        return pl.BlockSpec((1, seq, HEAD_DIM), lambda b, h, j: (b, 0, base + h))

    cmp_spec = pl.BlockSpec((1, 1, n_rows, HEAD_DIM), lambda b, h, j: (b, h, 0, 0))
    q_base = off["q_n"] // qw
    return pl.pallas_call(
        functools.partial(_nsa_kernel, seq=seq, rep=rep, n_slc=n_slc, n_sel=n_sel, var_len=var_len, blocks=blocks),
        grid=(batch, g, n_qb // blocks),
        scratch_shapes=[
            pltpu.VMEM((seq // Q_BLOCK, HEAD_DIM + ONES_ROWS, Q_BLOCK), BF16),
            pltpu.VMEM((seq // Q_BLOCK, HEAD_DIM + ONES_ROWS, Q_BLOCK), BF16),
        ],
        in_specs=[
            pl.BlockSpec((1, blocks * Q_BLOCK, qw), lambda b, h, j: (b, j, q_base + h)),
            kv_spec("ks"), kv_spec("vs"), kv_spec("kw"), kv_spec("vw"),
            cmp_spec, cmp_spec,
            pl.BlockSpec((1, blocks * Q_BLOCK, LANES), lambda b, h, j: (b, j, h)),
            pl.BlockSpec((LANES, n_rows), lambda b, h, j: (0, 0)),
        ],
        out_specs=pl.BlockSpec((1, blocks * Q_BLOCK, qw), lambda b, h, j: (b, j, h)),
        out_shape=jax.ShapeDtypeStruct((batch, seq, g * qw), BF16),
        compiler_params=_params("parallel", "parallel", "arbitrary"),
        name="nsa_attention",
    )(proj, proj, proj, proj, proj, kc, vc, gates, ovl)


def _diff_kernel(q_ref, k_ref, v_ref, lam_ref, sg_ref, o_ref, vt_ref, *, seq, tq, chunk, lambda_init):
    j = pl.program_id(2)

    @pl.when(j == 0)
    def _():
        for i in range(seq // tq):
            vt_ref[i] = _transposed_values(v_ref[0, i * tq:(i + 1) * tq, :], 0)

    lv = lam_ref[...]
    lam = (jnp.exp(jnp.sum(lv[0:1] * lv[1:2], axis=-1, keepdims=True))
           - jnp.exp(jnp.sum(lv[2:3] * lv[3:4], axis=-1, keepdims=True)) + lambda_init)
    q = q_ref[0]
    qs = [q[:, c * HEAD_DIM:(c + 1) * HEAD_DIM] for c in range(2)]
    tri = jnp.where(lax.broadcasted_iota(jnp.int32, (tq, tq), 0) <= lax.broadcasted_iota(jnp.int32, (tq, tq), 1),
                    0.0, NEG)

    def attend(v):
        spans = [(v * tq, tq, tri)]
        c0 = 0
        while c0 < v * tq:
            n = min(chunk, v * tq - c0)
            spans.append((c0, n, None))
            c0 += n
        specs = [(c, s) for s in spans for c in range(2)]
        scores = lambda c, s: _dot_nt(k_ref[0, s[0]:s[0] + s[1], c * HEAD_DIM:(c + 1) * HEAD_DIM], qs[c])
        pieces = ([], [])
        ahead = 2
        sts = [scores(*spec) for spec in specs[:ahead]]
        for i, (c, s) in enumerate(specs):
            if i + ahead < len(specs):
                sts.append(scores(*specs[i + ahead]))
            vt = jnp.concatenate([vt_ref[s[0] // tq + t] for t in range(s[1] // tq)], axis=1)
            pieces[c].append(_softmax_piece(sts[i], vt, 2 * HEAD_DIM, s[2]))
        return _merge_pieces(pieces[0]) - lam * _merge_pieces(pieces[1])

    for v in range(seq // tq):
        @pl.when(j == v)
        def _(v=v):
            o = attend(v).T
            o_ref[0] = (_rms(o, sg_ref[...]) * (1.0 - lambda_init)).astype(BF16)


def _diff_attention(proj, lam, subln, off, batch, seq, heads, lambda_init):
    vw = 2 * HEAD_DIM
    tq = _row_tile(seq, 512)
    qb, kb, vb = off["q_d"] // vw, off["k_d"] // vw, off["v_d"] // vw
    chunk = tq
    return pl.pallas_call(
        functools.partial(_diff_kernel, seq=seq, tq=tq, chunk=chunk, lambda_init=lambda_init),
        grid=(batch, heads, seq // tq),
        scratch_shapes=[pltpu.VMEM((seq // tq, vw, tq), BF16)],
        in_specs=[
            pl.BlockSpec((1, tq, vw), lambda b, h, j: (b, j, qb + h)),
            pl.BlockSpec((1, seq, vw), lambda b, h, j: (b, 0, kb + h)),
            pl.BlockSpec((1, seq, vw), lambda b, h, j: (b, 0, vb + h)),
            pl.BlockSpec((4, HEAD_DIM), lambda b, h, j: (0, 0)),
            pl.BlockSpec((1, vw), lambda b, h, j: (0, 0)),
        ],
        out_specs=pl.BlockSpec((1, tq, vw), lambda b, h, j: (b, j, h)),
        out_shape=jax.ShapeDtypeStruct((batch, seq, heads * vw), BF16),
        compiler_params=_params("parallel", "parallel", "arbitrary"),
        name="diff_attention",
    )(proj, proj, proj, lam, subln)


def _out_kernel(x_ref, a_ref, b_ref, wa_ref, wb_ref, o_ref):
    o_ref[...] = x_ref[...] + _dot(a_ref[...], wa_ref[...]) + _dot(b_ref[...], wb_ref[...])


def _out_project(x, a, b, wa, wb):
    n, d = x.shape
    ka, kb = a.shape[1], b.shape[1]
    tm = _row_tile(n, 512)
    tn = LANES * _largest_divisor(d // LANES, 16)
    return pl.pallas_call(
        _out_kernel,
        grid=(n // tm, d // tn),
        in_specs=[
            pl.BlockSpec((tm, tn), lambda i, j: (i, j)),
            pl.BlockSpec((tm, ka), lambda i, j: (i, 0)),
            pl.BlockSpec((tm, kb), lambda i, j: (i, 0)),
            pl.BlockSpec((ka, tn), lambda i, j: (0, j)),
            pl.BlockSpec((kb, tn), lambda i, j: (0, j)),
        ],
        out_specs=pl.BlockSpec((tm, tn), lambda i, j: (i, j)),
        out_shape=jax.ShapeDtypeStruct((n, d), F32),
        compiler_params=_params("parallel", "arbitrary"),
        name="out_proj",
    )(x, a, b, wa, wb)


def _rope_tables(t):
    inv = 1.0 / (ROPE_THETA ** (jnp.arange(0, HEAD_DIM, 2, dtype=F32) / HEAD_DIM))
    ang = jnp.arange(t, dtype=F32)[:, None] * inv[None, :]
    ang = jnp.concatenate([ang, ang], axis=-1)
    sign = jnp.concatenate([-jnp.ones((HEAD_DIM // 2,), F32), jnp.ones((HEAD_DIM // 2,), F32)])
    return jnp.cos(ang), jnp.sin(ang) * sign[None, :]


def _layout(d_model):
    nsa_heads = d_model // (2 * HEAD_DIM)
    diff_heads = d_model // (4 * HEAD_DIM)
    kv = NSA_KV_HEADS * HEAD_DIM
    sizes = dict(q_n=nsa_heads * HEAD_DIM, kc=kv, vc=kv, ks=kv, vs=kv, kw=kv, vw=kv, gates=3 * nsa_heads,
                 q_d=2 * diff_heads * HEAD_DIM, k_d=2 * diff_heads * HEAD_DIM, v_d=diff_heads * 2 * HEAD_DIM)
    orig, o = {}, 0
    for name in ("q_n", "kc", "vc", "ks", "vs", "kw", "vw", "gates", "q_d", "k_d", "v_d"):
        orig[name] = o
        o += sizes[name]
    rope_names = ("q_n", "kc", "ks", "kw", "q_d", "k_d")
    plain_names = ("vc", "vs", "vw", "v_d")
    new, o = {}, 0
    for name in rope_names + plain_names:
        new[name] = o
        o += sizes[name]
    n_rope_groups = sum(sizes[nm] for nm in rope_names) // LANES
    return sizes, orig, new, rope_names + plain_names, n_rope_groups, nsa_heads, diff_heads


def kernel(x, ffn1_norm, ffn1_w_gate, ffn1_w_up, ffn1_w_down, mix_norm, w_in, cmp_pos_k, cmp_pos_v, cmp_wk1, cmp_wk2, cmp_wv1, cmp_wv2, lam_q1, lam_k1, lam_q2, lam_k2, diff_subln, w_out, ffn2_norm, ffn2_w_gate, ffn2_w_up, ffn2_w_down, final_norm):
    batch, seq, d_model = x.shape
    depth = ffn1_norm.shape[0]
    n = batch * seq
    sizes, orig, off, order, n_rope_groups, nsa_heads, diff_heads = _layout(d_model)
    g = NSA_KV_HEADS
    rep = nsa_heads // g
    cos, sin_signed = _rope_tables(seq)
    fg = final_norm.reshape(1, d_model)
    ffn1_stacked = (ffn1_w_gate, ffn1_w_up, ffn1_w_down)
    ffn2_stacked = (ffn2_w_gate, ffn2_w_up, ffn2_w_down)
    w_ffn = tuple(w[0].astype(BF16) for w in ffn1_stacked)

    xf = x.reshape(n, d_model)
    for l in range(depth):
        lambda_init = 0.8 - 0.6 * math.exp(-0.3 * l)
        xf, w_ffn = _ffn(xf, ffn1_norm[l].reshape(1, d_model), *w_ffn, fg, False, (ffn2_stacked, l))

        wl = w_in[l]
        w_main = jnp.concatenate([wl[:, orig[nm]:orig[nm] + sizes[nm]] for nm in order], axis=1).astype(BF16)
        gate_tiles = []
        for h in range(g):
            cols = wl[:, orig["gates"] + h * 3 * rep: orig["gates"] + (h + 1) * 3 * rep]
            gate_tiles.append(jnp.pad(cols, ((0, 0), (0, LANES - 3 * rep))))
        w_gate = jnp.concatenate(gate_tiles, axis=1).astype(BF16)
        query_groups = tuple((off[nm] // LANES, (off[nm] + sizes[nm]) // LANES) for nm in ("q_n", "q_d"))
        proj, gates = _project(xf, mix_norm[l].reshape(1, d_model), w_main, w_gate, cos, sin_signed, n_rope_groups,
                               query_groups)
        proj3 = proj.reshape(batch, seq, -1)
        gates3 = gates.reshape(batch, seq, -1)

        kc, vc = _compress(proj3, off, batch, seq,
                           cmp_pos_k[l].reshape(1, -1).astype(BF16), cmp_pos_v[l].reshape(1, -1).astype(BF16),
                           cmp_wk1[l].astype(BF16), cmp_wk2[l].astype(BF16),
                           cmp_wv1[l].astype(BF16), cmp_wv2[l].astype(BF16))
        o_nsa = _nsa_attention(proj3, gates3, kc, vc, off, batch, seq, rep)
        lam = jnp.stack([lam_q1[l], lam_k1[l], lam_q2[l], lam_k2[l]])
        o_diff = _diff_attention(proj3, lam, diff_subln[l].reshape(1, -1), off, batch, seq, diff_heads, lambda_init)

        half = sizes["q_n"]
        wo = w_out[l].astype(BF16)
        xf = _out_project(xf, o_nsa.reshape(n, -1), o_diff.reshape(n, -1), wo[:half], wo[half:])

        last = l == depth - 1
        xf, w_ffn = _ffn(xf, ffn2_norm[l].reshape(1, d_model), *w_ffn, fg, last,
                         None if last else (ffn1_stacked, l + 1))
    return xf.reshape(batch, seq, d_model)
```

```python
import functools
import math

import numpy as np
import jax
import jax.numpy as jnp
from jax import lax
from jax.experimental import pallas as pl
from jax.experimental.pallas import tpu as pltpu

HEAD_DIM = 128
NSA_KV_HEADS = 2
CMP_LEN = 32
CMP_STRIDE = 16
SLC_LEN = 64
SLC_TOPK = 16
WINDOW = 512
Q_BLOCK = 128
ROPE_THETA = 10000.0
EPS = 1e-6
NEG = -1e30
MASK_FLOOR = -1e20
LOG2E = 1.4426950408889634
QK_SCALE = HEAD_DIM ** -0.5 * LOG2E
SLC_VARIANT_LEN = 512
NSA_BLOCKS_PER_STEP = 4
LANES = 128
VMEM_LIMIT = 58 * 1024 * 1024

F32 = jnp.float32
BF16 = jnp.bfloat16


def _largest_divisor(n, cap):
    for d in range(min(n, cap), 0, -1):
        if n % d == 0:
            return d
    return 1


def _row_tile(n, cap):
    for d in range(min(n, cap), 7, -1):
        if n % d == 0 and d % 8 == 0:
            return d
    return n


def _rms(x, g):
    return x * lax.rsqrt(jnp.mean(x * x, axis=-1, keepdims=True) + EPS) * g


def _dot(a, b):
    return jnp.dot(a, b, preferred_element_type=F32)


def _dot_nt(a, b):
    return lax.dot_general(a, b, (((1,), (1,)), ((), ())), preferred_element_type=F32)


def _params(*sem):
    return pltpu.CompilerParams(dimension_semantics=sem, vmem_limit_bytes=VMEM_LIMIT)


def _ffn_kernel(*refs, final_norm, cast_next):
    if cast_next:
        (x_ref, g_ref, wg_ref, wu_ref, wd_ref, fg_ref, ng_ref, nu_ref, nd_ref,
         o_ref, og_ref, ou_ref, od_ref, h_ref) = refs
        og_ref[...] = ng_ref[...].astype(BF16)
        ou_ref[...] = nu_ref[...].astype(BF16)
        od_ref[...] = nd_ref[...].astype(BF16)
    else:
        x_ref, g_ref, wg_ref, wu_ref, wd_ref, fg_ref, o_ref, h_ref = refs
    j = pl.program_id(1)

    @pl.when(j == 0)
    def _():
        x = x_ref[...]
        h_ref[...] = _rms(x, g_ref[...]).astype(BF16)
        o_ref[...] = x

    h = h_ref[...]
    a = _dot(h, wg_ref[...])
    b = _dot(h, wu_ref[...])
    act = (a * jax.nn.sigmoid(a) * b * 0.5).astype(BF16)
    o_ref[...] += _dot(act, wd_ref[...])

    if final_norm:
        @pl.when(j == pl.num_programs(1) - 1)
        def _():
            o_ref[...] = _rms(o_ref[...], fg_ref[...])


def _ffn(x, g, wg, wu, wd, fg, final_norm, next_weights=None):
    n, d = x.shape
    f = wg.shape[1]
    tm = _row_tile(n, 1024)
    tf = LANES * _largest_divisor(f // LANES, 4)
    n_i, n_j = n // tm, f // tf
    in_specs = [
        pl.BlockSpec((tm, d), lambda i, j: (i, 0)),
        pl.BlockSpec((1, d), lambda i, j: (0, 0)),
        pl.BlockSpec((d, tf), lambda i, j: (0, j)),
        pl.BlockSpec((d, tf), lambda i, j: (0, j)),
        pl.BlockSpec((tf, d), lambda i, j: (j, 0)),
        pl.BlockSpec((1, d), lambda i, j: (0, 0)),
    ]
    out_specs = [pl.BlockSpec((tm, d), lambda i, j: (i, 0))]
    out_shape = [jax.ShapeDtypeStruct((n, d), F32)]
    args = [x, g, wg, wu, wd, fg]
    if next_weights is not None:
        stacked, layer = next_weights
        dr, fr = d // n_i, tf // n_i
        assert dr * n_i == d and fr * n_i == tf and dr % 16 == 0 and fr % 16 == 0
        assert all(w.shape[1:] == s for w, s in zip(stacked, ((d, f), (d, f), (f, d))))
        in_specs += [
            pl.BlockSpec((None, dr, tf), lambda i, j: (layer, i, j)),
            pl.BlockSpec((None, dr, tf), lambda i, j: (layer, i, j)),
            pl.BlockSpec((None, fr, d), lambda i, j: (layer, j * n_i + i, 0)),
        ]
        out_specs += [
            pl.BlockSpec((dr, tf), lambda i, j: (i, j)),
            pl.BlockSpec((dr, tf), lambda i, j: (i, j)),
            pl.BlockSpec((fr, d), lambda i, j: (j * n_i + i, 0)),
        ]
        out_shape += [jax.ShapeDtypeStruct(w.shape[1:], BF16) for w in stacked]
        args += list(stacked)
    outs = pl.pallas_call(
        functools.partial(_ffn_kernel, final_norm=final_norm, cast_next=next_weights is not None),
        grid=(n_i, n_j),
        in_specs=in_specs,
        out_specs=out_specs,
        out_shape=out_shape,
        scratch_shapes=[pltpu.VMEM((tm, d), BF16)],
        compiler_params=_params("parallel", "arbitrary"),
        name="ffn",
    )(*args)
    return outs[0], tuple(outs[1:])


def _proj_kernel(x_ref, g_ref, w_ref, wgate_ref, cos_ref, sin_ref, o_ref, gate_ref, h_ref,
                 *, groups_per_tile, n_rope_groups, query_groups):
    j = pl.program_id(1)

    @pl.when(j == 0)
    def _():
        h = _rms(x_ref[...], g_ref[...]).astype(BF16)
        h_ref[...] = h
        gate_ref[...] = _dot(h, wgate_ref[...])

    acc = _dot(h_ref[...], w_ref[...])
    cos = cos_ref[...]
    sin = sin_ref[...]
    for gi in range(groups_per_tile):
        group = j * groups_per_tile + gi
        is_rope = group < n_rope_groups
        is_query = functools.reduce(jnp.logical_or, [(group >= lo) & (group < hi) for lo, hi in query_groups])
        c = jnp.where(is_rope, cos, 1.0)
        s = jnp.where(is_rope, sin, 0.0)
        xg = acc[:, gi * LANES:(gi + 1) * LANES]
        y = xg * c + pltpu.roll(xg, HEAD_DIM // 2, axis=1) * s
        y = y * jnp.where(is_query, QK_SCALE, 1.0)
        o_ref[:, gi * LANES:(gi + 1) * LANES] = y.astype(BF16)


def _project(x, g, w_main, w_gate, cos, sin_signed, n_rope_groups, query_groups):
    n, d = x.shape
    t = cos.shape[0]
    width = w_main.shape[1]
    gw = w_gate.shape[1]
    n_groups = width // LANES
    gpt = _largest_divisor(n_groups, 11)
    tn = gpt * LANES
    tm = _row_tile(t, 1024)
    t_blocks = t // tm
    return pl.pallas_call(
        functools.partial(_proj_kernel, groups_per_tile=gpt, n_rope_groups=n_rope_groups,
                          query_groups=query_groups),
        grid=(n // tm, width // tn),
        in_specs=[
            pl.BlockSpec((tm, d), lambda i, j: (i, 0)),
            pl.BlockSpec((1, d), lambda i, j: (0, 0)),
            pl.BlockSpec((d, tn), lambda i, j: (0, j)),
            pl.BlockSpec((d, gw), lambda i, j: (0, 0)),
            pl.BlockSpec((tm, HEAD_DIM), lambda i, j: (i % t_blocks, 0)),
            pl.BlockSpec((tm, HEAD_DIM), lambda i, j: (i % t_blocks, 0)),
        ],
        out_specs=[
            pl.BlockSpec((tm, tn), lambda i, j: (i, j)),
            pl.BlockSpec((tm, gw), lambda i, j: (i, 0)),
        ],
        out_shape=[
            jax.ShapeDtypeStruct((n, width), BF16),
            jax.ShapeDtypeStruct((n, gw), F32),
        ],
        scratch_shapes=[pltpu.VMEM((tm, d), BF16)],
        compiler_params=_params("parallel", "arbitrary"),
        name="in_proj",
    )(x, g, w_main, w_gate, cos, sin_signed)


def _cmp_kernel(k_ref, v_ref, pk_ref, pv_ref, wk1_ref, wk2_ref, wv1_ref, wv2_ref, ok_ref, ov_ref, x_ref):
    seq = k_ref.shape[1]
    n_chunks = seq // CMP_STRIDE

    def one(kv_ref, p_ref, w1_ref, w2_ref, o_ref):
        x_ref[...] = kv_ref[0].astype(F32)
        a = jnp.zeros((n_chunks, w1_ref.shape[1]), F32)
        b = jnp.zeros((n_chunks, w1_ref.shape[1]), F32)
        for l in range(CMP_STRIDE):
            rows = x_ref[pl.ds(l, n_chunks, stride=CMP_STRIDE), :].astype(BF16)
            a = a + _dot(rows, w1_ref[l * HEAD_DIM:(l + 1) * HEAD_DIM, :])
            b = b + _dot(rows, w1_ref[(CMP_STRIDE + l) * HEAD_DIM:(CMP_STRIDE + l + 1) * HEAD_DIM, :])
        p = jnp.broadcast_to(p_ref[...], (8, CMP_LEN * HEAD_DIM))
        bias = _dot(p, w1_ref[...])[0:1]
        pre = a + pltpu.roll(b, n_chunks - 1, axis=0) + bias
        hid = (pre * jax.nn.sigmoid(pre)).astype(BF16)
        o_ref[0, 0] = _dot(hid, w2_ref[...]).astype(BF16)

    one(k_ref, pk_ref, wk1_ref, wk2_ref, ok_ref)
    one(v_ref, pv_ref, wv1_ref, wv2_ref, ov_ref)


def _compress(proj, off, batch, seq, pk, pv, wk1, wk2, wv1, wv2):
    g = NSA_KV_HEADS
    nc = seq // CMP_STRIDE
    hid = wk1.shape[1]
    dk = wk2.shape[1]
    assert CMP_LEN == 2 * CMP_STRIDE and dk == HEAD_DIM
    kb, vb = off["kc"] // HEAD_DIM, off["vc"] // HEAD_DIM
    full = lambda shape: pl.BlockSpec(shape, lambda i, j: tuple(0 for _ in shape))
    out_spec = pl.BlockSpec((1, 1, nc, dk), lambda i, j: (i, j, 0, 0))
    return pl.pallas_call(
        _cmp_kernel,
        grid=(batch, g),
        in_specs=[pl.BlockSpec((1, seq, HEAD_DIM), lambda i, j: (i, 0, kb + j)),
                  pl.BlockSpec((1, seq, HEAD_DIM), lambda i, j: (i, 0, vb + j)),
                  full((1, CMP_LEN * dk)), full((1, CMP_LEN * dk)),
                  full((CMP_LEN * dk, hid)), full((hid, dk)), full((CMP_LEN * dk, hid)), full((hid, dk))],
        out_specs=[out_spec, out_spec],
        out_shape=[jax.ShapeDtypeStruct((batch, g, nc, dk), BF16)] * 2,
        scratch_shapes=[pltpu.VMEM((seq, HEAD_DIM), F32)],
        compiler_params=_params("parallel", "parallel"),
        name="nsa_compress",
    )(proj, proj, pk, pv, wk1, wk2, wv1, wv2)


ONES_ROWS = 16


def _transposed_values(v, ones_rows):
    vt = v.astype(F32).T
    if ones_rows:
        vt = jnp.concatenate([vt, jnp.ones((ones_rows, vt.shape[1]), F32)], axis=0)
    return vt.astype(BF16)


def _softmax_piece(st, vt, dv, elem_bias=None, blk_bias=None):
    nk, nc = st.shape
    if elem_bias is not None:
        st = st + elem_bias
    nb = 1 if blk_bias is None else blk_bias.shape[0]
    s4 = st.reshape(nb, nk // nb // 8, 8, nc)
    bm = jnp.max(s4, axis=1)
    if blk_bias is not None:
        bm = bm + blk_bias[:, None, :]
    m = jnp.max(jnp.max(bm, axis=0), axis=0, keepdims=True)
    m = jnp.maximum(m, MASK_FLOOR)
    shift = -m if blk_bias is None else blk_bias - m
    e = jnp.exp2(s4 + shift[:, None, None, :])
    o = _dot(vt, e.reshape(nk, nc).astype(BF16))
    if vt.shape[0] > dv:
        return m, o[dv:dv + 1], o[0:dv]
    return m, jnp.sum(jnp.sum(jnp.sum(e, axis=0), axis=0), axis=0, keepdims=True), o


def _merge_pieces(pieces):
    m = functools.reduce(jnp.maximum, [p[0] for p in pieces])
    ws = [jnp.exp2(p[0] - m) for p in pieces]
    l = sum(w * p[1] for w, p in zip(ws, pieces))
    o = sum(w * p[2] for w, p in zip(ws, pieces))
    return o / l


def _interleave(gens):
    results = [None] * len(gens)
    live = list(range(len(gens)))
    while live:
        for i in list(live):
            try:
                next(gens[i])
            except StopIteration as stop:
                results[i] = stop.value
                live.remove(i)
    return results


def _nsa_block(j, q, gates, ks_ref, kw_ref, kc_ref, vc_ref, ovl_ref, vst_ref, vwt_ref,
               *, rep, n_slc, n_sel, var_len):
    s0 = j * Q_BLOCK
    q4 = jnp.concatenate([q[:, r * HEAD_DIM:(r + 1) * HEAD_DIM] for r in range(rep)], axis=0)
    t_row = s0 + lax.broadcasted_iota(jnp.int32, (1, Q_BLOCK), 1)
    head = lambda a, r: a[:, r * Q_BLOCK:(r + 1) * Q_BLOCK]

    nc = rep * Q_BLOCK
    kk = lax.broadcasted_iota(jnp.int32, (Q_BLOCK, Q_BLOCK), 0)
    qq = lax.broadcasted_iota(jnp.int32, (Q_BLOCK, Q_BLOCK), 1)
    tri = jnp.concatenate([jnp.where(kk <= qq, 0.0, NEG)] * rep, axis=1)
    anti = jnp.concatenate([jnp.where(kk > qq, 0.0, NEG)] * rep, axis=1)
    valid_row = lambda ok: jnp.where(ok, jnp.zeros((1, nc), F32), jnp.full((1, nc), NEG, F32))
    n_back = WINDOW // Q_BLOCK
    far = jnp.maximum(j - n_back, 0)
    mid = jnp.maximum(j - (n_back - 1), 0)
    mid_bias = jnp.concatenate([valid_row(mid + i < j) for i in range(n_back - 1)], axis=0)
    rows_of = lambda ref, blk0, n: ref[0, pl.ds(pl.multiple_of(blk0 * Q_BLOCK, Q_BLOCK), n * Q_BLOCK), :]
    tiles_of = lambda ref, blk0, n: jnp.concatenate([ref[blk0 + i] for i in range(n)], axis=1)
    early = [
        (rows_of(kw_ref, j, 1), lambda: vwt_ref[j], tri, None),
        (rows_of(kw_ref, far, 1), lambda: vwt_ref[far], anti, valid_row(j >= n_back)),
        (rows_of(kw_ref, mid, n_back - 1), lambda: tiles_of(vwt_ref, mid, n_back - 1), None, mid_bias),
        (rows_of(ks_ref, j, 1), lambda: vst_ref[j], tri, None),
    ]
    early_scores = [_dot_nt(k, q4) for k, _, _, _ in early]
    first_scores = _dot_nt(ks_ref[0, 0:var_len, :], q4)
    early_piece = lambda i: _softmax_piece(early_scores[i], early[i][1](), HEAD_DIM, early[i][2], early[i][3])

    kc = kc_ref[0, 0]
    n_rows = kc.shape[0]
    n_idx = lax.broadcasted_iota(jnp.int32, (n_rows, Q_BLOCK), 0)
    c_mask = (n_idx * CMP_STRIDE + (CMP_LEN - 1)) <= t_row
    st = _dot_nt(kc, q4)
    yield
    win_pieces = [early_piece(0), early_piece(1)]
    p_sum = jnp.zeros((n_rows, Q_BLOCK), F32)
    ps = []
    for r in range(rep):
        t = jnp.where(c_mask, head(st, r), NEG)
        e = jnp.where(c_mask, jnp.exp2(t - jnp.max(t, axis=0, keepdims=True)), 0.0)
        l = jnp.sum(e, axis=0, keepdims=True)
        p = e / jnp.where(l > 0.0, l, 1.0)
        p_sum = p_sum + p
        ps.append(p.astype(BF16))
    o_cmp = _dot(vc_ref[0, 0].astype(F32).T.astype(BF16), jnp.concatenate(ps, axis=1))

    hi = p_sum.astype(BF16)
    lo = (p_sum - hi.astype(F32)).astype(BF16)
    ovl = ovl_ref[...]
    n_pad = -(-n_slc // 8) * 8
    imp = (_dot(ovl, hi) + _dot(ovl, lo))[0:n_pad]
    yield
    win_pieces.append(early_piece(2))
    slc_diag = early_piece(3)
    blk = lax.broadcasted_iota(jnp.int32, (n_pad, Q_BLOCK), 0)
    cur = t_row // SLC_LEN
    forced = (blk == 0) | (blk == cur) | (blk == cur - 1)
    blk_causal = blk * SLC_LEN <= t_row
    imp = jnp.where(forced, 1e4, imp)
    imp = jnp.where(blk_causal, imp, -1.0)
    rank = jnp.zeros((n_pad, Q_BLOCK), F32)
    for sp in range(n_slc):
        row = imp[sp:sp + 1, :]
        ge = jnp.where(row >= imp, 1.0, 0.0)
        gt = jnp.where(row > imp, 1.0, 0.0)
        rank = rank + jnp.where(blk > sp, ge, gt)
    blocks_per_q = Q_BLOCK // SLC_LEN
    blk_bias = jnp.where((rank < float(n_sel)) & (blk < j * blocks_per_q), 0.0, NEG)
    blk_bias = jnp.concatenate([blk_bias] * rep, axis=1)
    yield

    gt = jax.nn.sigmoid(gates).T
    gate = lambda branch: jnp.concatenate([jnp.broadcast_to(gt[3 * r + branch:3 * r + branch + 1], (HEAD_DIM, Q_BLOCK))
                                           for r in range(rep)], axis=1)
    partial = gate(0) * o_cmp + gate(2) * _merge_pieces(win_pieces)
    q_per_var = var_len // Q_BLOCK
    blk_per_var = var_len // SLC_LEN
    scores = lambda ci: _dot_nt(ks_ref[0, ci * var_len:(ci + 1) * var_len, :], q4)
    chunk_piece = lambda ci, st: _softmax_piece(st, tiles_of(vst_ref, ci * q_per_var, q_per_var), HEAD_DIM,
                                                None, blk_bias[ci * blk_per_var:(ci + 1) * blk_per_var])
    first_chunk = chunk_piece(0, first_scores)
    return dict(partial=partial, g_slc=gate(1), pieces=[slc_diag, first_chunk], scores=scores, chunk_piece=chunk_piece)


def _nsa_selected(ctx, n_chunks):
    pieces = list(ctx["pieces"])
    ahead = 2
    sts = {ci: ctx["scores"](ci) for ci in range(1, min(1 + ahead, n_chunks))}
    for ci in range(1, n_chunks):
        if ci + ahead < n_chunks:
            sts[ci + ahead] = ctx["scores"](ci + ahead)
        pieces.append(ctx["chunk_piece"](ci, sts[ci]))
        yield
    return ctx["partial"] + ctx["g_slc"] * _merge_pieces(pieces)


def _nsa_kernel(q_ref, ks_ref, vs_ref, kw_ref, vw_ref, kc_ref, vc_ref, gate_ref, ovl_ref,
                o_ref, vst_ref, vwt_ref, *, seq, rep, n_slc, n_sel, var_len, blocks):
    jj = pl.program_id(2)

    @pl.when(jj == 0)
    def _():
        for i in range(seq // Q_BLOCK):
            vst_ref[i] = _transposed_values(vs_ref[0, i * Q_BLOCK:(i + 1) * Q_BLOCK, :], ONES_ROWS)
            vwt_ref[i] = _transposed_values(vw_ref[0, i * Q_BLOCK:(i + 1) * Q_BLOCK, :], ONES_ROWS)

    rows = lambda h: slice(h * Q_BLOCK, (h + 1) * Q_BLOCK)
    ctxs = _interleave([
        _nsa_block(jj * blocks + h, q_ref[0, rows(h), :], gate_ref[0, rows(h), :], ks_ref, kw_ref, kc_ref, vc_ref,
                   ovl_ref, vst_ref, vwt_ref, rep=rep, n_slc=n_slc, n_sel=n_sel, var_len=var_len)
        for h in range(blocks)])

    steps_per_var = var_len // Q_BLOCK // blocks
    for v in range(seq // var_len):
        @pl.when(jj // steps_per_var == v)
        def _(n_chunks=v + 1):
            outs = _interleave([_nsa_selected(ctx, n_chunks) for ctx in ctxs])
            for h, out in enumerate(outs):
                for r in range(rep):
                    o_ref[0, rows(h), r * HEAD_DIM:(r + 1) * HEAD_DIM] = (
                        out[:, r * Q_BLOCK:(r + 1) * Q_BLOCK].T.astype(BF16))


def _block_overlap(n_rows, n_cmp, n_slc):
    c0 = np.arange(n_cmp) * CMP_STRIDE
    s0 = np.arange(n_slc) * SLC_LEN
    lo = np.maximum(c0[None, :], s0[:, None])
    hi = np.minimum(c0[None, :] + CMP_LEN, s0[:, None] + SLC_LEN)
    out = np.zeros((LANES, n_rows), np.float32)
    out[:n_slc, :n_cmp] = np.clip(hi - lo, 0, None) / CMP_LEN
    return out


def _nsa_attention(proj, gates, kc, vc, off, batch, seq, rep):
    g = NSA_KV_HEADS
    n_qb = seq // Q_BLOCK
    n_slc = seq // SLC_LEN
    n_sel = min(SLC_TOPK, n_slc)
    n_rows = kc.shape[2]
    n_cmp = (seq - CMP_LEN) // CMP_STRIDE + 1
    qw = rep * HEAD_DIM
    var_len = SLC_VARIANT_LEN if seq % SLC_VARIANT_LEN == 0 else seq
    blocks = NSA_BLOCKS_PER_STEP
    assert (var_len // Q_BLOCK) % blocks == 0
    ovl = jnp.asarray(_block_overlap(n_rows, n_cmp, n_slc), BF16)

    def kv_spec(name):
        base = off[name] // HEAD_DIM
        return pl.BlockSpec((1, seq, HEAD_DIM), lambda b, h, j: (b, 0, base + h))

    cmp_spec = pl.BlockSpec((1, 1, n_rows, HEAD_DIM), lambda b, h, j: (b, h, 0, 0))
    q_base = off["q_n"] // qw
    return pl.pallas_call(
        functools.partial(_nsa_kernel, seq=seq, rep=rep, n_slc=n_slc, n_sel=n_sel, var_len=var_len, blocks=blocks),
        grid=(batch, g, n_qb // blocks),
        scratch_shapes=[
            pltpu.VMEM((seq // Q_BLOCK, HEAD_DIM + ONES_ROWS, Q_BLOCK), BF16),
            pltpu.VMEM((seq // Q_BLOCK, HEAD_DIM + ONES_ROWS, Q_BLOCK), BF16),
        ],
        in_specs=[
            pl.BlockSpec((1, blocks * Q_BLOCK, qw), lambda b, h, j: (b, j, q_base + h)),
            kv_spec("ks"), kv_spec("vs"), kv_spec("kw"), kv_spec("vw"),
            cmp_spec, cmp_spec,
            pl.BlockSpec((1, blocks * Q_BLOCK, LANES), lambda b, h, j: (b, j, h)),
            pl.BlockSpec((LANES, n_rows), lambda b, h, j: (0, 0)),
        ],
        out_specs=pl.BlockSpec((1, blocks * Q_BLOCK, qw), lambda b, h, j: (b, j, h)),
        out_shape=jax.ShapeDtypeStruct((batch, seq, g * qw), BF16),
        compiler_params=_params("parallel", "parallel", "arbitrary"),
        name="nsa_attention",
    )(proj, proj, proj, proj, proj, kc, vc, gates, ovl)


def _diff_block(v, q, k_ref, vts, tri, lam, tq):
    qs = [q[:, c * HEAD_DIM:(c + 1) * HEAD_DIM] for c in range(2)]
    specs = [(c, kb) for kb in [v] + list(range(v)) for c in range(2)]
    scores = lambda c, kb: _dot_nt(k_ref[0, kb * tq:(kb + 1) * tq, c * HEAD_DIM:(c + 1) * HEAD_DIM], qs[c])
    pieces = ([], [])
    ahead = 2
    sts = [scores(*spec) for spec in specs[:ahead]]
    for i, (c, kb) in enumerate(specs):
        if i + ahead < len(specs):
            sts.append(scores(*specs[i + ahead]))
        pieces[c].append(_softmax_piece(sts[i], vts[kb], 2 * HEAD_DIM, tri if kb == v else None))
        yield
    return _merge_pieces(pieces[0]) - lam * _merge_pieces(pieces[1])


def _diff_kernel(q_ref, k_ref, v_ref, lam_ref, sg_ref, o_ref, *, seq, tq, lambda_init):
    n_blocks = seq // tq
    rows = lambda i: slice(i * tq, (i + 1) * tq)
    vts = [_transposed_values(v_ref[0, rows(i), :], 0) for i in range(n_blocks)]
    lv = lam_ref[...]
    lam = (jnp.exp(jnp.sum(lv[0:1] * lv[1:2], axis=-1, keepdims=True))
           - jnp.exp(jnp.sum(lv[2:3] * lv[3:4], axis=-1, keepdims=True)) + lambda_init)
    tri = jnp.where(lax.broadcasted_iota(jnp.int32, (tq, tq), 0) <= lax.broadcasted_iota(jnp.int32, (tq, tq), 1),
                    0.0, NEG)
    outs = _interleave([_diff_block(v, q_ref[0, rows(v), :], k_ref, vts, tri, lam, tq) for v in range(n_blocks)])
    for v, o in enumerate(outs):
        o_ref[0, rows(v), :] = (_rms(o.T, sg_ref[...]) * (1.0 - lambda_init)).astype(BF16)


def _diff_attention(proj, lam, subln, off, batch, seq, heads, lambda_init):
    vw = 2 * HEAD_DIM
    tq = _row_tile(seq, 512)
    qb, kb, vb = off["q_d"] // vw, off["k_d"] // vw, off["v_d"] // vw
    return pl.pallas_call(
        functools.partial(_diff_kernel, seq=seq, tq=tq, lambda_init=lambda_init),
        grid=(batch, heads),
        in_specs=[
            pl.BlockSpec((1, seq, vw), lambda b, h: (b, 0, qb + h)),
            pl.BlockSpec((1, seq, vw), lambda b, h: (b, 0, kb + h)),
            pl.BlockSpec((1, seq, vw), lambda b, h: (b, 0, vb + h)),
            pl.BlockSpec((4, HEAD_DIM), lambda b, h: (0, 0)),
            pl.BlockSpec((1, vw), lambda b, h: (0, 0)),
        ],
        out_specs=pl.BlockSpec((1, seq, vw), lambda b, h: (b, 0, h)),
        out_shape=jax.ShapeDtypeStruct((batch, seq, heads * vw), BF16),
        compiler_params=_params("parallel", "parallel"),
        name="diff_attention",
    )(proj, proj, proj, lam, subln)


def _out_kernel(x_ref, a_ref, b_ref, wa_ref, wb_ref, o_ref):
    o_ref[...] = x_ref[...] + _dot(a_ref[...], wa_ref[...]) + _dot(b_ref[...], wb_ref[...])


def _out_project(x, a, b, wa, wb):
    n, d = x.shape
    ka, kb = a.shape[1], b.shape[1]
    tm = _row_tile(n, 512)
    tn = LANES * _largest_divisor(d // LANES, 16)
    return pl.pallas_call(
        _out_kernel,
        grid=(n // tm, d // tn),
        in_specs=[
            pl.BlockSpec((tm, tn), lambda i, j: (i, j)),
            pl.BlockSpec((tm, ka), lambda i, j: (i, 0)),
            pl.BlockSpec((tm, kb), lambda i, j: (i, 0)),
            pl.BlockSpec((ka, tn), lambda i, j: (0, j)),
            pl.BlockSpec((kb, tn), lambda i, j: (0, j)),
        ],
        out_specs=pl.BlockSpec((tm, tn), lambda i, j: (i, j)),
        out_shape=jax.ShapeDtypeStruct((n, d), F32),
        compiler_params=_params("parallel", "arbitrary"),
        name="out_proj",
    )(x, a, b, wa, wb)


def _rope_tables(t):
    inv = 1.0 / (ROPE_THETA ** (jnp.arange(0, HEAD_DIM, 2, dtype=F32) / HEAD_DIM))
    ang = jnp.arange(t, dtype=F32)[:, None] * inv[None, :]
    ang = jnp.concatenate([ang, ang], axis=-1)
    sign = jnp.concatenate([-jnp.ones((HEAD_DIM // 2,), F32), jnp.ones((HEAD_DIM // 2,), F32)])
    return jnp.cos(ang), jnp.sin(ang) * sign[None, :]


def _layout(d_model):
    nsa_heads = d_model // (2 * HEAD_DIM)
    diff_heads = d_model // (4 * HEAD_DIM)
    kv = NSA_KV_HEADS * HEAD_DIM
    sizes = dict(q_n=nsa_heads * HEAD_DIM, kc=kv, vc=kv, ks=kv, vs=kv, kw=kv, vw=kv, gates=3 * nsa_heads,
                 q_d=2 * diff_heads * HEAD_DIM, k_d=2 * diff_heads * HEAD_DIM, v_d=diff_heads * 2 * HEAD_DIM)
    orig, o = {}, 0
    for name in ("q_n", "kc", "vc", "ks", "vs", "kw", "vw", "gates", "q_d", "k_d", "v_d"):
        orig[name] = o
        o += sizes[name]
    rope_names = ("q_n", "kc", "ks", "kw", "q_d", "k_d")
    plain_names = ("vc", "vs", "vw", "v_d")
    new, o = {}, 0
    for name in rope_names + plain_names:
        new[name] = o
        o += sizes[name]
    n_rope_groups = sum(sizes[nm] for nm in rope_names) // LANES
    return sizes, orig, new, rope_names + plain_names, n_rope_groups, nsa_heads, diff_heads


def kernel(x, ffn1_norm, ffn1_w_gate, ffn1_w_up, ffn1_w_down, mix_norm, w_in, cmp_pos_k, cmp_pos_v, cmp_wk1, cmp_wk2, cmp_wv1, cmp_wv2, lam_q1, lam_k1, lam_q2, lam_k2, diff_subln, w_out, ffn2_norm, ffn2_w_gate, ffn2_w_up, ffn2_w_down, final_norm):
    batch, seq, d_model = x.shape
    depth = ffn1_norm.shape[0]
    n = batch * seq
    sizes, orig, off, order, n_rope_groups, nsa_heads, diff_heads = _layout(d_model)
    g = NSA_KV_HEADS
    rep = nsa_heads // g
    cos, sin_signed = _rope_tables(seq)
    fg = final_norm.reshape(1, d_model)
    ffn1_stacked = (ffn1_w_gate, ffn1_w_up, ffn1_w_down)
    ffn2_stacked = (ffn2_w_gate, ffn2_w_up, ffn2_w_down)
    w_ffn = tuple(w[0].astype(BF16) for w in ffn1_stacked)

    xf = x.reshape(n, d_model)
    for l in range(depth):
        lambda_init = 0.8 - 0.6 * math.exp(-0.3 * l)
        xf, w_ffn = _ffn(xf, ffn1_norm[l].reshape(1, d_model), *w_ffn, fg, False, (ffn2_stacked, l))

        wl = w_in[l]
        w_main = jnp.concatenate([wl[:, orig[nm]:orig[nm] + sizes[nm]] for nm in order], axis=1).astype(BF16)
        gate_tiles = []
        for h in range(g):
            cols = wl[:, orig["gates"] + h * 3 * rep: orig["gates"] + (h + 1) * 3 * rep]
            gate_tiles.append(jnp.pad(cols, ((0, 0), (0, LANES - 3 * rep))))
        w_gate = jnp.concatenate(gate_tiles, axis=1).astype(BF16)
        query_groups = tuple((off[nm] // LANES, (off[nm] + sizes[nm]) // LANES) for nm in ("q_n", "q_d"))
        proj, gates = _project(xf, mix_norm[l].reshape(1, d_model), w_main, w_gate, cos, sin_signed, n_rope_groups,
                               query_groups)
        proj3 = proj.reshape(batch, seq, -1)
        gates3 = gates.reshape(batch, seq, -1)

        kc, vc = _compress(proj3, off, batch, seq,
                           cmp_pos_k[l].reshape(1, -1).astype(BF16), cmp_pos_v[l].reshape(1, -1).astype(BF16),
                           cmp_wk1[l].astype(BF16), cmp_wk2[l].astype(BF16),
                           cmp_wv1[l].astype(BF16), cmp_wv2[l].astype(BF16))
        o_nsa = _nsa_attention(proj3, gates3, kc, vc, off, batch, seq, rep)
        lam = jnp.stack([lam_q1[l], lam_k1[l], lam_q2[l], lam_k2[l]])
        o_diff = _diff_attention(proj3, lam, diff_subln[l].reshape(1, -1), off, batch, seq, diff_heads, lambda_init)

        half = sizes["q_n"]
        wo = w_out[l].astype(BF16)
        xf = _out_project(xf, o_nsa.reshape(n, -1), o_diff.reshape(n, -1), wo[:half], wo[half:])

        last = l == depth - 1
        xf, w_ffn = _ffn(xf, ffn2_norm[l].reshape(1, d_model), *w_ffn, fg, last,
                         None if last else (ffn1_stacked, l + 1))
    return xf.reshape(batch, seq, d_model)
```

```python
import functools
import math

import numpy as np
import jax
import jax.numpy as jnp
from jax import lax
from jax.experimental import pallas as pl
from jax.experimental.pallas import tpu as pltpu

HEAD_DIM = 128
NSA_KV_HEADS = 2
CMP_LEN = 32
CMP_STRIDE = 16
SLC_LEN = 64
SLC_TOPK = 16
WINDOW = 512
Q_BLOCK = 128
ROPE_THETA = 10000.0
EPS = 1e-6
NEG = -1e30
MASK_FLOOR = -1e20
LOG2E = 1.4426950408889634
QK_SCALE = HEAD_DIM ** -0.5 * LOG2E
SLC_VARIANT_LEN = 512
LANES = 128
VMEM_LIMIT = 58 * 1024 * 1024

F32 = jnp.float32
BF16 = jnp.bfloat16


def _largest_divisor(n, cap):
    for d in range(min(n, cap), 0, -1):
        if n % d == 0:
            return d
    return 1


def _row_tile(n, cap):
    for d in range(min(n, cap), 7, -1):
        if n % d == 0 and d % 8 == 0:
            return d
    return n


def _rms(x, g):
    return x * lax.rsqrt(jnp.mean(x * x, axis=-1, keepdims=True) + EPS) * g


def _dot(a, b):
    return jnp.dot(a, b, preferred_element_type=F32)


def _dot_nt(a, b):
    return lax.dot_general(a, b, (((1,), (1,)), ((), ())), preferred_element_type=F32)


def _params(*sem):
    return pltpu.CompilerParams(dimension_semantics=sem, vmem_limit_bytes=VMEM_LIMIT)


def _ffn_kernel(*refs, final_norm, cast_next):
    if cast_next:
        (x_ref, g_ref, wg_ref, wu_ref, wd_ref, fg_ref, ng_ref, nu_ref, nd_ref,
         o_ref, og_ref, ou_ref, od_ref, h_ref) = refs
        og_ref[...] = ng_ref[...].astype(BF16)
        ou_ref[...] = nu_ref[...].astype(BF16)
        od_ref[...] = nd_ref[...].astype(BF16)
    else:
        x_ref, g_ref, wg_ref, wu_ref, wd_ref, fg_ref, o_ref, h_ref = refs
    j = pl.program_id(1)

    @pl.when(j == 0)
    def _():
        x = x_ref[...]
        h_ref[...] = _rms(x, g_ref[...]).astype(BF16)
        o_ref[...] = x

    h = h_ref[...]
    a = _dot(h, wg_ref[...])
    b = _dot(h, wu_ref[...])
    act = (a * jax.nn.sigmoid(a) * b * 0.5).astype(BF16)
    o_ref[...] += _dot(act, wd_ref[...])

    if final_norm:
        @pl.when(j == pl.num_programs(1) - 1)
        def _():
            o_ref[...] = _rms(o_ref[...], fg_ref[...])


def _ffn(x, g, wg, wu, wd, fg, final_norm, next_weights=None):
    n, d = x.shape
    f = wg.shape[1]
    tm = _row_tile(n, 1024)
    tf = LANES * _largest_divisor(f // LANES, 4)
    n_i, n_j = n // tm, f // tf
    in_specs = [
        pl.BlockSpec((tm, d), lambda i, j: (i, 0)),
        pl.BlockSpec((1, d), lambda i, j: (0, 0)),
        pl.BlockSpec((d, tf), lambda i, j: (0, j)),
        pl.BlockSpec((d, tf), lambda i, j: (0, j)),
        pl.BlockSpec((tf, d), lambda i, j: (j, 0)),
        pl.BlockSpec((1, d), lambda i, j: (0, 0)),
    ]
    out_specs = [pl.BlockSpec((tm, d), lambda i, j: (i, 0))]
    out_shape = [jax.ShapeDtypeStruct((n, d), F32)]
    args = [x, g, wg, wu, wd, fg]
    if next_weights is not None:
        stacked, layer = next_weights
        dr, fr = d // n_i, tf // n_i
        assert dr * n_i == d and fr * n_i == tf and dr % 16 == 0 and fr % 16 == 0
        assert all(w.shape[1:] == s for w, s in zip(stacked, ((d, f), (d, f), (f, d))))
        in_specs += [
            pl.BlockSpec((None, dr, tf), lambda i, j: (layer, i, j)),
            pl.BlockSpec((None, dr, tf), lambda i, j: (layer, i, j)),
            pl.BlockSpec((None, fr, d), lambda i, j: (layer, j * n_i + i, 0)),
        ]
        out_specs += [
            pl.BlockSpec((dr, tf), lambda i, j: (i, j)),
            pl.BlockSpec((dr, tf), lambda i, j: (i, j)),
            pl.BlockSpec((fr, d), lambda i, j: (j * n_i + i, 0)),
        ]
        out_shape += [jax.ShapeDtypeStruct(w.shape[1:], BF16) for w in stacked]
        args += list(stacked)
    outs = pl.pallas_call(
        functools.partial(_ffn_kernel, final_norm=final_norm, cast_next=next_weights is not None),
        grid=(n_i, n_j),
        in_specs=in_specs,
        out_specs=out_specs,
        out_shape=out_shape,
        scratch_shapes=[pltpu.VMEM((tm, d), BF16)],
        compiler_params=_params("parallel", "arbitrary"),
        name="ffn",
    )(*args)
    return outs[0], tuple(outs[1:])


def _proj_kernel(x_ref, g_ref, w_ref, wgate_ref, cos_ref, sin_ref, o_ref, gate_ref, h_ref,
                 *, groups_per_tile, n_rope_groups, query_groups):
    j = pl.program_id(1)

    @pl.when(j == 0)
    def _():
        h = _rms(x_ref[...], g_ref[...]).astype(BF16)
        h_ref[...] = h
        gate_ref[...] = _dot(h, wgate_ref[...])

    acc = _dot(h_ref[...], w_ref[...])
    cos = cos_ref[...]
    sin = sin_ref[...]
    for gi in range(groups_per_tile):
        group = j * groups_per_tile + gi
        is_rope = group < n_rope_groups
        is_query = functools.reduce(jnp.logical_or, [(group >= lo) & (group < hi) for lo, hi in query_groups])
        c = jnp.where(is_rope, cos, 1.0)
        s = jnp.where(is_rope, sin, 0.0)
        xg = acc[:, gi * LANES:(gi + 1) * LANES]
        y = xg * c + pltpu.roll(xg, HEAD_DIM // 2, axis=1) * s
        y = y * jnp.where(is_query, QK_SCALE, 1.0)
        o_ref[:, gi * LANES:(gi + 1) * LANES] = y.astype(BF16)


def _project(x, g, w_main, w_gate, cos, sin_signed, n_rope_groups, query_groups):
    n, d = x.shape
    t = cos.shape[0]
    width = w_main.shape[1]
    gw = w_gate.shape[1]
    n_groups = width // LANES
    gpt = _largest_divisor(n_groups, 11)
    tn = gpt * LANES
    tm = _row_tile(t, 1024)
    t_blocks = t // tm
    return pl.pallas_call(
        functools.partial(_proj_kernel, groups_per_tile=gpt, n_rope_groups=n_rope_groups,
                          query_groups=query_groups),
        grid=(n // tm, width // tn),
        in_specs=[
            pl.BlockSpec((tm, d), lambda i, j: (i, 0)),
            pl.BlockSpec((1, d), lambda i, j: (0, 0)),
            pl.BlockSpec((d, tn), lambda i, j: (0, j)),
            pl.BlockSpec((d, gw), lambda i, j: (0, 0)),
            pl.BlockSpec((tm, HEAD_DIM), lambda i, j: (i % t_blocks, 0)),
            pl.BlockSpec((tm, HEAD_DIM), lambda i, j: (i % t_blocks, 0)),
        ],
        out_specs=[
            pl.BlockSpec((tm, tn), lambda i, j: (i, j)),
            pl.BlockSpec((tm, gw), lambda i, j: (i, 0)),
        ],
        out_shape=[
            jax.ShapeDtypeStruct((n, width), BF16),
            jax.ShapeDtypeStruct((n, gw), F32),
        ],
        scratch_shapes=[pltpu.VMEM((tm, d), BF16)],
        compiler_params=_params("parallel", "arbitrary"),
        name="in_proj",
    )(x, g, w_main, w_gate, cos, sin_signed)


def _cmp_kernel(k_ref, v_ref, pk_ref, pv_ref, wk1_ref, wk2_ref, wv1_ref, wv2_ref, ok_ref, ov_ref, x_ref):
    seq = k_ref.shape[1]
    n_chunks = seq // CMP_STRIDE

    def one(kv_ref, p_ref, w1_ref, w2_ref, o_ref):
        x_ref[...] = kv_ref[0].astype(F32)
        a = jnp.zeros((n_chunks, w1_ref.shape[1]), F32)
        b = jnp.zeros((n_chunks, w1_ref.shape[1]), F32)
        for l in range(CMP_STRIDE):
            rows = x_ref[pl.ds(l, n_chunks, stride=CMP_STRIDE), :].astype(BF16)
            a = a + _dot(rows, w1_ref[l * HEAD_DIM:(l + 1) * HEAD_DIM, :])
            b = b + _dot(rows, w1_ref[(CMP_STRIDE + l) * HEAD_DIM:(CMP_STRIDE + l + 1) * HEAD_DIM, :])
        p = jnp.broadcast_to(p_ref[...], (8, CMP_LEN * HEAD_DIM))
        bias = _dot(p, w1_ref[...])[0:1]
        pre = a + pltpu.roll(b, n_chunks - 1, axis=0) + bias
        hid = (pre * jax.nn.sigmoid(pre)).astype(BF16)
        o_ref[0, 0] = _dot(hid, w2_ref[...]).astype(BF16)

    one(k_ref, pk_ref, wk1_ref, wk2_ref, ok_ref)
    one(v_ref, pv_ref, wv1_ref, wv2_ref, ov_ref)


def _compress(proj, off, batch, seq, pk, pv, wk1, wk2, wv1, wv2):
    g = NSA_KV_HEADS
    nc = seq // CMP_STRIDE
    hid = wk1.shape[1]
    dk = wk2.shape[1]
    assert CMP_LEN == 2 * CMP_STRIDE and dk == HEAD_DIM
    kb, vb = off["kc"] // HEAD_DIM, off["vc"] // HEAD_DIM
    full = lambda shape: pl.BlockSpec(shape, lambda i, j: tuple(0 for _ in shape))
    out_spec = pl.BlockSpec((1, 1, nc, dk), lambda i, j: (i, j, 0, 0))
    return pl.pallas_call(
        _cmp_kernel,
        grid=(batch, g),
        in_specs=[pl.BlockSpec((1, seq, HEAD_DIM), lambda i, j: (i, 0, kb + j)),
                  pl.BlockSpec((1, seq, HEAD_DIM), lambda i, j: (i, 0, vb + j)),
                  full((1, CMP_LEN * dk)), full((1, CMP_LEN * dk)),
                  full((CMP_LEN * dk, hid)), full((hid, dk)), full((CMP_LEN * dk, hid)), full((hid, dk))],
        out_specs=[out_spec, out_spec],
        out_shape=[jax.ShapeDtypeStruct((batch, g, nc, dk), BF16)] * 2,
        scratch_shapes=[pltpu.VMEM((seq, HEAD_DIM), F32)],
        compiler_params=_params("parallel", "parallel"),
        name="nsa_compress",
    )(proj, proj, pk, pv, wk1, wk2, wv1, wv2)


ONES_ROWS = 16


def _transposed_values(v, ones_rows):
    vt = v.astype(F32).T
    if ones_rows:
        vt = jnp.concatenate([vt, jnp.ones((ones_rows, vt.shape[1]), F32)], axis=0)
    return vt.astype(BF16)


def _softmax_piece(st, vt, dv, elem_bias=None, blk_bias=None):
    nk, nc = st.shape
    if elem_bias is not None:
        st = st + elem_bias
    nb = 1 if blk_bias is None else blk_bias.shape[0]
    s4 = st.reshape(nb, nk // nb // 8, 8, nc)
    bm = jnp.max(s4, axis=1)
    if blk_bias is not None:
        bm = bm + blk_bias[:, None, :]
    m = jnp.max(jnp.max(bm, axis=0), axis=0, keepdims=True)
    m = jnp.maximum(m, MASK_FLOOR)
    shift = -m if blk_bias is None else blk_bias - m
    e = jnp.exp2(s4 + shift[:, None, None, :])
    o = _dot(vt, e.reshape(nk, nc).astype(BF16))
    if vt.shape[0] > dv:
        return m, o[dv:dv + 1], o[0:dv]
    return m, jnp.sum(jnp.sum(jnp.sum(e, axis=0), axis=0), axis=0, keepdims=True), o


def _merge_pieces(pieces):
    m = functools.reduce(jnp.maximum, [p[0] for p in pieces])
    ws = [jnp.exp2(p[0] - m) for p in pieces]
    l = sum(w * p[1] for w, p in zip(ws, pieces))
    o = sum(w * p[2] for w, p in zip(ws, pieces))
    return o / l


def _interleave(gens):
    results = [None] * len(gens)
    live = list(range(len(gens)))
    while live:
        for i in list(live):
            try:
                next(gens[i])
            except StopIteration as stop:
                results[i] = stop.value
                live.remove(i)
    return results


def _nsa_block(j, q, gates, ks_ref, kw_ref, kc_ref, vc_ref, ovl_ref, vst_ref, vwt_ref,
               *, rep, n_slc, n_sel, var_len):
    s0 = j * Q_BLOCK
    q4 = jnp.concatenate([q[:, r * HEAD_DIM:(r + 1) * HEAD_DIM] for r in range(rep)], axis=0)
    t_row = s0 + lax.broadcasted_iota(jnp.int32, (1, Q_BLOCK), 1)
    head = lambda a, r: a[:, r * Q_BLOCK:(r + 1) * Q_BLOCK]

    nc = rep * Q_BLOCK
    kk = lax.broadcasted_iota(jnp.int32, (Q_BLOCK, Q_BLOCK), 0)
    qq = lax.broadcasted_iota(jnp.int32, (Q_BLOCK, Q_BLOCK), 1)
    tri = jnp.concatenate([jnp.where(kk <= qq, 0.0, NEG)] * rep, axis=1)
    anti = jnp.concatenate([jnp.where(kk > qq, 0.0, NEG)] * rep, axis=1)
    valid_row = lambda ok: jnp.where(ok, jnp.zeros((1, nc), F32), jnp.full((1, nc), NEG, F32))
    n_back = WINDOW // Q_BLOCK
    far = jnp.maximum(j - n_back, 0)
    mid = jnp.maximum(j - (n_back - 1), 0)
    mid_bias = jnp.concatenate([valid_row(mid + i < j) for i in range(n_back - 1)], axis=0)
    rows_of = lambda ref, blk0, n: ref[0, pl.ds(pl.multiple_of(blk0 * Q_BLOCK, Q_BLOCK), n * Q_BLOCK), :]
    tiles_of = lambda ref, blk0, n: jnp.concatenate([ref[blk0 + i] for i in range(n)], axis=1)
    early = [
        (rows_of(kw_ref, j, 1), lambda: vwt_ref[j], tri, None),
        (rows_of(kw_ref, far, 1), lambda: vwt_ref[far], anti, valid_row(j >= n_back)),
        (rows_of(kw_ref, mid, n_back - 1), lambda: tiles_of(vwt_ref, mid, n_back - 1), None, mid_bias),
        (rows_of(ks_ref, j, 1), lambda: vst_ref[j], tri, None),
    ]
    early_scores = [_dot_nt(k, q4) for k, _, _, _ in early]
    first_scores = _dot_nt(ks_ref[0, 0:var_len, :], q4)
    early_piece = lambda i: _softmax_piece(early_scores[i], early[i][1](), HEAD_DIM, early[i][2], early[i][3])

    kc = kc_ref[0, 0]
    n_rows = kc.shape[0]
    n_idx = lax.broadcasted_iota(jnp.int32, (n_rows, Q_BLOCK), 0)
    c_mask = (n_idx * CMP_STRIDE + (CMP_LEN - 1)) <= t_row
    st = _dot_nt(kc, q4)
    yield
    win_pieces = [early_piece(0), early_piece(1)]
    p_sum = jnp.zeros((n_rows, Q_BLOCK), F32)
    ps = []
    for r in range(rep):
        t = jnp.where(c_mask, head(st, r), NEG)
        e = jnp.where(c_mask, jnp.exp2(t - jnp.max(t, axis=0, keepdims=True)), 0.0)
        l = jnp.sum(e, axis=0, keepdims=True)
        p = e / jnp.where(l > 0.0, l, 1.0)
        p_sum = p_sum + p
        ps.append(p.astype(BF16))
    o_cmp = _dot(vc_ref[0, 0].astype(F32).T.astype(BF16), jnp.concatenate(ps, axis=1))

    hi = p_sum.astype(BF16)
    lo = (p_sum - hi.astype(F32)).astype(BF16)
    ovl = ovl_ref[...]
    n_pad = -(-n_slc // 8) * 8
    imp = (_dot(ovl, hi) + _dot(ovl, lo))[0:n_pad]
    yield
    win_pieces.append(early_piece(2))
    slc_diag = early_piece(3)
    blk = lax.broadcasted_iota(jnp.int32, (n_pad, Q_BLOCK), 0)
    cur = t_row // SLC_LEN
    forced = (blk == 0) | (blk == cur) | (blk == cur - 1)
    blk_causal = blk * SLC_LEN <= t_row
    imp = jnp.where(forced, 1e4, imp)
    imp = jnp.where(blk_causal, imp, -1.0)
    rank = jnp.zeros((n_pad, Q_BLOCK), F32)
    for sp in range(n_slc):
        row = imp[sp:sp + 1, :]
        ge = jnp.where(row >= imp, 1.0, 0.0)
        gt = jnp.where(row > imp, 1.0, 0.0)
        rank = rank + jnp.where(blk > sp, ge, gt)
    blocks_per_q = Q_BLOCK // SLC_LEN
    blk_bias = jnp.where((rank < float(n_sel)) & (blk < j * blocks_per_q), 0.0, NEG)
    blk_bias = jnp.concatenate([blk_bias] * rep, axis=1)
    yield

    gt = jax.nn.sigmoid(gates).T
    gate = lambda branch: jnp.concatenate([jnp.broadcast_to(gt[3 * r + branch:3 * r + branch + 1], (HEAD_DIM, Q_BLOCK))
                                           for r in range(rep)], axis=1)
    partial = gate(0) * o_cmp + gate(2) * _merge_pieces(win_pieces)
    q_per_var = var_len // Q_BLOCK
    blk_per_var = var_len // SLC_LEN
    blk_per_q = Q_BLOCK // SLC_LEN

    def scores(ci, n_q):
        return _dot_nt(ks_ref[0, ci * var_len:ci * var_len + n_q * Q_BLOCK, :], q4)

    def chunk_piece(ci, n_q, st):
        b0 = ci * blk_per_var
        return _softmax_piece(st, tiles_of(vst_ref, ci * q_per_var, n_q), HEAD_DIM,
                              None, blk_bias[b0:b0 + n_q * blk_per_q])

    first_chunk = chunk_piece(0, q_per_var, first_scores)
    return dict(partial=partial, g_slc=gate(1), pieces=[slc_diag, first_chunk], scores=scores, chunk_piece=chunk_piece,
                q_per_var=q_per_var)


def _nsa_selected(ctx, v, h):
    pieces = list(ctx["pieces"])
    todo = [(ci, ctx["q_per_var"]) for ci in range(1, v)] + ([(v, h)] if v > 0 and h > 0 else [])
    ahead = 2
    sts = [ctx["scores"](*t) for t in todo[:ahead]]
    for i, (ci, n_q) in enumerate(todo):
        if i + ahead < len(todo):
            sts.append(ctx["scores"](*todo[i + ahead]))
        pieces.append(ctx["chunk_piece"](ci, n_q, sts[i]))
        yield
    return ctx["partial"] + ctx["g_slc"] * _merge_pieces(pieces)


def _nsa_kernel(q_ref, ks_ref, vs_ref, kw_ref, vw_ref, kc_ref, vc_ref, gate_ref, ovl_ref,
                o_ref, vst_ref, vwt_ref, *, seq, rep, n_slc, n_sel, var_len, blocks):
    jj = pl.program_id(2)

    @pl.when(jj == 0)
    def _():
        for i in range(seq // Q_BLOCK):
            vst_ref[i] = _transposed_values(vs_ref[0, i * Q_BLOCK:(i + 1) * Q_BLOCK, :], ONES_ROWS)
            vwt_ref[i] = _transposed_values(vw_ref[0, i * Q_BLOCK:(i + 1) * Q_BLOCK, :], ONES_ROWS)

    rows = lambda h: slice(h * Q_BLOCK, (h + 1) * Q_BLOCK)
    ctxs = _interleave([
        _nsa_block(jj * blocks + h, q_ref[0, rows(h), :], gate_ref[0, rows(h), :], ks_ref, kw_ref, kc_ref, vc_ref,
                   ovl_ref, vst_ref, vwt_ref, rep=rep, n_slc=n_slc, n_sel=n_sel, var_len=var_len)
        for h in range(blocks)])

    assert blocks * Q_BLOCK == var_len
    for v in range(seq // var_len):
        @pl.when(jj == v)
        def _(v=v):
            outs = _interleave([_nsa_selected(ctx, v, h) for h, ctx in enumerate(ctxs)])
            for h, out in enumerate(outs):
                for r in range(rep):
                    o_ref[0, rows(h), r * HEAD_DIM:(r + 1) * HEAD_DIM] = (
                        out[:, r * Q_BLOCK:(r + 1) * Q_BLOCK].T.astype(BF16))


def _block_overlap(n_rows, n_cmp, n_slc):
    c0 = np.arange(n_cmp) * CMP_STRIDE
    s0 = np.arange(n_slc) * SLC_LEN
    lo = np.maximum(c0[None, :], s0[:, None])
    hi = np.minimum(c0[None, :] + CMP_LEN, s0[:, None] + SLC_LEN)
    out = np.zeros((LANES, n_rows), np.float32)
    out[:n_slc, :n_cmp] = np.clip(hi - lo, 0, None) / CMP_LEN
    return out


def _nsa_attention(proj, gates, kc, vc, off, batch, seq, rep):
    g = NSA_KV_HEADS
    n_qb = seq // Q_BLOCK
    n_slc = seq // SLC_LEN
    n_sel = min(SLC_TOPK, n_slc)
    n_rows = kc.shape[2]
    n_cmp = (seq - CMP_LEN) // CMP_STRIDE + 1
    qw = rep * HEAD_DIM
    var_len = SLC_VARIANT_LEN if seq % SLC_VARIANT_LEN == 0 else seq
    blocks = var_len // Q_BLOCK
    ovl = jnp.asarray(_block_overlap(n_rows, n_cmp, n_slc), BF16)

    def kv_spec(name):
        base = off[name] // HEAD_DIM
        return pl.BlockSpec((1, seq, HEAD_DIM), lambda b, h, j: (b, 0, base + h))

    cmp_spec = pl.BlockSpec((1, 1, n_rows, HEAD_DIM), lambda b, h, j: (b, h, 0, 0))
    q_base = off["q_n"] // qw
    return pl.pallas_call(
        functools.partial(_nsa_kernel, seq=seq, rep=rep, n_slc=n_slc, n_sel=n_sel, var_len=var_len, blocks=blocks),
        grid=(batch, g, n_qb // blocks),
        scratch_shapes=[
            pltpu.VMEM((seq // Q_BLOCK, HEAD_DIM + ONES_ROWS, Q_BLOCK), BF16),
            pltpu.VMEM((seq // Q_BLOCK, HEAD_DIM + ONES_ROWS, Q_BLOCK), BF16),
        ],
        in_specs=[
            pl.BlockSpec((1, blocks * Q_BLOCK, qw), lambda b, h, j: (b, j, q_base + h)),
            kv_spec("ks"), kv_spec("vs"), kv_spec("kw"), kv_spec("vw"),
            cmp_spec, cmp_spec,
            pl.BlockSpec((1, blocks * Q_BLOCK, LANES), lambda b, h, j: (b, j, h)),
            pl.BlockSpec((LANES, n_rows), lambda b, h, j: (0, 0)),
        ],
        out_specs=pl.BlockSpec((1, blocks * Q_BLOCK, qw), lambda b, h, j: (b, j, h)),
        out_shape=jax.ShapeDtypeStruct((batch, seq, g * qw), BF16),
        compiler_params=_params("parallel", "parallel", "arbitrary"),
        name="nsa_attention",
    )(proj, proj, proj, proj, proj, kc, vc, gates, ovl)


def _diff_block(v, q, k_ref, vts, tri, lam, tq):
    qs = [q[:, c * HEAD_DIM:(c + 1) * HEAD_DIM] for c in range(2)]
    specs = [(c, kb) for kb in [v] + list(range(v)) for c in range(2)]
    scores = lambda c, kb: _dot_nt(k_ref[0, kb * tq:(kb + 1) * tq, c * HEAD_DIM:(c + 1) * HEAD_DIM], qs[c])
    pieces = ([], [])
    ahead = 2
    sts = [scores(*spec) for spec in specs[:ahead]]
    for i, (c, kb) in enumerate(specs):
        if i + ahead < len(specs):
            sts.append(scores(*specs[i + ahead]))
        pieces[c].append(_softmax_piece(sts[i], vts[kb], 2 * HEAD_DIM, tri if kb == v else None))
        yield
    return _merge_pieces(pieces[0]) - lam * _merge_pieces(pieces[1])


def _diff_kernel(q_ref, k_ref, v_ref, lam_ref, sg_ref, o_ref, *, seq, tq, lambda_init):
    n_blocks = seq // tq
    rows = lambda i: slice(i * tq, (i + 1) * tq)
    vts = [_transposed_values(v_ref[0, rows(i), :], 0) for i in range(n_blocks)]
    lv = lam_ref[...]
    lam = (jnp.exp(jnp.sum(lv[0:1] * lv[1:2], axis=-1, keepdims=True))
           - jnp.exp(jnp.sum(lv[2:3] * lv[3:4], axis=-1, keepdims=True)) + lambda_init)
    tri = jnp.where(lax.broadcasted_iota(jnp.int32, (tq, tq), 0) <= lax.broadcasted_iota(jnp.int32, (tq, tq), 1),
                    0.0, NEG)
    outs = _interleave([_diff_block(v, q_ref[0, rows(v), :], k_ref, vts, tri, lam, tq) for v in range(n_blocks)])
    for v, o in enumerate(outs):
        o_ref[0, rows(v), :] = (_rms(o.T, sg_ref[...]) * (1.0 - lambda_init)).astype(BF16)


def _diff_attention(proj, lam, subln, off, batch, seq, heads, lambda_init):
    vw = 2 * HEAD_DIM
    tq = _row_tile(seq, 512)
    qb, kb, vb = off["q_d"] // vw, off["k_d"] // vw, off["v_d"] // vw
    return pl.pallas_call(
        functools.partial(_diff_kernel, seq=seq, tq=tq, lambda_init=lambda_init),
        grid=(batch, heads),
        in_specs=[
            pl.BlockSpec((1, seq, vw), lambda b, h: (b, 0, qb + h)),
            pl.BlockSpec((1, seq, vw), lambda b, h: (b, 0, kb + h)),
            pl.BlockSpec((1, seq, vw), lambda b, h: (b, 0, vb + h)),
            pl.BlockSpec((4, HEAD_DIM), lambda b, h: (0, 0)),
            pl.BlockSpec((1, vw), lambda b, h: (0, 0)),
        ],
        out_specs=pl.BlockSpec((1, seq, vw), lambda b, h: (b, 0, h)),
        out_shape=jax.ShapeDtypeStruct((batch, seq, heads * vw), BF16),
        compiler_params=_params("parallel", "parallel"),
        name="diff_attention",
    )(proj, proj, proj, lam, subln)


def _out_kernel(x_ref, a_ref, b_ref, wa_ref, wb_ref, o_ref):
    o_ref[...] = x_ref[...] + _dot(a_ref[...], wa_ref[...]) + _dot(b_ref[...], wb_ref[...])


def _out_project(x, a, b, wa, wb):
    n, d = x.shape
    ka, kb = a.shape[1], b.shape[1]
    tm = _row_tile(n, 512)
    tn = LANES * _largest_divisor(d // LANES, 16)
    return pl.pallas_call(
        _out_kernel,
        grid=(n // tm, d // tn),
        in_specs=[
            pl.BlockSpec((tm, tn), lambda i, j: (i, j)),
            pl.BlockSpec((tm, ka), lambda i, j: (i, 0)),
            pl.BlockSpec((tm, kb), lambda i, j: (i, 0)),
            pl.BlockSpec((ka, tn), lambda i, j: (0, j)),
            pl.BlockSpec((kb, tn), lambda i, j: (0, j)),
        ],
        out_specs=pl.BlockSpec((tm, tn), lambda i, j: (i, j)),
        out_shape=jax.ShapeDtypeStruct((n, d), F32),
        compiler_params=_params("parallel", "arbitrary"),
        name="out_proj",
    )(x, a, b, wa, wb)


def _rope_tables(t):
    inv = 1.0 / (ROPE_THETA ** (jnp.arange(0, HEAD_DIM, 2, dtype=F32) / HEAD_DIM))
    ang = jnp.arange(t, dtype=F32)[:, None] * inv[None, :]
    ang = jnp.concatenate([ang, ang], axis=-1)
    sign = jnp.concatenate([-jnp.ones((HEAD_DIM // 2,), F32), jnp.ones((HEAD_DIM // 2,), F32)])
    return jnp.cos(ang), jnp.sin(ang) * sign[None, :]


def _layout(d_model):
    nsa_heads = d_model // (2 * HEAD_DIM)
    diff_heads = d_model // (4 * HEAD_DIM)
    kv = NSA_KV_HEADS * HEAD_DIM
    sizes = dict(q_n=nsa_heads * HEAD_DIM, kc=kv, vc=kv, ks=kv, vs=kv, kw=kv, vw=kv, gates=3 * nsa_heads,
                 q_d=2 * diff_heads * HEAD_DIM, k_d=2 * diff_heads * HEAD_DIM, v_d=diff_heads * 2 * HEAD_DIM)
    orig, o = {}, 0
    for name in ("q_n", "kc", "vc", "ks", "vs", "kw", "vw", "gates", "q_d", "k_d", "v_d"):
        orig[name] = o
        o += sizes[name]
    rope_names = ("q_n", "kc", "ks", "kw", "q_d", "k_d")
    plain_names = ("vc", "vs", "vw", "v_d")
    new, o = {}, 0
    for name in rope_names + plain_names:
        new[name] = o
        o += sizes[name]
    n_rope_groups = sum(sizes[nm] for nm in rope_names) // LANES
    return sizes, orig, new, rope_names + plain_names, n_rope_groups, nsa_heads, diff_heads


def kernel(x, ffn1_norm, ffn1_w_gate, ffn1_w_up, ffn1_w_down, mix_norm, w_in, cmp_pos_k, cmp_pos_v, cmp_wk1, cmp_wk2, cmp_wv1, cmp_wv2, lam_q1, lam_k1, lam_q2, lam_k2, diff_subln, w_out, ffn2_norm, ffn2_w_gate, ffn2_w_up, ffn2_w_down, final_norm):
    batch, seq, d_model = x.shape
    depth = ffn1_norm.shape[0]
    n = batch * seq
    sizes, orig, off, order, n_rope_groups, nsa_heads, diff_heads = _layout(d_model)
    g = NSA_KV_HEADS
    rep = nsa_heads // g
    cos, sin_signed = _rope_tables(seq)
    fg = final_norm.reshape(1, d_model)
    ffn1_stacked = (ffn1_w_gate, ffn1_w_up, ffn1_w_down)
    ffn2_stacked = (ffn2_w_gate, ffn2_w_up, ffn2_w_down)
    w_ffn = tuple(w[0].astype(BF16) for w in ffn1_stacked)

    xf = x.reshape(n, d_model)
    for l in range(depth):
        lambda_init = 0.8 - 0.6 * math.exp(-0.3 * l)
        xf, w_ffn = _ffn(xf, ffn1_norm[l].reshape(1, d_model), *w_ffn, fg, False, (ffn2_stacked, l))

        wl = w_in[l]
        w_main = jnp.concatenate([wl[:, orig[nm]:orig[nm] + sizes[nm]] for nm in order], axis=1).astype(BF16)
        gate_tiles = []
        for h in range(g):
            cols = wl[:, orig["gates"] + h * 3 * rep: orig["gates"] + (h + 1) * 3 * rep]
            gate_tiles.append(jnp.pad(cols, ((0, 0), (0, LANES - 3 * rep))))
        w_gate = jnp.concatenate(gate_tiles, axis=1).astype(BF16)
        query_groups = tuple((off[nm] // LANES, (off[nm] + sizes[nm]) // LANES) for nm in ("q_n", "q_d"))
        proj, gates = _project(xf, mix_norm[l].reshape(1, d_model), w_main, w_gate, cos, sin_signed, n_rope_groups,
                               query_groups)
        proj3 = proj.reshape(batch, seq, -1)
        gates3 = gates.reshape(batch, seq, -1)

        kc, vc = _compress(proj3, off, batch, seq,
                           cmp_pos_k[l].reshape(1, -1).astype(BF16), cmp_pos_v[l].reshape(1, -1).astype(BF16),
                           cmp_wk1[l].astype(BF16), cmp_wk2[l].astype(BF16),
                           cmp_wv1[l].astype(BF16), cmp_wv2[l].astype(BF16))
        o_nsa = _nsa_attention(proj3, gates3, kc, vc, off, batch, seq, rep)
        lam = jnp.stack([lam_q1[l], lam_k1[l], lam_q2[l], lam_k2[l]])
        o_diff = _diff_attention(proj3, lam, diff_subln[l].reshape(1, -1), off, batch, seq, diff_heads, lambda_init)

        half = sizes["q_n"]
        wo = w_out[l].astype(BF16)
        xf = _out_project(xf, o_nsa.reshape(n, -1), o_diff.reshape(n, -1), wo[:half], wo[half:])

        last = l == depth - 1
        xf, w_ffn = _ffn(xf, ffn2_norm[l].reshape(1, d_model), *w_ffn, fg, last,
                         None if last else (ffn1_stacked, l + 1))
    return xf.reshape(batch, seq, d_model)
```

```python
import functools
import math

import numpy as np
import jax
import jax.numpy as jnp
from jax import lax
from jax.experimental import pallas as pl
from jax.experimental.pallas import tpu as pltpu

HEAD_DIM = 128
NSA_KV_HEADS = 2
CMP_LEN = 32
CMP_STRIDE = 16
SLC_LEN = 64
SLC_TOPK = 16
WINDOW = 512
Q_BLOCK = 128
ROPE_THETA = 10000.0
EPS = 1e-6
NEG = -1e30
MASK_FLOOR = -1e20
LOG2E = 1.4426950408889634
QK_SCALE = HEAD_DIM ** -0.5 * LOG2E
SLC_VARIANT_LEN = 512
LANES = 128
VMEM_LIMIT = 58 * 1024 * 1024
FFN_ROW_TILE = 1024
FFN_HIDDEN_GROUPS = 4
PROJ_ROW_TILE = 1024
PROJ_GROUPS_PER_TILE = 11
OUT_ROW_TILE = 512
OUT_GROUPS_PER_TILE = 16
DIFF_Q_TILE = 512

F32 = jnp.float32
BF16 = jnp.bfloat16


def _largest_divisor(n, cap):
    for d in range(min(n, cap), 0, -1):
        if n % d == 0:
            return d
    return 1


def _row_tile(n, cap):
    for d in range(min(n, cap), 7, -1):
        if n % d == 0 and d % 8 == 0:
            return d
    return n


def _rms(x, g):
    return x * lax.rsqrt(jnp.mean(x * x, axis=-1, keepdims=True) + EPS) * g


def _dot(a, b):
    return jnp.dot(a, b, preferred_element_type=F32)


def _dot_nt(a, b):
    return lax.dot_general(a, b, (((1,), (1,)), ((), ())), preferred_element_type=F32)


def _params(*sem):
    return pltpu.CompilerParams(dimension_semantics=sem, vmem_limit_bytes=VMEM_LIMIT)


def _ffn_kernel(*refs, final_norm, cast_next):
    if cast_next:
        (x_ref, g_ref, wg_ref, wu_ref, wd_ref, fg_ref, ng_ref, nu_ref, nd_ref,
         o_ref, og_ref, ou_ref, od_ref, h_ref) = refs
        og_ref[...] = ng_ref[...].astype(BF16)
        ou_ref[...] = nu_ref[...].astype(BF16)
        od_ref[...] = nd_ref[...].astype(BF16)
    else:
        x_ref, g_ref, wg_ref, wu_ref, wd_ref, fg_ref, o_ref, h_ref = refs
    j = pl.program_id(1)

    @pl.when(j == 0)
    def _():
        x = x_ref[...]
        h_ref[...] = _rms(x, g_ref[...]).astype(BF16)
        o_ref[...] = x

    h = h_ref[...]
    a = _dot(h, wg_ref[...])
    b = _dot(h, wu_ref[...])
    act = (a * jax.nn.sigmoid(a) * b * 0.5).astype(BF16)
    o_ref[...] += _dot(act, wd_ref[...])

    if final_norm:
        @pl.when(j == pl.num_programs(1) - 1)
        def _():
            o_ref[...] = _rms(o_ref[...], fg_ref[...])


def _ffn(x, g, wg, wu, wd, fg, final_norm, next_weights=None):
    n, d = x.shape
    f = wg.shape[1]
    tm = _row_tile(n, FFN_ROW_TILE)
    tf = LANES * _largest_divisor(f // LANES, FFN_HIDDEN_GROUPS)
    n_i, n_j = n // tm, f // tf
    in_specs = [
        pl.BlockSpec((tm, d), lambda i, j: (i, 0)),
        pl.BlockSpec((1, d), lambda i, j: (0, 0)),
        pl.BlockSpec((d, tf), lambda i, j: (0, j)),
        pl.BlockSpec((d, tf), lambda i, j: (0, j)),
        pl.BlockSpec((tf, d), lambda i, j: (j, 0)),
        pl.BlockSpec((1, d), lambda i, j: (0, 0)),
    ]
    out_specs = [pl.BlockSpec((tm, d), lambda i, j: (i, 0))]
    out_shape = [jax.ShapeDtypeStruct((n, d), F32)]
    args = [x, g, wg, wu, wd, fg]
    if next_weights is not None:
        stacked, layer = next_weights
        dr, fr = d // n_i, tf // n_i
        assert dr * n_i == d and fr * n_i == tf and dr % 16 == 0 and fr % 16 == 0
        assert all(w.shape[1:] == s for w, s in zip(stacked, ((d, f), (d, f), (f, d))))
        in_specs += [
            pl.BlockSpec((None, dr, tf), lambda i, j: (layer, i, j)),
            pl.BlockSpec((None, dr, tf), lambda i, j: (layer, i, j)),
            pl.BlockSpec((None, fr, d), lambda i, j: (layer, j * n_i + i, 0)),
        ]
        out_specs += [
            pl.BlockSpec((dr, tf), lambda i, j: (i, j)),
            pl.BlockSpec((dr, tf), lambda i, j: (i, j)),
            pl.BlockSpec((fr, d), lambda i, j: (j * n_i + i, 0)),
        ]
        out_shape += [jax.ShapeDtypeStruct(w.shape[1:], BF16) for w in stacked]
        args += list(stacked)
    outs = pl.pallas_call(
        functools.partial(_ffn_kernel, final_norm=final_norm, cast_next=next_weights is not None),
        grid=(n_i, n_j),
        in_specs=in_specs,
        out_specs=out_specs,
        out_shape=out_shape,
        scratch_shapes=[pltpu.VMEM((tm, d), BF16)],
        compiler_params=_params("parallel", "arbitrary"),
        name="ffn",
    )(*args)
    return outs[0], tuple(outs[1:])


def _proj_kernel(x_ref, g_ref, w_ref, wgate_ref, cos_ref, sin_ref, o_ref, gate_ref, h_ref,
                 *, groups_per_tile, rope_groups, query_groups):
    j = pl.program_id(1)

    @pl.when(j == 0)
    def _():
        h = _rms(x_ref[...], g_ref[...]).astype(BF16)
        h_ref[...] = h
        gate_ref[...] = _dot(h, wgate_ref[...])

    acc = _dot(h_ref[...], w_ref[...])
    cos = cos_ref[...]
    sin = sin_ref[...]
    for gi in range(groups_per_tile):
        group = j * groups_per_tile + gi
        in_any = lambda spans: functools.reduce(jnp.logical_or, [(group >= lo) & (group < hi) for lo, hi in spans])
        is_rope = in_any(rope_groups)
        is_query = in_any(query_groups)
        c = jnp.where(is_rope, cos, 1.0)
        s = jnp.where(is_rope, sin, 0.0)
        xg = acc[:, gi * LANES:(gi + 1) * LANES]
        y = xg * c + pltpu.roll(xg, HEAD_DIM // 2, axis=1) * s
        y = y * jnp.where(is_query, QK_SCALE, 1.0)
        o_ref[:, gi * LANES:(gi + 1) * LANES] = y.astype(BF16)


def _project(x, g, w_main, w_gate, cos, sin_signed, rope_groups, query_groups):
    n, d = x.shape
    t = cos.shape[0]
    width = w_main.shape[1]
    gw = w_gate.shape[1]
    n_groups = width // LANES
    gpt = _largest_divisor(n_groups, PROJ_GROUPS_PER_TILE)
    tn = gpt * LANES
    tm = _row_tile(t, PROJ_ROW_TILE)
    t_blocks = t // tm
    return pl.pallas_call(
        functools.partial(_proj_kernel, groups_per_tile=gpt, rope_groups=rope_groups,
                          query_groups=query_groups),
        grid=(n // tm, width // tn),
        in_specs=[
            pl.BlockSpec((tm, d), lambda i, j: (i, 0)),
            pl.BlockSpec((1, d), lambda i, j: (0, 0)),
            pl.BlockSpec((d, tn), lambda i, j: (0, j)),
            pl.BlockSpec((d, gw), lambda i, j: (0, 0)),
            pl.BlockSpec((tm, HEAD_DIM), lambda i, j: (i % t_blocks, 0)),
            pl.BlockSpec((tm, HEAD_DIM), lambda i, j: (i % t_blocks, 0)),
        ],
        out_specs=[
            pl.BlockSpec((tm, tn), lambda i, j: (i, j)),
            pl.BlockSpec((tm, gw), lambda i, j: (i, 0)),
        ],
        out_shape=[
            jax.ShapeDtypeStruct((n, width), BF16),
            jax.ShapeDtypeStruct((n, gw), F32),
        ],
        scratch_shapes=[pltpu.VMEM((tm, d), BF16)],
        compiler_params=_params("parallel", "arbitrary"),
        name="in_proj",
    )(x, g, w_main, w_gate, cos, sin_signed)


def _cmp_kernel(k_ref, v_ref, pk_ref, pv_ref, wk1_ref, wk2_ref, wv1_ref, wv2_ref, ok_ref, ov_ref, x_ref):
    seq = k_ref.shape[1]
    n_chunks = seq // CMP_STRIDE

    def one(kv_ref, p_ref, w1_ref, w2_ref, o_ref):
        x_ref[...] = kv_ref[0].astype(F32)
        a = jnp.zeros((n_chunks, w1_ref.shape[1]), F32)
        b = jnp.zeros((n_chunks, w1_ref.shape[1]), F32)
        for l in range(CMP_STRIDE):
            rows = x_ref[pl.ds(l, n_chunks, stride=CMP_STRIDE), :].astype(BF16)
            a = a + _dot(rows, w1_ref[l * HEAD_DIM:(l + 1) * HEAD_DIM, :])
            b = b + _dot(rows, w1_ref[(CMP_STRIDE + l) * HEAD_DIM:(CMP_STRIDE + l + 1) * HEAD_DIM, :])
        p = jnp.broadcast_to(p_ref[...], (8, CMP_LEN * HEAD_DIM))
        bias = _dot(p, w1_ref[...])[0:1]
        pre = a + pltpu.roll(b, n_chunks - 1, axis=0) + bias
        hid = (pre * jax.nn.sigmoid(pre)).astype(BF16)
        o_ref[0, 0] = _dot(hid, w2_ref[...]).astype(BF16)

    one(k_ref, pk_ref, wk1_ref, wk2_ref, ok_ref)
    one(v_ref, pv_ref, wv1_ref, wv2_ref, ov_ref)


def _compress(proj, off, batch, seq, pk, pv, wk1, wk2, wv1, wv2):
    g = NSA_KV_HEADS
    nc = seq // CMP_STRIDE
    hid = wk1.shape[1]
    dk = wk2.shape[1]
    assert CMP_LEN == 2 * CMP_STRIDE and dk == HEAD_DIM
    kb, vb = off["kc"] // HEAD_DIM, off["vc"] // HEAD_DIM
    full = lambda shape: pl.BlockSpec(shape, lambda i, j: tuple(0 for _ in shape))
    out_spec = pl.BlockSpec((1, 1, nc, dk), lambda i, j: (i, j, 0, 0))
    return pl.pallas_call(
        _cmp_kernel,
        grid=(batch, g),
        in_specs=[pl.BlockSpec((1, seq, HEAD_DIM), lambda i, j: (i, 0, kb + j)),
                  pl.BlockSpec((1, seq, HEAD_DIM), lambda i, j: (i, 0, vb + j)),
                  full((1, CMP_LEN * dk)), full((1, CMP_LEN * dk)),
                  full((CMP_LEN * dk, hid)), full((hid, dk)), full((CMP_LEN * dk, hid)), full((hid, dk))],
        out_specs=[out_spec, out_spec],
        out_shape=[jax.ShapeDtypeStruct((batch, g, nc, dk), BF16)] * 2,
        scratch_shapes=[pltpu.VMEM((seq, HEAD_DIM), F32)],
        compiler_params=_params("parallel", "parallel"),
        name="nsa_compress",
    )(proj, proj, pk, pv, wk1, wk2, wv1, wv2)


ONES_ROWS = 16


def _transposed_values(v, ones_rows):
    vt = v.astype(F32).T
    if ones_rows:
        vt = jnp.concatenate([vt, jnp.ones((ones_rows, vt.shape[1]), F32)], axis=0)
    return vt.astype(BF16)


def _softmax_piece(st, vt, dv, elem_bias=None, blk_bias=None):
    nk, nc = st.shape
    if elem_bias is not None:
        st = st + elem_bias
    nb = 1 if blk_bias is None else blk_bias.shape[0]
    s4 = st.reshape(nb, nk // nb // 8, 8, nc)
    bm = jnp.max(s4, axis=1)
    if blk_bias is not None:
        bm = bm + blk_bias[:, None, :]
    m = jnp.max(jnp.max(bm, axis=0), axis=0, keepdims=True)
    m = jnp.maximum(m, MASK_FLOOR)
    shift = -m if blk_bias is None else blk_bias - m
    e = jnp.exp2(s4 + shift[:, None, None, :])
    o = _dot(vt, e.reshape(nk, nc).astype(BF16))
    if vt.shape[0] > dv:
        return m, o[dv:dv + 1], o[0:dv]
    return m, jnp.sum(jnp.sum(jnp.sum(e, axis=0), axis=0), axis=0, keepdims=True), o


def _merge_pieces(pieces):
    m = functools.reduce(jnp.maximum, [p[0] for p in pieces])
    ws = [jnp.exp2(p[0] - m) for p in pieces]
    l = sum(w * p[1] for w, p in zip(ws, pieces))
    o = sum(w * p[2] for w, p in zip(ws, pieces))
    return o / l


def _interleave(gens):
    results = [None] * len(gens)
    live = list(range(len(gens)))
    while live:
        for i in list(live):
            try:
                next(gens[i])
            except StopIteration as stop:
                results[i] = stop.value
                live.remove(i)
    return results


def _nsa_block(j, q, gates, ks_ref, kw_ref, kc_ref, vc_ref, ovl_ref, vst_ref, vwt_ref,
               *, rep, n_slc, n_sel, var_len):
    s0 = j * Q_BLOCK
    q4 = jnp.concatenate([q[:, r * HEAD_DIM:(r + 1) * HEAD_DIM] for r in range(rep)], axis=0)
    t_row = s0 + lax.broadcasted_iota(jnp.int32, (1, Q_BLOCK), 1)
    head = lambda a, r: a[:, r * Q_BLOCK:(r + 1) * Q_BLOCK]

    nc = rep * Q_BLOCK
    kk = lax.broadcasted_iota(jnp.int32, (Q_BLOCK, Q_BLOCK), 0)
    qq = lax.broadcasted_iota(jnp.int32, (Q_BLOCK, Q_BLOCK), 1)
    tri = jnp.concatenate([jnp.where(kk <= qq, 0.0, NEG)] * rep, axis=1)
    anti = jnp.concatenate([jnp.where(kk > qq, 0.0, NEG)] * rep, axis=1)
    valid_row = lambda ok: jnp.where(ok, jnp.zeros((1, nc), F32), jnp.full((1, nc), NEG, F32))
    n_back = WINDOW // Q_BLOCK
    far = jnp.maximum(j - n_back, 0)
    mid = jnp.maximum(j - (n_back - 1), 0)
    mid_bias = jnp.concatenate([valid_row(mid + i < j) for i in range(n_back - 1)], axis=0)
    rows_of = lambda ref, blk0, n: ref[0, pl.ds(pl.multiple_of(blk0 * Q_BLOCK, Q_BLOCK), n * Q_BLOCK), :]
    tiles_of = lambda ref, blk0, n: jnp.concatenate([ref[blk0 + i] for i in range(n)], axis=1)
    early = [
        (rows_of(kw_ref, j, 1), lambda: vwt_ref[j], tri, None),
        (rows_of(kw_ref, far, 1), lambda: vwt_ref[far], anti, valid_row(j >= n_back)),
        (rows_of(kw_ref, mid, n_back - 1), lambda: tiles_of(vwt_ref, mid, n_back - 1), None, mid_bias),
        (rows_of(ks_ref, j, 1), lambda: vst_ref[j], tri, None),
    ]
    early_scores = [_dot_nt(k, q4) for k, _, _, _ in early]
    first_scores = _dot_nt(ks_ref[0, 0:var_len, :], q4)
    early_piece = lambda i: _softmax_piece(early_scores[i], early[i][1](), HEAD_DIM, early[i][2], early[i][3])

    kc = kc_ref[0, 0]
    n_rows = kc.shape[0]
    n_idx = lax.broadcasted_iota(jnp.int32, (n_rows, Q_BLOCK), 0)
    c_mask = (n_idx * CMP_STRIDE + (CMP_LEN - 1)) <= t_row
    st = _dot_nt(kc, q4)
    yield
    win_pieces = [early_piece(0), early_piece(1)]
    p_sum = jnp.zeros((n_rows, Q_BLOCK), F32)
    ps = []
    for r in range(rep):
        t = jnp.where(c_mask, head(st, r), NEG)
        e = jnp.where(c_mask, jnp.exp2(t - jnp.max(t, axis=0, keepdims=True)), 0.0)
        l = jnp.sum(e, axis=0, keepdims=True)
        p = e / jnp.where(l > 0.0, l, 1.0)
        p_sum = p_sum + p
        ps.append(p.astype(BF16))
    o_cmp = _dot(vc_ref[0, 0].astype(F32).T.astype(BF16), jnp.concatenate(ps, axis=1))

    hi = p_sum.astype(BF16)
    lo = (p_sum - hi.astype(F32)).astype(BF16)
    ovl = ovl_ref[...]
    n_pad = -(-n_slc // 8) * 8
    imp = (_dot(ovl, hi) + _dot(ovl, lo))[0:n_pad]
    yield
    win_pieces.append(early_piece(2))
    slc_diag = early_piece(3)
    blk = lax.broadcasted_iota(jnp.int32, (n_pad, Q_BLOCK), 0)
    cur = t_row // SLC_LEN
    forced = (blk == 0) | (blk == cur) | (blk == cur - 1)
    blk_causal = blk * SLC_LEN <= t_row
    imp = jnp.where(forced, 1e4, imp)
    imp = jnp.where(blk_causal, imp, -1.0)
    rank = jnp.zeros((n_pad, Q_BLOCK), F32)
    for sp in range(n_slc):
        row = imp[sp:sp + 1, :]
        ge = jnp.where(row >= imp, 1.0, 0.0)
        gt = jnp.where(row > imp, 1.0, 0.0)
        rank = rank + jnp.where(blk > sp, ge, gt)
    blocks_per_q = Q_BLOCK // SLC_LEN
    blk_bias = jnp.where((rank < float(n_sel)) & (blk < j * blocks_per_q), 0.0, NEG)
    blk_bias = jnp.concatenate([blk_bias] * rep, axis=1)
    yield

    gt = jax.nn.sigmoid(gates).T
    gate = lambda branch: jnp.concatenate([jnp.broadcast_to(gt[3 * r + branch:3 * r + branch + 1], (HEAD_DIM, Q_BLOCK))
                                           for r in range(rep)], axis=1)
    partial = gate(0) * o_cmp + gate(2) * _merge_pieces(win_pieces)
    q_per_var = var_len // Q_BLOCK
    blk_per_var = var_len // SLC_LEN
    blk_per_q = Q_BLOCK // SLC_LEN

    def scores(ci, n_q):
        return _dot_nt(ks_ref[0, ci * var_len:ci * var_len + n_q * Q_BLOCK, :], q4)

    def chunk_piece(ci, n_q, st):
        b0 = ci * blk_per_var
        return _softmax_piece(st, tiles_of(vst_ref, ci * q_per_var, n_q), HEAD_DIM,
                              None, blk_bias[b0:b0 + n_q * blk_per_q])

    first_chunk = chunk_piece(0, q_per_var, first_scores)
    return dict(partial=partial, g_slc=gate(1), pieces=[slc_diag, first_chunk], scores=scores, chunk_piece=chunk_piece,
                q_per_var=q_per_var)


def _nsa_selected(ctx, v, h):
    pieces = list(ctx["pieces"])
    todo = [(ci, ctx["q_per_var"]) for ci in range(1, v)] + ([(v, h)] if v > 0 and h > 0 else [])
    ahead = 2
    sts = [ctx["scores"](*t) for t in todo[:ahead]]
    for i, (ci, n_q) in enumerate(todo):
        if i + ahead < len(todo):
            sts.append(ctx["scores"](*todo[i + ahead]))
        pieces.append(ctx["chunk_piece"](ci, n_q, sts[i]))
        yield
    return ctx["partial"] + ctx["g_slc"] * _merge_pieces(pieces)


def _nsa_kernel(q_ref, ks_ref, vs_ref, kw_ref, vw_ref, kc_ref, vc_ref, gate_ref, ovl_ref,
                o_ref, vst_ref, vwt_ref, *, seq, rep, n_slc, n_sel, var_len, blocks):
    jj = pl.program_id(2)

    @pl.when(jj == 0)
    def _():
        for i in range(seq // Q_BLOCK):
            vst_ref[i] = _transposed_values(vs_ref[0, i * Q_BLOCK:(i + 1) * Q_BLOCK, :], ONES_ROWS)
            vwt_ref[i] = _transposed_values(vw_ref[0, i * Q_BLOCK:(i + 1) * Q_BLOCK, :], ONES_ROWS)

    rows = lambda h: slice(h * Q_BLOCK, (h + 1) * Q_BLOCK)
    ctxs = _interleave([
        _nsa_block(jj * blocks + h, q_ref[0, rows(h), :], gate_ref[0, rows(h), :], ks_ref, kw_ref, kc_ref, vc_ref,
                   ovl_ref, vst_ref, vwt_ref, rep=rep, n_slc=n_slc, n_sel=n_sel, var_len=var_len)
        for h in range(blocks)])

    assert blocks * Q_BLOCK == var_len
    for v in range(seq // var_len):
        @pl.when(jj == v)
        def _(v=v):
            outs = _interleave([_nsa_selected(ctx, v, h) for h, ctx in enumerate(ctxs)])
            for h, out in enumerate(outs):
                for r in range(rep):
                    o_ref[0, rows(h), r * HEAD_DIM:(r + 1) * HEAD_DIM] = (
                        out[:, r * Q_BLOCK:(r + 1) * Q_BLOCK].T.astype(BF16))


def _block_overlap(n_rows, n_cmp, n_slc):
    c0 = np.arange(n_cmp) * CMP_STRIDE
    s0 = np.arange(n_slc) * SLC_LEN
    lo = np.maximum(c0[None, :], s0[:, None])
    hi = np.minimum(c0[None, :] + CMP_LEN, s0[:, None] + SLC_LEN)
    out = np.zeros((LANES, n_rows), np.float32)
    out[:n_slc, :n_cmp] = np.clip(hi - lo, 0, None) / CMP_LEN
    return out


def _nsa_attention(proj, gates, kc, vc, off, batch, seq, rep):
    g = NSA_KV_HEADS
    n_qb = seq // Q_BLOCK
    n_slc = seq // SLC_LEN
    n_sel = min(SLC_TOPK, n_slc)
    n_rows = kc.shape[2]
    n_cmp = (seq - CMP_LEN) // CMP_STRIDE + 1
    qw = rep * HEAD_DIM
    var_len = SLC_VARIANT_LEN if seq % SLC_VARIANT_LEN == 0 else seq
    blocks = var_len // Q_BLOCK
    ovl = jnp.asarray(_block_overlap(n_rows, n_cmp, n_slc), BF16)

    def kv_spec(name):
        base = off[name] // HEAD_DIM
        return pl.BlockSpec((1, seq, HEAD_DIM), lambda b, h, j: (b, 0, base + h))

    cmp_spec = pl.BlockSpec((1, 1, n_rows, HEAD_DIM), lambda b, h, j: (b, h, 0, 0))
    q_base = off["q_n"] // qw
    return pl.pallas_call(
        functools.partial(_nsa_kernel, seq=seq, rep=rep, n_slc=n_slc, n_sel=n_sel, var_len=var_len, blocks=blocks),
        grid=(batch, g, n_qb // blocks),
        scratch_shapes=[
            pltpu.VMEM((seq // Q_BLOCK, HEAD_DIM + ONES_ROWS, Q_BLOCK), BF16),
            pltpu.VMEM((seq // Q_BLOCK, HEAD_DIM + ONES_ROWS, Q_BLOCK), BF16),
        ],
        in_specs=[
            pl.BlockSpec((1, blocks * Q_BLOCK, qw), lambda b, h, j: (b, j, q_base + h)),
            kv_spec("ks"), kv_spec("vs"), kv_spec("kw"), kv_spec("vw"),
            cmp_spec, cmp_spec,
            pl.BlockSpec((1, blocks * Q_BLOCK, LANES), lambda b, h, j: (b, j, h)),
            pl.BlockSpec((LANES, n_rows), lambda b, h, j: (0, 0)),
        ],
        out_specs=pl.BlockSpec((1, blocks * Q_BLOCK, qw), lambda b, h, j: (b, j, h)),
        out_shape=jax.ShapeDtypeStruct((batch, seq, g * qw), BF16),
        compiler_params=_params("parallel", "parallel", "arbitrary"),
        name="nsa_attention",
    )(proj, proj, proj, proj, proj, kc, vc, gates, ovl)


def _diff_block(v, q, k_ref, vts, tri, lam, tq):
    qs = [q[:, c * HEAD_DIM:(c + 1) * HEAD_DIM] for c in range(2)]
    specs = [(c, kb) for kb in [v] + list(range(v)) for c in range(2)]
    scores = lambda c, kb: _dot_nt(k_ref[0, kb * tq:(kb + 1) * tq, c * HEAD_DIM:(c + 1) * HEAD_DIM], qs[c])
    pieces = ([], [])
    ahead = 2
    sts = [scores(*spec) for spec in specs[:ahead]]
    for i, (c, kb) in enumerate(specs):
        if i + ahead < len(specs):
            sts.append(scores(*specs[i + ahead]))
        pieces[c].append(_softmax_piece(sts[i], vts[kb], 2 * HEAD_DIM, tri if kb == v else None))
        yield
    return _merge_pieces(pieces[0]) - lam * _merge_pieces(pieces[1])


def _diff_kernel(q_ref, k_ref, v_ref, lam_ref, sg_ref, o_ref, *, seq, tq, lambda_init):
    n_blocks = seq // tq
    rows = lambda i: slice(i * tq, (i + 1) * tq)
    vts = [_transposed_values(v_ref[0, rows(i), :], 0) for i in range(n_blocks)]
    lv = lam_ref[...]
    lam = (jnp.exp(jnp.sum(lv[0:1] * lv[1:2], axis=-1, keepdims=True))
           - jnp.exp(jnp.sum(lv[2:3] * lv[3:4], axis=-1, keepdims=True)) + lambda_init)
    tri = jnp.where(lax.broadcasted_iota(jnp.int32, (tq, tq), 0) <= lax.broadcasted_iota(jnp.int32, (tq, tq), 1),
                    0.0, NEG)
    outs = _interleave([_diff_block(v, q_ref[0, rows(v), :], k_ref, vts, tri, lam, tq) for v in range(n_blocks)])
    for v, o in enumerate(outs):
        o_ref[0, rows(v), :] = (_rms(o.T, sg_ref[...]) * (1.0 - lambda_init)).astype(BF16)


def _diff_attention(proj, lam, subln, off, batch, seq, heads, lambda_init):
    vw = 2 * HEAD_DIM
    tq = _row_tile(seq, DIFF_Q_TILE)
    qb, kb, vb = off["q_d"] // vw, off["k_d"] // vw, off["v_d"] // vw
    return pl.pallas_call(
        functools.partial(_diff_kernel, seq=seq, tq=tq, lambda_init=lambda_init),
        grid=(batch, heads),
        in_specs=[
            pl.BlockSpec((1, seq, vw), lambda b, h: (b, 0, qb + h)),
            pl.BlockSpec((1, seq, vw), lambda b, h: (b, 0, kb + h)),
            pl.BlockSpec((1, seq, vw), lambda b, h: (b, 0, vb + h)),
            pl.BlockSpec((4, HEAD_DIM), lambda b, h: (0, 0)),
            pl.BlockSpec((1, vw), lambda b, h: (0, 0)),
        ],
        out_specs=pl.BlockSpec((1, seq, vw), lambda b, h: (b, 0, h)),
        out_shape=jax.ShapeDtypeStruct((batch, seq, heads * vw), BF16),
        compiler_params=_params("parallel", "parallel"),
        name="diff_attention",
    )(proj, proj, proj, lam, subln)


def _out_kernel(x_ref, a_ref, b_ref, wa_ref, wb_ref, o_ref):
    o_ref[...] = x_ref[...] + _dot(a_ref[...], wa_ref[...]) + _dot(b_ref[...], wb_ref[...])


def _out_project(x, a, b, wa, wb):
    n, d = x.shape
    ka, kb = a.shape[1], b.shape[1]
    tm = _row_tile(n, OUT_ROW_TILE)
    tn = LANES * _largest_divisor(d // LANES, OUT_GROUPS_PER_TILE)
    return pl.pallas_call(
        _out_kernel,
        grid=(n // tm, d // tn),
        in_specs=[
            pl.BlockSpec((tm, tn), lambda i, j: (i, j)),
            pl.BlockSpec((tm, ka), lambda i, j: (i, 0)),
            pl.BlockSpec((tm, kb), lambda i, j: (i, 0)),
            pl.BlockSpec((ka, tn), lambda i, j: (0, j)),
            pl.BlockSpec((kb, tn), lambda i, j: (0, j)),
        ],
        out_specs=pl.BlockSpec((tm, tn), lambda i, j: (i, j)),
        out_shape=jax.ShapeDtypeStruct((n, d), F32),
        compiler_params=_params("parallel", "arbitrary"),
        name="out_proj",
    )(x, a, b, wa, wb)


def _rope_tables(t):
    inv = 1.0 / (ROPE_THETA ** (jnp.arange(0, HEAD_DIM, 2, dtype=F32) / HEAD_DIM))
    ang = jnp.arange(t, dtype=F32)[:, None] * inv[None, :]
    ang = jnp.concatenate([ang, ang], axis=-1)
    sign = jnp.concatenate([-jnp.ones((HEAD_DIM // 2,), F32), jnp.ones((HEAD_DIM // 2,), F32)])
    return jnp.cos(ang), jnp.sin(ang) * sign[None, :]


def _layout(d_model):
    nsa_heads = d_model // (2 * HEAD_DIM)
    diff_heads = d_model // (4 * HEAD_DIM)
    kv = NSA_KV_HEADS * HEAD_DIM
    sizes = dict(q_n=nsa_heads * HEAD_DIM, kc=kv, vc=kv, ks=kv, vs=kv, kw=kv, vw=kv, gates=3 * nsa_heads,
                 q_d=2 * diff_heads * HEAD_DIM, k_d=2 * diff_heads * HEAD_DIM, v_d=diff_heads * 2 * HEAD_DIM)
    orig, off, o, p = {}, {}, 0, 0
    for name in ("q_n", "kc", "vc", "ks", "vs", "kw", "vw", "gates", "q_d", "k_d", "v_d"):
        orig[name] = o
        o += sizes[name]
        if name != "gates":
            off[name] = p
            p += sizes[name]
    groups = lambda names: tuple((off[nm] // LANES, (off[nm] + sizes[nm]) // LANES) for nm in names)
    return sizes, orig, off, groups(("q_n", "kc", "ks", "kw", "q_d", "k_d")), groups(("q_n", "q_d")), nsa_heads, diff_heads


def kernel(x, ffn1_norm, ffn1_w_gate, ffn1_w_up, ffn1_w_down, mix_norm, w_in, cmp_pos_k, cmp_pos_v, cmp_wk1, cmp_wk2, cmp_wv1, cmp_wv2, lam_q1, lam_k1, lam_q2, lam_k2, diff_subln, w_out, ffn2_norm, ffn2_w_gate, ffn2_w_up, ffn2_w_down, final_norm):
    batch, seq, d_model = x.shape
    depth = ffn1_norm.shape[0]
    n = batch * seq
    sizes, orig, off, rope_groups, query_groups, nsa_heads, diff_heads = _layout(d_model)
    g = NSA_KV_HEADS
    rep = nsa_heads // g
    cos, sin_signed = _rope_tables(seq)
    fg = final_norm.reshape(1, d_model)
    ffn1_stacked = (ffn1_w_gate, ffn1_w_up, ffn1_w_down)
    ffn2_stacked = (ffn2_w_gate, ffn2_w_up, ffn2_w_down)
    w_ffn = tuple(w[0].astype(BF16) for w in ffn1_stacked)

    xf = x.reshape(n, d_model)
    for l in range(depth):
        lambda_init = 0.8 - 0.6 * math.exp(-0.3 * l)
        xf, w_ffn = _ffn(xf, ffn1_norm[l].reshape(1, d_model), *w_ffn, fg, False, (ffn2_stacked, l))

        wl = w_in[l]
        gates_end = orig["gates"] + sizes["gates"]
        w_main = jnp.concatenate([wl[:, :orig["gates"]], wl[:, gates_end:]], axis=1).astype(BF16)
        gate_tiles = []
        for h in range(g):
            cols = wl[:, orig["gates"] + h * 3 * rep: orig["gates"] + (h + 1) * 3 * rep]
            gate_tiles.append(jnp.pad(cols, ((0, 0), (0, LANES - 3 * rep))))
        w_gate = jnp.concatenate(gate_tiles, axis=1).astype(BF16)
        proj, gates = _project(xf, mix_norm[l].reshape(1, d_model), w_main, w_gate, cos, sin_signed, rope_groups,
                               query_groups)
        proj3 = proj.reshape(batch, seq, -1)
        gates3 = gates.reshape(batch, seq, -1)

        kc, vc = _compress(proj3, off, batch, seq,
                           cmp_pos_k[l].reshape(1, -1).astype(BF16), cmp_pos_v[l].reshape(1, -1).astype(BF16),
                           cmp_wk1[l].astype(BF16), cmp_wk2[l].astype(BF16),
                           cmp_wv1[l].astype(BF16), cmp_wv2[l].astype(BF16))
        o_nsa = _nsa_attention(proj3, gates3, kc, vc, off, batch, seq, rep)
        lam = jnp.stack([lam_q1[l], lam_k1[l], lam_q2[l], lam_k2[l]])
        o_diff = _diff_attention(proj3, lam, diff_subln[l].reshape(1, -1), off, batch, seq, diff_heads, lambda_init)

        half = sizes["q_n"]
        wo = w_out[l].astype(BF16)
        xf = _out_project(xf, o_nsa.reshape(n, -1), o_diff.reshape(n, -1), wo[:half], wo[half:])

        last = l == depth - 1
        xf, w_ffn = _ffn(xf, ffn2_norm[l].reshape(1, d_model), *w_ffn, fg, last,
                         None if last else (ffn1_stacked, l + 1))
    return xf.reshape(batch, seq, d_model)
```

```python
import functools
import math

import numpy as np
import jax
import jax.numpy as jnp
from jax import lax
from jax.experimental import pallas as pl
from jax.experimental.pallas import tpu as pltpu

HEAD_DIM = 128
NSA_KV_HEADS = 2
CMP_LEN = 32
CMP_STRIDE = 16
SLC_LEN = 64
SLC_TOPK = 16
WINDOW = 512
Q_BLOCK = 128
ROPE_THETA = 10000.0
EPS = 1e-6
NEG = -1e30
MASK_FLOOR = -1e20
LOG2E = 1.4426950408889634
QK_SCALE = HEAD_DIM ** -0.5 * LOG2E
SLC_VARIANT_LEN = 512
LANES = 128
VMEM_LIMIT = 58 * 1024 * 1024
FFN_ROW_TILE = 1024
FFN_HIDDEN_GROUPS = 4
PROJ_ROW_TILE = 1024
PROJ_GROUPS_PER_TILE = 11
OUT_ROW_TILE = 512
OUT_GROUPS_PER_TILE = 16
DIFF_Q_TILE = 512

F32 = jnp.float32
BF16 = jnp.bfloat16


def _largest_divisor(n, cap):
    for d in range(min(n, cap), 0, -1):
        if n % d == 0:
            return d
    return 1


def _row_tile(n, cap):
    for d in range(min(n, cap), 7, -1):
        if n % d == 0 and d % 8 == 0:
            return d
    return n


def _rms(x, g):
    return x * lax.rsqrt(jnp.mean(x * x, axis=-1, keepdims=True) + EPS) * g


def _dot(a, b):
    return jnp.dot(a, b, preferred_element_type=F32)


def _dot_nt(a, b):
    return lax.dot_general(a, b, (((1,), (1,)), ((), ())), preferred_element_type=F32)


def _params(*sem):
    return pltpu.CompilerParams(dimension_semantics=sem, vmem_limit_bytes=VMEM_LIMIT)


def _ffn_kernel(*refs, final_norm, cast_next):
    if cast_next:
        (x_ref, g_ref, wg_ref, wu_ref, wd_ref, fg_ref, ng_ref, nu_ref, nd_ref,
         o_ref, og_ref, ou_ref, od_ref, h_ref) = refs
        og_ref[...] = ng_ref[...].astype(BF16)
        ou_ref[...] = nu_ref[...].astype(BF16)
        od_ref[...] = nd_ref[...].astype(BF16)
    else:
        x_ref, g_ref, wg_ref, wu_ref, wd_ref, fg_ref, o_ref, h_ref = refs
    j = pl.program_id(1)

    @pl.when(j == 0)
    def _():
        x = x_ref[...]
        h_ref[...] = _rms(x, g_ref[...]).astype(BF16)
        o_ref[...] = x

    h = h_ref[...]
    a = _dot(h, wg_ref[...])
    b = _dot(h, wu_ref[...])
    act = (a * jax.nn.sigmoid(a) * b * 0.5).astype(BF16)
    o_ref[...] += _dot(act, wd_ref[...])

    if final_norm:
        @pl.when(j == pl.num_programs(1) - 1)
        def _():
            o_ref[...] = _rms(o_ref[...], fg_ref[...])


def _ffn(x, g, wg, wu, wd, fg, final_norm, next_weights=None):
    n, d = x.shape
    f = wg.shape[1]
    tm = _row_tile(n, FFN_ROW_TILE)
    tf = LANES * _largest_divisor(f // LANES, FFN_HIDDEN_GROUPS)
    n_i, n_j = n // tm, f // tf
    in_specs = [
        pl.BlockSpec((tm, d), lambda i, j: (i, 0)),
        pl.BlockSpec((1, d), lambda i, j: (0, 0)),
        pl.BlockSpec((d, tf), lambda i, j: (0, j)),
        pl.BlockSpec((d, tf), lambda i, j: (0, j)),
        pl.BlockSpec((tf, d), lambda i, j: (j, 0)),
        pl.BlockSpec((1, d), lambda i, j: (0, 0)),
    ]
    out_specs = [pl.BlockSpec((tm, d), lambda i, j: (i, 0))]
    out_shape = [jax.ShapeDtypeStruct((n, d), F32)]
    args = [x, g, wg, wu, wd, fg]
    if next_weights is not None:
        stacked, layer = next_weights
        dr, fr = d // n_i, tf // n_i
        assert dr * n_i == d and fr * n_i == tf and dr % 16 == 0 and fr % 16 == 0
        assert all(w.shape[1:] == s for w, s in zip(stacked, ((d, f), (d, f), (f, d))))
        in_specs += [
            pl.BlockSpec((None, dr, tf), lambda i, j: (layer, i, j)),
            pl.BlockSpec((None, dr, tf), lambda i, j: (layer, i, j)),
            pl.BlockSpec((None, fr, d), lambda i, j: (layer, j * n_i + i, 0)),
        ]
        out_specs += [
            pl.BlockSpec((dr, tf), lambda i, j: (i, j)),
            pl.BlockSpec((dr, tf), lambda i, j: (i, j)),
            pl.BlockSpec((fr, d), lambda i, j: (j * n_i + i, 0)),
        ]
        out_shape += [jax.ShapeDtypeStruct(w.shape[1:], BF16) for w in stacked]
        args += list(stacked)
    outs = pl.pallas_call(
        functools.partial(_ffn_kernel, final_norm=final_norm, cast_next=next_weights is not None),
        grid=(n_i, n_j),
        in_specs=in_specs,
        out_specs=out_specs,
        out_shape=out_shape,
        scratch_shapes=[pltpu.VMEM((tm, d), BF16)],
        compiler_params=_params("parallel", "arbitrary"),
        name="ffn",
    )(*args)
    return outs[0], tuple(outs[1:])


def _proj_kernel(x_ref, g_ref, w_ref, wgate_ref, cos_ref, sin_ref, o_ref, gate_ref, h_ref,
                 *, groups_per_tile, rope_groups, query_groups):
    j = pl.program_id(1)

    @pl.when(j == 0)
    def _():
        h = _rms(x_ref[...], g_ref[...]).astype(BF16)
        h_ref[...] = h
        gate_ref[...] = _dot(h, wgate_ref[...])

    acc = _dot(h_ref[...], w_ref[...])
    cos = cos_ref[...]
    sin = sin_ref[...]
    for gi in range(groups_per_tile):
        group = j * groups_per_tile + gi
        in_any = lambda spans: functools.reduce(jnp.logical_or, [(group >= lo) & (group < hi) for lo, hi in spans])
        is_rope = in_any(rope_groups)
        is_query = in_any(query_groups)
        c = jnp.where(is_rope, cos, 1.0)
        s = jnp.where(is_rope, sin, 0.0)
        xg = acc[:, gi * LANES:(gi + 1) * LANES]
        y = xg * c + pltpu.roll(xg, HEAD_DIM // 2, axis=1) * s
        y = y * jnp.where(is_query, QK_SCALE, 1.0)
        o_ref[:, gi * LANES:(gi + 1) * LANES] = y.astype(BF16)


def _project(x, g, w_main, w_gate, cos, sin_signed, rope_groups, query_groups):
    n, d = x.shape
    t = cos.shape[0]
    width = w_main.shape[1]
    gw = w_gate.shape[1]
    n_groups = width // LANES
    gpt = _largest_divisor(n_groups, PROJ_GROUPS_PER_TILE)
    tn = gpt * LANES
    tm = _row_tile(t, PROJ_ROW_TILE)
    t_blocks = t // tm
    return pl.pallas_call(
        functools.partial(_proj_kernel, groups_per_tile=gpt, rope_groups=rope_groups,
                          query_groups=query_groups),
        grid=(n // tm, width // tn),
        in_specs=[
            pl.BlockSpec((tm, d), lambda i, j: (i, 0)),
            pl.BlockSpec((1, d), lambda i, j: (0, 0)),
            pl.BlockSpec((d, tn), lambda i, j: (0, j)),
            pl.BlockSpec((d, gw), lambda i, j: (0, 0)),
            pl.BlockSpec((tm, HEAD_DIM), lambda i, j: (i % t_blocks, 0)),
            pl.BlockSpec((tm, HEAD_DIM), lambda i, j: (i % t_blocks, 0)),
        ],
        out_specs=[
            pl.BlockSpec((tm, tn), lambda i, j: (i, j)),
            pl.BlockSpec((tm, gw), lambda i, j: (i, 0)),
        ],
        out_shape=[
            jax.ShapeDtypeStruct((n, width), BF16),
            jax.ShapeDtypeStruct((n, gw), F32),
        ],
        scratch_shapes=[pltpu.VMEM((tm, d), BF16)],
        compiler_params=_params("parallel", "arbitrary"),
        name="in_proj",
    )(x, g, w_main, w_gate, cos, sin_signed)


def _cmp_kernel(k_ref, v_ref, pk_ref, pv_ref, wk1_ref, wk2_ref, wv1_ref, wv2_ref, ok_ref, ov_ref, x_ref):
    seq = k_ref.shape[1]
    n_chunks = seq // CMP_STRIDE

    def one(kv_ref, p_ref, w1_ref, w2_ref, o_ref):
        x_ref[...] = kv_ref[0].astype(F32)
        a = jnp.zeros((n_chunks, w1_ref.shape[1]), F32)
        b = jnp.zeros((n_chunks, w1_ref.shape[1]), F32)
        for l in range(CMP_STRIDE):
            rows = x_ref[pl.ds(l, n_chunks, stride=CMP_STRIDE), :].astype(BF16)
            a = a + _dot(rows, w1_ref[l * HEAD_DIM:(l + 1) * HEAD_DIM, :])
            b = b + _dot(rows, w1_ref[(CMP_STRIDE + l) * HEAD_DIM:(CMP_STRIDE + l + 1) * HEAD_DIM, :])
        p = jnp.broadcast_to(p_ref[...], (8, CMP_LEN * HEAD_DIM))
        bias = _dot(p, w1_ref[...])[0:1]
        pre = a + pltpu.roll(b, n_chunks - 1, axis=0) + bias
        hid = (pre * jax.nn.sigmoid(pre)).astype(BF16)
        o_ref[0, 0] = _dot(hid, w2_ref[...]).astype(BF16)

    one(k_ref, pk_ref, wk1_ref, wk2_ref, ok_ref)
    one(v_ref, pv_ref, wv1_ref, wv2_ref, ov_ref)


def _compress(proj, off, batch, seq, pk, pv, wk1, wk2, wv1, wv2):
    g = NSA_KV_HEADS
    nc = seq // CMP_STRIDE
    hid = wk1.shape[1]
    dk = wk2.shape[1]
    assert CMP_LEN == 2 * CMP_STRIDE and dk == HEAD_DIM
    kb, vb = off["kc"] // HEAD_DIM, off["vc"] // HEAD_DIM
    full = lambda shape: pl.BlockSpec(shape, lambda i, j: tuple(0 for _ in shape))
    out_spec = pl.BlockSpec((1, 1, nc, dk), lambda i, j: (i, j, 0, 0))
    return pl.pallas_call(
        _cmp_kernel,
        grid=(batch, g),
        in_specs=[pl.BlockSpec((1, seq, HEAD_DIM), lambda i, j: (i, 0, kb + j)),
                  pl.BlockSpec((1, seq, HEAD_DIM), lambda i, j: (i, 0, vb + j)),
                  full((1, CMP_LEN * dk)), full((1, CMP_LEN * dk)),
                  full((CMP_LEN * dk, hid)), full((hid, dk)), full((CMP_LEN * dk, hid)), full((hid, dk))],
        out_specs=[out_spec, out_spec],
        out_shape=[jax.ShapeDtypeStruct((batch, g, nc, dk), BF16)] * 2,
        scratch_shapes=[pltpu.VMEM((seq, HEAD_DIM), F32)],
        compiler_params=_params("parallel", "parallel"),
        name="nsa_compress",
    )(proj, proj, pk, pv, wk1, wk2, wv1, wv2)


ONES_ROWS = 16


def _transposed_values(v, ones_rows):
    vt = v.astype(F32).T
    if ones_rows:
        vt = jnp.concatenate([vt, jnp.ones((ones_rows, vt.shape[1]), F32)], axis=0)
    return vt.astype(BF16)


def _softmax_piece(st, vt, dv, elem_bias=None, blk_bias=None):
    nk, nc = st.shape
    if elem_bias is not None:
        st = st + elem_bias
    nb = 1 if blk_bias is None else blk_bias.shape[0]
    s4 = st.reshape(nb, nk // nb // 8, 8, nc)
    bm = jnp.max(s4, axis=1)
    if blk_bias is not None:
        bm = bm + blk_bias[:, None, :]
    m = jnp.max(jnp.max(bm, axis=0), axis=0, keepdims=True)
    m = jnp.maximum(m, MASK_FLOOR)
    shift = -m if blk_bias is None else blk_bias - m
    e = jnp.exp2(s4 + shift[:, None, None, :])
    o = _dot(vt, e.reshape(nk, nc).astype(BF16))
    if vt.shape[0] > dv:
        return m, o[dv:dv + 1], o[0:dv]
    return m, jnp.sum(jnp.sum(jnp.sum(e, axis=0), axis=0), axis=0, keepdims=True), o


def _merge_pieces(pieces):
    m = functools.reduce(jnp.maximum, [p[0] for p in pieces])
    ws = [jnp.exp2(p[0] - m) for p in pieces]
    l = sum(w * p[1] for w, p in zip(ws, pieces))
    o = sum(w * p[2] for w, p in zip(ws, pieces))
    return o / l


def _interleave(gens):
    results = [None] * len(gens)
    live = list(range(len(gens)))
    while live:
        for i in list(live):
            try:
                next(gens[i])
            except StopIteration as stop:
                results[i] = stop.value
                live.remove(i)
    return results


def _nsa_block(j, q, gates, ks_ref, kw_ref, kc_ref, vc_ref, ovl_ref, vst_ref, vwt_ref,
               *, rep, n_slc, n_sel, var_len):
    s0 = j * Q_BLOCK
    q4 = jnp.concatenate([q[:, r * HEAD_DIM:(r + 1) * HEAD_DIM] for r in range(rep)], axis=0)
    t_row = s0 + lax.broadcasted_iota(jnp.int32, (1, Q_BLOCK), 1)
    head = lambda a, r: a[:, r * Q_BLOCK:(r + 1) * Q_BLOCK]

    nc = rep * Q_BLOCK
    kk = lax.broadcasted_iota(jnp.int32, (Q_BLOCK, Q_BLOCK), 0)
    qq = lax.broadcasted_iota(jnp.int32, (Q_BLOCK, Q_BLOCK), 1)
    tri = jnp.concatenate([jnp.where(kk <= qq, 0.0, NEG)] * rep, axis=1)
    anti = jnp.concatenate([jnp.where(kk > qq, 0.0, NEG)] * rep, axis=1)
    valid_row = lambda ok: jnp.where(ok, jnp.zeros((1, nc), F32), jnp.full((1, nc), NEG, F32))
    n_back = WINDOW // Q_BLOCK
    far = jnp.maximum(j - n_back, 0)
    mid = jnp.maximum(j - (n_back - 1), 0)
    mid_bias = jnp.concatenate([valid_row(mid + i < j) for i in range(n_back - 1)], axis=0)
    rows_of = lambda ref, blk0, n: ref[0, pl.ds(pl.multiple_of(blk0 * Q_BLOCK, Q_BLOCK), n * Q_BLOCK), :]
    tiles_of = lambda ref, blk0, n: jnp.concatenate([ref[blk0 + i] for i in range(n)], axis=1)
    edge_keys = jnp.concatenate([rows_of(kw_ref, far, 1), rows_of(kw_ref, j, 1)], axis=0)
    edge_bias = jnp.concatenate([valid_row(j >= n_back), jnp.zeros((1, nc), F32)], axis=0)
    early = [
        (edge_keys, lambda: jnp.concatenate([vwt_ref[far], vwt_ref[j]], axis=1),
         jnp.concatenate([anti, tri], axis=0), edge_bias),
        (rows_of(kw_ref, mid, n_back - 1), lambda: tiles_of(vwt_ref, mid, n_back - 1), None, mid_bias),
        (rows_of(ks_ref, j, 1), lambda: vst_ref[j], tri, None),
    ]
    early_scores = [_dot_nt(k, q4) for k, _, _, _ in early]
    first_scores = _dot_nt(ks_ref[0, 0:var_len, :], q4)
    early_piece = lambda i: _softmax_piece(early_scores[i], early[i][1](), HEAD_DIM, early[i][2], early[i][3])

    kc = kc_ref[0, 0]
    n_rows = kc.shape[0]
    n_idx = lax.broadcasted_iota(jnp.int32, (n_rows, Q_BLOCK), 0)
    c_mask = (n_idx * CMP_STRIDE + (CMP_LEN - 1)) <= t_row
    st = _dot_nt(kc, q4)
    yield
    win_pieces = [early_piece(0)]
    p_sum = jnp.zeros((n_rows, Q_BLOCK), F32)
    ps = []
    for r in range(rep):
        t = jnp.where(c_mask, head(st, r), NEG)
        e = jnp.where(c_mask, jnp.exp2(t - jnp.max(t, axis=0, keepdims=True)), 0.0)
        l = jnp.sum(e, axis=0, keepdims=True)
        p = e / jnp.where(l > 0.0, l, 1.0)
        p_sum = p_sum + p
        ps.append(p.astype(BF16))
    o_cmp = _dot(vc_ref[0, 0].astype(F32).T.astype(BF16), jnp.concatenate(ps, axis=1))

    hi = p_sum.astype(BF16)
    lo = (p_sum - hi.astype(F32)).astype(BF16)
    ovl = ovl_ref[...]
    n_pad = -(-n_slc // 8) * 8
    imp = (_dot(ovl, hi) + _dot(ovl, lo))[0:n_pad]
    yield
    win_pieces.append(early_piece(1))
    slc_diag = early_piece(2)
    blk = lax.broadcasted_iota(jnp.int32, (n_pad, Q_BLOCK), 0)
    cur = t_row // SLC_LEN
    forced = (blk == 0) | (blk == cur) | (blk == cur - 1)
    blk_causal = blk * SLC_LEN <= t_row
    imp = jnp.where(forced, 1e4, imp)
    imp = jnp.where(blk_causal, imp, -1.0)
    rank = jnp.zeros((n_pad, Q_BLOCK), F32)
    for sp in range(n_slc):
        row = imp[sp:sp + 1, :]
        ge = jnp.where(row >= imp, 1.0, 0.0)
        gt = jnp.where(row > imp, 1.0, 0.0)
        rank = rank + jnp.where(blk > sp, ge, gt)
    blocks_per_q = Q_BLOCK // SLC_LEN
    blk_bias = jnp.where((rank < float(n_sel)) & (blk < j * blocks_per_q), 0.0, NEG)
    blk_bias = jnp.concatenate([blk_bias] * rep, axis=1)
    yield

    gt = jax.nn.sigmoid(gates).T
    gate = lambda branch: jnp.concatenate([jnp.broadcast_to(gt[3 * r + branch:3 * r + branch + 1], (HEAD_DIM, Q_BLOCK))
                                           for r in range(rep)], axis=1)
    partial = gate(0) * o_cmp + gate(2) * _merge_pieces(win_pieces)
    q_per_var = var_len // Q_BLOCK
    blk_per_var = var_len // SLC_LEN
    blk_per_q = Q_BLOCK // SLC_LEN

    def scores(ci, n_q):
        return _dot_nt(ks_ref[0, ci * var_len:ci * var_len + n_q * Q_BLOCK, :], q4)

    def chunk_piece(ci, n_q, st):
        b0 = ci * blk_per_var
        return _softmax_piece(st, tiles_of(vst_ref, ci * q_per_var, n_q), HEAD_DIM,
                              None, blk_bias[b0:b0 + n_q * blk_per_q])

    first_chunk = chunk_piece(0, q_per_var, first_scores)
    return dict(partial=partial, g_slc=gate(1), pieces=[slc_diag, first_chunk], scores=scores, chunk_piece=chunk_piece,
                q_per_var=q_per_var)


def _nsa_selected(ctx, v, h):
    pieces = list(ctx["pieces"])
    todo = [(ci, ctx["q_per_var"]) for ci in range(1, v)] + ([(v, h)] if v > 0 and h > 0 else [])
    ahead = 2
    sts = [ctx["scores"](*t) for t in todo[:ahead]]
    for i, (ci, n_q) in enumerate(todo):
        if i + ahead < len(todo):
            sts.append(ctx["scores"](*todo[i + ahead]))
        pieces.append(ctx["chunk_piece"](ci, n_q, sts[i]))
        yield
    return ctx["partial"] + ctx["g_slc"] * _merge_pieces(pieces)


def _nsa_kernel(q_ref, ks_ref, vs_ref, kw_ref, vw_ref, kc_ref, vc_ref, gate_ref, ovl_ref,
                o_ref, vst_ref, vwt_ref, *, seq, rep, n_slc, n_sel, var_len, blocks):
    jj = pl.program_id(2)

    @pl.when(jj == 0)
    def _():
        for i in range(seq // Q_BLOCK):
            vst_ref[i] = _transposed_values(vs_ref[0, i * Q_BLOCK:(i + 1) * Q_BLOCK, :], ONES_ROWS)
            vwt_ref[i] = _transposed_values(vw_ref[0, i * Q_BLOCK:(i + 1) * Q_BLOCK, :], ONES_ROWS)

    rows = lambda h: slice(h * Q_BLOCK, (h + 1) * Q_BLOCK)
    ctxs = _interleave([
        _nsa_block(jj * blocks + h, q_ref[0, rows(h), :], gate_ref[0, rows(h), :], ks_ref, kw_ref, kc_ref, vc_ref,
                   ovl_ref, vst_ref, vwt_ref, rep=rep, n_slc=n_slc, n_sel=n_sel, var_len=var_len)
        for h in range(blocks)])

    assert blocks * Q_BLOCK == var_len
    for v in range(seq // var_len):
        @pl.when(jj == v)
        def _(v=v):
            outs = _interleave([_nsa_selected(ctx, v, h) for h, ctx in enumerate(ctxs)])
            for h, out in enumerate(outs):
                for r in range(rep):
                    o_ref[0, rows(h), r * HEAD_DIM:(r + 1) * HEAD_DIM] = (
                        out[:, r * Q_BLOCK:(r + 1) * Q_BLOCK].T.astype(BF16))


def _block_overlap(n_rows, n_cmp, n_slc):
    c0 = np.arange(n_cmp) * CMP_STRIDE
    s0 = np.arange(n_slc) * SLC_LEN
    lo = np.maximum(c0[None, :], s0[:, None])
    hi = np.minimum(c0[None, :] + CMP_LEN, s0[:, None] + SLC_LEN)
    out = np.zeros((LANES, n_rows), np.float32)
    out[:n_slc, :n_cmp] = np.clip(hi - lo, 0, None) / CMP_LEN
    return out


def _nsa_attention(proj, gates, kc, vc, off, batch, seq, rep):
    g = NSA_KV_HEADS
    n_qb = seq // Q_BLOCK
    n_slc = seq // SLC_LEN
    n_sel = min(SLC_TOPK, n_slc)
    n_rows = kc.shape[2]
    n_cmp = (seq - CMP_LEN) // CMP_STRIDE + 1
    qw = rep * HEAD_DIM
    var_len = SLC_VARIANT_LEN if seq % SLC_VARIANT_LEN == 0 else seq
    blocks = var_len // Q_BLOCK
    ovl = jnp.asarray(_block_overlap(n_rows, n_cmp, n_slc), BF16)

    def kv_spec(name):
        base = off[name] // HEAD_DIM
        return pl.BlockSpec((1, seq, HEAD_DIM), lambda b, h, j: (b, 0, base + h))

    cmp_spec = pl.BlockSpec((1, 1, n_rows, HEAD_DIM), lambda b, h, j: (b, h, 0, 0))
    q_base = off["q_n"] // qw
    return pl.pallas_call(
        functools.partial(_nsa_kernel, seq=seq, rep=rep, n_slc=n_slc, n_sel=n_sel, var_len=var_len, blocks=blocks),
        grid=(batch, g, n_qb // blocks),
        scratch_shapes=[
            pltpu.VMEM((seq // Q_BLOCK, HEAD_DIM + ONES_ROWS, Q_BLOCK), BF16),
            pltpu.VMEM((seq // Q_BLOCK, HEAD_DIM + ONES_ROWS, Q_BLOCK), BF16),
        ],
        in_specs=[
            pl.BlockSpec((1, blocks * Q_BLOCK, qw), lambda b, h, j: (b, j, q_base + h)),
            kv_spec("ks"), kv_spec("vs"), kv_spec("kw"), kv_spec("vw"),
            cmp_spec, cmp_spec,
            pl.BlockSpec((1, blocks * Q_BLOCK, LANES), lambda b, h, j: (b, j, h)),
            pl.BlockSpec((LANES, n_rows), lambda b, h, j: (0, 0)),
        ],
        out_specs=pl.BlockSpec((1, blocks * Q_BLOCK, qw), lambda b, h, j: (b, j, h)),
        out_shape=jax.ShapeDtypeStruct((batch, seq, g * qw), BF16),
        compiler_params=_params("parallel", "parallel", "arbitrary"),
        name="nsa_attention",
    )(proj, proj, proj, proj, proj, kc, vc, gates, ovl)


def _diff_block(v, q, k_ref, vts, tri, lam, tq):
    qs = [q[:, c * HEAD_DIM:(c + 1) * HEAD_DIM] for c in range(2)]
    specs = [(c, kb) for kb in [v] + list(range(v)) for c in range(2)]
    scores = lambda c, kb: _dot_nt(k_ref[0, kb * tq:(kb + 1) * tq, c * HEAD_DIM:(c + 1) * HEAD_DIM], qs[c])
    pieces = ([], [])
    ahead = 2
    sts = [scores(*spec) for spec in specs[:ahead]]
    for i, (c, kb) in enumerate(specs):
        if i + ahead < len(specs):
            sts.append(scores(*specs[i + ahead]))
        pieces[c].append(_softmax_piece(sts[i], vts[kb], 2 * HEAD_DIM, tri if kb == v else None))
        yield
    return _merge_pieces(pieces[0]) - lam * _merge_pieces(pieces[1])


def _diff_kernel(q_ref, k_ref, v_ref, lam_ref, sg_ref, o_ref, *, seq, tq, lambda_init):
    n_blocks = seq // tq
    rows = lambda i: slice(i * tq, (i + 1) * tq)
    vts = [_transposed_values(v_ref[0, rows(i), :], 0) for i in range(n_blocks)]
    lv = lam_ref[...]
    lam = (jnp.exp(jnp.sum(lv[0:1] * lv[1:2], axis=-1, keepdims=True))
           - jnp.exp(jnp.sum(lv[2:3] * lv[3:4], axis=-1, keepdims=True)) + lambda_init)
    tri = jnp.where(lax.broadcasted_iota(jnp.int32, (tq, tq), 0) <= lax.broadcasted_iota(jnp.int32, (tq, tq), 1),
                    0.0, NEG)
    outs = _interleave([_diff_block(v, q_ref[0, rows(v), :], k_ref, vts, tri, lam, tq) for v in range(n_blocks)])
    for v, o in enumerate(outs):
        o_ref[0, rows(v), :] = (_rms(o.T, sg_ref[...]) * (1.0 - lambda_init)).astype(BF16)


def _diff_attention(proj, lam, subln, off, batch, seq, heads, lambda_init):
    vw = 2 * HEAD_DIM
    tq = _row_tile(seq, DIFF_Q_TILE)
    qb, kb, vb = off["q_d"] // vw, off["k_d"] // vw, off["v_d"] // vw
    return pl.pallas_call(
        functools.partial(_diff_kernel, seq=seq, tq=tq, lambda_init=lambda_init),
        grid=(batch, heads),
        in_specs=[
            pl.BlockSpec((1, seq, vw), lambda b, h: (b, 0, qb + h)),
            pl.BlockSpec((1, seq, vw), lambda b, h: (b, 0, kb + h)),
            pl.BlockSpec((1, seq, vw), lambda b, h: (b, 0, vb + h)),
            pl.BlockSpec((4, HEAD_DIM), lambda b, h: (0, 0)),
            pl.BlockSpec((1, vw), lambda b, h: (0, 0)),
        ],
        out_specs=pl.BlockSpec((1, seq, vw), lambda b, h: (b, 0, h)),
        out_shape=jax.ShapeDtypeStruct((batch, seq, heads * vw), BF16),
        compiler_params=_params("parallel", "parallel"),
        name="diff_attention",
    )(proj, proj, proj, lam, subln)


def _out_kernel(x_ref, a_ref, b_ref, wa_ref, wb_ref, o_ref):
    o_ref[...] = x_ref[...] + _dot(a_ref[...], wa_ref[...]) + _dot(b_ref[...], wb_ref[...])


def _out_project(x, a, b, wa, wb):
    n, d = x.shape
    ka, kb = a.shape[1], b.shape[1]
    tm = _row_tile(n, OUT_ROW_TILE)
    tn = LANES * _largest_divisor(d // LANES, OUT_GROUPS_PER_TILE)
    return pl.pallas_call(
        _out_kernel,
        grid=(n // tm, d // tn),
        in_specs=[
            pl.BlockSpec((tm, tn), lambda i, j: (i, j)),
            pl.BlockSpec((tm, ka), lambda i, j: (i, 0)),
            pl.BlockSpec((tm, kb), lambda i, j: (i, 0)),
            pl.BlockSpec((ka, tn), lambda i, j: (0, j)),
            pl.BlockSpec((kb, tn), lambda i, j: (0, j)),
        ],
        out_specs=pl.BlockSpec((tm, tn), lambda i, j: (i, j)),
        out_shape=jax.ShapeDtypeStruct((n, d), F32),
        compiler_params=_params("parallel", "arbitrary"),
        name="out_proj",
    )(x, a, b, wa, wb)


def _rope_tables(t):
    inv = 1.0 / (ROPE_THETA ** (jnp.arange(0, HEAD_DIM, 2, dtype=F32) / HEAD_DIM))
    ang = jnp.arange(t, dtype=F32)[:, None] * inv[None, :]
    ang = jnp.concatenate([ang, ang], axis=-1)
    sign = jnp.concatenate([-jnp.ones((HEAD_DIM // 2,), F32), jnp.ones((HEAD_DIM // 2,), F32)])
    return jnp.cos(ang), jnp.sin(ang) * sign[None, :]


def _layout(d_model):
    nsa_heads = d_model // (2 * HEAD_DIM)
    diff_heads = d_model // (4 * HEAD_DIM)
    kv = NSA_KV_HEADS * HEAD_DIM
    sizes = dict(q_n=nsa_heads * HEAD_DIM, kc=kv, vc=kv, ks=kv, vs=kv, kw=kv, vw=kv, gates=3 * nsa_heads,
                 q_d=2 * diff_heads * HEAD_DIM, k_d=2 * diff_heads * HEAD_DIM, v_d=diff_heads * 2 * HEAD_DIM)
    orig, off, o, p = {}, {}, 0, 0
    for name in ("q_n", "kc", "vc", "ks", "vs", "kw", "vw", "gates", "q_d", "k_d", "v_d"):
        orig[name] = o
        o += sizes[name]
        if name != "gates":
            off[name] = p
            p += sizes[name]
    groups = lambda names: tuple((off[nm] // LANES, (off[nm] + sizes[nm]) // LANES) for nm in names)
    return sizes, orig, off, groups(("q_n", "kc", "ks", "kw", "q_d", "k_d")), groups(("q_n", "q_d")), nsa_heads, diff_heads


def kernel(x, ffn1_norm, ffn1_w_gate, ffn1_w_up, ffn1_w_down, mix_norm, w_in, cmp_pos_k, cmp_pos_v, cmp_wk1, cmp_wk2, cmp_wv1, cmp_wv2, lam_q1, lam_k1, lam_q2, lam_k2, diff_subln, w_out, ffn2_norm, ffn2_w_gate, ffn2_w_up, ffn2_w_down, final_norm):
    batch, seq, d_model = x.shape
    depth = ffn1_norm.shape[0]
    n = batch * seq
    sizes, orig, off, rope_groups, query_groups, nsa_heads, diff_heads = _layout(d_model)
    g = NSA_KV_HEADS
    rep = nsa_heads // g
    cos, sin_signed = _rope_tables(seq)
    fg = final_norm.reshape(1, d_model)
    ffn1_stacked = (ffn1_w_gate, ffn1_w_up, ffn1_w_down)
    ffn2_stacked = (ffn2_w_gate, ffn2_w_up, ffn2_w_down)
    w_ffn = tuple(w[0].astype(BF16) for w in ffn1_stacked)

    xf = x.reshape(n, d_model)
    for l in range(depth):
        lambda_init = 0.8 - 0.6 * math.exp(-0.3 * l)
        xf, w_ffn = _ffn(xf, ffn1_norm[l].reshape(1, d_model), *w_ffn, fg, False, (ffn2_stacked, l))

        wl = w_in[l]
        gates_end = orig["gates"] + sizes["gates"]
        w_main = jnp.concatenate([wl[:, :orig["gates"]], wl[:, gates_end:]], axis=1).astype(BF16)
        gate_tiles = []
        for h in range(g):
            cols = wl[:, orig["gates"] + h * 3 * rep: orig["gates"] + (h + 1) * 3 * rep]
            gate_tiles.append(jnp.pad(cols, ((0, 0), (0, LANES - 3 * rep))))
        w_gate = jnp.concatenate(gate_tiles, axis=1).astype(BF16)
        proj, gates = _project(xf, mix_norm[l].reshape(1, d_model), w_main, w_gate, cos, sin_signed, rope_groups,
                               query_groups)
        proj3 = proj.reshape(batch, seq, -1)
        gates3 = gates.reshape(batch, seq, -1)

        kc, vc = _compress(proj3, off, batch, seq,
                           cmp_pos_k[l].reshape(1, -1).astype(BF16), cmp_pos_v[l].reshape(1, -1).astype(BF16),
                           cmp_wk1[l].astype(BF16), cmp_wk2[l].astype(BF16),
                           cmp_wv1[l].astype(BF16), cmp_wv2[l].astype(BF16))
        o_nsa = _nsa_attention(proj3, gates3, kc, vc, off, batch, seq, rep)
        lam = jnp.stack([lam_q1[l], lam_k1[l], lam_q2[l], lam_k2[l]])
        o_diff = _diff_attention(proj3, lam, diff_subln[l].reshape(1, -1), off, batch, seq, diff_heads, lambda_init)

        half = sizes["q_n"]
        wo = w_out[l].astype(BF16)
        xf = _out_project(xf, o_nsa.reshape(n, -1), o_diff.reshape(n, -1), wo[:half], wo[half:])

        last = l == depth - 1
        xf, w_ffn = _ffn(xf, ffn2_norm[l].reshape(1, d_model), *w_ffn, fg, last,
                         None if last else (ffn1_stacked, l + 1))
    return xf.reshape(batch, seq, d_model)
```

```python
import functools
import math

import numpy as np
import jax
import jax.numpy as jnp
from jax import lax
from jax.experimental import pallas as pl
from jax.experimental.pallas import tpu as pltpu

HEAD_DIM = 128
NSA_KV_HEADS = 2
CMP_LEN = 32
CMP_STRIDE = 16
SLC_LEN = 64
SLC_TOPK = 16
WINDOW = 512
Q_BLOCK = 128
ROPE_THETA = 10000.0
EPS = 1e-6
NEG = -1e30
MASK_FLOOR = -1e20
LOG2E = 1.4426950408889634
QK_SCALE = HEAD_DIM ** -0.5 * LOG2E
SLC_VARIANT_LEN = 512
LANES = 128
VMEM_LIMIT = 58 * 1024 * 1024
FFN_ROW_TILE = 1024
FFN_HIDDEN_GROUPS = 4
PROJ_ROW_TILE = 512
PROJ_GROUPS_PER_TILE = 22
OUT_ROW_TILE = 512
OUT_GROUPS_PER_TILE = 16
DIFF_Q_TILE = 512

F32 = jnp.float32
BF16 = jnp.bfloat16


def _largest_divisor(n, cap):
    for d in range(min(n, cap), 0, -1):
        if n % d == 0:
            return d
    return 1


def _row_tile(n, cap):
    for d in range(min(n, cap), 7, -1):
        if n % d == 0 and d % 8 == 0:
            return d
    return n


def _rms(x, g):
    return x * lax.rsqrt(jnp.mean(x * x, axis=-1, keepdims=True) + EPS) * g


def _dot(a, b):
    return jnp.dot(a, b, preferred_element_type=F32)


def _dot_nt(a, b):
    return lax.dot_general(a, b, (((1,), (1,)), ((), ())), preferred_element_type=F32)


def _params(*sem):
    return pltpu.CompilerParams(dimension_semantics=sem, vmem_limit_bytes=VMEM_LIMIT)


def _ffn_kernel(*refs, final_norm, cast_next):
    if cast_next:
        (x_ref, g_ref, wg_ref, wu_ref, wd_ref, fg_ref, ng_ref, nu_ref, nd_ref,
         o_ref, og_ref, ou_ref, od_ref, h_ref) = refs
        og_ref[...] = ng_ref[...].astype(BF16)
        ou_ref[...] = nu_ref[...].astype(BF16)
        od_ref[...] = nd_ref[...].astype(BF16)
    else:
        x_ref, g_ref, wg_ref, wu_ref, wd_ref, fg_ref, o_ref, h_ref = refs
    j = pl.program_id(1)

    @pl.when(j == 0)
    def _():
        x = x_ref[...]
        h_ref[...] = _rms(x, g_ref[...]).astype(BF16)
        o_ref[...] = x

    h = h_ref[...]
    a = _dot(h, wg_ref[...])
    b = _dot(h, wu_ref[...])
    act = (a * jax.nn.sigmoid(a) * b * 0.5).astype(BF16)
    o_ref[...] += _dot(act, wd_ref[...])

    if final_norm:
        @pl.when(j == pl.num_programs(1) - 1)
        def _():
            o_ref[...] = _rms(o_ref[...], fg_ref[...])


def _ffn(x, g, wg, wu, wd, fg, final_norm, next_weights=None):
    n, d = x.shape
    f = wg.shape[1]
    tm = _row_tile(n, FFN_ROW_TILE)
    tf = LANES * _largest_divisor(f // LANES, FFN_HIDDEN_GROUPS)
    n_i, n_j = n // tm, f // tf
    in_specs = [
        pl.BlockSpec((tm, d), lambda i, j: (i, 0)),
        pl.BlockSpec((1, d), lambda i, j: (0, 0)),
        pl.BlockSpec((d, tf), lambda i, j: (0, j)),
        pl.BlockSpec((d, tf), lambda i, j: (0, j)),
        pl.BlockSpec((tf, d), lambda i, j: (j, 0)),
        pl.BlockSpec((1, d), lambda i, j: (0, 0)),
    ]
    out_specs = [pl.BlockSpec((tm, d), lambda i, j: (i, 0))]
    out_shape = [jax.ShapeDtypeStruct((n, d), F32)]
    args = [x, g, wg, wu, wd, fg]
    if next_weights is not None:
        stacked, layer = next_weights
        dr, fr = d // n_i, tf // n_i
        assert dr * n_i == d and fr * n_i == tf and dr % 16 == 0 and fr % 16 == 0
        assert all(w.shape[1:] == s for w, s in zip(stacked, ((d, f), (d, f), (f, d))))
        in_specs += [
            pl.BlockSpec((None, dr, tf), lambda i, j: (layer, i, j)),
            pl.BlockSpec((None, dr, tf), lambda i, j: (layer, i, j)),
            pl.BlockSpec((None, fr, d), lambda i, j: (layer, j * n_i + i, 0)),
        ]
        out_specs += [
            pl.BlockSpec((dr, tf), lambda i, j: (i, j)),
            pl.BlockSpec((dr, tf), lambda i, j: (i, j)),
            pl.BlockSpec((fr, d), lambda i, j: (j * n_i + i, 0)),
        ]
        out_shape += [jax.ShapeDtypeStruct(w.shape[1:], BF16) for w in stacked]
        args += list(stacked)
    outs = pl.pallas_call(
        functools.partial(_ffn_kernel, final_norm=final_norm, cast_next=next_weights is not None),
        grid=(n_i, n_j),
        in_specs=in_specs,
        out_specs=out_specs,
        out_shape=out_shape,
        scratch_shapes=[pltpu.VMEM((tm, d), BF16)],
        compiler_params=_params("parallel", "arbitrary"),
        name="ffn",
    )(*args)
    return outs[0], tuple(outs[1:])


def _proj_kernel(x_ref, g_ref, w_ref, wgate_ref, cos_ref, sin_ref, o_ref, gate_ref, h_ref,
                 *, groups_per_tile, rope_groups, query_groups):
    j = pl.program_id(1)

    @pl.when(j == 0)
    def _():
        h = _rms(x_ref[...], g_ref[...]).astype(BF16)
        h_ref[...] = h
        gate_ref[...] = _dot(h, wgate_ref[...])

    acc = _dot(h_ref[...], w_ref[...])
    cos = cos_ref[...]
    sin = sin_ref[...]
    for gi in range(groups_per_tile):
        group = j * groups_per_tile + gi
        in_any = lambda spans: functools.reduce(jnp.logical_or, [(group >= lo) & (group < hi) for lo, hi in spans])
        is_rope = in_any(rope_groups)
        is_query = in_any(query_groups)
        c = jnp.where(is_rope, cos, 1.0)
        s = jnp.where(is_rope, sin, 0.0)
        xg = acc[:, gi * LANES:(gi + 1) * LANES]
        y = xg * c + pltpu.roll(xg, HEAD_DIM // 2, axis=1) * s
        y = y * jnp.where(is_query, QK_SCALE, 1.0)
        o_ref[:, gi * LANES:(gi + 1) * LANES] = y.astype(BF16)


def _project(x, g, w_main, w_gate, cos, sin_signed, rope_groups, query_groups):
    n, d = x.shape
    t = cos.shape[0]
    width = w_main.shape[1]
    gw = w_gate.shape[1]
    n_groups = width // LANES
    gpt = _largest_divisor(n_groups, PROJ_GROUPS_PER_TILE)
    tn = gpt * LANES
    tm = _row_tile(t, PROJ_ROW_TILE)
    t_blocks = t // tm
    return pl.pallas_call(
        functools.partial(_proj_kernel, groups_per_tile=gpt, rope_groups=rope_groups,
                          query_groups=query_groups),
        grid=(n // tm, width // tn),
        in_specs=[
            pl.BlockSpec((tm, d), lambda i, j: (i, 0)),
            pl.BlockSpec((1, d), lambda i, j: (0, 0)),
            pl.BlockSpec((d, tn), lambda i, j: (0, j)),
            pl.BlockSpec((d, gw), lambda i, j: (0, 0)),
            pl.BlockSpec((tm, HEAD_DIM), lambda i, j: (i % t_blocks, 0)),
            pl.BlockSpec((tm, HEAD_DIM), lambda i, j: (i % t_blocks, 0)),
        ],
        out_specs=[
            pl.BlockSpec((tm, tn), lambda i, j: (i, j)),
            pl.BlockSpec((tm, gw), lambda i, j: (i, 0)),
        ],
        out_shape=[
            jax.ShapeDtypeStruct((n, width), BF16),
            jax.ShapeDtypeStruct((n, gw), F32),
        ],
        scratch_shapes=[pltpu.VMEM((tm, d), BF16)],
        compiler_params=_params("parallel", "arbitrary"),
        name="in_proj",
    )(x, g, w_main, w_gate, cos, sin_signed)


def _cmp_kernel(k_ref, v_ref, pk_ref, pv_ref, wk1_ref, wk2_ref, wv1_ref, wv2_ref, ok_ref, ov_ref, x_ref):
    seq = k_ref.shape[1]
    n_chunks = seq // CMP_STRIDE

    def one(kv_ref, p_ref, w1_ref, w2_ref, o_ref):
        x_ref[...] = kv_ref[0].astype(F32)
        a = jnp.zeros((n_chunks, w1_ref.shape[1]), F32)
        b = jnp.zeros((n_chunks, w1_ref.shape[1]), F32)
        for l in range(CMP_STRIDE):
            rows = x_ref[pl.ds(l, n_chunks, stride=CMP_STRIDE), :].astype(BF16)
            a = a + _dot(rows, w1_ref[l * HEAD_DIM:(l + 1) * HEAD_DIM, :])
            b = b + _dot(rows, w1_ref[(CMP_STRIDE + l) * HEAD_DIM:(CMP_STRIDE + l + 1) * HEAD_DIM, :])
        p = jnp.broadcast_to(p_ref[...], (8, CMP_LEN * HEAD_DIM))
        bias = _dot(p, w1_ref[...])[0:1]
        pre = a + pltpu.roll(b, n_chunks - 1, axis=0) + bias
        hid = (pre * jax.nn.sigmoid(pre)).astype(BF16)
        o_ref[0, 0] = _dot(hid, w2_ref[...]).astype(BF16)

    one(k_ref, pk_ref, wk1_ref, wk2_ref, ok_ref)
    one(v_ref, pv_ref, wv1_ref, wv2_ref, ov_ref)


def _compress(proj, off, batch, seq, pk, pv, wk1, wk2, wv1, wv2):
    g = NSA_KV_HEADS
    nc = seq // CMP_STRIDE
    hid = wk1.shape[1]
    dk = wk2.shape[1]
    assert CMP_LEN == 2 * CMP_STRIDE and dk == HEAD_DIM
    kb, vb = off["kc"] // HEAD_DIM, off["vc"] // HEAD_DIM
    full = lambda shape: pl.BlockSpec(shape, lambda i, j: tuple(0 for _ in shape))
    out_spec = pl.BlockSpec((1, 1, nc, dk), lambda i, j: (i, j, 0, 0))
    return pl.pallas_call(
        _cmp_kernel,
        grid=(batch, g),
        in_specs=[pl.BlockSpec((1, seq, HEAD_DIM), lambda i, j: (i, 0, kb + j)),
                  pl.BlockSpec((1, seq, HEAD_DIM), lambda i, j: (i, 0, vb + j)),
                  full((1, CMP_LEN * dk)), full((1, CMP_LEN * dk)),
                  full((CMP_LEN * dk, hid)), full((hid, dk)), full((CMP_LEN * dk, hid)), full((hid, dk))],
        out_specs=[out_spec, out_spec],
        out_shape=[jax.ShapeDtypeStruct((batch, g, nc, dk), BF16)] * 2,
        scratch_shapes=[pltpu.VMEM((seq, HEAD_DIM), F32)],
        compiler_params=_params("parallel", "parallel"),
        name="nsa_compress",
    )(proj, proj, pk, pv, wk1, wk2, wv1, wv2)


ONES_ROWS = 16


def _transposed_values(v, ones_rows):
    vt = v.astype(F32).T
    if ones_rows:
        vt = jnp.concatenate([vt, jnp.ones((ones_rows, vt.shape[1]), F32)], axis=0)
    return vt.astype(BF16)


def _softmax_piece(st, vt, dv, elem_bias=None, blk_bias=None):
    nk, nc = st.shape
    if elem_bias is not None:
        st = st + elem_bias
    nb = 1 if blk_bias is None else blk_bias.shape[0]
    s4 = st.reshape(nb, nk // nb // 8, 8, nc)
    bm = jnp.max(s4, axis=1)
    if blk_bias is not None:
        bm = bm + blk_bias[:, None, :]
    m = jnp.max(jnp.max(bm, axis=0), axis=0, keepdims=True)
    m = jnp.maximum(m, MASK_FLOOR)
    shift = -m if blk_bias is None else blk_bias - m
    e = jnp.exp2(s4 + shift[:, None, None, :])
    o = _dot(vt, e.reshape(nk, nc).astype(BF16))
    if vt.shape[0] > dv:
        return m, o[dv:dv + 1], o[0:dv]
    return m, jnp.sum(jnp.sum(jnp.sum(e, axis=0), axis=0), axis=0, keepdims=True), o


def _merge_pieces(pieces):
    m = functools.reduce(jnp.maximum, [p[0] for p in pieces])
    ws = [jnp.exp2(p[0] - m) for p in pieces]
    l = sum(w * p[1] for w, p in zip(ws, pieces))
    o = sum(w * p[2] for w, p in zip(ws, pieces))
    return o / l


def _interleave(gens):
    results = [None] * len(gens)
    live = list(range(len(gens)))
    while live:
        for i in list(live):
            try:
                next(gens[i])
            except StopIteration as stop:
                results[i] = stop.value
                live.remove(i)
    return results


def _nsa_block(j, q, gates, ks_ref, kw_ref, kc_ref, vc_ref, ovl_ref, vst_ref, vwt_ref,
               *, rep, n_slc, n_sel, var_len):
    s0 = j * Q_BLOCK
    q4 = jnp.concatenate([q[:, r * HEAD_DIM:(r + 1) * HEAD_DIM] for r in range(rep)], axis=0)
    t_row = s0 + lax.broadcasted_iota(jnp.int32, (1, Q_BLOCK), 1)
    head = lambda a, r: a[:, r * Q_BLOCK:(r + 1) * Q_BLOCK]

    nc = rep * Q_BLOCK
    kk = lax.broadcasted_iota(jnp.int32, (Q_BLOCK, Q_BLOCK), 0)
    qq = lax.broadcasted_iota(jnp.int32, (Q_BLOCK, Q_BLOCK), 1)
    tri = jnp.concatenate([jnp.where(kk <= qq, 0.0, NEG)] * rep, axis=1)
    anti = jnp.concatenate([jnp.where(kk > qq, 0.0, NEG)] * rep, axis=1)
    valid_row = lambda ok: jnp.where(ok, jnp.zeros((1, nc), F32), jnp.full((1, nc), NEG, F32))
    n_back = WINDOW // Q_BLOCK
    far = jnp.maximum(j - n_back, 0)
    mid = jnp.maximum(j - (n_back - 1), 0)
    mid_bias = jnp.concatenate([valid_row(mid + i < j) for i in range(n_back - 1)], axis=0)
    rows_of = lambda ref, blk0, n: ref[0, pl.ds(pl.multiple_of(blk0 * Q_BLOCK, Q_BLOCK), n * Q_BLOCK), :]
    tiles_of = lambda ref, blk0, n: jnp.concatenate([ref[blk0 + i] for i in range(n)], axis=1)
    edge_keys = jnp.concatenate([rows_of(kw_ref, far, 1), rows_of(kw_ref, j, 1)], axis=0)
    edge_bias = jnp.concatenate([valid_row(j >= n_back), jnp.zeros((1, nc), F32)], axis=0)
    early = [
        (edge_keys, lambda: jnp.concatenate([vwt_ref[far], vwt_ref[j]], axis=1),
         jnp.concatenate([anti, tri], axis=0), edge_bias),
        (rows_of(kw_ref, mid, n_back - 1), lambda: tiles_of(vwt_ref, mid, n_back - 1), None, mid_bias),
        (rows_of(ks_ref, j, 1), lambda: vst_ref[j], tri, None),
    ]
    early_scores = [_dot_nt(k, q4) for k, _, _, _ in early]
    first_scores = _dot_nt(ks_ref[0, 0:var_len, :], q4)
    early_piece = lambda i: _softmax_piece(early_scores[i], early[i][1](), HEAD_DIM, early[i][2], early[i][3])

    kc = kc_ref[0, 0]
    n_rows = kc.shape[0]
    n_idx = lax.broadcasted_iota(jnp.int32, (n_rows, Q_BLOCK), 0)
    c_mask = (n_idx * CMP_STRIDE + (CMP_LEN - 1)) <= t_row
    st = _dot_nt(kc, q4)
    yield
    win_pieces = [early_piece(0)]
    p_sum = jnp.zeros((n_rows, Q_BLOCK), F32)
    ps = []
    for r in range(rep):
        t = jnp.where(c_mask, head(st, r), NEG)
        e = jnp.where(c_mask, jnp.exp2(t - jnp.max(t, axis=0, keepdims=True)), 0.0)
        l = jnp.sum(e, axis=0, keepdims=True)
        p = e / jnp.where(l > 0.0, l, 1.0)
        p_sum = p_sum + p
        ps.append(p.astype(BF16))
    o_cmp = _dot(vc_ref[0, 0].astype(F32).T.astype(BF16), jnp.concatenate(ps, axis=1))

    hi = p_sum.astype(BF16)
    lo = (p_sum - hi.astype(F32)).astype(BF16)
    ovl = ovl_ref[...]
    n_pad = -(-n_slc // 8) * 8
    imp = (_dot(ovl, hi) + _dot(ovl, lo))[0:n_pad]
    yield
    win_pieces.append(early_piece(1))
    slc_diag = early_piece(2)
    blk = lax.broadcasted_iota(jnp.int32, (n_pad, Q_BLOCK), 0)
    cur = t_row // SLC_LEN
    forced = (blk == 0) | (blk == cur) | (blk == cur - 1)
    blk_causal = blk * SLC_LEN <= t_row
    imp = jnp.where(forced, 1e4, imp)
    imp = jnp.where(blk_causal, imp, -1.0)
    rank = jnp.zeros((n_pad, Q_BLOCK), F32)
    for sp in range(n_slc):
        row = imp[sp:sp + 1, :]
        ge = jnp.where(row >= imp, 1.0, 0.0)
        gt = jnp.where(row > imp, 1.0, 0.0)
        rank = rank + jnp.where(blk > sp, ge, gt)
    blocks_per_q = Q_BLOCK // SLC_LEN
    blk_bias = jnp.where((rank < float(n_sel)) & (blk < j * blocks_per_q), 0.0, NEG)
    blk_bias = jnp.concatenate([blk_bias] * rep, axis=1)
    yield

    gt = jax.nn.sigmoid(gates).T
    gate = lambda branch: jnp.concatenate([jnp.broadcast_to(gt[3 * r + branch:3 * r + branch + 1], (HEAD_DIM, Q_BLOCK))
                                           for r in range(rep)], axis=1)
    partial = gate(0) * o_cmp + gate(2) * _merge_pieces(win_pieces)
    q_per_var = var_len // Q_BLOCK
    blk_per_var = var_len // SLC_LEN
    blk_per_q = Q_BLOCK // SLC_LEN

    def scores(ci, n_q):
        return _dot_nt(ks_ref[0, ci * var_len:ci * var_len + n_q * Q_BLOCK, :], q4)

    def chunk_piece(ci, n_q, st):
        b0 = ci * blk_per_var
        return _softmax_piece(st, tiles_of(vst_ref, ci * q_per_var, n_q), HEAD_DIM,
                              None, blk_bias[b0:b0 + n_q * blk_per_q])

    first_chunk = chunk_piece(0, q_per_var, first_scores)
    return dict(partial=partial, g_slc=gate(1), pieces=[slc_diag, first_chunk], scores=scores, chunk_piece=chunk_piece,
                q_per_var=q_per_var)


def _nsa_selected(ctx, v, h):
    pieces = list(ctx["pieces"])
    todo = [(ci, ctx["q_per_var"]) for ci in range(1, v)] + ([(v, h)] if v > 0 and h > 0 else [])
    ahead = 2
    sts = [ctx["scores"](*t) for t in todo[:ahead]]
    for i, (ci, n_q) in enumerate(todo):
        if i + ahead < len(todo):
            sts.append(ctx["scores"](*todo[i + ahead]))
        pieces.append(ctx["chunk_piece"](ci, n_q, sts[i]))
        yield
    return ctx["partial"] + ctx["g_slc"] * _merge_pieces(pieces)


def _nsa_kernel(q_ref, ks_ref, vs_ref, kw_ref, vw_ref, kc_ref, vc_ref, gate_ref, ovl_ref,
                o_ref, vst_ref, vwt_ref, *, seq, rep, n_slc, n_sel, var_len, blocks):
    jj = pl.program_id(2)

    @pl.when(jj == 0)
    def _():
        for i in range(seq // Q_BLOCK):
            vst_ref[i] = _transposed_values(vs_ref[0, i * Q_BLOCK:(i + 1) * Q_BLOCK, :], ONES_ROWS)
            vwt_ref[i] = _transposed_values(vw_ref[0, i * Q_BLOCK:(i + 1) * Q_BLOCK, :], ONES_ROWS)

    rows = lambda h: slice(h * Q_BLOCK, (h + 1) * Q_BLOCK)
    ctxs = _interleave([
        _nsa_block(jj * blocks + h, q_ref[0, rows(h), :], gate_ref[0, rows(h), :], ks_ref, kw_ref, kc_ref, vc_ref,
                   ovl_ref, vst_ref, vwt_ref, rep=rep, n_slc=n_slc, n_sel=n_sel, var_len=var_len)
        for h in range(blocks)])

    assert blocks * Q_BLOCK == var_len
    for v in range(seq // var_len):
        @pl.when(jj == v)
        def _(v=v):
            outs = _interleave([_nsa_selected(ctx, v, h) for h, ctx in enumerate(ctxs)])
            for h, out in enumerate(outs):
                for r in range(rep):
                    o_ref[0, rows(h), r * HEAD_DIM:(r + 1) * HEAD_DIM] = (
                        out[:, r * Q_BLOCK:(r + 1) * Q_BLOCK].T.astype(BF16))


def _block_overlap(n_rows, n_cmp, n_slc):
    c0 = np.arange(n_cmp) * CMP_STRIDE
    s0 = np.arange(n_slc) * SLC_LEN
    lo = np.maximum(c0[None, :], s0[:, None])
    hi = np.minimum(c0[None, :] + CMP_LEN, s0[:, None] + SLC_LEN)
    out = np.zeros((LANES, n_rows), np.float32)
    out[:n_slc, :n_cmp] = np.clip(hi - lo, 0, None) / CMP_LEN
    return out


def _nsa_attention(proj, gates, kc, vc, off, batch, seq, rep):
    g = NSA_KV_HEADS
    n_qb = seq // Q_BLOCK
    n_slc = seq // SLC_LEN
    n_sel = min(SLC_TOPK, n_slc)
    n_rows = kc.shape[2]
    n_cmp = (seq - CMP_LEN) // CMP_STRIDE + 1
    qw = rep * HEAD_DIM
    var_len = SLC_VARIANT_LEN if seq % SLC_VARIANT_LEN == 0 else seq
    blocks = var_len // Q_BLOCK
    ovl = jnp.asarray(_block_overlap(n_rows, n_cmp, n_slc), BF16)

    def kv_spec(name):
        base = off[name] // HEAD_DIM
        return pl.BlockSpec((1, seq, HEAD_DIM), lambda b, h, j: (b, 0, base + h))

    cmp_spec = pl.BlockSpec((1, 1, n_rows, HEAD_DIM), lambda b, h, j: (b, h, 0, 0))
    q_base = off["q_n"] // qw
    return pl.pallas_call(
        functools.partial(_nsa_kernel, seq=seq, rep=rep, n_slc=n_slc, n_sel=n_sel, var_len=var_len, blocks=blocks),
        grid=(batch, g, n_qb // blocks),
        scratch_shapes=[
            pltpu.VMEM((seq // Q_BLOCK, HEAD_DIM + ONES_ROWS, Q_BLOCK), BF16),
            pltpu.VMEM((seq // Q_BLOCK, HEAD_DIM + ONES_ROWS, Q_BLOCK), BF16),
        ],
        in_specs=[
            pl.BlockSpec((1, blocks * Q_BLOCK, qw), lambda b, h, j: (b, j, q_base + h)),
            kv_spec("ks"), kv_spec("vs"), kv_spec("kw"), kv_spec("vw"),
            cmp_spec, cmp_spec,
            pl.BlockSpec((1, blocks * Q_BLOCK, LANES), lambda b, h, j: (b, j, h)),
            pl.BlockSpec((LANES, n_rows), lambda b, h, j: (0, 0)),
        ],
        out_specs=pl.BlockSpec((1, blocks * Q_BLOCK, qw), lambda b, h, j: (b, j, h)),
        out_shape=jax.ShapeDtypeStruct((batch, seq, g * qw), BF16),
        compiler_params=_params("parallel", "parallel", "arbitrary"),
        name="nsa_attention",
    )(proj, proj, proj, proj, proj, kc, vc, gates, ovl)


def _diff_block(v, q, k_ref, vts, tri, lam, tq):
    qs = [q[:, c * HEAD_DIM:(c + 1) * HEAD_DIM] for c in range(2)]
    specs = [(c, kb) for kb in [v] + list(range(v)) for c in range(2)]
    scores = lambda c, kb: _dot_nt(k_ref[0, kb * tq:(kb + 1) * tq, c * HEAD_DIM:(c + 1) * HEAD_DIM], qs[c])
    pieces = ([], [])
    ahead = 2
    sts = [scores(*spec) for spec in specs[:ahead]]
    for i, (c, kb) in enumerate(specs):
        if i + ahead < len(specs):
            sts.append(scores(*specs[i + ahead]))
        pieces[c].append(_softmax_piece(sts[i], vts[kb], 2 * HEAD_DIM, tri if kb == v else None))
        yield
    return _merge_pieces(pieces[0]) - lam * _merge_pieces(pieces[1])


def _diff_kernel(q_ref, k_ref, v_ref, lam_ref, sg_ref, o_ref, *, seq, tq, lambda_init):
    n_blocks = seq // tq
    rows = lambda i: slice(i * tq, (i + 1) * tq)
    vts = [_transposed_values(v_ref[0, rows(i), :], 0) for i in range(n_blocks)]
    lv = lam_ref[...]
    lam = (jnp.exp(jnp.sum(lv[0:1] * lv[1:2], axis=-1, keepdims=True))
           - jnp.exp(jnp.sum(lv[2:3] * lv[3:4], axis=-1, keepdims=True)) + lambda_init)
    tri = jnp.where(lax.broadcasted_iota(jnp.int32, (tq, tq), 0) <= lax.broadcasted_iota(jnp.int32, (tq, tq), 1),
                    0.0, NEG)
    outs = _interleave([_diff_block(v, q_ref[0, rows(v), :], k_ref, vts, tri, lam, tq) for v in range(n_blocks)])
    for v, o in enumerate(outs):
        o_ref[0, rows(v), :] = (_rms(o.T, sg_ref[...]) * (1.0 - lambda_init)).astype(BF16)


def _diff_attention(proj, lam, subln, off, batch, seq, heads, lambda_init):
    vw = 2 * HEAD_DIM
    tq = _row_tile(seq, DIFF_Q_TILE)
    qb, kb, vb = off["q_d"] // vw, off["k_d"] // vw, off["v_d"] // vw
    return pl.pallas_call(
        functools.partial(_diff_kernel, seq=seq, tq=tq, lambda_init=lambda_init),
        grid=(batch, heads),
        in_specs=[
            pl.BlockSpec((1, seq, vw), lambda b, h: (b, 0, qb + h)),
            pl.BlockSpec((1, seq, vw), lambda b, h: (b, 0, kb + h)),
            pl.BlockSpec((1, seq, vw), lambda b, h: (b, 0, vb + h)),
            pl.BlockSpec((4, HEAD_DIM), lambda b, h: (0, 0)),
            pl.BlockSpec((1, vw), lambda b, h: (0, 0)),
        ],
        out_specs=pl.BlockSpec((1, seq, vw), lambda b, h: (b, 0, h)),
        out_shape=jax.ShapeDtypeStruct((batch, seq, heads * vw), BF16),
        compiler_params=_params("parallel", "parallel"),
        name="diff_attention",
    )(proj, proj, proj, lam, subln)


def _out_kernel(x_ref, a_ref, b_ref, wa_ref, wb_ref, o_ref):
    o_ref[...] = x_ref[...] + _dot(a_ref[...], wa_ref[...]) + _dot(b_ref[...], wb_ref[...])


def _out_project(x, a, b, wa, wb):
    n, d = x.shape
    ka, kb = a.shape[1], b.shape[1]
    tm = _row_tile(n, OUT_ROW_TILE)
    tn = LANES * _largest_divisor(d // LANES, OUT_GROUPS_PER_TILE)
    return pl.pallas_call(
        _out_kernel,
        grid=(n // tm, d // tn),
        in_specs=[
            pl.BlockSpec((tm, tn), lambda i, j: (i, j)),
            pl.BlockSpec((tm, ka), lambda i, j: (i, 0)),
            pl.BlockSpec((tm, kb), lambda i, j: (i, 0)),
            pl.BlockSpec((ka, tn), lambda i, j: (0, j)),
            pl.BlockSpec((kb, tn), lambda i, j: (0, j)),
        ],
        out_specs=pl.BlockSpec((tm, tn), lambda i, j: (i, j)),
        out_shape=jax.ShapeDtypeStruct((n, d), F32),
        compiler_params=_params("parallel", "arbitrary"),
        name="out_proj",
    )(x, a, b, wa, wb)


def _rope_tables(t):
    inv = 1.0 / (ROPE_THETA ** (jnp.arange(0, HEAD_DIM, 2, dtype=F32) / HEAD_DIM))
    ang = jnp.arange(t, dtype=F32)[:, None] * inv[None, :]
    ang = jnp.concatenate([ang, ang], axis=-1)
    sign = jnp.concatenate([-jnp.ones((HEAD_DIM // 2,), F32), jnp.ones((HEAD_DIM // 2,), F32)])
    return jnp.cos(ang), jnp.sin(ang) * sign[None, :]


def _layout(d_model):
    nsa_heads = d_model // (2 * HEAD_DIM)
    diff_heads = d_model // (4 * HEAD_DIM)
    kv = NSA_KV_HEADS * HEAD_DIM
    sizes = dict(q_n=nsa_heads * HEAD_DIM, kc=kv, vc=kv, ks=kv, vs=kv, kw=kv, vw=kv, gates=3 * nsa_heads,
                 q_d=2 * diff_heads * HEAD_DIM, k_d=2 * diff_heads * HEAD_DIM, v_d=diff_heads * 2 * HEAD_DIM)
    orig, off, o, p = {}, {}, 0, 0
    for name in ("q_n", "kc", "vc", "ks", "vs", "kw", "vw", "gates", "q_d", "k_d", "v_d"):
        orig[name] = o
        o += sizes[name]
        if name != "gates":
            off[name] = p
            p += sizes[name]
    groups = lambda names: tuple((off[nm] // LANES, (off[nm] + sizes[nm]) // LANES) for nm in names)
    return sizes, orig, off, groups(("q_n", "kc", "ks", "kw", "q_d", "k_d")), groups(("q_n", "q_d")), nsa_heads, diff_heads


def kernel(x, ffn1_norm, ffn1_w_gate, ffn1_w_up, ffn1_w_down, mix_norm, w_in, cmp_pos_k, cmp_pos_v, cmp_wk1, cmp_wk2, cmp_wv1, cmp_wv2, lam_q1, lam_k1, lam_q2, lam_k2, diff_subln, w_out, ffn2_norm, ffn2_w_gate, ffn2_w_up, ffn2_w_down, final_norm):
    batch, seq, d_model = x.shape
    depth = ffn1_norm.shape[0]
    n = batch * seq
    sizes, orig, off, rope_groups, query_groups, nsa_heads, diff_heads = _layout(d_model)
    g = NSA_KV_HEADS
    rep = nsa_heads // g
    cos, sin_signed = _rope_tables(seq)
    fg = final_norm.reshape(1, d_model)
    ffn1_stacked = (ffn1_w_gate, ffn1_w_up, ffn1_w_down)
    ffn2_stacked = (ffn2_w_gate, ffn2_w_up, ffn2_w_down)
    w_ffn = tuple(w[0].astype(BF16) for w in ffn1_stacked)

    xf = x.reshape(n, d_model)
    for l in range(depth):
        lambda_init = 0.8 - 0.6 * math.exp(-0.3 * l)
        xf, w_ffn = _ffn(xf, ffn1_norm[l].reshape(1, d_model), *w_ffn, fg, False, (ffn2_stacked, l))

        wl = w_in[l]
        gates_end = orig["gates"] + sizes["gates"]
        w_main = jnp.concatenate([wl[:, :orig["gates"]], wl[:, gates_end:]], axis=1).astype(BF16)
        gate_tiles = []
        for h in range(g):
            cols = wl[:, orig["gates"] + h * 3 * rep: orig["gates"] + (h + 1) * 3 * rep]
            gate_tiles.append(jnp.pad(cols, ((0, 0), (0, LANES - 3 * rep))))
        w_gate = jnp.concatenate(gate_tiles, axis=1).astype(BF16)
        proj, gates = _project(xf, mix_norm[l].reshape(1, d_model), w_main, w_gate, cos, sin_signed, rope_groups,
                               query_groups)
        proj3 = proj.reshape(batch, seq, -1)
        gates3 = gates.reshape(batch, seq, -1)

        kc, vc = _compress(proj3, off, batch, seq,
                           cmp_pos_k[l].reshape(1, -1).astype(BF16), cmp_pos_v[l].reshape(1, -1).astype(BF16),
                           cmp_wk1[l].astype(BF16), cmp_wk2[l].astype(BF16),
                           cmp_wv1[l].astype(BF16), cmp_wv2[l].astype(BF16))
        o_nsa = _nsa_attention(proj3, gates3, kc, vc, off, batch, seq, rep)
        lam = jnp.stack([lam_q1[l], lam_k1[l], lam_q2[l], lam_k2[l]])
        o_diff = _diff_attention(proj3, lam, diff_subln[l].reshape(1, -1), off, batch, seq, diff_heads, lambda_init)

        half = sizes["q_n"]
        wo = w_out[l].astype(BF16)
        xf = _out_project(xf, o_nsa.reshape(n, -1), o_diff.reshape(n, -1), wo[:half], wo[half:])

        last = l == depth - 1
        xf, w_ffn = _ffn(xf, ffn2_norm[l].reshape(1, d_model), *w_ffn, fg, last,
                         None if last else (ffn1_stacked, l + 1))
    return xf.reshape(batch, seq, d_model)
```

```python
import functools
import math

import numpy as np
import jax
import jax.numpy as jnp
from jax import lax
from jax.experimental import pallas as pl
from jax.experimental.pallas import tpu as pltpu

HEAD_DIM = 128
NSA_KV_HEADS = 2
CMP_LEN = 32
CMP_STRIDE = 16
SLC_LEN = 64
SLC_TOPK = 16
WINDOW = 512
Q_BLOCK = 128
ROPE_THETA = 10000.0
EPS = 1e-6
NEG = -1e30
MASK_FLOOR = -1e20
LOG2E = 1.4426950408889634
QK_SCALE = HEAD_DIM ** -0.5 * LOG2E
SLC_VARIANT_LEN = 512
LANES = 128
VMEM_LIMIT = 58 * 1024 * 1024
FFN_ROW_TILE = 1024
FFN_HIDDEN_GROUPS = 4
PROJ_ROW_TILE = 512
PROJ_GROUPS_PER_TILE = 22
OUT_ROW_TILE = 512
OUT_GROUPS_PER_TILE = 16
DIFF_Q_TILE = 512

F32 = jnp.float32
BF16 = jnp.bfloat16


def _largest_divisor(n, cap):
    for d in range(min(n, cap), 0, -1):
        if n % d == 0:
            return d
    return 1


def _row_tile(n, cap):
    for d in range(min(n, cap), 7, -1):
        if n % d == 0 and d % 8 == 0:
            return d
    return n


def _rms(x, g):
    return x * lax.rsqrt(jnp.mean(x * x, axis=-1, keepdims=True) + EPS) * g


def _dot(a, b):
    return jnp.dot(a, b, preferred_element_type=F32)


def _dot_nt(a, b):
    return lax.dot_general(a, b, (((1,), (1,)), ((), ())), preferred_element_type=F32)


def _params(*sem):
    return pltpu.CompilerParams(dimension_semantics=sem, vmem_limit_bytes=VMEM_LIMIT)


def _ffn_kernel(*refs, final_norm, cast_next):
    if cast_next:
        (x_ref, g_ref, wg_ref, wu_ref, wd_ref, fg_ref, ng_ref, nu_ref, nd_ref,
         o_ref, og_ref, ou_ref, od_ref, h_ref) = refs
        og_ref[...] = ng_ref[...].astype(BF16)
        ou_ref[...] = nu_ref[...].astype(BF16)
        od_ref[...] = nd_ref[...].astype(BF16)
    else:
        x_ref, g_ref, wg_ref, wu_ref, wd_ref, fg_ref, o_ref, h_ref = refs
    j = pl.program_id(1)

    @pl.when(j == 0)
    def _():
        x = x_ref[...]
        h_ref[...] = _rms(x, g_ref[...]).astype(BF16)
        o_ref[...] = x

    h = h_ref[...]
    a = _dot(h, wg_ref[...])
    b = _dot(h, wu_ref[...])
    act = (a * jax.nn.sigmoid(a) * b * 0.5).astype(BF16)
    o_ref[...] += _dot(act, wd_ref[...])

    if final_norm:
        @pl.when(j == pl.num_programs(1) - 1)
        def _():
            o_ref[...] = _rms(o_ref[...], fg_ref[...])


def _ffn(x, g, wg, wu, wd, fg, final_norm, next_weights=None):
    n, d = x.shape
    f = wg.shape[1]
    tm = _row_tile(n, FFN_ROW_TILE)
    tf = LANES * _largest_divisor(f // LANES, FFN_HIDDEN_GROUPS)
    n_i, n_j = n // tm, f // tf
    in_specs = [
        pl.BlockSpec((tm, d), lambda i, j: (i, 0)),
        pl.BlockSpec((1, d), lambda i, j: (0, 0)),
        pl.BlockSpec((d, tf), lambda i, j: (0, j)),
        pl.BlockSpec((d, tf), lambda i, j: (0, j)),
        pl.BlockSpec((tf, d), lambda i, j: (j, 0)),
        pl.BlockSpec((1, d), lambda i, j: (0, 0)),
    ]
    out_specs = [pl.BlockSpec((tm, d), lambda i, j: (i, 0))]
    out_shape = [jax.ShapeDtypeStruct((n, d), F32)]
    args = [x, g, wg, wu, wd, fg]
    if next_weights is not None:
        stacked, layer = next_weights
        dr, fr = d // n_i, tf // n_i
        assert dr * n_i == d and fr * n_i == tf and dr % 16 == 0 and fr % 16 == 0
        assert all(w.shape[1:] == s for w, s in zip(stacked, ((d, f), (d, f), (f, d))))
        in_specs += [
            pl.BlockSpec((None, dr, tf), lambda i, j: (layer, i, j)),
            pl.BlockSpec((None, dr, tf), lambda i, j: (layer, i, j)),
            pl.BlockSpec((None, fr, d), lambda i, j: (layer, j * n_i + i, 0)),
        ]
        out_specs += [
            pl.BlockSpec((dr, tf), lambda i, j: (i, j)),
            pl.BlockSpec((dr, tf), lambda i, j: (i, j)),
            pl.BlockSpec((fr, d), lambda i, j: (j * n_i + i, 0)),
        ]
        out_shape += [jax.ShapeDtypeStruct(w.shape[1:], BF16) for w in stacked]
        args += list(stacked)
    outs = pl.pallas_call(
        functools.partial(_ffn_kernel, final_norm=final_norm, cast_next=next_weights is not None),
        grid=(n_i, n_j),
        in_specs=in_specs,
        out_specs=out_specs,
        out_shape=out_shape,
        scratch_shapes=[pltpu.VMEM((tm, d), BF16)],
        compiler_params=_params("parallel", "arbitrary"),
        name="ffn",
    )(*args)
    return outs[0], tuple(outs[1:])


def _proj_kernel(x_ref, g_ref, w_ref, wgate_ref, cos_ref, sin_ref, o_ref, gate_ref, h_ref,
                 *, groups_per_tile, rope_groups, query_groups):
    j = pl.program_id(1)

    @pl.when(j == 0)
    def _():
        h = _rms(x_ref[...], g_ref[...]).astype(BF16)
        h_ref[...] = h
        gate_ref[...] = _dot(h, wgate_ref[...])

    acc = _dot(h_ref[...], w_ref[...])
    cos = cos_ref[...]
    sin = sin_ref[...]
    for gi in range(groups_per_tile):
        group = j * groups_per_tile + gi
        in_any = lambda spans: functools.reduce(jnp.logical_or, [(group >= lo) & (group < hi) for lo, hi in spans])
        is_rope = in_any(rope_groups)
        is_query = in_any(query_groups)
        c = jnp.where(is_rope, cos, 1.0)
        s = jnp.where(is_rope, sin, 0.0)
        xg = acc[:, gi * LANES:(gi + 1) * LANES]
        y = xg * c + pltpu.roll(xg, HEAD_DIM // 2, axis=1) * s
        y = y * jnp.where(is_query, QK_SCALE, 1.0)
        o_ref[:, gi * LANES:(gi + 1) * LANES] = y.astype(BF16)


def _project(x, g, w_main, w_gate, cos, sin_signed, rope_groups, query_groups):
    n, d = x.shape
    t = cos.shape[0]
    width = w_main.shape[1]
    gw = w_gate.shape[1]
    n_groups = width // LANES
    gpt = _largest_divisor(n_groups, PROJ_GROUPS_PER_TILE)
    tn = gpt * LANES
    tm = _row_tile(t, PROJ_ROW_TILE)
    t_blocks = t // tm
    return pl.pallas_call(
        functools.partial(_proj_kernel, groups_per_tile=gpt, rope_groups=rope_groups,
                          query_groups=query_groups),
        grid=(n // tm, width // tn),
        in_specs=[
            pl.BlockSpec((tm, d), lambda i, j: (i, 0)),
            pl.BlockSpec((1, d), lambda i, j: (0, 0)),
            pl.BlockSpec((d, tn), lambda i, j: (0, j)),
            pl.BlockSpec((d, gw), lambda i, j: (0, 0)),
            pl.BlockSpec((tm, HEAD_DIM), lambda i, j: (i % t_blocks, 0)),
            pl.BlockSpec((tm, HEAD_DIM), lambda i, j: (i % t_blocks, 0)),
        ],
        out_specs=[
            pl.BlockSpec((tm, tn), lambda i, j: (i, j)),
            pl.BlockSpec((tm, gw), lambda i, j: (i, 0)),
        ],
        out_shape=[
            jax.ShapeDtypeStruct((n, width), BF16),
            jax.ShapeDtypeStruct((n, gw), F32),
        ],
        scratch_shapes=[pltpu.VMEM((tm, d), BF16)],
        compiler_params=_params("parallel", "arbitrary"),
        name="in_proj",
    )(x, g, w_main, w_gate, cos, sin_signed)


def _cmp_kernel(k_ref, v_ref, pk_ref, pv_ref, wk1_ref, wk2_ref, wv1_ref, wv2_ref, ok_ref, ov_ref, x_ref):
    seq = k_ref.shape[1]
    n_chunks = seq // CMP_STRIDE

    def one(kv_ref, p_ref, w1_ref, w2_ref, o_ref):
        x_ref[...] = kv_ref[0].astype(F32)
        a = jnp.zeros((n_chunks, w1_ref.shape[1]), F32)
        b = jnp.zeros((n_chunks, w1_ref.shape[1]), F32)
        for l in range(CMP_STRIDE):
            rows = x_ref[pl.ds(l, n_chunks, stride=CMP_STRIDE), :].astype(BF16)
            a = a + _dot(rows, w1_ref[l * HEAD_DIM:(l + 1) * HEAD_DIM, :])
            b = b + _dot(rows, w1_ref[(CMP_STRIDE + l) * HEAD_DIM:(CMP_STRIDE + l + 1) * HEAD_DIM, :])
        p = jnp.broadcast_to(p_ref[...], (8, CMP_LEN * HEAD_DIM))
        bias = _dot(p, w1_ref[...])[0:1]
        pre = a + pltpu.roll(b, n_chunks - 1, axis=0) + bias
        hid = (pre * jax.nn.sigmoid(pre)).astype(BF16)
        o_ref[0, 0] = _dot(hid, w2_ref[...]).astype(BF16)

    one(k_ref, pk_ref, wk1_ref, wk2_ref, ok_ref)
    one(v_ref, pv_ref, wv1_ref, wv2_ref, ov_ref)


def _compress(proj, off, batch, seq, pk, pv, wk1, wk2, wv1, wv2):
    g = NSA_KV_HEADS
    nc = seq // CMP_STRIDE
    hid = wk1.shape[1]
    dk = wk2.shape[1]
    assert CMP_LEN == 2 * CMP_STRIDE and dk == HEAD_DIM
    kb, vb = off["kc"] // HEAD_DIM, off["vc"] // HEAD_DIM
    full = lambda shape: pl.BlockSpec(shape, lambda i, j: tuple(0 for _ in shape))
    out_spec = pl.BlockSpec((1, 1, nc, dk), lambda i, j: (i, j, 0, 0))
    return pl.pallas_call(
        _cmp_kernel,
        grid=(batch, g),
        in_specs=[pl.BlockSpec((1, seq, HEAD_DIM), lambda i, j: (i, 0, kb + j)),
                  pl.BlockSpec((1, seq, HEAD_DIM), lambda i, j: (i, 0, vb + j)),
                  full((1, CMP_LEN * dk)), full((1, CMP_LEN * dk)),
                  full((CMP_LEN * dk, hid)), full((hid, dk)), full((CMP_LEN * dk, hid)), full((hid, dk))],
        out_specs=[out_spec, out_spec],
        out_shape=[jax.ShapeDtypeStruct((batch, g, nc, dk), BF16)] * 2,
        scratch_shapes=[pltpu.VMEM((seq, HEAD_DIM), F32)],
        compiler_params=_params("parallel", "parallel"),
        name="nsa_compress",
    )(proj, proj, pk, pv, wk1, wk2, wv1, wv2)


ONES_ROWS = 16


def _transposed_values(v, ones_rows):
    vt = v.astype(F32).T
    if ones_rows:
        vt = jnp.concatenate([vt, jnp.ones((ones_rows, vt.shape[1]), F32)], axis=0)
    return vt.astype(BF16)


def _softmax_piece(st, vt, dv, elem_bias=None, blk_bias=None):
    nk, nc = st.shape
    if elem_bias is not None:
        st = st + elem_bias
    nb = 1 if blk_bias is None else blk_bias.shape[0]
    s4 = st.reshape(nb, nk // nb // 8, 8, nc)
    bm = jnp.max(s4, axis=1)
    if blk_bias is not None:
        bm = bm + blk_bias[:, None, :]
    m = jnp.max(jnp.max(bm, axis=0), axis=0, keepdims=True)
    m = jnp.maximum(m, MASK_FLOOR)
    shift = -m if blk_bias is None else blk_bias - m
    e = jnp.exp2(s4 + shift[:, None, None, :])
    o = _dot(vt, e.reshape(nk, nc).astype(BF16))
    if vt.shape[0] > dv:
        return m, o[dv:dv + 1], o[0:dv]
    return m, jnp.sum(jnp.sum(jnp.sum(e, axis=0), axis=0), axis=0, keepdims=True), o


def _merge_pieces(pieces):
    m = functools.reduce(jnp.maximum, [p[0] for p in pieces])
    ws = [jnp.exp2(p[0] - m) for p in pieces]
    l = sum(w * p[1] for w, p in zip(ws, pieces))
    o = sum(w * p[2] for w, p in zip(ws, pieces))
    return o / l


def _interleave(gens):
    results = [None] * len(gens)
    live = list(range(len(gens)))
    while live:
        for i in list(live):
            try:
                next(gens[i])
            except StopIteration as stop:
                results[i] = stop.value
                live.remove(i)
    return results


def _nsa_block(j, q, gates, ks_ref, kw_ref, kc_ref, vc_ref, ovl_ref, vst_ref, vwt_ref,
               *, rep, n_slc, n_sel, var_len):
    s0 = j * Q_BLOCK
    q4 = jnp.concatenate([q[:, r * HEAD_DIM:(r + 1) * HEAD_DIM] for r in range(rep)], axis=0)
    t_row = s0 + lax.broadcasted_iota(jnp.int32, (1, Q_BLOCK), 1)
    head = lambda a, r: a[:, r * Q_BLOCK:(r + 1) * Q_BLOCK]

    nc = rep * Q_BLOCK
    kk = lax.broadcasted_iota(jnp.int32, (Q_BLOCK, Q_BLOCK), 0)
    qq = lax.broadcasted_iota(jnp.int32, (Q_BLOCK, Q_BLOCK), 1)
    tri = jnp.concatenate([jnp.where(kk <= qq, 0.0, NEG)] * rep, axis=1)
    anti = jnp.concatenate([jnp.where(kk > qq, 0.0, NEG)] * rep, axis=1)
    valid_row = lambda ok: jnp.where(ok, jnp.zeros((1, nc), F32), jnp.full((1, nc), NEG, F32))
    n_back = WINDOW // Q_BLOCK
    far = jnp.maximum(j - n_back, 0)
    mid = jnp.maximum(j - (n_back - 1), 0)
    mid_bias = jnp.concatenate([valid_row(mid + i < j) for i in range(n_back - 1)], axis=0)
    rows_of = lambda ref, blk0, n: ref[0, pl.ds(pl.multiple_of(blk0 * Q_BLOCK, Q_BLOCK), n * Q_BLOCK), :]
    tiles_of = lambda ref, blk0, n: jnp.concatenate([ref[blk0 + i] for i in range(n)], axis=1)
    edge_keys = jnp.concatenate([rows_of(kw_ref, far, 1), rows_of(kw_ref, j, 1)], axis=0)
    edge_bias = jnp.concatenate([valid_row(j >= n_back), jnp.zeros((1, nc), F32)], axis=0)
    early = [
        (edge_keys, lambda: jnp.concatenate([vwt_ref[far], vwt_ref[j]], axis=1),
         jnp.concatenate([anti, tri], axis=0), edge_bias),
        (rows_of(kw_ref, mid, n_back - 1), lambda: tiles_of(vwt_ref, mid, n_back - 1), None, mid_bias),
        (rows_of(ks_ref, j, 1), lambda: vst_ref[j], tri, None),
    ]
    early_scores = [_dot_nt(k, q4) for k, _, _, _ in early]
    first_scores = _dot_nt(ks_ref[0, 0:var_len, :], q4)
    early_piece = lambda i: _softmax_piece(early_scores[i], early[i][1](), HEAD_DIM, early[i][2], early[i][3])

    kc = kc_ref[0, 0]
    n_rows = kc.shape[0]
    n_idx = lax.broadcasted_iota(jnp.int32, (n_rows, Q_BLOCK), 0)
    c_mask = (n_idx * CMP_STRIDE + (CMP_LEN - 1)) <= t_row
    st = _dot_nt(kc, q4)
    yield
    win_pieces = [early_piece(0)]
    p_sum = jnp.zeros((n_rows, Q_BLOCK), F32)
    ps = []
    for r in range(rep):
        t = jnp.where(c_mask, head(st, r), NEG)
        e = jnp.where(c_mask, jnp.exp2(t - jnp.max(t, axis=0, keepdims=True)), 0.0)
        l = jnp.sum(e, axis=0, keepdims=True)
        p = e / jnp.where(l > 0.0, l, 1.0)
        p_sum = p_sum + p
        ps.append(p.astype(BF16))
    o_cmp = _dot(vc_ref[0, 0].astype(F32).T.astype(BF16), jnp.concatenate(ps, axis=1))

    hi = p_sum.astype(BF16)
    lo = (p_sum - hi.astype(F32)).astype(BF16)
    ovl = ovl_ref[...]
    n_pad = -(-n_slc // 8) * 8
    imp = (_dot(ovl, hi) + _dot(ovl, lo))[0:n_pad]
    yield
    win_pieces.append(early_piece(1))
    slc_diag = early_piece(2)
    blk = lax.broadcasted_iota(jnp.int32, (n_pad, Q_BLOCK), 0)
    cur = t_row // SLC_LEN
    forced = (blk == 0) | (blk == cur) | (blk == cur - 1)
    blk_causal = blk * SLC_LEN <= t_row
    imp = jnp.where(forced, 1e4, imp)
    imp = jnp.where(blk_causal, imp, -1.0)
    rank = jnp.zeros((n_pad, Q_BLOCK), F32)
    for sp in range(n_slc):
        row = imp[sp:sp + 1, :]
        ge = jnp.where(row >= imp, 1.0, 0.0)
        gt = jnp.where(row > imp, 1.0, 0.0)
        rank = rank + jnp.where(blk > sp, ge, gt)
    blocks_per_q = Q_BLOCK // SLC_LEN
    blk_bias = jnp.where((rank < float(n_sel)) & (blk < j * blocks_per_q), 0.0, NEG)
    blk_bias = jnp.concatenate([blk_bias] * rep, axis=1)
    yield

    gt = jax.nn.sigmoid(gates).T
    gate = lambda branch: jnp.concatenate([jnp.broadcast_to(gt[3 * r + branch:3 * r + branch + 1], (HEAD_DIM, Q_BLOCK))
                                           for r in range(rep)], axis=1)
    partial = gate(0) * o_cmp + gate(2) * _merge_pieces(win_pieces)
    q_per_var = var_len // Q_BLOCK
    blk_per_var = var_len // SLC_LEN
    blk_per_q = Q_BLOCK // SLC_LEN

    def scores(ci, n_q):
        return _dot_nt(ks_ref[0, ci * var_len:ci * var_len + n_q * Q_BLOCK, :], q4)

    def chunk_piece(ci, n_q, st):
        b0 = ci * blk_per_var
        return _softmax_piece(st, tiles_of(vst_ref, ci * q_per_var, n_q), HEAD_DIM,
                              None, blk_bias[b0:b0 + n_q * blk_per_q])

    first_chunk = chunk_piece(0, q_per_var, first_scores)
    return dict(partial=partial, g_slc=gate(1), pieces=[slc_diag, first_chunk], scores=scores, chunk_piece=chunk_piece,
                q_per_var=q_per_var)


def _nsa_selected(ctx, v, h):
    pieces = list(ctx["pieces"])
    todo = [(ci, ctx["q_per_var"]) for ci in range(1, v)] + ([(v, h)] if v > 0 and h > 0 else [])
    ahead = 2
    sts = [ctx["scores"](*t) for t in todo[:ahead]]
    for i, (ci, n_q) in enumerate(todo):
        if i + ahead < len(todo):
            sts.append(ctx["scores"](*todo[i + ahead]))
        pieces.append(ctx["chunk_piece"](ci, n_q, sts[i]))
        yield
    return ctx["partial"] + ctx["g_slc"] * _merge_pieces(pieces)


def _nsa_kernel(q_ref, ks_ref, vs_ref, kw_ref, vw_ref, kc_ref, vc_ref, gate_ref, ovl_ref,
                o_ref, vst_ref, vwt_ref, *, seq, rep, n_slc, n_sel, var_len, blocks):
    jj = pl.program_id(2)

    @pl.when(jj == 0)
    def _():
        for i in range(seq // Q_BLOCK):
            vst_ref[i] = _transposed_values(vs_ref[0, i * Q_BLOCK:(i + 1) * Q_BLOCK, :], ONES_ROWS)
            vwt_ref[i] = _transposed_values(vw_ref[0, i * Q_BLOCK:(i + 1) * Q_BLOCK, :], ONES_ROWS)

    rows = lambda h: slice(h * Q_BLOCK, (h + 1) * Q_BLOCK)
    ctxs = _interleave([
        _nsa_block(jj * blocks + h, q_ref[0, rows(h), :], gate_ref[0, rows(h), :], ks_ref, kw_ref, kc_ref, vc_ref,
                   ovl_ref, vst_ref, vwt_ref, rep=rep, n_slc=n_slc, n_sel=n_sel, var_len=var_len)
        for h in range(blocks)])

    assert blocks * Q_BLOCK == var_len
    for v in range(seq // var_len):
        @pl.when(jj == v)
        def _(v=v):
            outs = _interleave([_nsa_selected(ctx, v, h) for h, ctx in enumerate(ctxs)])
            for h, out in enumerate(outs):
                for r in range(rep):
                    o_ref[0, rows(h), r * HEAD_DIM:(r + 1) * HEAD_DIM] = (
                        out[:, r * Q_BLOCK:(r + 1) * Q_BLOCK].T.astype(BF16))


def _block_overlap(n_rows, n_cmp, n_slc):
    c0 = np.arange(n_cmp) * CMP_STRIDE
    s0 = np.arange(n_slc) * SLC_LEN
    lo = np.maximum(c0[None, :], s0[:, None])
    hi = np.minimum(c0[None, :] + CMP_LEN, s0[:, None] + SLC_LEN)
    out = np.zeros((LANES, n_rows), np.float32)
    out[:n_slc, :n_cmp] = np.clip(hi - lo, 0, None) / CMP_LEN
    return out


def _nsa_attention(proj, gates, kc, vc, off, batch, seq, rep):
    g = NSA_KV_HEADS
    n_qb = seq // Q_BLOCK
    n_slc = seq // SLC_LEN
    n_sel = min(SLC_TOPK, n_slc)
    n_rows = kc.shape[2]
    n_cmp = (seq - CMP_LEN) // CMP_STRIDE + 1
    qw = rep * HEAD_DIM
    var_len = SLC_VARIANT_LEN if seq % SLC_VARIANT_LEN == 0 else seq
    blocks = var_len // Q_BLOCK
    ovl = jnp.asarray(_block_overlap(n_rows, n_cmp, n_slc), BF16)

    def kv_spec(name):
        base = off[name] // HEAD_DIM
        return pl.BlockSpec((1, seq, HEAD_DIM), lambda b, h, j: (b, 0, base + h))

    cmp_spec = pl.BlockSpec((1, 1, n_rows, HEAD_DIM), lambda b, h, j: (b, h, 0, 0))
    q_base = off["q_n"] // qw
    return pl.pallas_call(
        functools.partial(_nsa_kernel, seq=seq, rep=rep, n_slc=n_slc, n_sel=n_sel, var_len=var_len, blocks=blocks),
        grid=(batch, g, n_qb // blocks),
        scratch_shapes=[
            pltpu.VMEM((seq // Q_BLOCK, HEAD_DIM + ONES_ROWS, Q_BLOCK), BF16),
            pltpu.VMEM((seq // Q_BLOCK, HEAD_DIM + ONES_ROWS, Q_BLOCK), BF16),
        ],
        in_specs=[
            pl.BlockSpec((1, blocks * Q_BLOCK, qw), lambda b, h, j: (b, j, q_base + h)),
            kv_spec("ks"), kv_spec("vs"), kv_spec("kw"), kv_spec("vw"),
            cmp_spec, cmp_spec,
            pl.BlockSpec((1, blocks * Q_BLOCK, LANES), lambda b, h, j: (b, j, h)),
            pl.BlockSpec((LANES, n_rows), lambda b, h, j: (0, 0)),
        ],
        out_specs=pl.BlockSpec((1, blocks * Q_BLOCK, qw), lambda b, h, j: (b, j, h)),
        out_shape=jax.ShapeDtypeStruct((batch, seq, g * qw), BF16),
        compiler_params=_params("parallel", "parallel", "arbitrary"),
        name="nsa_attention",
    )(proj, proj, proj, proj, proj, kc, vc, gates, ovl)


def _diff_block(v, q, k_ref, vts, tri, lam, tq):
    qs = [q[:, c * HEAD_DIM:(c + 1) * HEAD_DIM] for c in range(2)]
    half = tq // 2
    chunks = [(v * tq, half, 0), (v * tq + half, half, half)] + [(kb * tq, tq, 0) for kb in range(v)]
    specs = [(c, ch) for ch in chunks for c in range(2)]
    lanes = lambda c: slice(c * HEAD_DIM, (c + 1) * HEAD_DIM)
    scores = lambda c, ch: _dot_nt(k_ref[0, ch[0]:ch[0] + ch[1], lanes(c)], qs[c][ch[2]:, :])
    pieces = ([], [])
    ahead = 2
    sts = [scores(*spec) for spec in specs[:ahead]]
    for i, (c, (k0, nk, q0)) in enumerate(specs):
        if i + ahead < len(specs):
            sts.append(scores(*specs[i + ahead]))
        vt = vts[k0 // tq][:, k0 % tq:k0 % tq + nk]
        diagonal = k0 >= v * tq
        bias = None if not diagonal else (tri if q0 else jnp.concatenate([tri, jnp.zeros_like(tri)], axis=1))
        m, l, o = _softmax_piece(sts[i], vt, 2 * HEAD_DIM, bias)
        if q0:
            m = jnp.concatenate([jnp.full((1, q0), MASK_FLOOR, F32), m], axis=1)
            l = jnp.concatenate([jnp.zeros((1, q0), F32), l], axis=1)
            o = jnp.concatenate([jnp.zeros((o.shape[0], q0), F32), o], axis=1)
        pieces[c].append((m, l, o))
        yield
    return _merge_pieces(pieces[0]) - lam * _merge_pieces(pieces[1])


def _diff_kernel(q_ref, k_ref, v_ref, lam_ref, sg_ref, o_ref, *, seq, tq, lambda_init):
    n_blocks = seq // tq
    rows = lambda i: slice(i * tq, (i + 1) * tq)
    vts = [_transposed_values(v_ref[0, rows(i), :], 0) for i in range(n_blocks)]
    lv = lam_ref[...]
    lam = (jnp.exp(jnp.sum(lv[0:1] * lv[1:2], axis=-1, keepdims=True))
           - jnp.exp(jnp.sum(lv[2:3] * lv[3:4], axis=-1, keepdims=True)) + lambda_init)
    half = tq // 2
    tri = jnp.where(lax.broadcasted_iota(jnp.int32, (half, half), 0) <= lax.broadcasted_iota(jnp.int32, (half, half), 1),
                    0.0, NEG)
    outs = _interleave([_diff_block(v, q_ref[0, rows(v), :], k_ref, vts, tri, lam, tq) for v in range(n_blocks)])
    for v, o in enumerate(outs):
        o_ref[0, rows(v), :] = (_rms(o.T, sg_ref[...]) * (1.0 - lambda_init)).astype(BF16)


def _diff_attention(proj, lam, subln, off, batch, seq, heads, lambda_init):
    vw = 2 * HEAD_DIM
    tq = _row_tile(seq, DIFF_Q_TILE)
    qb, kb, vb = off["q_d"] // vw, off["k_d"] // vw, off["v_d"] // vw
    return pl.pallas_call(
        functools.partial(_diff_kernel, seq=seq, tq=tq, lambda_init=lambda_init),
        grid=(batch, heads),
        in_specs=[
            pl.BlockSpec((1, seq, vw), lambda b, h: (b, 0, qb + h)),
            pl.BlockSpec((1, seq, vw), lambda b, h: (b, 0, kb + h)),
            pl.BlockSpec((1, seq, vw), lambda b, h: (b, 0, vb + h)),
            pl.BlockSpec((4, HEAD_DIM), lambda b, h: (0, 0)),
            pl.BlockSpec((1, vw), lambda b, h: (0, 0)),
        ],
        out_specs=pl.BlockSpec((1, seq, vw), lambda b, h: (b, 0, h)),
        out_shape=jax.ShapeDtypeStruct((batch, seq, heads * vw), BF16),
        compiler_params=_params("parallel", "parallel"),
        name="diff_attention",
    )(proj, proj, proj, lam, subln)


def _out_kernel(x_ref, a_ref, b_ref, wa_ref, wb_ref, o_ref):
    o_ref[...] = x_ref[...] + _dot(a_ref[...], wa_ref[...]) + _dot(b_ref[...], wb_ref[...])


def _out_project(x, a, b, wa, wb):
    n, d = x.shape
    ka, kb = a.shape[1], b.shape[1]
    tm = _row_tile(n, OUT_ROW_TILE)
    tn = LANES * _largest_divisor(d // LANES, OUT_GROUPS_PER_TILE)
    return pl.pallas_call(
        _out_kernel,
        grid=(n // tm, d // tn),
        in_specs=[
            pl.BlockSpec((tm, tn), lambda i, j: (i, j)),
            pl.BlockSpec((tm, ka), lambda i, j: (i, 0)),
            pl.BlockSpec((tm, kb), lambda i, j: (i, 0)),
            pl.BlockSpec((ka, tn), lambda i, j: (0, j)),
            pl.BlockSpec((kb, tn), lambda i, j: (0, j)),
        ],
        out_specs=pl.BlockSpec((tm, tn), lambda i, j: (i, j)),
        out_shape=jax.ShapeDtypeStruct((n, d), F32),
        compiler_params=_params("parallel", "arbitrary"),
        name="out_proj",
    )(x, a, b, wa, wb)


def _rope_tables(t):
    inv = 1.0 / (ROPE_THETA ** (jnp.arange(0, HEAD_DIM, 2, dtype=F32) / HEAD_DIM))
    ang = jnp.arange(t, dtype=F32)[:, None] * inv[None, :]
    ang = jnp.concatenate([ang, ang], axis=-1)
    sign = jnp.concatenate([-jnp.ones((HEAD_DIM // 2,), F32), jnp.ones((HEAD_DIM // 2,), F32)])
    return jnp.cos(ang), jnp.sin(ang) * sign[None, :]


def _layout(d_model):
    nsa_heads = d_model // (2 * HEAD_DIM)
    diff_heads = d_model // (4 * HEAD_DIM)
    kv = NSA_KV_HEADS * HEAD_DIM
    sizes = dict(q_n=nsa_heads * HEAD_DIM, kc=kv, vc=kv, ks=kv, vs=kv, kw=kv, vw=kv, gates=3 * nsa_heads,
                 q_d=2 * diff_heads * HEAD_DIM, k_d=2 * diff_heads * HEAD_DIM, v_d=diff_heads * 2 * HEAD_DIM)
    orig, off, o, p = {}, {}, 0, 0
    for name in ("q_n", "kc", "vc", "ks", "vs", "kw", "vw", "gates", "q_d", "k_d", "v_d"):
        orig[name] = o
        o += sizes[name]
        if name != "gates":
            off[name] = p
            p += sizes[name]
    groups = lambda names: tuple((off[nm] // LANES, (off[nm] + sizes[nm]) // LANES) for nm in names)
    return sizes, orig, off, groups(("q_n", "kc", "ks", "kw", "q_d", "k_d")), groups(("q_n", "q_d")), nsa_heads, diff_heads


def kernel(x, ffn1_norm, ffn1_w_gate, ffn1_w_up, ffn1_w_down, mix_norm, w_in, cmp_pos_k, cmp_pos_v, cmp_wk1, cmp_wk2, cmp_wv1, cmp_wv2, lam_q1, lam_k1, lam_q2, lam_k2, diff_subln, w_out, ffn2_norm, ffn2_w_gate, ffn2_w_up, ffn2_w_down, final_norm):
    batch, seq, d_model = x.shape
    depth = ffn1_norm.shape[0]
    n = batch * seq
    sizes, orig, off, rope_groups, query_groups, nsa_heads, diff_heads = _layout(d_model)
    g = NSA_KV_HEADS
    rep = nsa_heads // g
    cos, sin_signed = _rope_tables(seq)
    fg = final_norm.reshape(1, d_model)
    ffn1_stacked = (ffn1_w_gate, ffn1_w_up, ffn1_w_down)
    ffn2_stacked = (ffn2_w_gate, ffn2_w_up, ffn2_w_down)
    w_ffn = tuple(w[0].astype(BF16) for w in ffn1_stacked)

    xf = x.reshape(n, d_model)
    for l in range(depth):
        lambda_init = 0.8 - 0.6 * math.exp(-0.3 * l)
        xf, w_ffn = _ffn(xf, ffn1_norm[l].reshape(1, d_model), *w_ffn, fg, False, (ffn2_stacked, l))

        wl = w_in[l]
        gates_end = orig["gates"] + sizes["gates"]
        w_main = jnp.concatenate([wl[:, :orig["gates"]], wl[:, gates_end:]], axis=1).astype(BF16)
        gate_tiles = []
        for h in range(g):
            cols = wl[:, orig["gates"] + h * 3 * rep: orig["gates"] + (h + 1) * 3 * rep]
            gate_tiles.append(jnp.pad(cols, ((0, 0), (0, LANES - 3 * rep))))
        w_gate = jnp.concatenate(gate_tiles, axis=1).astype(BF16)
        proj, gates = _project(xf, mix_norm[l].reshape(1, d_model), w_main, w_gate, cos, sin_signed, rope_groups,
                               query_groups)
        proj3 = proj.reshape(batch, seq, -1)
        gates3 = gates.reshape(batch, seq, -1)

        kc, vc = _compress(proj3, off, batch, seq,
                           cmp_pos_k[l].reshape(1, -1).astype(BF16), cmp_pos_v[l].reshape(1, -1).astype(BF16),
                           cmp_wk1[l].astype(BF16), cmp_wk2[l].astype(BF16),
                           cmp_wv1[l].astype(BF16), cmp_wv2[l].astype(BF16))
        o_nsa = _nsa_attention(proj3, gates3, kc, vc, off, batch, seq, rep)
        lam = jnp.stack([lam_q1[l], lam_k1[l], lam_q2[l], lam_k2[l]])
        o_diff = _diff_attention(proj3, lam, diff_subln[l].reshape(1, -1), off, batch, seq, diff_heads, lambda_init)

        half = sizes["q_n"]
        wo = w_out[l].astype(BF16)
        xf = _out_project(xf, o_nsa.reshape(n, -1), o_diff.reshape(n, -1), wo[:half], wo[half:])

        last = l == depth - 1
        xf, w_ffn = _ffn(xf, ffn2_norm[l].reshape(1, d_model), *w_ffn, fg, last,
                         None if last else (ffn1_stacked, l + 1))
    return xf.reshape(batch, seq, d_model)
```

```python
import functools
import math

import numpy as np
import jax
import jax.numpy as jnp
from jax import lax
from jax.experimental import pallas as pl
from jax.experimental.pallas import tpu as pltpu

HEAD_DIM = 128
NSA_KV_HEADS = 2
CMP_LEN = 32
CMP_STRIDE = 16
SLC_LEN = 64
SLC_TOPK = 16
WINDOW = 512
Q_BLOCK = 128
ROPE_THETA = 10000.0
EPS = 1e-6
NEG = -1e30
MASK_FLOOR = -1e20
LOG2E = 1.4426950408889634
QK_SCALE = HEAD_DIM ** -0.5 * LOG2E
SLC_VARIANT_LEN = 512
LANES = 128
VMEM_LIMIT = 58 * 1024 * 1024
FFN_ROW_TILE = 1024
FFN_HIDDEN_GROUPS = 4
PROJ_ROW_TILE = 512
PROJ_GROUPS_PER_TILE = 22
OUT_ROW_TILE = 512
OUT_GROUPS_PER_TILE = 16
DIFF_Q_TILE = 512

F32 = jnp.float32
BF16 = jnp.bfloat16


def _largest_divisor(n, cap):
    for d in range(min(n, cap), 0, -1):
        if n % d == 0:
            return d
    return 1


def _row_tile(n, cap):
    for d in range(min(n, cap), 7, -1):
        if n % d == 0 and d % 8 == 0:
            return d
    return n


def _rms(x, g):
    return x * lax.rsqrt(jnp.mean(x * x, axis=-1, keepdims=True) + EPS) * g


def _dot(a, b):
    return jnp.dot(a, b, preferred_element_type=F32)


def _dot_nt(a, b):
    return lax.dot_general(a, b, (((1,), (1,)), ((), ())), preferred_element_type=F32)


def _params(*sem):
    return pltpu.CompilerParams(dimension_semantics=sem, vmem_limit_bytes=VMEM_LIMIT)


def _ffn_kernel(*refs, final_norm, cast_next):
    if cast_next:
        (x_ref, g_ref, wg_ref, wu_ref, wd_ref, fg_ref, ng_ref, nu_ref, nd_ref,
         o_ref, og_ref, ou_ref, od_ref, h_ref) = refs
        og_ref[...] = ng_ref[...].astype(BF16)
        ou_ref[...] = nu_ref[...].astype(BF16)
        od_ref[...] = nd_ref[...].astype(BF16)
    else:
        x_ref, g_ref, wg_ref, wu_ref, wd_ref, fg_ref, o_ref, h_ref = refs
    j = pl.program_id(1)

    def half_step_tile():
        h = h_ref[...]
        a = _dot(h, wg_ref[...])
        b = _dot(h, wu_ref[...])
        act = (a * jax.nn.sigmoid(a) * b * 0.5).astype(BF16)
        return _dot(act, wd_ref[...])

    @pl.when(j == 0)
    def _():
        h_ref[...] = _rms(x_ref[...], g_ref[...]).astype(BF16)
        o_ref[...] = x_ref[...] + half_step_tile()

    @pl.when(j > 0)
    def _():
        o_ref[...] += half_step_tile()

    if final_norm:
        @pl.when(j == pl.num_programs(1) - 1)
        def _():
            o_ref[...] = _rms(o_ref[...], fg_ref[...])


def _ffn(x, g, wg, wu, wd, fg, final_norm, next_weights=None):
    n, d = x.shape
    f = wg.shape[1]
    tm = _row_tile(n, FFN_ROW_TILE)
    tf = LANES * _largest_divisor(f // LANES, FFN_HIDDEN_GROUPS)
    n_i, n_j = n // tm, f // tf
    in_specs = [
        pl.BlockSpec((tm, d), lambda i, j: (i, 0)),
        pl.BlockSpec((1, d), lambda i, j: (0, 0)),
        pl.BlockSpec((d, tf), lambda i, j: (0, j)),
        pl.BlockSpec((d, tf), lambda i, j: (0, j)),
        pl.BlockSpec((tf, d), lambda i, j: (j, 0)),
        pl.BlockSpec((1, d), lambda i, j: (0, 0)),
    ]
    out_specs = [pl.BlockSpec((tm, d), lambda i, j: (i, 0))]
    out_shape = [jax.ShapeDtypeStruct((n, d), F32)]
    args = [x, g, wg, wu, wd, fg]
    if next_weights is not None:
        stacked, layer = next_weights
        dr, fr = d // n_i, tf // n_i
        assert dr * n_i == d and fr * n_i == tf and dr % 16 == 0 and fr % 16 == 0
        assert all(w.shape[1:] == s for w, s in zip(stacked, ((d, f), (d, f), (f, d))))
        in_specs += [
            pl.BlockSpec((None, dr, tf), lambda i, j: (layer, i, j)),
            pl.BlockSpec((None, dr, tf), lambda i, j: (layer, i, j)),
            pl.BlockSpec((None, fr, d), lambda i, j: (layer, j * n_i + i, 0)),
        ]
        out_specs += [
            pl.BlockSpec((dr, tf), lambda i, j: (i, j)),
            pl.BlockSpec((dr, tf), lambda i, j: (i, j)),
            pl.BlockSpec((fr, d), lambda i, j: (j * n_i + i, 0)),
        ]
        out_shape += [jax.ShapeDtypeStruct(w.shape[1:], BF16) for w in stacked]
        args += list(stacked)
    outs = pl.pallas_call(
        functools.partial(_ffn_kernel, final_norm=final_norm, cast_next=next_weights is not None),
        grid=(n_i, n_j),
        in_specs=in_specs,
        out_specs=out_specs,
        out_shape=out_shape,
        scratch_shapes=[pltpu.VMEM((tm, d), BF16)],
        compiler_params=_params("parallel", "arbitrary"),
        name="ffn",
    )(*args)
    return outs[0], tuple(outs[1:])


def _proj_kernel(x_ref, g_ref, w_ref, wgate_ref, cos_ref, sin_ref, o_ref, gate_ref, h_ref,
                 *, groups_per_tile, rope_groups, query_groups):
    j = pl.program_id(1)

    @pl.when(j == 0)
    def _():
        h = _rms(x_ref[...], g_ref[...]).astype(BF16)
        h_ref[...] = h
        gate_ref[...] = _dot(h, wgate_ref[...])

    acc = _dot(h_ref[...], w_ref[...])
    cos = cos_ref[...]
    sin = sin_ref[...]
    for gi in range(groups_per_tile):
        group = j * groups_per_tile + gi
        in_any = lambda spans: functools.reduce(jnp.logical_or, [(group >= lo) & (group < hi) for lo, hi in spans])
        is_rope = in_any(rope_groups)
        is_query = in_any(query_groups)
        c = jnp.where(is_rope, cos, 1.0)
        s = jnp.where(is_rope, sin, 0.0)
        xg = acc[:, gi * LANES:(gi + 1) * LANES]
        y = xg * c + pltpu.roll(xg, HEAD_DIM // 2, axis=1) * s
        y = y * jnp.where(is_query, QK_SCALE, 1.0)
        o_ref[:, gi * LANES:(gi + 1) * LANES] = y.astype(BF16)


def _project(x, g, w_main, w_gate, cos, sin_signed, rope_groups, query_groups):
    n, d = x.shape
    t = cos.shape[0]
    width = w_main.shape[1]
    gw = w_gate.shape[1]
    n_groups = width // LANES
    gpt = _largest_divisor(n_groups, PROJ_GROUPS_PER_TILE)
    tn = gpt * LANES
    tm = _row_tile(t, PROJ_ROW_TILE)
    t_blocks = t // tm
    return pl.pallas_call(
        functools.partial(_proj_kernel, groups_per_tile=gpt, rope_groups=rope_groups,
                          query_groups=query_groups),
        grid=(n // tm, width // tn),
        in_specs=[
            pl.BlockSpec((tm, d), lambda i, j: (i, 0)),
            pl.BlockSpec((1, d), lambda i, j: (0, 0)),
            pl.BlockSpec((d, tn), lambda i, j: (0, j)),
            pl.BlockSpec((d, gw), lambda i, j: (0, 0)),
            pl.BlockSpec((tm, HEAD_DIM), lambda i, j: (i % t_blocks, 0)),
            pl.BlockSpec((tm, HEAD_DIM), lambda i, j: (i % t_blocks, 0)),
        ],
        out_specs=[
            pl.BlockSpec((tm, tn), lambda i, j: (i, j)),
            pl.BlockSpec((tm, gw), lambda i, j: (i, 0)),
        ],
        out_shape=[
            jax.ShapeDtypeStruct((n, width), BF16),
            jax.ShapeDtypeStruct((n, gw), F32),
        ],
        scratch_shapes=[pltpu.VMEM((tm, d), BF16)],
        compiler_params=_params("parallel", "arbitrary"),
        name="in_proj",
    )(x, g, w_main, w_gate, cos, sin_signed)


def _cmp_kernel(k_ref, v_ref, pk_ref, pv_ref, wk1_ref, wk2_ref, wv1_ref, wv2_ref, ok_ref, ov_ref, x_ref):
    seq = k_ref.shape[1]
    n_chunks = seq // CMP_STRIDE

    def one(kv_ref, p_ref, w1_ref, w2_ref, o_ref):
        x_ref[...] = kv_ref[0].astype(F32)
        a = jnp.zeros((n_chunks, w1_ref.shape[1]), F32)
        b = jnp.zeros((n_chunks, w1_ref.shape[1]), F32)
        for l in range(CMP_STRIDE):
            rows = x_ref[pl.ds(l, n_chunks, stride=CMP_STRIDE), :].astype(BF16)
            a = a + _dot(rows, w1_ref[l * HEAD_DIM:(l + 1) * HEAD_DIM, :])
            b = b + _dot(rows, w1_ref[(CMP_STRIDE + l) * HEAD_DIM:(CMP_STRIDE + l + 1) * HEAD_DIM, :])
        p = jnp.broadcast_to(p_ref[...], (8, CMP_LEN * HEAD_DIM))
        bias = _dot(p, w1_ref[...])[0:1]
        pre = a + pltpu.roll(b, n_chunks - 1, axis=0) + bias
        hid = (pre * jax.nn.sigmoid(pre)).astype(BF16)
        o_ref[0, 0] = _dot(hid, w2_ref[...]).astype(BF16)

    one(k_ref, pk_ref, wk1_ref, wk2_ref, ok_ref)
    one(v_ref, pv_ref, wv1_ref, wv2_ref, ov_ref)


def _compress(proj, off, batch, seq, pk, pv, wk1, wk2, wv1, wv2):
    g = NSA_KV_HEADS
    nc = seq // CMP_STRIDE
    hid = wk1.shape[1]
    dk = wk2.shape[1]
    assert CMP_LEN == 2 * CMP_STRIDE and dk == HEAD_DIM
    kb, vb = off["kc"] // HEAD_DIM, off["vc"] // HEAD_DIM
    full = lambda shape: pl.BlockSpec(shape, lambda i, j: tuple(0 for _ in shape))
    out_spec = pl.BlockSpec((1, 1, nc, dk), lambda i, j: (i, j, 0, 0))
    return pl.pallas_call(
        _cmp_kernel,
        grid=(batch, g),
        in_specs=[pl.BlockSpec((1, seq, HEAD_DIM), lambda i, j: (i, 0, kb + j)),
                  pl.BlockSpec((1, seq, HEAD_DIM), lambda i, j: (i, 0, vb + j)),
                  full((1, CMP_LEN * dk)), full((1, CMP_LEN * dk)),
                  full((CMP_LEN * dk, hid)), full((hid, dk)), full((CMP_LEN * dk, hid)), full((hid, dk))],
        out_specs=[out_spec, out_spec],
        out_shape=[jax.ShapeDtypeStruct((batch, g, nc, dk), BF16)] * 2,
        scratch_shapes=[pltpu.VMEM((seq, HEAD_DIM), F32)],
        compiler_params=_params("parallel", "parallel"),
        name="nsa_compress",
    )(proj, proj, pk, pv, wk1, wk2, wv1, wv2)


ONES_ROWS = 16


def _transposed_values(v, ones_rows):
    vt = v.astype(F32).T
    if ones_rows:
        vt = jnp.concatenate([vt, jnp.ones((ones_rows, vt.shape[1]), F32)], axis=0)
    return vt.astype(BF16)


def _softmax_piece(st, vt, dv, elem_bias=None, blk_bias=None):
    nk, nc = st.shape
    if elem_bias is not None:
        st = st + elem_bias
    nb = 1 if blk_bias is None else blk_bias.shape[0]
    s4 = st.reshape(nb, nk // nb // 8, 8, nc)
    bm = jnp.max(s4, axis=1)
    if blk_bias is not None:
        bm = bm + blk_bias[:, None, :]
    m = jnp.max(jnp.max(bm, axis=0), axis=0, keepdims=True)
    m = jnp.maximum(m, MASK_FLOOR)
    shift = -m if blk_bias is None else blk_bias - m
    e = jnp.exp2(s4 + shift[:, None, None, :])
    o = _dot(vt, e.reshape(nk, nc).astype(BF16))
    if vt.shape[0] > dv:
        return m, o[dv:dv + 1], o[0:dv]
    return m, jnp.sum(jnp.sum(jnp.sum(e, axis=0), axis=0), axis=0, keepdims=True), o


def _merge_pieces(pieces):
    m = functools.reduce(jnp.maximum, [p[0] for p in pieces])
    ws = [jnp.exp2(p[0] - m) for p in pieces]
    l = sum(w * p[1] for w, p in zip(ws, pieces))
    o = sum(w * p[2] for w, p in zip(ws, pieces))
    return o / l


def _interleave(gens):
    results = [None] * len(gens)
    live = list(range(len(gens)))
    while live:
        for i in list(live):
            try:
                next(gens[i])
            except StopIteration as stop:
                results[i] = stop.value
                live.remove(i)
    return results


def _nsa_block(j, q, gates, ks_ref, kw_ref, kc_ref, vc_ref, ovl_ref, vst_ref, vwt_ref,
               *, rep, n_slc, n_sel, var_len):
    s0 = j * Q_BLOCK
    q4 = jnp.concatenate([q[:, r * HEAD_DIM:(r + 1) * HEAD_DIM] for r in range(rep)], axis=0)
    t_row = s0 + lax.broadcasted_iota(jnp.int32, (1, Q_BLOCK), 1)
    head = lambda a, r: a[:, r * Q_BLOCK:(r + 1) * Q_BLOCK]

    nc = rep * Q_BLOCK
    kk = lax.broadcasted_iota(jnp.int32, (Q_BLOCK, Q_BLOCK), 0)
    qq = lax.broadcasted_iota(jnp.int32, (Q_BLOCK, Q_BLOCK), 1)
    tri = jnp.concatenate([jnp.where(kk <= qq, 0.0, NEG)] * rep, axis=1)
    anti = jnp.concatenate([jnp.where(kk > qq, 0.0, NEG)] * rep, axis=1)
    valid_row = lambda ok: jnp.where(ok, jnp.zeros((1, nc), F32), jnp.full((1, nc), NEG, F32))
    n_back = WINDOW // Q_BLOCK
    far = jnp.maximum(j - n_back, 0)
    mid = jnp.maximum(j - (n_back - 1), 0)
    mid_bias = jnp.concatenate([valid_row(mid + i < j) for i in range(n_back - 1)], axis=0)
    rows_of = lambda ref, blk0, n: ref[0, pl.ds(pl.multiple_of(blk0 * Q_BLOCK, Q_BLOCK), n * Q_BLOCK), :]
    tiles_of = lambda ref, blk0, n: jnp.concatenate([ref[blk0 + i] for i in range(n)], axis=1)
    edge_keys = jnp.concatenate([rows_of(kw_ref, far, 1), rows_of(kw_ref, j, 1)], axis=0)
    edge_bias = jnp.concatenate([valid_row(j >= n_back), jnp.zeros((1, nc), F32)], axis=0)
    early = [
        (edge_keys, lambda: jnp.concatenate([vwt_ref[far], vwt_ref[j]], axis=1),
         jnp.concatenate([anti, tri], axis=0), edge_bias),
        (rows_of(kw_ref, mid, n_back - 1), lambda: tiles_of(vwt_ref, mid, n_back - 1), None, mid_bias),
        (rows_of(ks_ref, j, 1), lambda: vst_ref[j], tri, None),
    ]
    early_scores = [_dot_nt(k, q4) for k, _, _, _ in early]
    first_scores = _dot_nt(ks_ref[0, 0:var_len, :], q4)
    early_piece = lambda i: _softmax_piece(early_scores[i], early[i][1](), HEAD_DIM, early[i][2], early[i][3])

    kc = kc_ref[0, 0]
    n_rows = kc.shape[0]
    n_idx = lax.broadcasted_iota(jnp.int32, (n_rows, Q_BLOCK), 0)
    c_mask = (n_idx * CMP_STRIDE + (CMP_LEN - 1)) <= t_row
    st = _dot_nt(kc, q4)
    yield
    win_pieces = [early_piece(0)]
    p_sum = jnp.zeros((n_rows, Q_BLOCK), F32)
    ps = []
    for r in range(rep):
        t = jnp.where(c_mask, head(st, r), NEG)
        e = jnp.where(c_mask, jnp.exp2(t - jnp.max(t, axis=0, keepdims=True)), 0.0)
        l = jnp.sum(e, axis=0, keepdims=True)
        p = e / jnp.where(l > 0.0, l, 1.0)
        p_sum = p_sum + p
        ps.append(p.astype(BF16))
    o_cmp = _dot(vc_ref[0, 0].astype(F32).T.astype(BF16), jnp.concatenate(ps, axis=1))

    hi = p_sum.astype(BF16)
    lo = (p_sum - hi.astype(F32)).astype(BF16)
    ovl = ovl_ref[...]
    n_pad = -(-n_slc // 8) * 8
    imp = (_dot(ovl, hi) + _dot(ovl, lo))[0:n_pad]
    yield
    win_pieces.append(early_piece(1))
    slc_diag = early_piece(2)
    blk = lax.broadcasted_iota(jnp.int32, (n_pad, Q_BLOCK), 0)
    cur = t_row // SLC_LEN
    forced = (blk == 0) | (blk == cur) | (blk == cur - 1)
    blk_causal = blk * SLC_LEN <= t_row
    imp = jnp.where(forced, 1e4, imp)
    imp = jnp.where(blk_causal, imp, -1.0)
    rank = jnp.zeros((n_pad, Q_BLOCK), F32)
    for sp in range(n_slc):
        row = imp[sp:sp + 1, :]
        ge = jnp.where(row >= imp, 1.0, 0.0)
        gt = jnp.where(row > imp, 1.0, 0.0)
        rank = rank + jnp.where(blk > sp, ge, gt)
    blocks_per_q = Q_BLOCK // SLC_LEN
    blk_bias = jnp.where((rank < float(n_sel)) & (blk < j * blocks_per_q), 0.0, NEG)
    blk_bias = jnp.concatenate([blk_bias] * rep, axis=1)
    yield

    gt = jax.nn.sigmoid(gates).T
    gate = lambda branch: jnp.concatenate([jnp.broadcast_to(gt[3 * r + branch:3 * r + branch + 1], (HEAD_DIM, Q_BLOCK))
                                           for r in range(rep)], axis=1)
    partial = gate(0) * o_cmp + gate(2) * _merge_pieces(win_pieces)
    q_per_var = var_len // Q_BLOCK
    blk_per_var = var_len // SLC_LEN
    blk_per_q = Q_BLOCK // SLC_LEN

    def scores(ci, n_q):
        return _dot_nt(ks_ref[0, ci * var_len:ci * var_len + n_q * Q_BLOCK, :], q4)

    def chunk_piece(ci, n_q, st):
        b0 = ci * blk_per_var
        return _softmax_piece(st, tiles_of(vst_ref, ci * q_per_var, n_q), HEAD_DIM,
                              None, blk_bias[b0:b0 + n_q * blk_per_q])

    first_chunk = chunk_piece(0, q_per_var, first_scores)
    return dict(partial=partial, g_slc=gate(1), pieces=[slc_diag, first_chunk], scores=scores, chunk_piece=chunk_piece,
                q_per_var=q_per_var)


def _nsa_selected(ctx, v, h):
    pieces = list(ctx["pieces"])
    todo = [(ci, ctx["q_per_var"]) for ci in range(1, v)] + ([(v, h)] if v > 0 and h > 0 else [])
    ahead = 2
    sts = [ctx["scores"](*t) for t in todo[:ahead]]
    for i, (ci, n_q) in enumerate(todo):
        if i + ahead < len(todo):
            sts.append(ctx["scores"](*todo[i + ahead]))
        pieces.append(ctx["chunk_piece"](ci, n_q, sts[i]))
        yield
    return ctx["partial"] + ctx["g_slc"] * _merge_pieces(pieces)


def _nsa_kernel(q_ref, ks_ref, vs_ref, kw_ref, vw_ref, kc_ref, vc_ref, gate_ref, ovl_ref,
                o_ref, vst_ref, vwt_ref, *, seq, rep, n_slc, n_sel, var_len, blocks):
    jj = pl.program_id(2)

    @pl.when(jj == 0)
    def _():
        for i in range(seq // Q_BLOCK):
            vst_ref[i] = _transposed_values(vs_ref[0, i * Q_BLOCK:(i + 1) * Q_BLOCK, :], ONES_ROWS)
            vwt_ref[i] = _transposed_values(vw_ref[0, i * Q_BLOCK:(i + 1) * Q_BLOCK, :], ONES_ROWS)

    rows = lambda h: slice(h * Q_BLOCK, (h + 1) * Q_BLOCK)
    ctxs = _interleave([
        _nsa_block(jj * blocks + h, q_ref[0, rows(h), :], gate_ref[0, rows(h), :], ks_ref, kw_ref, kc_ref, vc_ref,
                   ovl_ref, vst_ref, vwt_ref, rep=rep, n_slc=n_slc, n_sel=n_sel, var_len=var_len)
        for h in range(blocks)])

    assert blocks * Q_BLOCK == var_len
    for v in range(seq // var_len):
        @pl.when(jj == v)
        def _(v=v):
            outs = _interleave([_nsa_selected(ctx, v, h) for h, ctx in enumerate(ctxs)])
            for h, out in enumerate(outs):
                for r in range(rep):
                    o_ref[0, rows(h), r * HEAD_DIM:(r + 1) * HEAD_DIM] = (
                        out[:, r * Q_BLOCK:(r + 1) * Q_BLOCK].T.astype(BF16))


def _block_overlap(n_rows, n_cmp, n_slc):
    c0 = np.arange(n_cmp) * CMP_STRIDE
    s0 = np.arange(n_slc) * SLC_LEN
    lo = np.maximum(c0[None, :], s0[:, None])
    hi = np.minimum(c0[None, :] + CMP_LEN, s0[:, None] + SLC_LEN)
    out = np.zeros((LANES, n_rows), np.float32)
    out[:n_slc, :n_cmp] = np.clip(hi - lo, 0, None) / CMP_LEN
    return out


def _nsa_attention(proj, gates, kc, vc, off, batch, seq, rep):
    g = NSA_KV_HEADS
    n_qb = seq // Q_BLOCK
    n_slc = seq // SLC_LEN
    n_sel = min(SLC_TOPK, n_slc)
    n_rows = kc.shape[2]
    n_cmp = (seq - CMP_LEN) // CMP_STRIDE + 1
    qw = rep * HEAD_DIM
    var_len = SLC_VARIANT_LEN if seq % SLC_VARIANT_LEN == 0 else seq
    blocks = var_len // Q_BLOCK
    ovl = jnp.asarray(_block_overlap(n_rows, n_cmp, n_slc), BF16)

    def kv_spec(name):
        base = off[name] // HEAD_DIM
        return pl.BlockSpec((1, seq, HEAD_DIM), lambda b, h, j: (b, 0, base + h))

    cmp_spec = pl.BlockSpec((1, 1, n_rows, HEAD_DIM), lambda b, h, j: (b, h, 0, 0))
    q_base = off["q_n"] // qw
    return pl.pallas_call(
        functools.partial(_nsa_kernel, seq=seq, rep=rep, n_slc=n_slc, n_sel=n_sel, var_len=var_len, blocks=blocks),
        grid=(batch, g, n_qb // blocks),
        scratch_shapes=[
            pltpu.VMEM((seq // Q_BLOCK, HEAD_DIM + ONES_ROWS, Q_BLOCK), BF16),
            pltpu.VMEM((seq // Q_BLOCK, HEAD_DIM + ONES_ROWS, Q_BLOCK), BF16),
        ],
        in_specs=[
            pl.BlockSpec((1, blocks * Q_BLOCK, qw), lambda b, h, j: (b, j, q_base + h)),
            kv_spec("ks"), kv_spec("vs"), kv_spec("kw"), kv_spec("vw"),
            cmp_spec, cmp_spec,
            pl.BlockSpec((1, blocks * Q_BLOCK, LANES), lambda b, h, j: (b, j, h)),
            pl.BlockSpec((LANES, n_rows), lambda b, h, j: (0, 0)),
        ],
        out_specs=pl.BlockSpec((1, blocks * Q_BLOCK, qw), lambda b, h, j: (b, j, h)),
        out_shape=jax.ShapeDtypeStruct((batch, seq, g * qw), BF16),
        compiler_params=_params("parallel", "parallel", "arbitrary"),
        name="nsa_attention",
    )(proj, proj, proj, proj, proj, kc, vc, gates, ovl)


def _diff_block(v, q, k_ref, vts, tri, lam, tq):
    qs = [q[:, c * HEAD_DIM:(c + 1) * HEAD_DIM] for c in range(2)]
    half = tq // 2
    chunks = [(v * tq, half, 0), (v * tq + half, half, half)] + [(kb * tq, tq, 0) for kb in range(v)]
    specs = [(c, ch) for ch in chunks for c in range(2)]
    lanes = lambda c: slice(c * HEAD_DIM, (c + 1) * HEAD_DIM)
    scores = lambda c, ch: _dot_nt(k_ref[0, ch[0]:ch[0] + ch[1], lanes(c)], qs[c][ch[2]:, :])
    pieces = ([], [])
    ahead = 2
    sts = [scores(*spec) for spec in specs[:ahead]]
    for i, (c, (k0, nk, q0)) in enumerate(specs):
        if i + ahead < len(specs):
            sts.append(scores(*specs[i + ahead]))
        vt = vts[k0 // tq][:, k0 % tq:k0 % tq + nk]
        diagonal = k0 >= v * tq
        bias = None if not diagonal else (tri if q0 else jnp.concatenate([tri, jnp.zeros_like(tri)], axis=1))
        m, l, o = _softmax_piece(sts[i], vt, 2 * HEAD_DIM, bias)
        if q0:
            m = jnp.concatenate([jnp.full((1, q0), MASK_FLOOR, F32), m], axis=1)
            l = jnp.concatenate([jnp.zeros((1, q0), F32), l], axis=1)
            o = jnp.concatenate([jnp.zeros((o.shape[0], q0), F32), o], axis=1)
        pieces[c].append((m, l, o))
        yield
    return _merge_pieces(pieces[0]) - lam * _merge_pieces(pieces[1])


def _diff_kernel(q_ref, k_ref, v_ref, lam_ref, sg_ref, o_ref, *, seq, tq, lambda_init):
    n_blocks = seq // tq
    rows = lambda i: slice(i * tq, (i + 1) * tq)
    vts = [_transposed_values(v_ref[0, rows(i), :], 0) for i in range(n_blocks)]
    lv = lam_ref[...]
    lam = (jnp.exp(jnp.sum(lv[0:1] * lv[1:2], axis=-1, keepdims=True))
           - jnp.exp(jnp.sum(lv[2:3] * lv[3:4], axis=-1, keepdims=True)) + lambda_init)
    half = tq // 2
    tri = jnp.where(lax.broadcasted_iota(jnp.int32, (half, half), 0) <= lax.broadcasted_iota(jnp.int32, (half, half), 1),
                    0.0, NEG)
    outs = _interleave([_diff_block(v, q_ref[0, rows(v), :], k_ref, vts, tri, lam, tq) for v in range(n_blocks)])
    for v, o in enumerate(outs):
        o_ref[0, rows(v), :] = (_rms(o.T, sg_ref[...]) * (1.0 - lambda_init)).astype(BF16)


def _diff_attention(proj, lam, subln, off, batch, seq, heads, lambda_init):
    vw = 2 * HEAD_DIM
    tq = _row_tile(seq, DIFF_Q_TILE)
    qb, kb, vb = off["q_d"] // vw, off["k_d"] // vw, off["v_d"] // vw
    return pl.pallas_call(
        functools.partial(_diff_kernel, seq=seq, tq=tq, lambda_init=lambda_init),
        grid=(batch, heads),
        in_specs=[
            pl.BlockSpec((1, seq, vw), lambda b, h: (b, 0, qb + h)),
            pl.BlockSpec((1, seq, vw), lambda b, h: (b, 0, kb + h)),
            pl.BlockSpec((1, seq, vw), lambda b, h: (b, 0, vb + h)),
            pl.BlockSpec((4, HEAD_DIM), lambda b, h: (0, 0)),
            pl.BlockSpec((1, vw), lambda b, h: (0, 0)),
        ],
        out_specs=pl.BlockSpec((1, seq, vw), lambda b, h: (b, 0, h)),
        out_shape=jax.ShapeDtypeStruct((batch, seq, heads * vw), BF16),
        compiler_params=_params("parallel", "parallel"),
        name="diff_attention",
    )(proj, proj, proj, lam, subln)


def _out_kernel(x_ref, a_ref, b_ref, wa_ref, wb_ref, o_ref):
    o_ref[...] = x_ref[...] + _dot(a_ref[...], wa_ref[...]) + _dot(b_ref[...], wb_ref[...])


def _out_project(x, a, b, wa, wb):
    n, d = x.shape
    ka, kb = a.shape[1], b.shape[1]
    tm = _row_tile(n, OUT_ROW_TILE)
    tn = LANES * _largest_divisor(d // LANES, OUT_GROUPS_PER_TILE)
    return pl.pallas_call(
        _out_kernel,
        grid=(n // tm, d // tn),
        in_specs=[
            pl.BlockSpec((tm, tn), lambda i, j: (i, j)),
            pl.BlockSpec((tm, ka), lambda i, j: (i, 0)),
            pl.BlockSpec((tm, kb), lambda i, j: (i, 0)),
            pl.BlockSpec((ka, tn), lambda i, j: (0, j)),
            pl.BlockSpec((kb, tn), lambda i, j: (0, j)),
        ],
        out_specs=pl.BlockSpec((tm, tn), lambda i, j: (i, j)),
        out_shape=jax.ShapeDtypeStruct((n, d), F32),
        compiler_params=_params("parallel", "arbitrary"),
        name="out_proj",
    )(x, a, b, wa, wb)


def _rope_tables(t):
    inv = 1.0 / (ROPE_THETA ** (jnp.arange(0, HEAD_DIM, 2, dtype=F32) / HEAD_DIM))
    ang = jnp.arange(t, dtype=F32)[:, None] * inv[None, :]
    ang = jnp.concatenate([ang, ang], axis=-1)
    sign = jnp.concatenate([-jnp.ones((HEAD_DIM // 2,), F32), jnp.ones((HEAD_DIM // 2,), F32)])
    return jnp.cos(ang), jnp.sin(ang) * sign[None, :]


def _layout(d_model):
    nsa_heads = d_model // (2 * HEAD_DIM)
    diff_heads = d_model // (4 * HEAD_DIM)
    kv = NSA_KV_HEADS * HEAD_DIM
    sizes = dict(q_n=nsa_heads * HEAD_DIM, kc=kv, vc=kv, ks=kv, vs=kv, kw=kv, vw=kv, gates=3 * nsa_heads,
                 q_d=2 * diff_heads * HEAD_DIM, k_d=2 * diff_heads * HEAD_DIM, v_d=diff_heads * 2 * HEAD_DIM)
    orig, off, o, p = {}, {}, 0, 0
    for name in ("q_n", "kc", "vc", "ks", "vs", "kw", "vw", "gates", "q_d", "k_d", "v_d"):
        orig[name] = o
        o += sizes[name]
        if name != "gates":
            off[name] = p
            p += sizes[name]
    groups = lambda names: tuple((off[nm] // LANES, (off[nm] + sizes[nm]) // LANES) for nm in names)
    return sizes, orig, off, groups(("q_n", "kc", "ks", "kw", "q_d", "k_d")), groups(("q_n", "q_d")), nsa_heads, diff_heads


def kernel(x, ffn1_norm, ffn1_w_gate, ffn1_w_up, ffn1_w_down, mix_norm, w_in, cmp_pos_k, cmp_pos_v, cmp_wk1, cmp_wk2, cmp_wv1, cmp_wv2, lam_q1, lam_k1, lam_q2, lam_k2, diff_subln, w_out, ffn2_norm, ffn2_w_gate, ffn2_w_up, ffn2_w_down, final_norm):
    batch, seq, d_model = x.shape
    depth = ffn1_norm.shape[0]
    n = batch * seq
    sizes, orig, off, rope_groups, query_groups, nsa_heads, diff_heads = _layout(d_model)
    g = NSA_KV_HEADS
    rep = nsa_heads // g
    cos, sin_signed = _rope_tables(seq)
    fg = final_norm.reshape(1, d_model)
    ffn1_stacked = (ffn1_w_gate, ffn1_w_up, ffn1_w_down)
    ffn2_stacked = (ffn2_w_gate, ffn2_w_up, ffn2_w_down)
    w_ffn = tuple(w[0].astype(BF16) for w in ffn1_stacked)

    xf = x.reshape(n, d_model)
    for l in range(depth):
        lambda_init = 0.8 - 0.6 * math.exp(-0.3 * l)
        xf, w_ffn = _ffn(xf, ffn1_norm[l].reshape(1, d_model), *w_ffn, fg, False, (ffn2_stacked, l))

        wl = w_in[l]
        gates_end = orig["gates"] + sizes["gates"]
        w_main = jnp.concatenate([wl[:, :orig["gates"]], wl[:, gates_end:]], axis=1).astype(BF16)
        gate_tiles = []
        for h in range(g):
            cols = wl[:, orig["gates"] + h * 3 * rep: orig["gates"] + (h + 1) * 3 * rep]
            gate_tiles.append(jnp.pad(cols, ((0, 0), (0, LANES - 3 * rep))))
        w_gate = jnp.concatenate(gate_tiles, axis=1).astype(BF16)
        proj, gates = _project(xf, mix_norm[l].reshape(1, d_model), w_main, w_gate, cos, sin_signed, rope_groups,
                               query_groups)
        proj3 = proj.reshape(batch, seq, -1)
        gates3 = gates.reshape(batch, seq, -1)

        kc, vc = _compress(proj3, off, batch, seq,
                           cmp_pos_k[l].reshape(1, -1).astype(BF16), cmp_pos_v[l].reshape(1, -1).astype(BF16),
                           cmp_wk1[l].astype(BF16), cmp_wk2[l].astype(BF16),
                           cmp_wv1[l].astype(BF16), cmp_wv2[l].astype(BF16))
        o_nsa = _nsa_attention(proj3, gates3, kc, vc, off, batch, seq, rep)
        lam = jnp.stack([lam_q1[l], lam_k1[l], lam_q2[l], lam_k2[l]])
        o_diff = _diff_attention(proj3, lam, diff_subln[l].reshape(1, -1), off, batch, seq, diff_heads, lambda_init)

        half = sizes["q_n"]
        wo = w_out[l].astype(BF16)
        xf = _out_project(xf, o_nsa.reshape(n, -1), o_diff.reshape(n, -1), wo[:half], wo[half:])

        last = l == depth - 1
        xf, w_ffn = _ffn(xf, ffn2_norm[l].reshape(1, d_model), *w_ffn, fg, last,
                         None if last else (ffn1_stacked, l + 1))
    return xf.reshape(batch, seq, d_model)
```

```python
import functools
import math

import numpy as np
import jax
import jax.numpy as jnp
from jax import lax
from jax.experimental import pallas as pl
from jax.experimental.pallas import tpu as pltpu

HEAD_DIM = 128
NSA_KV_HEADS = 2
CMP_LEN = 32
CMP_STRIDE = 16
SLC_LEN = 64
SLC_TOPK = 16
WINDOW = 512
Q_BLOCK = 128
ROPE_THETA = 10000.0
EPS = 1e-6
NEG = -1e30
MASK_FLOOR = -1e20
LOG2E = 1.4426950408889634
QK_SCALE = HEAD_DIM ** -0.5 * LOG2E
SLC_VARIANT_LEN = 512
LANES = 128
VMEM_LIMIT = 58 * 1024 * 1024
FFN_ROW_TILE = 1024
FFN_HIDDEN_GROUPS = 4
PROJ_ROW_TILE = 512
PROJ_GROUPS_PER_TILE = 22
OUT_ROW_TILE = 512
OUT_GROUPS_PER_TILE = 16
DIFF_Q_TILE = 512

F32 = jnp.float32
BF16 = jnp.bfloat16


def _largest_divisor(n, cap):
    for d in range(min(n, cap), 0, -1):
        if n % d == 0:
            return d
    return 1


def _row_tile(n, cap):
    for d in range(min(n, cap), 7, -1):
        if n % d == 0 and d % 8 == 0:
            return d
    return n


def _rms(x, g):
    return x * lax.rsqrt(jnp.mean(x * x, axis=-1, keepdims=True) + EPS) * g


def _dot(a, b):
    return jnp.dot(a, b, preferred_element_type=F32)


def _dot_nt(a, b):
    return lax.dot_general(a, b, (((1,), (1,)), ((), ())), preferred_element_type=F32)


def _params(*sem):
    return pltpu.CompilerParams(dimension_semantics=sem, vmem_limit_bytes=VMEM_LIMIT)


def _ffn_kernel(*refs, final_norm, cast_next):
    if cast_next:
        (x_ref, g_ref, wg_ref, wu_ref, wd_ref, fg_ref, ng_ref, nu_ref, nd_ref,
         o_ref, og_ref, ou_ref, od_ref, h_ref) = refs
        og_ref[...] = ng_ref[...].astype(BF16)
        ou_ref[...] = nu_ref[...].astype(BF16)
        od_ref[...] = nd_ref[...].astype(BF16)
    else:
        x_ref, g_ref, wg_ref, wu_ref, wd_ref, fg_ref, o_ref, h_ref = refs
    j = pl.program_id(1)

    def half_step_tile():
        h = h_ref[...]
        a = _dot(h, wg_ref[...])
        b = _dot(h, wu_ref[...])
        act = (a * jax.nn.sigmoid(a) * b * 0.5).astype(BF16)
        return _dot(act, wd_ref[...])

    @pl.when(j == 0)
    def _():
        h_ref[...] = _rms(x_ref[...], g_ref[...]).astype(BF16)
        o_ref[...] = x_ref[...] + half_step_tile()

    @pl.when(j > 0)
    def _():
        o_ref[...] += half_step_tile()

    if final_norm:
        @pl.when(j == pl.num_programs(1) - 1)
        def _():
            o_ref[...] = _rms(o_ref[...], fg_ref[...])


def _ffn_hidden_tile(f):
    return LANES * _largest_divisor(f // LANES, FFN_HIDDEN_GROUPS)


def _tile_major(w, tf):
    d, f = w.shape
    return w.reshape(d, f // tf, tf).transpose(1, 0, 2)


def _ffn(x, g, wg, wu, wd, fg, final_norm, next_weights=None):
    n, d = x.shape
    f = wd.shape[0]
    tm = _row_tile(n, FFN_ROW_TILE)
    tf = _ffn_hidden_tile(f)
    n_i, n_j = n // tm, f // tf
    assert wg.shape == wu.shape == (n_j, d, tf)
    in_specs = [
        pl.BlockSpec((tm, d), lambda i, j: (i, 0)),
        pl.BlockSpec((1, d), lambda i, j: (0, 0)),
        pl.BlockSpec((None, d, tf), lambda i, j: (j, 0, 0)),
        pl.BlockSpec((None, d, tf), lambda i, j: (j, 0, 0)),
        pl.BlockSpec((tf, d), lambda i, j: (j, 0)),
        pl.BlockSpec((1, d), lambda i, j: (0, 0)),
    ]
    out_specs = [pl.BlockSpec((tm, d), lambda i, j: (i, 0))]
    out_shape = [jax.ShapeDtypeStruct((n, d), F32)]
    args = [x, g, wg, wu, wd, fg]
    if next_weights is not None:
        stacked, layer = next_weights
        dr, fr = d // n_i, tf // n_i
        assert dr * n_i == d and fr * n_i == tf and dr % 16 == 0 and fr % 16 == 0
        assert all(w.shape[1:] == s for w, s in zip(stacked, ((d, f), (d, f), (f, d))))
        in_specs += [
            pl.BlockSpec((None, dr, tf), lambda i, j: (layer, i, j)),
            pl.BlockSpec((None, dr, tf), lambda i, j: (layer, i, j)),
            pl.BlockSpec((None, fr, d), lambda i, j: (layer, j * n_i + i, 0)),
        ]
        out_specs += [
            pl.BlockSpec((None, dr, tf), lambda i, j: (j, i, 0)),
            pl.BlockSpec((None, dr, tf), lambda i, j: (j, i, 0)),
            pl.BlockSpec((fr, d), lambda i, j: (j * n_i + i, 0)),
        ]
        out_shape += [jax.ShapeDtypeStruct(s, BF16) for s in ((n_j, d, tf), (n_j, d, tf), (f, d))]
        args += list(stacked)
    outs = pl.pallas_call(
        functools.partial(_ffn_kernel, final_norm=final_norm, cast_next=next_weights is not None),
        grid=(n_i, n_j),
        in_specs=in_specs,
        out_specs=out_specs,
        out_shape=out_shape,
        scratch_shapes=[pltpu.VMEM((tm, d), BF16)],
        compiler_params=_params("parallel", "arbitrary"),
        name="ffn",
    )(*args)
    return outs[0], tuple(outs[1:])


def _proj_kernel(x_ref, g_ref, w_ref, wgate_ref, cos_ref, sin_ref, o_ref, gate_ref, h_ref,
                 *, groups_per_tile, rope_groups, query_groups):
    j = pl.program_id(1)

    @pl.when(j == 0)
    def _():
        h = _rms(x_ref[...], g_ref[...]).astype(BF16)
        h_ref[...] = h
        gate_ref[...] = _dot(h, wgate_ref[...])

    acc = _dot(h_ref[...], w_ref[...])
    cos = cos_ref[...]
    sin = sin_ref[...]
    for gi in range(groups_per_tile):
        group = j * groups_per_tile + gi
        in_any = lambda spans: functools.reduce(jnp.logical_or, [(group >= lo) & (group < hi) for lo, hi in spans])
        is_rope = in_any(rope_groups)
        is_query = in_any(query_groups)
        c = jnp.where(is_rope, cos, 1.0)
        s = jnp.where(is_rope, sin, 0.0)
        xg = acc[:, gi * LANES:(gi + 1) * LANES]
        y = xg * c + pltpu.roll(xg, HEAD_DIM // 2, axis=1) * s
        y = y * jnp.where(is_query, QK_SCALE, 1.0)
        o_ref[:, gi * LANES:(gi + 1) * LANES] = y.astype(BF16)


def _project(x, g, w_main, w_gate, cos, sin_signed, rope_groups, query_groups):
    n, d = x.shape
    t = cos.shape[0]
    width = w_main.shape[1]
    gw = w_gate.shape[1]
    n_groups = width // LANES
    gpt = _largest_divisor(n_groups, PROJ_GROUPS_PER_TILE)
    tn = gpt * LANES
    tm = _row_tile(t, PROJ_ROW_TILE)
    t_blocks = t // tm
    return pl.pallas_call(
        functools.partial(_proj_kernel, groups_per_tile=gpt, rope_groups=rope_groups,
                          query_groups=query_groups),
        grid=(n // tm, width // tn),
        in_specs=[
            pl.BlockSpec((tm, d), lambda i, j: (i, 0)),
            pl.BlockSpec((1, d), lambda i, j: (0, 0)),
            pl.BlockSpec((d, tn), lambda i, j: (0, j)),
            pl.BlockSpec((d, gw), lambda i, j: (0, 0)),
            pl.BlockSpec((tm, HEAD_DIM), lambda i, j: (i % t_blocks, 0)),
            pl.BlockSpec((tm, HEAD_DIM), lambda i, j: (i % t_blocks, 0)),
        ],
        out_specs=[
            pl.BlockSpec((tm, tn), lambda i, j: (i, j)),
            pl.BlockSpec((tm, gw), lambda i, j: (i, 0)),
        ],
        out_shape=[
            jax.ShapeDtypeStruct((n, width), BF16),
            jax.ShapeDtypeStruct((n, gw), F32),
        ],
        scratch_shapes=[pltpu.VMEM((tm, d), BF16)],
        compiler_params=_params("parallel", "arbitrary"),
        name="in_proj",
    )(x, g, w_main, w_gate, cos, sin_signed)


def _cmp_kernel(k_ref, v_ref, pk_ref, pv_ref, wk1_ref, wk2_ref, wv1_ref, wv2_ref, ok_ref, ov_ref, x_ref):
    seq = k_ref.shape[1]
    n_chunks = seq // CMP_STRIDE

    def one(kv_ref, p_ref, w1_ref, w2_ref, o_ref):
        x_ref[...] = kv_ref[0].astype(F32)
        a = jnp.zeros((n_chunks, w1_ref.shape[1]), F32)
        b = jnp.zeros((n_chunks, w1_ref.shape[1]), F32)
        for l in range(CMP_STRIDE):
            rows = x_ref[pl.ds(l, n_chunks, stride=CMP_STRIDE), :].astype(BF16)
            a = a + _dot(rows, w1_ref[l * HEAD_DIM:(l + 1) * HEAD_DIM, :])
            b = b + _dot(rows, w1_ref[(CMP_STRIDE + l) * HEAD_DIM:(CMP_STRIDE + l + 1) * HEAD_DIM, :])
        p = jnp.broadcast_to(p_ref[...], (8, CMP_LEN * HEAD_DIM))
        bias = _dot(p, w1_ref[...])[0:1]
        pre = a + pltpu.roll(b, n_chunks - 1, axis=0) + bias
        hid = (pre * jax.nn.sigmoid(pre)).astype(BF16)
        o_ref[0, 0] = _dot(hid, w2_ref[...]).astype(BF16)

    one(k_ref, pk_ref, wk1_ref, wk2_ref, ok_ref)
    one(v_ref, pv_ref, wv1_ref, wv2_ref, ov_ref)


def _compress(proj, off, batch, seq, pk, pv, wk1, wk2, wv1, wv2):
    g = NSA_KV_HEADS
    nc = seq // CMP_STRIDE
    hid = wk1.shape[1]
    dk = wk2.shape[1]
    assert CMP_LEN == 2 * CMP_STRIDE and dk == HEAD_DIM
    kb, vb = off["kc"] // HEAD_DIM, off["vc"] // HEAD_DIM
    full = lambda shape: pl.BlockSpec(shape, lambda i, j: tuple(0 for _ in shape))
    out_spec = pl.BlockSpec((1, 1, nc, dk), lambda i, j: (i, j, 0, 0))
    return pl.pallas_call(
        _cmp_kernel,
        grid=(batch, g),
        in_specs=[pl.BlockSpec((1, seq, HEAD_DIM), lambda i, j: (i, 0, kb + j)),
                  pl.BlockSpec((1, seq, HEAD_DIM), lambda i, j: (i, 0, vb + j)),
                  full((1, CMP_LEN * dk)), full((1, CMP_LEN * dk)),
                  full((CMP_LEN * dk, hid)), full((hid, dk)), full((CMP_LEN * dk, hid)), full((hid, dk))],
        out_specs=[out_spec, out_spec],
        out_shape=[jax.ShapeDtypeStruct((batch, g, nc, dk), BF16)] * 2,
        scratch_shapes=[pltpu.VMEM((seq, HEAD_DIM), F32)],
        compiler_params=_params("parallel", "parallel"),
        name="nsa_compress",
    )(proj, proj, pk, pv, wk1, wk2, wv1, wv2)


ONES_ROWS = 16


def _transposed_values(v, ones_rows):
    vt = v.astype(F32).T
    if ones_rows:
        vt = jnp.concatenate([vt, jnp.ones((ones_rows, vt.shape[1]), F32)], axis=0)
    return vt.astype(BF16)


def _softmax_piece(st, vt, dv, elem_bias=None, blk_bias=None):
    nk, nc = st.shape
    if elem_bias is not None:
        st = st + elem_bias
    nb = 1 if blk_bias is None else blk_bias.shape[0]
    s4 = st.reshape(nb, nk // nb // 8, 8, nc)
    bm = jnp.max(s4, axis=1)
    if blk_bias is not None:
        bm = bm + blk_bias[:, None, :]
    m = jnp.max(jnp.max(bm, axis=0), axis=0, keepdims=True)
    m = jnp.maximum(m, MASK_FLOOR)
    shift = -m if blk_bias is None else blk_bias - m
    e = jnp.exp2(s4 + shift[:, None, None, :])
    o = _dot(vt, e.reshape(nk, nc).astype(BF16))
    if vt.shape[0] > dv:
        return m, o[dv:dv + 1], o[0:dv]
    return m, jnp.sum(jnp.sum(jnp.sum(e, axis=0), axis=0), axis=0, keepdims=True), o


def _merge_pieces(pieces):
    m = functools.reduce(jnp.maximum, [p[0] for p in pieces])
    ws = [jnp.exp2(p[0] - m) for p in pieces]
    l = sum(w * p[1] for w, p in zip(ws, pieces))
    o = sum(w * p[2] for w, p in zip(ws, pieces))
    return o / l


def _interleave(gens):
    results = [None] * len(gens)
    live = list(range(len(gens)))
    while live:
        for i in list(live):
            try:
                next(gens[i])
            except StopIteration as stop:
                results[i] = stop.value
                live.remove(i)
    return results


def _nsa_block(j, q, gates, ks_ref, kw_ref, kc_ref, vc_ref, ovl_ref, vst_ref, vwt_ref,
               *, rep, n_slc, n_sel, var_len):
    s0 = j * Q_BLOCK
    q4 = jnp.concatenate([q[:, r * HEAD_DIM:(r + 1) * HEAD_DIM] for r in range(rep)], axis=0)
    t_row = s0 + lax.broadcasted_iota(jnp.int32, (1, Q_BLOCK), 1)
    head = lambda a, r: a[:, r * Q_BLOCK:(r + 1) * Q_BLOCK]

    nc = rep * Q_BLOCK
    kk = lax.broadcasted_iota(jnp.int32, (Q_BLOCK, Q_BLOCK), 0)
    qq = lax.broadcasted_iota(jnp.int32, (Q_BLOCK, Q_BLOCK), 1)
    tri = jnp.concatenate([jnp.where(kk <= qq, 0.0, NEG)] * rep, axis=1)
    anti = jnp.concatenate([jnp.where(kk > qq, 0.0, NEG)] * rep, axis=1)
    valid_row = lambda ok: jnp.where(ok, jnp.zeros((1, nc), F32), jnp.full((1, nc), NEG, F32))
    n_back = WINDOW // Q_BLOCK
    far = jnp.maximum(j - n_back, 0)
    mid = jnp.maximum(j - (n_back - 1), 0)
    mid_bias = jnp.concatenate([valid_row(mid + i < j) for i in range(n_back - 1)], axis=0)
    rows_of = lambda ref, blk0, n: ref[0, pl.ds(pl.multiple_of(blk0 * Q_BLOCK, Q_BLOCK), n * Q_BLOCK), :]
    tiles_of = lambda ref, blk0, n: jnp.concatenate([ref[blk0 + i] for i in range(n)], axis=1)
    edge_keys = jnp.concatenate([rows_of(kw_ref, far, 1), rows_of(kw_ref, j, 1)], axis=0)
    edge_bias = jnp.concatenate([valid_row(j >= n_back), jnp.zeros((1, nc), F32)], axis=0)
    early = [
        (edge_keys, lambda: jnp.concatenate([vwt_ref[far], vwt_ref[j]], axis=1),
         jnp.concatenate([anti, tri], axis=0), edge_bias),
        (rows_of(kw_ref, mid, n_back - 1), lambda: tiles_of(vwt_ref, mid, n_back - 1), None, mid_bias),
        (rows_of(ks_ref, j, 1), lambda: vst_ref[j], tri, None),
    ]
    early_scores = [_dot_nt(k, q4) for k, _, _, _ in early]
    first_scores = _dot_nt(ks_ref[0, 0:var_len, :], q4)
    early_piece = lambda i: _softmax_piece(early_scores[i], early[i][1](), HEAD_DIM, early[i][2], early[i][3])

    kc = kc_ref[0, 0]
    n_rows = kc.shape[0]
    n_idx = lax.broadcasted_iota(jnp.int32, (n_rows, Q_BLOCK), 0)
    c_mask = (n_idx * CMP_STRIDE + (CMP_LEN - 1)) <= t_row
    st = _dot_nt(kc, q4)
    yield
    win_pieces = [early_piece(0)]
    p_sum = jnp.zeros((n_rows, Q_BLOCK), F32)
    ps = []
    for r in range(rep):
        t = jnp.where(c_mask, head(st, r), NEG)
        e = jnp.where(c_mask, jnp.exp2(t - jnp.max(t, axis=0, keepdims=True)), 0.0)
        l = jnp.sum(e, axis=0, keepdims=True)
        p = e / jnp.where(l > 0.0, l, 1.0)
        p_sum = p_sum + p
        ps.append(p.astype(BF16))
    o_cmp = _dot(vc_ref[0, 0].astype(F32).T.astype(BF16), jnp.concatenate(ps, axis=1))

    hi = p_sum.astype(BF16)
    lo = (p_sum - hi.astype(F32)).astype(BF16)
    ovl = ovl_ref[...]
    n_pad = -(-n_slc // 8) * 8
    imp = (_dot(ovl, hi) + _dot(ovl, lo))[0:n_pad]
    yield
    win_pieces.append(early_piece(1))
    slc_diag = early_piece(2)
    blk = lax.broadcasted_iota(jnp.int32, (n_pad, Q_BLOCK), 0)
    cur = t_row // SLC_LEN
    forced = (blk == 0) | (blk == cur) | (blk == cur - 1)
    blk_causal = blk * SLC_LEN <= t_row
    imp = jnp.where(forced, 1e4, imp)
    imp = jnp.where(blk_causal, imp, -1.0)
    rank = jnp.zeros((n_pad, Q_BLOCK), F32)
    for sp in range(n_slc):
        row = imp[sp:sp + 1, :]
        ge = jnp.where(row >= imp, 1.0, 0.0)
        gt = jnp.where(row > imp, 1.0, 0.0)
        rank = rank + jnp.where(blk > sp, ge, gt)
    blocks_per_q = Q_BLOCK // SLC_LEN
    blk_bias = jnp.where((rank < float(n_sel)) & (blk < j * blocks_per_q), 0.0, NEG)
    blk_bias = jnp.concatenate([blk_bias] * rep, axis=1)
    yield

    gt = jax.nn.sigmoid(gates).T
    gate = lambda branch: jnp.concatenate([jnp.broadcast_to(gt[3 * r + branch:3 * r + branch + 1], (HEAD_DIM, Q_BLOCK))
                                           for r in range(rep)], axis=1)
    partial = gate(0) * o_cmp + gate(2) * _merge_pieces(win_pieces)
    q_per_var = var_len // Q_BLOCK
    blk_per_var = var_len // SLC_LEN
    blk_per_q = Q_BLOCK // SLC_LEN

    def scores(ci, n_q):
        return _dot_nt(ks_ref[0, ci * var_len:ci * var_len + n_q * Q_BLOCK, :], q4)

    def chunk_piece(ci, n_q, st):
        b0 = ci * blk_per_var
        return _softmax_piece(st, tiles_of(vst_ref, ci * q_per_var, n_q), HEAD_DIM,
                              None, blk_bias[b0:b0 + n_q * blk_per_q])

    first_chunk = chunk_piece(0, q_per_var, first_scores)
    return dict(partial=partial, g_slc=gate(1), pieces=[slc_diag, first_chunk], scores=scores, chunk_piece=chunk_piece,
                q_per_var=q_per_var)


def _nsa_selected(ctx, v, h):
    pieces = list(ctx["pieces"])
    todo = [(ci, ctx["q_per_var"]) for ci in range(1, v)] + ([(v, h)] if v > 0 and h > 0 else [])
    ahead = 2
    sts = [ctx["scores"](*t) for t in todo[:ahead]]
    for i, (ci, n_q) in enumerate(todo):
        if i + ahead < len(todo):
            sts.append(ctx["scores"](*todo[i + ahead]))
        pieces.append(ctx["chunk_piece"](ci, n_q, sts[i]))
        yield
    return ctx["partial"] + ctx["g_slc"] * _merge_pieces(pieces)


def _nsa_kernel(q_ref, ks_ref, vs_ref, kw_ref, vw_ref, kc_ref, vc_ref, gate_ref, ovl_ref,
                o_ref, vst_ref, vwt_ref, *, seq, rep, n_slc, n_sel, var_len, blocks):
    jj = pl.program_id(2)

    @pl.when(jj == 0)
    def _():
        for i in range(seq // Q_BLOCK):
            vst_ref[i] = _transposed_values(vs_ref[0, i * Q_BLOCK:(i + 1) * Q_BLOCK, :], ONES_ROWS)
            vwt_ref[i] = _transposed_values(vw_ref[0, i * Q_BLOCK:(i + 1) * Q_BLOCK, :], ONES_ROWS)

    rows = lambda h: slice(h * Q_BLOCK, (h + 1) * Q_BLOCK)
    ctxs = _interleave([
        _nsa_block(jj * blocks + h, q_ref[0, rows(h), :], gate_ref[0, rows(h), :], ks_ref, kw_ref, kc_ref, vc_ref,
                   ovl_ref, vst_ref, vwt_ref, rep=rep, n_slc=n_slc, n_sel=n_sel, var_len=var_len)
        for h in range(blocks)])

    assert blocks * Q_BLOCK == var_len
    for v in range(seq // var_len):
        @pl.when(jj == v)
        def _(v=v):
            outs = _interleave([_nsa_selected(ctx, v, h) for h, ctx in enumerate(ctxs)])
            for h, out in enumerate(outs):
                for r in range(rep):
                    o_ref[0, rows(h), r * HEAD_DIM:(r + 1) * HEAD_DIM] = (
                        out[:, r * Q_BLOCK:(r + 1) * Q_BLOCK].T.astype(BF16))


def _block_overlap(n_rows, n_cmp, n_slc):
    c0 = np.arange(n_cmp) * CMP_STRIDE
    s0 = np.arange(n_slc) * SLC_LEN
    lo = np.maximum(c0[None, :], s0[:, None])
    hi = np.minimum(c0[None, :] + CMP_LEN, s0[:, None] + SLC_LEN)
    out = np.zeros((LANES, n_rows), np.float32)
    out[:n_slc, :n_cmp] = np.clip(hi - lo, 0, None) / CMP_LEN
    return out


def _nsa_attention(proj, gates, kc, vc, off, batch, seq, rep):
    g = NSA_KV_HEADS
    n_qb = seq // Q_BLOCK
    n_slc = seq // SLC_LEN
    n_sel = min(SLC_TOPK, n_slc)
    n_rows = kc.shape[2]
    n_cmp = (seq - CMP_LEN) // CMP_STRIDE + 1
    qw = rep * HEAD_DIM
    var_len = SLC_VARIANT_LEN if seq % SLC_VARIANT_LEN == 0 else seq
    blocks = var_len // Q_BLOCK
    ovl = jnp.asarray(_block_overlap(n_rows, n_cmp, n_slc), BF16)

    def kv_spec(name):
        base = off[name] // HEAD_DIM
        return pl.BlockSpec((1, seq, HEAD_DIM), lambda b, h, j: (b, 0, base + h))

    cmp_spec = pl.BlockSpec((1, 1, n_rows, HEAD_DIM), lambda b, h, j: (b, h, 0, 0))
    q_base = off["q_n"] // qw
    return pl.pallas_call(
        functools.partial(_nsa_kernel, seq=seq, rep=rep, n_slc=n_slc, n_sel=n_sel, var_len=var_len, blocks=blocks),
        grid=(batch, g, n_qb // blocks),
        scratch_shapes=[
            pltpu.VMEM((seq // Q_BLOCK, HEAD_DIM + ONES_ROWS, Q_BLOCK), BF16),
            pltpu.VMEM((seq // Q_BLOCK, HEAD_DIM + ONES_ROWS, Q_BLOCK), BF16),
        ],
        in_specs=[
            pl.BlockSpec((1, blocks * Q_BLOCK, qw), lambda b, h, j: (b, j, q_base + h)),
            kv_spec("ks"), kv_spec("vs"), kv_spec("kw"), kv_spec("vw"),
            cmp_spec, cmp_spec,
            pl.BlockSpec((1, blocks * Q_BLOCK, LANES), lambda b, h, j: (b, j, h)),
            pl.BlockSpec((LANES, n_rows), lambda b, h, j: (0, 0)),
        ],
        out_specs=pl.BlockSpec((1, blocks * Q_BLOCK, qw), lambda b, h, j: (b, j, h)),
        out_shape=jax.ShapeDtypeStruct((batch, seq, g * qw), BF16),
        compiler_params=_params("parallel", "parallel", "arbitrary"),
        name="nsa_attention",
    )(proj, proj, proj, proj, proj, kc, vc, gates, ovl)


def _diff_block(v, q, k_ref, vts, tri, lam, tq):
    qs = [q[:, c * HEAD_DIM:(c + 1) * HEAD_DIM] for c in range(2)]
    half = tq // 2
    chunks = [(v * tq, half, 0), (v * tq + half, half, half)] + [(kb * tq, tq, 0) for kb in range(v)]
    specs = [(c, ch) for ch in chunks for c in range(2)]
    lanes = lambda c: slice(c * HEAD_DIM, (c + 1) * HEAD_DIM)
    scores = lambda c, ch: _dot_nt(k_ref[0, ch[0]:ch[0] + ch[1], lanes(c)], qs[c][ch[2]:, :])
    pieces = ([], [])
    ahead = 2
    sts = [scores(*spec) for spec in specs[:ahead]]
    for i, (c, (k0, nk, q0)) in enumerate(specs):
        if i + ahead < len(specs):
            sts.append(scores(*specs[i + ahead]))
        vt = vts[k0 // tq][:, k0 % tq:k0 % tq + nk]
        diagonal = k0 >= v * tq
        bias = None if not diagonal else (tri if q0 else jnp.concatenate([tri, jnp.zeros_like(tri)], axis=1))
        m, l, o = _softmax_piece(sts[i], vt, 2 * HEAD_DIM, bias)
        if q0:
            m = jnp.concatenate([jnp.full((1, q0), MASK_FLOOR, F32), m], axis=1)
            l = jnp.concatenate([jnp.zeros((1, q0), F32), l], axis=1)
            o = jnp.concatenate([jnp.zeros((o.shape[0], q0), F32), o], axis=1)
        pieces[c].append((m, l, o))
        yield
    return _merge_pieces(pieces[0]) - lam * _merge_pieces(pieces[1])


def _diff_kernel(q_ref, k_ref, v_ref, lam_ref, sg_ref, o_ref, *, seq, tq, lambda_init):
    n_blocks = seq // tq
    rows = lambda i: slice(i * tq, (i + 1) * tq)
    vts = [_transposed_values(v_ref[0, rows(i), :], 0) for i in range(n_blocks)]
    lv = lam_ref[...]
    lam = (jnp.exp(jnp.sum(lv[0:1] * lv[1:2], axis=-1, keepdims=True))
           - jnp.exp(jnp.sum(lv[2:3] * lv[3:4], axis=-1, keepdims=True)) + lambda_init)
    half = tq // 2
    tri = jnp.where(lax.broadcasted_iota(jnp.int32, (half, half), 0) <= lax.broadcasted_iota(jnp.int32, (half, half), 1),
                    0.0, NEG)
    outs = _interleave([_diff_block(v, q_ref[0, rows(v), :], k_ref, vts, tri, lam, tq) for v in range(n_blocks)])
    for v, o in enumerate(outs):
        o_ref[0, rows(v), :] = (_rms(o.T, sg_ref[...]) * (1.0 - lambda_init)).astype(BF16)


def _diff_attention(proj, lam, subln, off, batch, seq, heads, lambda_init):
    vw = 2 * HEAD_DIM
    tq = _row_tile(seq, DIFF_Q_TILE)
    qb, kb, vb = off["q_d"] // vw, off["k_d"] // vw, off["v_d"] // vw
    return pl.pallas_call(
        functools.partial(_diff_kernel, seq=seq, tq=tq, lambda_init=lambda_init),
        grid=(batch, heads),
        in_specs=[
            pl.BlockSpec((1, seq, vw), lambda b, h: (b, 0, qb + h)),
            pl.BlockSpec((1, seq, vw), lambda b, h: (b, 0, kb + h)),
            pl.BlockSpec((1, seq, vw), lambda b, h: (b, 0, vb + h)),
            pl.BlockSpec((4, HEAD_DIM), lambda b, h: (0, 0)),
            pl.BlockSpec((1, vw), lambda b, h: (0, 0)),
        ],
        out_specs=pl.BlockSpec((1, seq, vw), lambda b, h: (b, 0, h)),
        out_shape=jax.ShapeDtypeStruct((batch, seq, heads * vw), BF16),
        compiler_params=_params("parallel", "parallel"),
        name="diff_attention",
    )(proj, proj, proj, lam, subln)


def _out_kernel(x_ref, a_ref, b_ref, wa_ref, wb_ref, o_ref):
    o_ref[...] = x_ref[...] + _dot(a_ref[...], wa_ref[...]) + _dot(b_ref[...], wb_ref[...])


def _out_project(x, a, b, wa, wb):
    n, d = x.shape
    ka, kb = a.shape[1], b.shape[1]
    tm = _row_tile(n, OUT_ROW_TILE)
    tn = LANES * _largest_divisor(d // LANES, OUT_GROUPS_PER_TILE)
    return pl.pallas_call(
        _out_kernel,
        grid=(n // tm, d // tn),
        in_specs=[
            pl.BlockSpec((tm, tn), lambda i, j: (i, j)),
            pl.BlockSpec((tm, ka), lambda i, j: (i, 0)),
            pl.BlockSpec((tm, kb), lambda i, j: (i, 0)),
            pl.BlockSpec((ka, tn), lambda i, j: (0, j)),
            pl.BlockSpec((kb, tn), lambda i, j: (0, j)),
        ],
        out_specs=pl.BlockSpec((tm, tn), lambda i, j: (i, j)),
        out_shape=jax.ShapeDtypeStruct((n, d), F32),
        compiler_params=_params("parallel", "arbitrary"),
        name="out_proj",
    )(x, a, b, wa, wb)


def _rope_tables(t):
    inv = 1.0 / (ROPE_THETA ** (jnp.arange(0, HEAD_DIM, 2, dtype=F32) / HEAD_DIM))
    ang = jnp.arange(t, dtype=F32)[:, None] * inv[None, :]
    ang = jnp.concatenate([ang, ang], axis=-1)
    sign = jnp.concatenate([-jnp.ones((HEAD_DIM // 2,), F32), jnp.ones((HEAD_DIM // 2,), F32)])
    return jnp.cos(ang), jnp.sin(ang) * sign[None, :]


def _layout(d_model):
    nsa_heads = d_model // (2 * HEAD_DIM)
    diff_heads = d_model // (4 * HEAD_DIM)
    kv = NSA_KV_HEADS * HEAD_DIM
    sizes = dict(q_n=nsa_heads * HEAD_DIM, kc=kv, vc=kv, ks=kv, vs=kv, kw=kv, vw=kv, gates=3 * nsa_heads,
                 q_d=2 * diff_heads * HEAD_DIM, k_d=2 * diff_heads * HEAD_DIM, v_d=diff_heads * 2 * HEAD_DIM)
    orig, off, o, p = {}, {}, 0, 0
    for name in ("q_n", "kc", "vc", "ks", "vs", "kw", "vw", "gates", "q_d", "k_d", "v_d"):
        orig[name] = o
        o += sizes[name]
        if name != "gates":
            off[name] = p
            p += sizes[name]
    groups = lambda names: tuple((off[nm] // LANES, (off[nm] + sizes[nm]) // LANES) for nm in names)
    return sizes, orig, off, groups(("q_n", "kc", "ks", "kw", "q_d", "k_d")), groups(("q_n", "q_d")), nsa_heads, diff_heads


def kernel(x, ffn1_norm, ffn1_w_gate, ffn1_w_up, ffn1_w_down, mix_norm, w_in, cmp_pos_k, cmp_pos_v, cmp_wk1, cmp_wk2, cmp_wv1, cmp_wv2, lam_q1, lam_k1, lam_q2, lam_k2, diff_subln, w_out, ffn2_norm, ffn2_w_gate, ffn2_w_up, ffn2_w_down, final_norm):
    batch, seq, d_model = x.shape
    depth = ffn1_norm.shape[0]
    n = batch * seq
    sizes, orig, off, rope_groups, query_groups, nsa_heads, diff_heads = _layout(d_model)
    g = NSA_KV_HEADS
    rep = nsa_heads // g
    cos, sin_signed = _rope_tables(seq)
    fg = final_norm.reshape(1, d_model)
    ffn1_stacked = (ffn1_w_gate, ffn1_w_up, ffn1_w_down)
    ffn2_stacked = (ffn2_w_gate, ffn2_w_up, ffn2_w_down)
    tf = _ffn_hidden_tile(ffn1_w_down.shape[1])
    w_ffn = (_tile_major(ffn1_w_gate[0].astype(BF16), tf), _tile_major(ffn1_w_up[0].astype(BF16), tf),
             ffn1_w_down[0].astype(BF16))

    xf = x.reshape(n, d_model)
    for l in range(depth):
        lambda_init = 0.8 - 0.6 * math.exp(-0.3 * l)
        xf, w_ffn = _ffn(xf, ffn1_norm[l].reshape(1, d_model), *w_ffn, fg, False, (ffn2_stacked, l))

        wl = w_in[l]
        gates_end = orig["gates"] + sizes["gates"]
        w_main = jnp.concatenate([wl[:, :orig["gates"]], wl[:, gates_end:]], axis=1).astype(BF16)
        gate_tiles = []
        for h in range(g):
            cols = wl[:, orig["gates"] + h * 3 * rep: orig["gates"] + (h + 1) * 3 * rep]
            gate_tiles.append(jnp.pad(cols, ((0, 0), (0, LANES - 3 * rep))))
        w_gate = jnp.concatenate(gate_tiles, axis=1).astype(BF16)
        proj, gates = _project(xf, mix_norm[l].reshape(1, d_model), w_main, w_gate, cos, sin_signed, rope_groups,
                               query_groups)
        proj3 = proj.reshape(batch, seq, -1)
        gates3 = gates.reshape(batch, seq, -1)

        kc, vc = _compress(proj3, off, batch, seq,
                           cmp_pos_k[l].reshape(1, -1).astype(BF16), cmp_pos_v[l].reshape(1, -1).astype(BF16),
                           cmp_wk1[l].astype(BF16), cmp_wk2[l].astype(BF16),
                           cmp_wv1[l].astype(BF16), cmp_wv2[l].astype(BF16))
        o_nsa = _nsa_attention(proj3, gates3, kc, vc, off, batch, seq, rep)
        lam = jnp.stack([lam_q1[l], lam_k1[l], lam_q2[l], lam_k2[l]])
        o_diff = _diff_attention(proj3, lam, diff_subln[l].reshape(1, -1), off, batch, seq, diff_heads, lambda_init)

        half = sizes["q_n"]
        wo = w_out[l].astype(BF16)
        xf = _out_project(xf, o_nsa.reshape(n, -1), o_diff.reshape(n, -1), wo[:half], wo[half:])

        last = l == depth - 1
        xf, w_ffn = _ffn(xf, ffn2_norm[l].reshape(1, d_model), *w_ffn, fg, last,
                         None if last else (ffn1_stacked, l + 1))
    return xf.reshape(batch, seq, d_model)
```

```python
import functools
import math

import numpy as np
import jax
import jax.numpy as jnp
from jax import lax
from jax.experimental import pallas as pl
from jax.experimental.pallas import tpu as pltpu

HEAD_DIM = 128
NSA_KV_HEADS = 2
CMP_LEN = 32
CMP_STRIDE = 16
SLC_LEN = 64
SLC_TOPK = 16
WINDOW = 512
Q_BLOCK = 128
ROPE_THETA = 10000.0
EPS = 1e-6
NEG = -1e30
MASK_FLOOR = -1e20
LOG2E = 1.4426950408889634
QK_SCALE = HEAD_DIM ** -0.5 * LOG2E
SLC_VARIANT_LEN = 512
LANES = 128
VMEM_LIMIT = 58 * 1024 * 1024
FFN_ROW_TILE = 1024
FFN_HIDDEN_GROUPS = 4
PROJ_ROW_TILE = 512
PROJ_GROUPS_PER_TILE = 22
OUT_ROW_TILE = 512
OUT_GROUPS_PER_TILE = 16
DIFF_Q_TILE = 512

F32 = jnp.float32
BF16 = jnp.bfloat16


def _largest_divisor(n, cap):
    for d in range(min(n, cap), 0, -1):
        if n % d == 0:
            return d
    return 1


def _row_tile(n, cap):
    for d in range(min(n, cap), 7, -1):
        if n % d == 0 and d % 8 == 0:
            return d
    return n


def _rms(x, g):
    return x * lax.rsqrt(jnp.mean(x * x, axis=-1, keepdims=True) + EPS) * g


def _dot(a, b):
    return jnp.dot(a, b, preferred_element_type=F32)


def _dot_nt(a, b):
    return lax.dot_general(a, b, (((1,), (1,)), ((), ())), preferred_element_type=F32)


def _params(*sem):
    return pltpu.CompilerParams(dimension_semantics=sem, vmem_limit_bytes=VMEM_LIMIT)


def _ffn_kernel(*refs, final_norm, cast_next):
    if cast_next:
        (x_ref, g_ref, wg_ref, wu_ref, wd_ref, fg_ref, ng_ref, nu_ref, nd_ref,
         o_ref, og_ref, ou_ref, od_ref, h_ref) = refs
        og_ref[...] = ng_ref[...].astype(BF16)
        ou_ref[...] = nu_ref[...].astype(BF16)
        od_ref[...] = nd_ref[...].astype(BF16)
    else:
        x_ref, g_ref, wg_ref, wu_ref, wd_ref, fg_ref, o_ref, h_ref = refs
    j = pl.program_id(1)

    def half_step_tile():
        h = h_ref[...]
        a = _dot(h, wg_ref[...])
        b = _dot(h, wu_ref[...])
        act = (a * jax.nn.sigmoid(a) * b * 0.5).astype(BF16)
        return _dot(act, wd_ref[...])

    @pl.when(j == 0)
    def _():
        h_ref[...] = _rms(x_ref[...], g_ref[...]).astype(BF16)
        o_ref[...] = x_ref[...] + half_step_tile()

    @pl.when(j > 0)
    def _():
        o_ref[...] += half_step_tile()

    if final_norm:
        @pl.when(j == pl.num_programs(1) - 1)
        def _():
            o_ref[...] = _rms(o_ref[...], fg_ref[...])


def _ffn(x, g, wg, wu, wd, fg, final_norm, next_weights=None):
    n, d = x.shape
    f = wg.shape[1]
    tm = _row_tile(n, FFN_ROW_TILE)
    tf = LANES * _largest_divisor(f // LANES, FFN_HIDDEN_GROUPS)
    n_i, n_j = n // tm, f // tf
    in_specs = [
        pl.BlockSpec((tm, d), lambda i, j: (i, 0)),
        pl.BlockSpec((1, d), lambda i, j: (0, 0)),
        pl.BlockSpec((d, tf), lambda i, j: (0, j)),
        pl.BlockSpec((d, tf), lambda i, j: (0, j)),
        pl.BlockSpec((tf, d), lambda i, j: (j, 0)),
        pl.BlockSpec((1, d), lambda i, j: (0, 0)),
    ]
    out_specs = [pl.BlockSpec((tm, d), lambda i, j: (i, 0))]
    out_shape = [jax.ShapeDtypeStruct((n, d), F32)]
    args = [x, g, wg, wu, wd, fg]
    if next_weights is not None:
        stacked, layer = next_weights
        dr, fr = d // n_i, tf // n_i
        assert dr * n_i == d and fr * n_i == tf and dr % 16 == 0 and fr % 16 == 0
        assert all(w.shape[1:] == s for w, s in zip(stacked, ((d, f), (d, f), (f, d))))
        in_specs += [
            pl.BlockSpec((None, dr, tf), lambda i, j: (layer, i, j)),
            pl.BlockSpec((None, dr, tf), lambda i, j: (layer, i, j)),
            pl.BlockSpec((None, fr, d), lambda i, j: (layer, j * n_i + i, 0)),
        ]
        out_specs += [
            pl.BlockSpec((dr, tf), lambda i, j: (i, j)),
            pl.BlockSpec((dr, tf), lambda i, j: (i, j)),
            pl.BlockSpec((fr, d), lambda i, j: (j * n_i + i, 0)),
        ]
        out_shape += [jax.ShapeDtypeStruct(w.shape[1:], BF16) for w in stacked]
        args += list(stacked)
    outs = pl.pallas_call(
        functools.partial(_ffn_kernel, final_norm=final_norm, cast_next=next_weights is not None),
        grid=(n_i, n_j),
        in_specs=in_specs,
        out_specs=out_specs,
        out_shape=out_shape,
        scratch_shapes=[pltpu.VMEM((tm, d), BF16)],
        compiler_params=_params("parallel", "arbitrary"),
        name="ffn",
    )(*args)
    return outs[0], tuple(outs[1:])


def _proj_kernel(x_ref, g_ref, w_ref, wgate_ref, cos_ref, sin_ref, o_ref, gate_ref, h_ref,
                 *, n_tiles, groups_per_tile, rope_groups, query_groups):
    j = pl.program_id(1)

    def column_tile(tile):
        acc = _dot(h_ref[...], w_ref[...])
        cos = cos_ref[...]
        sin = sin_ref[...]
        for gi in range(groups_per_tile):
            group = tile * groups_per_tile + gi
            in_any = lambda spans: any(lo <= group < hi for lo, hi in spans)
            y = acc[:, gi * LANES:(gi + 1) * LANES]
            if in_any(rope_groups):
                y = y * cos + pltpu.roll(y, HEAD_DIM // 2, axis=1) * sin
            if in_any(query_groups):
                y = y * QK_SCALE
            o_ref[:, gi * LANES:(gi + 1) * LANES] = y.astype(BF16)

    for tile in range(n_tiles):
        @pl.when(j == tile)
        def _(tile=tile):
            if tile == 0:
                h = _rms(x_ref[...], g_ref[...]).astype(BF16)
                h_ref[...] = h
                gate_ref[...] = _dot(h, wgate_ref[...])
            column_tile(tile)


def _project(x, g, w_main, w_gate, cos, sin_signed, rope_groups, query_groups):
    n, d = x.shape
    t = cos.shape[0]
    width = w_main.shape[1]
    gw = w_gate.shape[1]
    n_groups = width // LANES
    gpt = _largest_divisor(n_groups, PROJ_GROUPS_PER_TILE)
    tn = gpt * LANES
    tm = _row_tile(t, PROJ_ROW_TILE)
    t_blocks = t // tm
    return pl.pallas_call(
        functools.partial(_proj_kernel, n_tiles=width // tn, groups_per_tile=gpt, rope_groups=rope_groups,
                          query_groups=query_groups),
        grid=(n // tm, width // tn),
        in_specs=[
            pl.BlockSpec((tm, d), lambda i, j: (i, 0)),
            pl.BlockSpec((1, d), lambda i, j: (0, 0)),
            pl.BlockSpec((d, tn), lambda i, j: (0, j)),
            pl.BlockSpec((d, gw), lambda i, j: (0, 0)),
            pl.BlockSpec((tm, HEAD_DIM), lambda i, j: (i % t_blocks, 0)),
            pl.BlockSpec((tm, HEAD_DIM), lambda i, j: (i % t_blocks, 0)),
        ],
        out_specs=[
            pl.BlockSpec((tm, tn), lambda i, j: (i, j)),
            pl.BlockSpec((tm, gw), lambda i, j: (i, 0)),
        ],
        out_shape=[
            jax.ShapeDtypeStruct((n, width), BF16),
            jax.ShapeDtypeStruct((n, gw), F32),
        ],
        scratch_shapes=[pltpu.VMEM((tm, d), BF16)],
        compiler_params=_params("parallel", "arbitrary"),
        name="in_proj",
    )(x, g, w_main, w_gate, cos, sin_signed)


def _cmp_kernel(k_ref, v_ref, pk_ref, pv_ref, wk1_ref, wk2_ref, wv1_ref, wv2_ref, ok_ref, ov_ref, x_ref):
    seq = k_ref.shape[1]
    n_chunks = seq // CMP_STRIDE

    def one(kv_ref, p_ref, w1_ref, w2_ref, o_ref):
        x_ref[...] = kv_ref[0].astype(F32)
        a = jnp.zeros((n_chunks, w1_ref.shape[1]), F32)
        b = jnp.zeros((n_chunks, w1_ref.shape[1]), F32)
        for l in range(CMP_STRIDE):
            rows = x_ref[pl.ds(l, n_chunks, stride=CMP_STRIDE), :].astype(BF16)
            a = a + _dot(rows, w1_ref[l * HEAD_DIM:(l + 1) * HEAD_DIM, :])
            b = b + _dot(rows, w1_ref[(CMP_STRIDE + l) * HEAD_DIM:(CMP_STRIDE + l + 1) * HEAD_DIM, :])
        p = jnp.broadcast_to(p_ref[...], (8, CMP_LEN * HEAD_DIM))
        bias = _dot(p, w1_ref[...])[0:1]
        pre = a + pltpu.roll(b, n_chunks - 1, axis=0) + bias
        hid = (pre * jax.nn.sigmoid(pre)).astype(BF16)
        o_ref[0, 0] = _dot(hid, w2_ref[...]).astype(BF16)

    one(k_ref, pk_ref, wk1_ref, wk2_ref, ok_ref)
    one(v_ref, pv_ref, wv1_ref, wv2_ref, ov_ref)


def _compress(proj, off, batch, seq, pk, pv, wk1, wk2, wv1, wv2):
    g = NSA_KV_HEADS
    nc = seq // CMP_STRIDE
    hid = wk1.shape[1]
    dk = wk2.shape[1]
    assert CMP_LEN == 2 * CMP_STRIDE and dk == HEAD_DIM
    kb, vb = off["kc"] // HEAD_DIM, off["vc"] // HEAD_DIM
    full = lambda shape: pl.BlockSpec(shape, lambda i, j: tuple(0 for _ in shape))
    out_spec = pl.BlockSpec((1, 1, nc, dk), lambda i, j: (i, j, 0, 0))
    return pl.pallas_call(
        _cmp_kernel,
        grid=(batch, g),
        in_specs=[pl.BlockSpec((1, seq, HEAD_DIM), lambda i, j: (i, 0, kb + j)),
                  pl.BlockSpec((1, seq, HEAD_DIM), lambda i, j: (i, 0, vb + j)),
                  full((1, CMP_LEN * dk)), full((1, CMP_LEN * dk)),
                  full((CMP_LEN * dk, hid)), full((hid, dk)), full((CMP_LEN * dk, hid)), full((hid, dk))],
        out_specs=[out_spec, out_spec],
        out_shape=[jax.ShapeDtypeStruct((batch, g, nc, dk), BF16)] * 2,
        scratch_shapes=[pltpu.VMEM((seq, HEAD_DIM), F32)],
        compiler_params=_params("parallel", "parallel"),
        name="nsa_compress",
    )(proj, proj, pk, pv, wk1, wk2, wv1, wv2)


ONES_ROWS = 16


def _transposed_values(v, ones_rows):
    vt = v.astype(F32).T
    if ones_rows:
        vt = jnp.concatenate([vt, jnp.ones((ones_rows, vt.shape[1]), F32)], axis=0)
    return vt.astype(BF16)


def _softmax_piece(st, vt, dv, elem_bias=None, blk_bias=None):
    nk, nc = st.shape
    if elem_bias is not None:
        st = st + elem_bias
    nb = 1 if blk_bias is None else blk_bias.shape[0]
    s4 = st.reshape(nb, nk // nb // 8, 8, nc)
    bm = jnp.max(s4, axis=1)
    if blk_bias is not None:
        bm = bm + blk_bias[:, None, :]
    m = jnp.max(jnp.max(bm, axis=0), axis=0, keepdims=True)
    m = jnp.maximum(m, MASK_FLOOR)
    shift = -m if blk_bias is None else blk_bias - m
    e = jnp.exp2(s4 + shift[:, None, None, :])
    o = _dot(vt, e.reshape(nk, nc).astype(BF16))
    if vt.shape[0] > dv:
        return m, o[dv:dv + 1], o[0:dv]
    return m, jnp.sum(jnp.sum(jnp.sum(e, axis=0), axis=0), axis=0, keepdims=True), o


def _merge_pieces(pieces):
    m = functools.reduce(jnp.maximum, [p[0] for p in pieces])
    ws = [jnp.exp2(p[0] - m) for p in pieces]
    l = sum(w * p[1] for w, p in zip(ws, pieces))
    o = sum(w * p[2] for w, p in zip(ws, pieces))
    return o / l


def _interleave(gens):
    results = [None] * len(gens)
    live = list(range(len(gens)))
    while live:
        for i in list(live):
            try:
                next(gens[i])
            except StopIteration as stop:
                results[i] = stop.value
                live.remove(i)
    return results


def _nsa_block(j, q, gates, ks_ref, kw_ref, kc_ref, vc_ref, ovl_ref, vst_ref, vwt_ref,
               *, rep, n_slc, n_sel, var_len):
    s0 = j * Q_BLOCK
    q4 = jnp.concatenate([q[:, r * HEAD_DIM:(r + 1) * HEAD_DIM] for r in range(rep)], axis=0)
    t_row = s0 + lax.broadcasted_iota(jnp.int32, (1, Q_BLOCK), 1)
    head = lambda a, r: a[:, r * Q_BLOCK:(r + 1) * Q_BLOCK]

    nc = rep * Q_BLOCK
    kk = lax.broadcasted_iota(jnp.int32, (Q_BLOCK, Q_BLOCK), 0)
    qq = lax.broadcasted_iota(jnp.int32, (Q_BLOCK, Q_BLOCK), 1)
    tri = jnp.concatenate([jnp.where(kk <= qq, 0.0, NEG)] * rep, axis=1)
    anti = jnp.concatenate([jnp.where(kk > qq, 0.0, NEG)] * rep, axis=1)
    valid_row = lambda ok: jnp.where(ok, jnp.zeros((1, nc), F32), jnp.full((1, nc), NEG, F32))
    n_back = WINDOW // Q_BLOCK
    far = jnp.maximum(j - n_back, 0)
    mid = jnp.maximum(j - (n_back - 1), 0)
    mid_bias = jnp.concatenate([valid_row(mid + i < j) for i in range(n_back - 1)], axis=0)
    rows_of = lambda ref, blk0, n: ref[0, pl.ds(pl.multiple_of(blk0 * Q_BLOCK, Q_BLOCK), n * Q_BLOCK), :]
    tiles_of = lambda ref, blk0, n: jnp.concatenate([ref[blk0 + i] for i in range(n)], axis=1)
    edge_keys = jnp.concatenate([rows_of(kw_ref, far, 1), rows_of(kw_ref, j, 1)], axis=0)
    edge_bias = jnp.concatenate([valid_row(j >= n_back), jnp.zeros((1, nc), F32)], axis=0)
    early = [
        (edge_keys, lambda: jnp.concatenate([vwt_ref[far], vwt_ref[j]], axis=1),
         jnp.concatenate([anti, tri], axis=0), edge_bias),
        (rows_of(kw_ref, mid, n_back - 1), lambda: tiles_of(vwt_ref, mid, n_back - 1), None, mid_bias),
        (rows_of(ks_ref, j, 1), lambda: vst_ref[j], tri, None),
    ]
    early_scores = [_dot_nt(k, q4) for k, _, _, _ in early]
    first_scores = _dot_nt(ks_ref[0, 0:var_len, :], q4)
    early_piece = lambda i: _softmax_piece(early_scores[i], early[i][1](), HEAD_DIM, early[i][2], early[i][3])

    kc = kc_ref[0, 0]
    n_rows = kc.shape[0]
    n_idx = lax.broadcasted_iota(jnp.int32, (n_rows, Q_BLOCK), 0)
    c_mask = (n_idx * CMP_STRIDE + (CMP_LEN - 1)) <= t_row
    st = _dot_nt(kc, q4)
    yield
    win_pieces = [early_piece(0)]
    p_sum = jnp.zeros((n_rows, Q_BLOCK), F32)
    ps = []
    for r in range(rep):
        t = jnp.where(c_mask, head(st, r), NEG)
        e = jnp.where(c_mask, jnp.exp2(t - jnp.max(t, axis=0, keepdims=True)), 0.0)
        l = jnp.sum(e, axis=0, keepdims=True)
        p = e / jnp.where(l > 0.0, l, 1.0)
        p_sum = p_sum + p
        ps.append(p.astype(BF16))
    o_cmp = _dot(vc_ref[0, 0].astype(F32).T.astype(BF16), jnp.concatenate(ps, axis=1))

    hi = p_sum.astype(BF16)
    lo = (p_sum - hi.astype(F32)).astype(BF16)
    ovl = ovl_ref[...]
    n_pad = -(-n_slc // 8) * 8
    imp = (_dot(ovl, hi) + _dot(ovl, lo))[0:n_pad]
    yield
    win_pieces.append(early_piece(1))
    slc_diag = early_piece(2)
    blk = lax.broadcasted_iota(jnp.int32, (n_pad, Q_BLOCK), 0)
    cur = t_row // SLC_LEN
    forced = (blk == 0) | (blk == cur) | (blk == cur - 1)
    blk_causal = blk * SLC_LEN <= t_row
    imp = jnp.where(forced, 1e4, imp)
    imp = jnp.where(blk_causal, imp, -1.0)
    rank = jnp.zeros((n_pad, Q_BLOCK), F32)
    for sp in range(n_slc):
        row = imp[sp:sp + 1, :]
        ge = jnp.where(row >= imp, 1.0, 0.0)
        gt = jnp.where(row > imp, 1.0, 0.0)
        rank = rank + jnp.where(blk > sp, ge, gt)
    blocks_per_q = Q_BLOCK // SLC_LEN
    blk_bias = jnp.where((rank < float(n_sel)) & (blk < j * blocks_per_q), 0.0, NEG)
    blk_bias = jnp.concatenate([blk_bias] * rep, axis=1)
    yield

    gt = jax.nn.sigmoid(gates).T
    gate = lambda branch: jnp.concatenate([jnp.broadcast_to(gt[3 * r + branch:3 * r + branch + 1], (HEAD_DIM, Q_BLOCK))
                                           for r in range(rep)], axis=1)
    partial = gate(0) * o_cmp + gate(2) * _merge_pieces(win_pieces)
    q_per_var = var_len // Q_BLOCK
    blk_per_var = var_len // SLC_LEN
    blk_per_q = Q_BLOCK // SLC_LEN

    def scores(ci, n_q):
        return _dot_nt(ks_ref[0, ci * var_len:ci * var_len + n_q * Q_BLOCK, :], q4)

    def chunk_piece(ci, n_q, st):
        b0 = ci * blk_per_var
        return _softmax_piece(st, tiles_of(vst_ref, ci * q_per_var, n_q), HEAD_DIM,
                              None, blk_bias[b0:b0 + n_q * blk_per_q])

    first_chunk = chunk_piece(0, q_per_var, first_scores)
    return dict(partial=partial, g_slc=gate(1), pieces=[slc_diag, first_chunk], scores=scores, chunk_piece=chunk_piece,
                q_per_var=q_per_var)


def _nsa_selected(ctx, v, h):
    pieces = list(ctx["pieces"])
    todo = [(ci, ctx["q_per_var"]) for ci in range(1, v)] + ([(v, h)] if v > 0 and h > 0 else [])
    ahead = 2
    sts = [ctx["scores"](*t) for t in todo[:ahead]]
    for i, (ci, n_q) in enumerate(todo):
        if i + ahead < len(todo):
            sts.append(ctx["scores"](*todo[i + ahead]))
        pieces.append(ctx["chunk_piece"](ci, n_q, sts[i]))
        yield
    return ctx["partial"] + ctx["g_slc"] * _merge_pieces(pieces)


def _nsa_kernel(q_ref, ks_ref, vs_ref, kw_ref, vw_ref, kc_ref, vc_ref, gate_ref, ovl_ref,
                o_ref, vst_ref, vwt_ref, *, seq, rep, n_slc, n_sel, var_len, blocks):
    jj = pl.program_id(2)

    @pl.when(jj == 0)
    def _():
        for i in range(seq // Q_BLOCK):
            vst_ref[i] = _transposed_values(vs_ref[0, i * Q_BLOCK:(i + 1) * Q_BLOCK, :], ONES_ROWS)
            vwt_ref[i] = _transposed_values(vw_ref[0, i * Q_BLOCK:(i + 1) * Q_BLOCK, :], ONES_ROWS)

    rows = lambda h: slice(h * Q_BLOCK, (h + 1) * Q_BLOCK)
    ctxs = _interleave([
        _nsa_block(jj * blocks + h, q_ref[0, rows(h), :], gate_ref[0, rows(h), :], ks_ref, kw_ref, kc_ref, vc_ref,
                   ovl_ref, vst_ref, vwt_ref, rep=rep, n_slc=n_slc, n_sel=n_sel, var_len=var_len)
        for h in range(blocks)])

    assert blocks * Q_BLOCK == var_len
    for v in range(seq // var_len):
        @pl.when(jj == v)
        def _(v=v):
            outs = _interleave([_nsa_selected(ctx, v, h) for h, ctx in enumerate(ctxs)])
            for h, out in enumerate(outs):
                for r in range(rep):
                    o_ref[0, rows(h), r * HEAD_DIM:(r + 1) * HEAD_DIM] = (
                        out[:, r * Q_BLOCK:(r + 1) * Q_BLOCK].T.astype(BF16))


def _block_overlap(n_rows, n_cmp, n_slc):
    c0 = np.arange(n_cmp) * CMP_STRIDE
    s0 = np.arange(n_slc) * SLC_LEN
    lo = np.maximum(c0[None, :], s0[:, None])
    hi = np.minimum(c0[None, :] + CMP_LEN, s0[:, None] + SLC_LEN)
    out = np.zeros((LANES, n_rows), np.float32)
    out[:n_slc, :n_cmp] = np.clip(hi - lo, 0, None) / CMP_LEN
    return out


def _nsa_attention(proj, gates, kc, vc, off, batch, seq, rep):
    g = NSA_KV_HEADS
    n_qb = seq // Q_BLOCK
    n_slc = seq // SLC_LEN
    n_sel = min(SLC_TOPK, n_slc)
    n_rows = kc.shape[2]
    n_cmp = (seq - CMP_LEN) // CMP_STRIDE + 1
    qw = rep * HEAD_DIM
    var_len = SLC_VARIANT_LEN if seq % SLC_VARIANT_LEN == 0 else seq
    blocks = var_len // Q_BLOCK
    ovl = jnp.asarray(_block_overlap(n_rows, n_cmp, n_slc), BF16)

    def kv_spec(name):
        base = off[name] // HEAD_DIM
        return pl.BlockSpec((1, seq, HEAD_DIM), lambda b, h, j: (b, 0, base + h))

    cmp_spec = pl.BlockSpec((1, 1, n_rows, HEAD_DIM), lambda b, h, j: (b, h, 0, 0))
    q_base = off["q_n"] // qw
    return pl.pallas_call(
        functools.partial(_nsa_kernel, seq=seq, rep=rep, n_slc=n_slc, n_sel=n_sel, var_len=var_len, blocks=blocks),
        grid=(batch, g, n_qb // blocks),
        scratch_shapes=[
            pltpu.VMEM((seq // Q_BLOCK, HEAD_DIM + ONES_ROWS, Q_BLOCK), BF16),
            pltpu.VMEM((seq // Q_BLOCK, HEAD_DIM + ONES_ROWS, Q_BLOCK), BF16),
        ],
        in_specs=[
            pl.BlockSpec((1, blocks * Q_BLOCK, qw), lambda b, h, j: (b, j, q_base + h)),
            kv_spec("ks"), kv_spec("vs"), kv_spec("kw"), kv_spec("vw"),
            cmp_spec, cmp_spec,
            pl.BlockSpec((1, blocks * Q_BLOCK, LANES), lambda b, h, j: (b, j, h)),
            pl.BlockSpec((LANES, n_rows), lambda b, h, j: (0, 0)),
        ],
        out_specs=pl.BlockSpec((1, blocks * Q_BLOCK, qw), lambda b, h, j: (b, j, h)),
        out_shape=jax.ShapeDtypeStruct((batch, seq, g * qw), BF16),
        compiler_params=_params("parallel", "parallel", "arbitrary"),
        name="nsa_attention",
    )(proj, proj, proj, proj, proj, kc, vc, gates, ovl)


def _diff_block(v, q, k_ref, vts, tri, lam, tq):
    qs = [q[:, c * HEAD_DIM:(c + 1) * HEAD_DIM] for c in range(2)]
    half = tq // 2
    chunks = [(v * tq, half, 0), (v * tq + half, half, half)] + [(kb * tq, tq, 0) for kb in range(v)]
    specs = [(c, ch) for ch in chunks for c in range(2)]
    lanes = lambda c: slice(c * HEAD_DIM, (c + 1) * HEAD_DIM)
    scores = lambda c, ch: _dot_nt(k_ref[0, ch[0]:ch[0] + ch[1], lanes(c)], qs[c][ch[2]:, :])
    pieces = ([], [])
    ahead = 2
    sts = [scores(*spec) for spec in specs[:ahead]]
    for i, (c, (k0, nk, q0)) in enumerate(specs):
        if i + ahead < len(specs):
            sts.append(scores(*specs[i + ahead]))
        vt = vts[k0 // tq][:, k0 % tq:k0 % tq + nk]
        diagonal = k0 >= v * tq
        bias = None if not diagonal else (tri if q0 else jnp.concatenate([tri, jnp.zeros_like(tri)], axis=1))
        m, l, o = _softmax_piece(sts[i], vt, 2 * HEAD_DIM, bias)
        if q0:
            m = jnp.concatenate([jnp.full((1, q0), MASK_FLOOR, F32), m], axis=1)
            l = jnp.concatenate([jnp.zeros((1, q0), F32), l], axis=1)
            o = jnp.concatenate([jnp.zeros((o.shape[0], q0), F32), o], axis=1)
        pieces[c].append((m, l, o))
        yield
    return _merge_pieces(pieces[0]) - lam * _merge_pieces(pieces[1])


def _diff_kernel(q_ref, k_ref, v_ref, lam_ref, sg_ref, o_ref, *, seq, tq, lambda_init):
    n_blocks = seq // tq
    rows = lambda i: slice(i * tq, (i + 1) * tq)
    vts = [_transposed_values(v_ref[0, rows(i), :], 0) for i in range(n_blocks)]
    lv = lam_ref[...]
    lam = (jnp.exp(jnp.sum(lv[0:1] * lv[1:2], axis=-1, keepdims=True))
           - jnp.exp(jnp.sum(lv[2:3] * lv[3:4], axis=-1, keepdims=True)) + lambda_init)
    half = tq // 2
    tri = jnp.where(lax.broadcasted_iota(jnp.int32, (half, half), 0) <= lax.broadcasted_iota(jnp.int32, (half, half), 1),
                    0.0, NEG)
    outs = _interleave([_diff_block(v, q_ref[0, rows(v), :], k_ref, vts, tri, lam, tq) for v in range(n_blocks)])
    for v, o in enumerate(outs):
        o_ref[0, rows(v), :] = (_rms(o.T, sg_ref[...]) * (1.0 - lambda_init)).astype(BF16)


def _diff_attention(proj, lam, subln, off, batch, seq, heads, lambda_init):
    vw = 2 * HEAD_DIM
    tq = _row_tile(seq, DIFF_Q_TILE)
    qb, kb, vb = off["q_d"] // vw, off["k_d"] // vw, off["v_d"] // vw
    return pl.pallas_call(
        functools.partial(_diff_kernel, seq=seq, tq=tq, lambda_init=lambda_init),
        grid=(batch, heads),
        in_specs=[
            pl.BlockSpec((1, seq, vw), lambda b, h: (b, 0, qb + h)),
            pl.BlockSpec((1, seq, vw), lambda b, h: (b, 0, kb + h)),
            pl.BlockSpec((1, seq, vw), lambda b, h: (b, 0, vb + h)),
            pl.BlockSpec((4, HEAD_DIM), lambda b, h: (0, 0)),
            pl.BlockSpec((1, vw), lambda b, h: (0, 0)),
        ],
        out_specs=pl.BlockSpec((1, seq, vw), lambda b, h: (b, 0, h)),
        out_shape=jax.ShapeDtypeStruct((batch, seq, heads * vw), BF16),
        compiler_params=_params("parallel", "parallel"),
        name="diff_attention",
    )(proj, proj, proj, lam, subln)


def _out_kernel(x_ref, a_ref, b_ref, wa_ref, wb_ref, o_ref):
    o_ref[...] = x_ref[...] + _dot(a_ref[...], wa_ref[...]) + _dot(b_ref[...], wb_ref[...])


def _out_project(x, a, b, wa, wb):
    n, d = x.shape
    ka, kb = a.shape[1], b.shape[1]
    tm = _row_tile(n, OUT_ROW_TILE)
    tn = LANES * _largest_divisor(d // LANES, OUT_GROUPS_PER_TILE)
    return pl.pallas_call(
        _out_kernel,
        grid=(n // tm, d // tn),
        in_specs=[
            pl.BlockSpec((tm, tn), lambda i, j: (i, j)),
            pl.BlockSpec((tm, ka), lambda i, j: (i, 0)),
            pl.BlockSpec((tm, kb), lambda i, j: (i, 0)),
            pl.BlockSpec((ka, tn), lambda i, j: (0, j)),
            pl.BlockSpec((kb, tn), lambda i, j: (0, j)),
        ],
        out_specs=pl.BlockSpec((tm, tn), lambda i, j: (i, j)),
        out_shape=jax.ShapeDtypeStruct((n, d), F32),
        compiler_params=_params("parallel", "arbitrary"),
        name="out_proj",
    )(x, a, b, wa, wb)


def _rope_tables(t):
    inv = 1.0 / (ROPE_THETA ** (jnp.arange(0, HEAD_DIM, 2, dtype=F32) / HEAD_DIM))
    ang = jnp.arange(t, dtype=F32)[:, None] * inv[None, :]
    ang = jnp.concatenate([ang, ang], axis=-1)
    sign = jnp.concatenate([-jnp.ones((HEAD_DIM // 2,), F32), jnp.ones((HEAD_DIM // 2,), F32)])
    return jnp.cos(ang), jnp.sin(ang) * sign[None, :]


def _layout(d_model):
    nsa_heads = d_model // (2 * HEAD_DIM)
    diff_heads = d_model // (4 * HEAD_DIM)
    kv = NSA_KV_HEADS * HEAD_DIM
    sizes = dict(q_n=nsa_heads * HEAD_DIM, kc=kv, vc=kv, ks=kv, vs=kv, kw=kv, vw=kv, gates=3 * nsa_heads,
                 q_d=2 * diff_heads * HEAD_DIM, k_d=2 * diff_heads * HEAD_DIM, v_d=diff_heads * 2 * HEAD_DIM)
    orig, off, o, p = {}, {}, 0, 0
    for name in ("q_n", "kc", "vc", "ks", "vs", "kw", "vw", "gates", "q_d", "k_d", "v_d"):
        orig[name] = o
        o += sizes[name]
        if name != "gates":
            off[name] = p
            p += sizes[name]
    groups = lambda names: tuple((off[nm] // LANES, (off[nm] + sizes[nm]) // LANES) for nm in names)
    return sizes, orig, off, groups(("q_n", "kc", "ks", "kw", "q_d", "k_d")), groups(("q_n", "q_d")), nsa_heads, diff_heads


def kernel(x, ffn1_norm, ffn1_w_gate, ffn1_w_up, ffn1_w_down, mix_norm, w_in, cmp_pos_k, cmp_pos_v, cmp_wk1, cmp_wk2, cmp_wv1, cmp_wv2, lam_q1, lam_k1, lam_q2, lam_k2, diff_subln, w_out, ffn2_norm, ffn2_w_gate, ffn2_w_up, ffn2_w_down, final_norm):
    batch, seq, d_model = x.shape
    depth = ffn1_norm.shape[0]
    n = batch * seq
    sizes, orig, off, rope_groups, query_groups, nsa_heads, diff_heads = _layout(d_model)
    g = NSA_KV_HEADS
    rep = nsa_heads // g
    cos, sin_signed = _rope_tables(seq)
    fg = final_norm.reshape(1, d_model)
    ffn1_stacked = (ffn1_w_gate, ffn1_w_up, ffn1_w_down)
    ffn2_stacked = (ffn2_w_gate, ffn2_w_up, ffn2_w_down)
    w_ffn = tuple(w[0].astype(BF16) for w in ffn1_stacked)

    xf = x.reshape(n, d_model)
    for l in range(depth):
        lambda_init = 0.8 - 0.6 * math.exp(-0.3 * l)
        xf, w_ffn = _ffn(xf, ffn1_norm[l].reshape(1, d_model), *w_ffn, fg, False, (ffn2_stacked, l))

        wl = w_in[l]
        gates_end = orig["gates"] + sizes["gates"]
        w_main = jnp.concatenate([wl[:, :orig["gates"]], wl[:, gates_end:]], axis=1).astype(BF16)
        gate_tiles = []
        for h in range(g):
            cols = wl[:, orig["gates"] + h * 3 * rep: orig["gates"] + (h + 1) * 3 * rep]
            gate_tiles.append(jnp.pad(cols, ((0, 0), (0, LANES - 3 * rep))))
        w_gate = jnp.concatenate(gate_tiles, axis=1).astype(BF16)
        proj, gates = _project(xf, mix_norm[l].reshape(1, d_model), w_main, w_gate, cos, sin_signed, rope_groups,
                               query_groups)
        proj3 = proj.reshape(batch, seq, -1)
        gates3 = gates.reshape(batch, seq, -1)

        kc, vc = _compress(proj3, off, batch, seq,
                           cmp_pos_k[l].reshape(1, -1).astype(BF16), cmp_pos_v[l].reshape(1, -1).astype(BF16),
                           cmp_wk1[l].astype(BF16), cmp_wk2[l].astype(BF16),
                           cmp_wv1[l].astype(BF16), cmp_wv2[l].astype(BF16))
        o_nsa = _nsa_attention(proj3, gates3, kc, vc, off, batch, seq, rep)
        lam = jnp.stack([lam_q1[l], lam_k1[l], lam_q2[l], lam_k2[l]])
        o_diff = _diff_attention(proj3, lam, diff_subln[l].reshape(1, -1), off, batch, seq, diff_heads, lambda_init)

        half = sizes["q_n"]
        wo = w_out[l].astype(BF16)
        xf = _out_project(xf, o_nsa.reshape(n, -1), o_diff.reshape(n, -1), wo[:half], wo[half:])

        last = l == depth - 1
        xf, w_ffn = _ffn(xf, ffn2_norm[l].reshape(1, d_model), *w_ffn, fg, last,
                         None if last else (ffn1_stacked, l + 1))
    return xf.reshape(batch, seq, d_model)
```

```python
import functools
import math

import numpy as np
import jax
import jax.numpy as jnp
from jax import lax
from jax.experimental import pallas as pl
from jax.experimental.pallas import tpu as pltpu

HEAD_DIM = 128
NSA_KV_HEADS = 2
CMP_LEN = 32
CMP_STRIDE = 16
SLC_LEN = 64
SLC_TOPK = 16
WINDOW = 512
Q_BLOCK = 128
ROPE_THETA = 10000.0
EPS = 1e-6
NEG = -1e30
MASK_FLOOR = -1e20
LOG2E = 1.4426950408889634
QK_SCALE = HEAD_DIM ** -0.5 * LOG2E
SLC_VARIANT_LEN = 512
LANES = 128
VMEM_LIMIT = 58 * 1024 * 1024
FFN_ROW_TILE = 1024
FFN_HIDDEN_GROUPS = 4
PROJ_ROW_TILE = 512
PROJ_GROUPS_PER_TILE = 22
OUT_ROW_TILE = 512
OUT_GROUPS_PER_TILE = 16
DIFF_Q_TILE = 512

F32 = jnp.float32
BF16 = jnp.bfloat16


def _largest_divisor(n, cap):
    for d in range(min(n, cap), 0, -1):
        if n % d == 0:
            return d
    return 1


def _row_tile(n, cap):
    for d in range(min(n, cap), 7, -1):
        if n % d == 0 and d % 8 == 0:
            return d
    return n


def _rms(x, g):
    return x * lax.rsqrt(jnp.mean(x * x, axis=-1, keepdims=True) + EPS) * g


def _dot(a, b):
    return jnp.dot(a, b, preferred_element_type=F32)


def _dot_nt(a, b):
    return lax.dot_general(a, b, (((1,), (1,)), ((), ())), preferred_element_type=F32)


def _params(*sem):
    return pltpu.CompilerParams(dimension_semantics=sem, vmem_limit_bytes=VMEM_LIMIT)


def _ffn_kernel(*refs, final_norm, cast_next):
    if cast_next:
        (x_ref, g_ref, wg_ref, wu_ref, wd_ref, fg_ref, ng_ref, nu_ref, nd_ref,
         o_ref, og_ref, ou_ref, od_ref, h_ref) = refs
        og_ref[...] = ng_ref[...].astype(BF16)
        ou_ref[...] = nu_ref[...].astype(BF16)
        od_ref[...] = nd_ref[...].astype(BF16)
    else:
        x_ref, g_ref, wg_ref, wu_ref, wd_ref, fg_ref, o_ref, h_ref = refs
    j = pl.program_id(1)

    def half_step_tile():
        h = h_ref[...]
        a = _dot(h, wg_ref[...])
        b = _dot(h, wu_ref[...])
        act = (a * jax.nn.sigmoid(a) * b * 0.5).astype(BF16)
        return _dot(act, wd_ref[...])

    @pl.when(j == 0)
    def _():
        h_ref[...] = _rms(x_ref[...], g_ref[...]).astype(BF16)
        o_ref[...] = x_ref[...] + half_step_tile()

    @pl.when(j > 0)
    def _():
        o_ref[...] += half_step_tile()

    if final_norm:
        @pl.when(j == pl.num_programs(1) - 1)
        def _():
            o_ref[...] = _rms(o_ref[...], fg_ref[...])


def _ffn(x, g, wg, wu, wd, fg, final_norm, next_weights=None):
    n, d = x.shape
    f = wg.shape[1]
    tm = _row_tile(n, FFN_ROW_TILE)
    tf = LANES * _largest_divisor(f // LANES, FFN_HIDDEN_GROUPS)
    n_i, n_j = n // tm, f // tf
    in_specs = [
        pl.BlockSpec((tm, d), lambda i, j: (i, 0)),
        pl.BlockSpec((1, d), lambda i, j: (0, 0)),
        pl.BlockSpec((d, tf), lambda i, j: (0, j)),
        pl.BlockSpec((d, tf), lambda i, j: (0, j)),
        pl.BlockSpec((tf, d), lambda i, j: (j, 0)),
        pl.BlockSpec((1, d), lambda i, j: (0, 0)),
    ]
    out_specs = [pl.BlockSpec((tm, d), lambda i, j: (i, 0))]
    out_shape = [jax.ShapeDtypeStruct((n, d), F32)]
    args = [x, g, wg, wu, wd, fg]
    if next_weights is not None:
        stacked, layer = next_weights
        dr, fr = d // n_i, tf // n_i
        assert dr * n_i == d and fr * n_i == tf and dr % 16 == 0 and fr % 16 == 0
        assert all(w.shape[1:] == s for w, s in zip(stacked, ((d, f), (d, f), (f, d))))
        in_specs += [
            pl.BlockSpec((None, dr, tf), lambda i, j: (layer, i, j)),
            pl.BlockSpec((None, dr, tf), lambda i, j: (layer, i, j)),
            pl.BlockSpec((None, fr, d), lambda i, j: (layer, j * n_i + i, 0)),
        ]
        out_specs += [
            pl.BlockSpec((dr, tf), lambda i, j: (i, j)),
            pl.BlockSpec((dr, tf), lambda i, j: (i, j)),
            pl.BlockSpec((fr, d), lambda i, j: (j * n_i + i, 0)),
        ]
        out_shape += [jax.ShapeDtypeStruct(w.shape[1:], BF16) for w in stacked]
        args += list(stacked)
    outs = pl.pallas_call(
        functools.partial(_ffn_kernel, final_norm=final_norm, cast_next=next_weights is not None),
        grid=(n_i, n_j),
        in_specs=in_specs,
        out_specs=out_specs,
        out_shape=out_shape,
        scratch_shapes=[pltpu.VMEM((tm, d), BF16)],
        compiler_params=_params("parallel", "arbitrary"),
        name="ffn",
    )(*args)
    return outs[0], tuple(outs[1:])


def _proj_kernel(x_ref, g_ref, w_ref, wgate_ref, cos_ref, sin_ref, o_ref, gate_ref, h_ref,
                 *, n_tiles, groups_per_tile, rope_groups, query_groups):
    j = pl.program_id(1)

    def column_tile(tile):
        acc = _dot(h_ref[...], w_ref[...])
        cos = cos_ref[...]
        sin = sin_ref[...]
        for gi in range(groups_per_tile):
            group = tile * groups_per_tile + gi
            in_any = lambda spans: any(lo <= group < hi for lo, hi in spans)
            y = acc[:, gi * LANES:(gi + 1) * LANES]
            if in_any(rope_groups):
                y = y * cos + pltpu.roll(y, HEAD_DIM // 2, axis=1) * sin
            if in_any(query_groups):
                y = y * QK_SCALE
            o_ref[:, gi * LANES:(gi + 1) * LANES] = y.astype(BF16)

    for tile in range(n_tiles):
        @pl.when(j == tile)
        def _(tile=tile):
            if tile == 0:
                h = _rms(x_ref[...], g_ref[...]).astype(BF16)
                h_ref[...] = h
                gate_ref[...] = _dot(h, wgate_ref[...])
            column_tile(tile)


def _project(x, g, w_main, w_gate, cos, sin_signed, rope_groups, query_groups):
    n, d = x.shape
    t = cos.shape[0]
    width = w_main.shape[1]
    gw = w_gate.shape[1]
    n_groups = width // LANES
    gpt = _largest_divisor(n_groups, PROJ_GROUPS_PER_TILE)
    tn = gpt * LANES
    tm = _row_tile(t, PROJ_ROW_TILE)
    t_blocks = t // tm
    return pl.pallas_call(
        functools.partial(_proj_kernel, n_tiles=width // tn, groups_per_tile=gpt, rope_groups=rope_groups,
                          query_groups=query_groups),
        grid=(n // tm, width // tn),
        in_specs=[
            pl.BlockSpec((tm, d), lambda i, j: (i, 0)),
            pl.BlockSpec((1, d), lambda i, j: (0, 0)),
            pl.BlockSpec((d, tn), lambda i, j: (0, j)),
            pl.BlockSpec((d, gw), lambda i, j: (0, 0)),
            pl.BlockSpec((tm, HEAD_DIM), lambda i, j: (i % t_blocks, 0)),
            pl.BlockSpec((tm, HEAD_DIM), lambda i, j: (i % t_blocks, 0)),
        ],
        out_specs=[
            pl.BlockSpec((tm, tn), lambda i, j: (i, j)),
            pl.BlockSpec((tm, gw), lambda i, j: (i, 0)),
        ],
        out_shape=[
            jax.ShapeDtypeStruct((n, width), BF16),
            jax.ShapeDtypeStruct((n, gw), F32),
        ],
        scratch_shapes=[pltpu.VMEM((tm, d), BF16)],
        compiler_params=_params("parallel", "arbitrary"),
        name="in_proj",
    )(x, g, w_main, w_gate, cos, sin_signed)


def _cmp_kernel(k_ref, v_ref, pk_ref, pv_ref, wk1_ref, wk2_ref, wv1_ref, wv2_ref, ok_ref, ov_ref, x_ref):
    seq = k_ref.shape[1]
    n_chunks = seq // CMP_STRIDE

    def one(kv_ref, p_ref, w1_ref, w2_ref, o_ref):
        x_ref[...] = kv_ref[0].astype(F32)
        a = jnp.zeros((n_chunks, w1_ref.shape[1]), F32)
        b = jnp.zeros((n_chunks, w1_ref.shape[1]), F32)
        for l in range(CMP_STRIDE):
            rows = x_ref[pl.ds(l, n_chunks, stride=CMP_STRIDE), :].astype(BF16)
            a = a + _dot(rows, w1_ref[l * HEAD_DIM:(l + 1) * HEAD_DIM, :])
            b = b + _dot(rows, w1_ref[(CMP_STRIDE + l) * HEAD_DIM:(CMP_STRIDE + l + 1) * HEAD_DIM, :])
        p = jnp.broadcast_to(p_ref[...], (8, CMP_LEN * HEAD_DIM))
        bias = _dot(p, w1_ref[...])[0:1]
        pre = a + pltpu.roll(b, n_chunks - 1, axis=0) + bias
        hid = (pre * jax.nn.sigmoid(pre)).astype(BF16)
        o_ref[0, 0] = _dot(hid, w2_ref[...]).astype(BF16)

    one(k_ref, pk_ref, wk1_ref, wk2_ref, ok_ref)
    one(v_ref, pv_ref, wv1_ref, wv2_ref, ov_ref)


def _compress(proj, off, batch, seq, pk, pv, wk1, wk2, wv1, wv2):
    g = NSA_KV_HEADS
    nc = seq // CMP_STRIDE
    hid = wk1.shape[1]
    dk = wk2.shape[1]
    assert CMP_LEN == 2 * CMP_STRIDE and dk == HEAD_DIM
    kb, vb = off["kc"] // HEAD_DIM, off["vc"] // HEAD_DIM
    full = lambda shape: pl.BlockSpec(shape, lambda i, j: tuple(0 for _ in shape))
    out_spec = pl.BlockSpec((1, 1, nc, dk), lambda i, j: (i, j, 0, 0))
    return pl.pallas_call(
        _cmp_kernel,
        grid=(batch, g),
        in_specs=[pl.BlockSpec((1, seq, HEAD_DIM), lambda i, j: (i, 0, kb + j)),
                  pl.BlockSpec((1, seq, HEAD_DIM), lambda i, j: (i, 0, vb + j)),
                  full((1, CMP_LEN * dk)), full((1, CMP_LEN * dk)),
                  full((CMP_LEN * dk, hid)), full((hid, dk)), full((CMP_LEN * dk, hid)), full((hid, dk))],
        out_specs=[out_spec, out_spec],
        out_shape=[jax.ShapeDtypeStruct((batch, g, nc, dk), BF16)] * 2,
        scratch_shapes=[pltpu.VMEM((seq, HEAD_DIM), F32)],
        compiler_params=_params("parallel", "parallel"),
        name="nsa_compress",
    )(proj, proj, pk, pv, wk1, wk2, wv1, wv2)


ONES_ROWS = 16


def _transposed_values(v, ones_rows):
    vt = v.astype(F32).T
    if ones_rows:
        vt = jnp.concatenate([vt, jnp.ones((ones_rows, vt.shape[1]), F32)], axis=0)
    return vt.astype(BF16)


def _softmax_piece(st, vt, dv, elem_bias=None, blk_bias=None):
    nk, nc = st.shape
    if elem_bias is not None:
        st = st + elem_bias
    nb = 1 if blk_bias is None else blk_bias.shape[0]
    s4 = st.reshape(nb, nk // nb // 8, 8, nc)
    bm = jnp.max(s4, axis=1)
    if blk_bias is not None:
        bm = bm + blk_bias[:, None, :]
    m = jnp.max(jnp.max(bm, axis=0), axis=0, keepdims=True)
    m = jnp.maximum(m, MASK_FLOOR)
    shift = -m if blk_bias is None else blk_bias - m
    e = jnp.exp2(s4 + shift[:, None, None, :])
    o = _dot(vt, e.reshape(nk, nc).astype(BF16))
    if vt.shape[0] > dv:
        return m, o[dv:dv + 1], o[0:dv]
    return m, jnp.sum(jnp.sum(jnp.sum(e, axis=0), axis=0), axis=0, keepdims=True), o


def _merge_pieces(pieces):
    m = functools.reduce(jnp.maximum, [p[0] for p in pieces])
    ws = [jnp.exp2(p[0] - m) for p in pieces]
    l = sum(w * p[1] for w, p in zip(ws, pieces))
    o = sum(w * p[2] for w, p in zip(ws, pieces))
    return o / l


def _interleave(gens):
    results = [None] * len(gens)
    live = list(range(len(gens)))
    while live:
        for i in list(live):
            try:
                next(gens[i])
            except StopIteration as stop:
                results[i] = stop.value
                live.remove(i)
    return results


def _nsa_block(j, q, gates, ks_ref, kw_ref, kc_ref, vc_ref, ovl_ref, vst_ref, vwt_ref,
               *, rep, n_slc, n_sel, var_len):
    s0 = j * Q_BLOCK
    q4 = jnp.concatenate([q[:, r * HEAD_DIM:(r + 1) * HEAD_DIM] for r in range(rep)], axis=0)
    t_row = s0 + lax.broadcasted_iota(jnp.int32, (1, Q_BLOCK), 1)
    head = lambda a, r: a[:, r * Q_BLOCK:(r + 1) * Q_BLOCK]

    kk = lax.broadcasted_iota(jnp.int32, (Q_BLOCK, Q_BLOCK), 0)
    qq = lax.broadcasted_iota(jnp.int32, (Q_BLOCK, Q_BLOCK), 1)
    tri = jnp.concatenate([jnp.where(kk <= qq, 0.0, NEG)] * rep, axis=1)
    anti = jnp.concatenate([jnp.where(kk > qq, 0.0, NEG)] * rep, axis=1)
    n_back = WINDOW // Q_BLOCK
    rows_of = lambda ref, blk0, n: ref[0, blk0 * Q_BLOCK:(blk0 + n) * Q_BLOCK, :]
    tiles_of = lambda ref, blk0, n: jnp.concatenate([ref[blk0 + i] for i in range(n)], axis=1)
    if j >= n_back:
        far = j - n_back
        edge = (jnp.concatenate([rows_of(kw_ref, far, 1), rows_of(kw_ref, j, 1)], axis=0),
                lambda: jnp.concatenate([vwt_ref[far], vwt_ref[j]], axis=1), jnp.concatenate([anti, tri], axis=0))
    else:
        edge = (rows_of(kw_ref, j, 1), lambda: vwt_ref[j], tri)
    n_mid = min(n_back - 1, j)
    early = [edge]
    if n_mid:
        early.append((rows_of(kw_ref, j - n_mid, n_mid), lambda: tiles_of(vwt_ref, j - n_mid, n_mid), None))
    early.append((rows_of(ks_ref, j, 1), lambda: vst_ref[j], tri))
    early_scores = [_dot_nt(k, q4) for k, _, _ in early]
    q_per_var = var_len // Q_BLOCK
    first_parts = q_per_var if j >= q_per_var else j
    first_scores = _dot_nt(ks_ref[0, 0:first_parts * Q_BLOCK, :], q4) if first_parts else None
    early_piece = lambda i: _softmax_piece(early_scores[i], early[i][1](), HEAD_DIM, early[i][2])

    kc = kc_ref[0, 0]
    n_rows = kc.shape[0]
    n_idx = lax.broadcasted_iota(jnp.int32, (n_rows, Q_BLOCK), 0)
    c_mask = (n_idx * CMP_STRIDE + (CMP_LEN - 1)) <= t_row
    st = _dot_nt(kc, q4)
    yield
    win_pieces = [early_piece(0)]
    p_sum = jnp.zeros((n_rows, Q_BLOCK), F32)
    ps = []
    for r in range(rep):
        t = jnp.where(c_mask, head(st, r), NEG)
        e = jnp.where(c_mask, jnp.exp2(t - jnp.max(t, axis=0, keepdims=True)), 0.0)
        l = jnp.sum(e, axis=0, keepdims=True)
        p = e / jnp.where(l > 0.0, l, 1.0)
        p_sum = p_sum + p
        ps.append(p.astype(BF16))
    o_cmp = _dot(vc_ref[0, 0].astype(F32).T.astype(BF16), jnp.concatenate(ps, axis=1))

    hi = p_sum.astype(BF16)
    lo = (p_sum - hi.astype(F32)).astype(BF16)
    ovl = ovl_ref[...]
    n_pad = -(-n_slc // 8) * 8
    imp = (_dot(ovl, hi) + _dot(ovl, lo))[0:n_pad]
    yield
    win_pieces += [early_piece(i) for i in range(1, len(early) - 1)]
    slc_diag = early_piece(len(early) - 1)
    blk = lax.broadcasted_iota(jnp.int32, (n_pad, Q_BLOCK), 0)
    cur = t_row // SLC_LEN
    forced = (blk == 0) | (blk == cur) | (blk == cur - 1)
    blk_causal = blk * SLC_LEN <= t_row
    imp = jnp.where(forced, 1e4, imp)
    imp = jnp.where(blk_causal, imp, -1.0)
    rank = jnp.zeros((n_pad, Q_BLOCK), F32)
    for sp in range(n_slc):
        row = imp[sp:sp + 1, :]
        ge = jnp.where(row >= imp, 1.0, 0.0)
        gt = jnp.where(row > imp, 1.0, 0.0)
        rank = rank + jnp.where(blk > sp, ge, gt)
    blocks_per_q = Q_BLOCK // SLC_LEN
    blk_bias = jnp.where((rank < float(n_sel)) & (blk < j * blocks_per_q), 0.0, NEG)
    blk_bias = jnp.concatenate([blk_bias] * rep, axis=1)
    yield

    gt = jax.nn.sigmoid(gates).T
    gate = lambda branch: jnp.concatenate([jnp.broadcast_to(gt[3 * r + branch:3 * r + branch + 1], (HEAD_DIM, Q_BLOCK))
                                           for r in range(rep)], axis=1)
    partial = gate(0) * o_cmp + gate(2) * _merge_pieces(win_pieces)
    blk_per_var = var_len // SLC_LEN
    blk_per_q = Q_BLOCK // SLC_LEN

    def scores(ci, n_q):
        return _dot_nt(ks_ref[0, ci * var_len:ci * var_len + n_q * Q_BLOCK, :], q4)

    def chunk_piece(ci, n_q, st):
        b0 = ci * blk_per_var
        return _softmax_piece(st, tiles_of(vst_ref, ci * q_per_var, n_q), HEAD_DIM,
                              None, blk_bias[b0:b0 + n_q * blk_per_q])

    pieces = [slc_diag] + ([chunk_piece(0, first_parts, first_scores)] if first_parts else [])
    return dict(partial=partial, g_slc=gate(1), pieces=pieces, scores=scores, chunk_piece=chunk_piece,
                q_per_var=q_per_var)


def _nsa_selected(ctx, v, h):
    pieces = list(ctx["pieces"])
    todo = [(ci, ctx["q_per_var"]) for ci in range(1, v)] + ([(v, h)] if v > 0 and h > 0 else [])
    ahead = 2
    sts = [ctx["scores"](*t) for t in todo[:ahead]]
    for i, (ci, n_q) in enumerate(todo):
        if i + ahead < len(todo):
            sts.append(ctx["scores"](*todo[i + ahead]))
        pieces.append(ctx["chunk_piece"](ci, n_q, sts[i]))
        yield
    return ctx["partial"] + ctx["g_slc"] * _merge_pieces(pieces)


def _nsa_kernel(q_ref, ks_ref, vs_ref, kw_ref, vw_ref, kc_ref, vc_ref, gate_ref, ovl_ref,
                o_ref, vst_ref, vwt_ref, *, seq, rep, n_slc, n_sel, var_len, blocks):
    jj = pl.program_id(2)

    @pl.when(jj == 0)
    def _():
        for i in range(seq // Q_BLOCK):
            vst_ref[i] = _transposed_values(vs_ref[0, i * Q_BLOCK:(i + 1) * Q_BLOCK, :], ONES_ROWS)
            vwt_ref[i] = _transposed_values(vw_ref[0, i * Q_BLOCK:(i + 1) * Q_BLOCK, :], ONES_ROWS)

    rows = lambda h: slice(h * Q_BLOCK, (h + 1) * Q_BLOCK)
    assert blocks * Q_BLOCK == var_len
    for v in range(seq // var_len):
        @pl.when(jj == v)
        def _(v=v):
            ctxs = _interleave([
                _nsa_block(v * blocks + h, q_ref[0, rows(h), :], gate_ref[0, rows(h), :], ks_ref, kw_ref, kc_ref,
                           vc_ref, ovl_ref, vst_ref, vwt_ref, rep=rep, n_slc=n_slc, n_sel=n_sel, var_len=var_len)
                for h in range(blocks)])
            outs = _interleave([_nsa_selected(ctx, v, h) for h, ctx in enumerate(ctxs)])
            for h, out in enumerate(outs):
                for r in range(rep):
                    o_ref[0, rows(h), r * HEAD_DIM:(r + 1) * HEAD_DIM] = (
                        out[:, r * Q_BLOCK:(r + 1) * Q_BLOCK].T.astype(BF16))


def _block_overlap(n_rows, n_cmp, n_slc):
    c0 = np.arange(n_cmp) * CMP_STRIDE
    s0 = np.arange(n_slc) * SLC_LEN
    lo = np.maximum(c0[None, :], s0[:, None])
    hi = np.minimum(c0[None, :] + CMP_LEN, s0[:, None] + SLC_LEN)
    out = np.zeros((LANES, n_rows), np.float32)
    out[:n_slc, :n_cmp] = np.clip(hi - lo, 0, None) / CMP_LEN
    return out


def _nsa_attention(proj, gates, kc, vc, off, batch, seq, rep):
    g = NSA_KV_HEADS
    n_qb = seq // Q_BLOCK
    n_slc = seq // SLC_LEN
    n_sel = min(SLC_TOPK, n_slc)
    n_rows = kc.shape[2]
    n_cmp = (seq - CMP_LEN) // CMP_STRIDE + 1
    qw = rep * HEAD_DIM
    var_len = SLC_VARIANT_LEN if seq % SLC_VARIANT_LEN == 0 else seq
    blocks = var_len // Q_BLOCK
    ovl = jnp.asarray(_block_overlap(n_rows, n_cmp, n_slc), BF16)

    def kv_spec(name):
        base = off[name] // HEAD_DIM
        return pl.BlockSpec((1, seq, HEAD_DIM), lambda b, h, j: (b, 0, base + h))

    cmp_spec = pl.BlockSpec((1, 1, n_rows, HEAD_DIM), lambda b, h, j: (b, h, 0, 0))
    q_base = off["q_n"] // qw
    return pl.pallas_call(
        functools.partial(_nsa_kernel, seq=seq, rep=rep, n_slc=n_slc, n_sel=n_sel, var_len=var_len, blocks=blocks),
        grid=(batch, g, n_qb // blocks),
        scratch_shapes=[
            pltpu.VMEM((seq // Q_BLOCK, HEAD_DIM + ONES_ROWS, Q_BLOCK), BF16),
            pltpu.VMEM((seq // Q_BLOCK, HEAD_DIM + ONES_ROWS, Q_BLOCK), BF16),
        ],
        in_specs=[
            pl.BlockSpec((1, blocks * Q_BLOCK, qw), lambda b, h, j: (b, j, q_base + h)),
            kv_spec("ks"), kv_spec("vs"), kv_spec("kw"), kv_spec("vw"),
            cmp_spec, cmp_spec,
            pl.BlockSpec((1, blocks * Q_BLOCK, LANES), lambda b, h, j: (b, j, h)),
            pl.BlockSpec((LANES, n_rows), lambda b, h, j: (0, 0)),
        ],
        out_specs=pl.BlockSpec((1, blocks * Q_BLOCK, qw), lambda b, h, j: (b, j, h)),
        out_shape=jax.ShapeDtypeStruct((batch, seq, g * qw), BF16),
        compiler_params=_params("parallel", "parallel", "arbitrary"),
        name="nsa_attention",
    )(proj, proj, proj, proj, proj, kc, vc, gates, ovl)


def _diff_block(v, q, k_ref, vts, tri, lam, tq):
    qs = [q[:, c * HEAD_DIM:(c + 1) * HEAD_DIM] for c in range(2)]
    half = tq // 2
    chunks = [(v * tq, half, 0), (v * tq + half, half, half)] + [(kb * tq, tq, 0) for kb in range(v)]
    specs = [(c, ch) for ch in chunks for c in range(2)]
    lanes = lambda c: slice(c * HEAD_DIM, (c + 1) * HEAD_DIM)
    scores = lambda c, ch: _dot_nt(k_ref[0, ch[0]:ch[0] + ch[1], lanes(c)], qs[c][ch[2]:, :])
    pieces = ([], [])
    ahead = 2
    sts = [scores(*spec) for spec in specs[:ahead]]
    for i, (c, (k0, nk, q0)) in enumerate(specs):
        if i + ahead < len(specs):
            sts.append(scores(*specs[i + ahead]))
        vt = vts[k0 // tq][:, k0 % tq:k0 % tq + nk]
        diagonal = k0 >= v * tq
        bias = None if not diagonal else (tri if q0 else jnp.concatenate([tri, jnp.zeros_like(tri)], axis=1))
        m, l, o = _softmax_piece(sts[i], vt, 2 * HEAD_DIM, bias)
        if q0:
            m = jnp.concatenate([jnp.full((1, q0), MASK_FLOOR, F32), m], axis=1)
            l = jnp.concatenate([jnp.zeros((1, q0), F32), l], axis=1)
            o = jnp.concatenate([jnp.zeros((o.shape[0], q0), F32), o], axis=1)
        pieces[c].append((m, l, o))
        yield
    return _merge_pieces(pieces[0]) - lam * _merge_pieces(pieces[1])


def _diff_kernel(q_ref, k_ref, v_ref, lam_ref, sg_ref, o_ref, *, seq, tq, lambda_init):
    n_blocks = seq // tq
    rows = lambda i: slice(i * tq, (i + 1) * tq)
    vts = [_transposed_values(v_ref[0, rows(i), :], 0) for i in range(n_blocks)]
    lv = lam_ref[...]
    lam = (jnp.exp(jnp.sum(lv[0:1] * lv[1:2], axis=-1, keepdims=True))
           - jnp.exp(jnp.sum(lv[2:3] * lv[3:4], axis=-1, keepdims=True)) + lambda_init)
    half = tq // 2
    tri = jnp.where(lax.broadcasted_iota(jnp.int32, (half, half), 0) <= lax.broadcasted_iota(jnp.int32, (half, half), 1),
                    0.0, NEG)
    outs = _interleave([_diff_block(v, q_ref[0, rows(v), :], k_ref, vts, tri, lam, tq) for v in range(n_blocks)])
    for v, o in enumerate(outs):
        o_ref[0, rows(v), :] = (_rms(o.T, sg_ref[...]) * (1.0 - lambda_init)).astype(BF16)


def _diff_attention(proj, lam, subln, off, batch, seq, heads, lambda_init):
    vw = 2 * HEAD_DIM
    tq = _row_tile(seq, DIFF_Q_TILE)
    qb, kb, vb = off["q_d"] // vw, off["k_d"] // vw, off["v_d"] // vw
    return pl.pallas_call(
        functools.partial(_diff_kernel, seq=seq, tq=tq, lambda_init=lambda_init),
        grid=(batch, heads),
        in_specs=[
            pl.BlockSpec((1, seq, vw), lambda b, h: (b, 0, qb + h)),
            pl.BlockSpec((1, seq, vw), lambda b, h: (b, 0, kb + h)),
            pl.BlockSpec((1, seq, vw), lambda b, h: (b, 0, vb + h)),
            pl.BlockSpec((4, HEAD_DIM), lambda b, h: (0, 0)),
            pl.BlockSpec((1, vw), lambda b, h: (0, 0)),
        ],
        out_specs=pl.BlockSpec((1, seq, vw), lambda b, h: (b, 0, h)),
        out_shape=jax.ShapeDtypeStruct((batch, seq, heads * vw), BF16),
        compiler_params=_params("parallel", "parallel"),
        name="diff_attention",
    )(proj, proj, proj, lam, subln)


def _out_kernel(x_ref, a_ref, b_ref, wa_ref, wb_ref, o_ref):
    o_ref[...] = x_ref[...] + _dot(a_ref[...], wa_ref[...]) + _dot(b_ref[...], wb_ref[...])


def _out_project(x, a, b, wa, wb):
    n, d = x.shape
    ka, kb = a.shape[1], b.shape[1]
    tm = _row_tile(n, OUT_ROW_TILE)
    tn = LANES * _largest_divisor(d // LANES, OUT_GROUPS_PER_TILE)
    return pl.pallas_call(
        _out_kernel,
        grid=(n // tm, d // tn),
        in_specs=[
            pl.BlockSpec((tm, tn), lambda i, j: (i, j)),
            pl.BlockSpec((tm, ka), lambda i, j: (i, 0)),
            pl.BlockSpec((tm, kb), lambda i, j: (i, 0)),
            pl.BlockSpec((ka, tn), lambda i, j: (0, j)),
            pl.BlockSpec((kb, tn), lambda i, j: (0, j)),
        ],
        out_specs=pl.BlockSpec((tm, tn), lambda i, j: (i, j)),
        out_shape=jax.ShapeDtypeStruct((n, d), F32),
        compiler_params=_params("parallel", "arbitrary"),
        name="out_proj",
    )(x, a, b, wa, wb)


def _rope_tables(t):
    inv = 1.0 / (ROPE_THETA ** (jnp.arange(0, HEAD_DIM, 2, dtype=F32) / HEAD_DIM))
    ang = jnp.arange(t, dtype=F32)[:, None] * inv[None, :]
    ang = jnp.concatenate([ang, ang], axis=-1)
    sign = jnp.concatenate([-jnp.ones((HEAD_DIM // 2,), F32), jnp.ones((HEAD_DIM // 2,), F32)])
    return jnp.cos(ang), jnp.sin(ang) * sign[None, :]


def _layout(d_model):
    nsa_heads = d_model // (2 * HEAD_DIM)
    diff_heads = d_model // (4 * HEAD_DIM)
    kv = NSA_KV_HEADS * HEAD_DIM
    sizes = dict(q_n=nsa_heads * HEAD_DIM, kc=kv, vc=kv, ks=kv, vs=kv, kw=kv, vw=kv, gates=3 * nsa_heads,
                 q_d=2 * diff_heads * HEAD_DIM, k_d=2 * diff_heads * HEAD_DIM, v_d=diff_heads * 2 * HEAD_DIM)
    orig, off, o, p = {}, {}, 0, 0
    for name in ("q_n", "kc", "vc", "ks", "vs", "kw", "vw", "gates", "q_d", "k_d", "v_d"):
        orig[name] = o
        o += sizes[name]
        if name != "gates":
            off[name] = p
            p += sizes[name]
    groups = lambda names: tuple((off[nm] // LANES, (off[nm] + sizes[nm]) // LANES) for nm in names)
    return sizes, orig, off, groups(("q_n", "kc", "ks", "kw", "q_d", "k_d")), groups(("q_n", "q_d")), nsa_heads, diff_heads


def kernel(x, ffn1_norm, ffn1_w_gate, ffn1_w_up, ffn1_w_down, mix_norm, w_in, cmp_pos_k, cmp_pos_v, cmp_wk1, cmp_wk2, cmp_wv1, cmp_wv2, lam_q1, lam_k1, lam_q2, lam_k2, diff_subln, w_out, ffn2_norm, ffn2_w_gate, ffn2_w_up, ffn2_w_down, final_norm):
    batch, seq, d_model = x.shape
    depth = ffn1_norm.shape[0]
    n = batch * seq
    sizes, orig, off, rope_groups, query_groups, nsa_heads, diff_heads = _layout(d_model)
    g = NSA_KV_HEADS
    rep = nsa_heads // g
    cos, sin_signed = _rope_tables(seq)
    fg = final_norm.reshape(1, d_model)
    ffn1_stacked = (ffn1_w_gate, ffn1_w_up, ffn1_w_down)
    ffn2_stacked = (ffn2_w_gate, ffn2_w_up, ffn2_w_down)
    w_ffn = tuple(w[0].astype(BF16) for w in ffn1_stacked)

    xf = x.reshape(n, d_model)
    for l in range(depth):
        lambda_init = 0.8 - 0.6 * math.exp(-0.3 * l)
        xf, w_ffn = _ffn(xf, ffn1_norm[l].reshape(1, d_model), *w_ffn, fg, False, (ffn2_stacked, l))

        wl = w_in[l]
        gates_end = orig["gates"] + sizes["gates"]
        w_main = jnp.concatenate([wl[:, :orig["gates"]], wl[:, gates_end:]], axis=1).astype(BF16)
        gate_tiles = []
        for h in range(g):
            cols = wl[:, orig["gates"] + h * 3 * rep: orig["gates"] + (h + 1) * 3 * rep]
            gate_tiles.append(jnp.pad(cols, ((0, 0), (0, LANES - 3 * rep))))
        w_gate = jnp.concatenate(gate_tiles, axis=1).astype(BF16)
        proj, gates = _project(xf, mix_norm[l].reshape(1, d_model), w_main, w_gate, cos, sin_signed, rope_groups,
                               query_groups)
        proj3 = proj.reshape(batch, seq, -1)
        gates3 = gates.reshape(batch, seq, -1)

        kc, vc = _compress(proj3, off, batch, seq,
                           cmp_pos_k[l].reshape(1, -1).astype(BF16), cmp_pos_v[l].reshape(1, -1).astype(BF16),
                           cmp_wk1[l].astype(BF16), cmp_wk2[l].astype(BF16),
                           cmp_wv1[l].astype(BF16), cmp_wv2[l].astype(BF16))
        o_nsa = _nsa_attention(proj3, gates3, kc, vc, off, batch, seq, rep)
        lam = jnp.stack([lam_q1[l], lam_k1[l], lam_q2[l], lam_k2[l]])
        o_diff = _diff_attention(proj3, lam, diff_subln[l].reshape(1, -1), off, batch, seq, diff_heads, lambda_init)

        half = sizes["q_n"]
        wo = w_out[l].astype(BF16)
        xf = _out_project(xf, o_nsa.reshape(n, -1), o_diff.reshape(n, -1), wo[:half], wo[half:])

        last = l == depth - 1
        xf, w_ffn = _ffn(xf, ffn2_norm[l].reshape(1, d_model), *w_ffn, fg, last,
                         None if last else (ffn1_stacked, l + 1))
    return xf.reshape(batch, seq, d_model)
```

```python
import functools
import math

import numpy as np
import jax
import jax.numpy as jnp
from jax import lax
from jax.experimental import pallas as pl
from jax.experimental.pallas import tpu as pltpu

HEAD_DIM = 128
NSA_KV_HEADS = 2
CMP_LEN = 32
CMP_STRIDE = 16
SLC_LEN = 64
SLC_TOPK = 16
WINDOW = 512
Q_BLOCK = 128
ROPE_THETA = 10000.0
EPS = 1e-6
NEG = -1e30
MASK_FLOOR = -1e20
LOG2E = 1.4426950408889634
QK_SCALE = HEAD_DIM ** -0.5 * LOG2E
SLC_VARIANT_LEN = 512
LANES = 128
VMEM_LIMIT = 58 * 1024 * 1024
FFN_ROW_TILE = 1024
FFN_HIDDEN_GROUPS = 4
PROJ_ROW_TILE = 512
PROJ_GROUPS_PER_TILE = 22
OUT_ROW_TILE = 512
OUT_GROUPS_PER_TILE = 16
DIFF_Q_TILE = 512

F32 = jnp.float32
BF16 = jnp.bfloat16


def _largest_divisor(n, cap):
    for d in range(min(n, cap), 0, -1):
        if n % d == 0:
            return d
    return 1


def _row_tile(n, cap):
    for d in range(min(n, cap), 7, -1):
        if n % d == 0 and d % 8 == 0:
            return d
    return n


def _rms(x, g):
    return x * lax.rsqrt(jnp.mean(x * x, axis=-1, keepdims=True) + EPS) * g


def _dot(a, b):
    return jnp.dot(a, b, preferred_element_type=F32)


def _dot_nt(a, b):
    return lax.dot_general(a, b, (((1,), (1,)), ((), ())), preferred_element_type=F32)


def _params(*sem):
    return pltpu.CompilerParams(dimension_semantics=sem, vmem_limit_bytes=VMEM_LIMIT)


def _ffn_kernel(*refs, final_norm, cast_next):
    if cast_next:
        (x_ref, g_ref, wg_ref, wu_ref, wd_ref, fg_ref, ng_ref, nu_ref, nd_ref,
         o_ref, og_ref, ou_ref, od_ref, h_ref) = refs
        og_ref[...] = ng_ref[...].astype(BF16)
        ou_ref[...] = nu_ref[...].astype(BF16)
        od_ref[...] = nd_ref[...].astype(BF16)
    else:
        x_ref, g_ref, wg_ref, wu_ref, wd_ref, fg_ref, o_ref, h_ref = refs
    j = pl.program_id(1)

    def half_step_tile():
        h = h_ref[...]
        a = _dot(h, wg_ref[...])
        b = _dot(h, wu_ref[...])
        act = (a * jax.nn.sigmoid(a) * b * 0.5).astype(BF16)
        return _dot(act, wd_ref[...])

    @pl.when(j == 0)
    def _():
        h_ref[...] = _rms(x_ref[...], g_ref[...]).astype(BF16)
        o_ref[...] = x_ref[...] + half_step_tile()

    @pl.when(j > 0)
    def _():
        o_ref[...] += half_step_tile()

    if final_norm:
        @pl.when(j == pl.num_programs(1) - 1)
        def _():
            o_ref[...] = _rms(o_ref[...], fg_ref[...])


def _ffn(x, g, wg, wu, wd, fg, final_norm, next_weights=None):
    n, d = x.shape
    f = wg.shape[1]
    tm = _row_tile(n, FFN_ROW_TILE)
    tf = LANES * _largest_divisor(f // LANES, FFN_HIDDEN_GROUPS)
    n_i, n_j = n // tm, f // tf
    in_specs = [
        pl.BlockSpec((tm, d), lambda i, j: (i, 0)),
        pl.BlockSpec((1, d), lambda i, j: (0, 0)),
        pl.BlockSpec((d, tf), lambda i, j: (0, j)),
        pl.BlockSpec((d, tf), lambda i, j: (0, j)),
        pl.BlockSpec((tf, d), lambda i, j: (j, 0)),
        pl.BlockSpec((1, d), lambda i, j: (0, 0)),
    ]
    out_specs = [pl.BlockSpec((tm, d), lambda i, j: (i, 0))]
    out_shape = [jax.ShapeDtypeStruct((n, d), F32)]
    args = [x, g, wg, wu, wd, fg]
    if next_weights is not None:
        stacked, layer = next_weights
        dr, fr = d // n_i, tf // n_i
        assert dr * n_i == d and fr * n_i == tf and dr % 16 == 0 and fr % 16 == 0
        assert all(w.shape[1:] == s for w, s in zip(stacked, ((d, f), (d, f), (f, d))))
        in_specs += [
            pl.BlockSpec((None, dr, tf), lambda i, j: (layer, i, j)),
            pl.BlockSpec((None, dr, tf), lambda i, j: (layer, i, j)),
            pl.BlockSpec((None, fr, d), lambda i, j: (layer, j * n_i + i, 0)),
        ]
        out_specs += [
            pl.BlockSpec((dr, tf), lambda i, j: (i, j)),
            pl.BlockSpec((dr, tf), lambda i, j: (i, j)),
            pl.BlockSpec((fr, d), lambda i, j: (j * n_i + i, 0)),
        ]
        out_shape += [jax.ShapeDtypeStruct(w.shape[1:], BF16) for w in stacked]
        args += list(stacked)
    outs = pl.pallas_call(
        functools.partial(_ffn_kernel, final_norm=final_norm, cast_next=next_weights is not None),
        grid=(n_i, n_j),
        in_specs=in_specs,
        out_specs=out_specs,
        out_shape=out_shape,
        scratch_shapes=[pltpu.VMEM((tm, d), BF16)],
        compiler_params=_params("parallel", "arbitrary"),
        name="ffn",
    )(*args)
    return outs[0], tuple(outs[1:])


def _proj_kernel(x_ref, g_ref, w_ref, wgate_ref, cos_ref, sin_ref, o_ref, gate_ref, h_ref,
                 *, n_tiles, groups_per_tile, rope_groups, query_groups):
    j = pl.program_id(1)

    def column_tile(tile):
        acc = _dot(h_ref[...], w_ref[...])
        cos = cos_ref[...]
        sin = sin_ref[...]
        for gi in range(groups_per_tile):
            group = tile * groups_per_tile + gi
            in_any = lambda spans: any(lo <= group < hi for lo, hi in spans)
            y = acc[:, gi * LANES:(gi + 1) * LANES]
            if in_any(rope_groups):
                y = y * cos + pltpu.roll(y, HEAD_DIM // 2, axis=1) * sin
            if in_any(query_groups):
                y = y * QK_SCALE
            o_ref[:, gi * LANES:(gi + 1) * LANES] = y.astype(BF16)

    for tile in range(n_tiles):
        @pl.when(j == tile)
        def _(tile=tile):
            if tile == 0:
                h = _rms(x_ref[...], g_ref[...]).astype(BF16)
                h_ref[...] = h
                gate_ref[...] = _dot(h, wgate_ref[...])
            column_tile(tile)


def _project(x, g, w_main, w_gate, cos, sin_signed, rope_groups, query_groups):
    n, d = x.shape
    t = cos.shape[0]
    width = w_main.shape[1]
    gw = w_gate.shape[1]
    n_groups = width // LANES
    gpt = _largest_divisor(n_groups, PROJ_GROUPS_PER_TILE)
    tn = gpt * LANES
    tm = _row_tile(t, PROJ_ROW_TILE)
    t_blocks = t // tm
    return pl.pallas_call(
        functools.partial(_proj_kernel, n_tiles=width // tn, groups_per_tile=gpt, rope_groups=rope_groups,
                          query_groups=query_groups),
        grid=(n // tm, width // tn),
        in_specs=[
            pl.BlockSpec((tm, d), lambda i, j: (i, 0)),
            pl.BlockSpec((1, d), lambda i, j: (0, 0)),
            pl.BlockSpec((d, tn), lambda i, j: (0, j)),
            pl.BlockSpec((d, gw), lambda i, j: (0, 0)),
            pl.BlockSpec((tm, HEAD_DIM), lambda i, j: (i % t_blocks, 0)),
            pl.BlockSpec((tm, HEAD_DIM), lambda i, j: (i % t_blocks, 0)),
        ],
        out_specs=[
            pl.BlockSpec((tm, tn), lambda i, j: (i, j)),
            pl.BlockSpec((tm, gw), lambda i, j: (i, 0)),
        ],
        out_shape=[
            jax.ShapeDtypeStruct((n, width), BF16),
            jax.ShapeDtypeStruct((n, gw), F32),
        ],
        scratch_shapes=[pltpu.VMEM((tm, d), BF16)],
        compiler_params=_params("parallel", "arbitrary"),
        name="in_proj",
    )(x, g, w_main, w_gate, cos, sin_signed)


def _cmp_kernel(k_ref, v_ref, pk_ref, pv_ref, wk1_ref, wk2_ref, wv1_ref, wv2_ref, ok_ref, ov_ref, x_ref):
    seq = k_ref.shape[1]
    n_chunks = seq // CMP_STRIDE

    def one(kv_ref, p_ref, w1_ref, w2_ref, o_ref):
        x_ref[...] = kv_ref[0].astype(F32)
        a = jnp.zeros((n_chunks, w1_ref.shape[1]), F32)
        b = jnp.zeros((n_chunks, w1_ref.shape[1]), F32)
        for l in range(CMP_STRIDE):
            rows = x_ref[pl.ds(l, n_chunks, stride=CMP_STRIDE), :].astype(BF16)
            a = a + _dot(rows, w1_ref[l * HEAD_DIM:(l + 1) * HEAD_DIM, :])
            b = b + _dot(rows, w1_ref[(CMP_STRIDE + l) * HEAD_DIM:(CMP_STRIDE + l + 1) * HEAD_DIM, :])
        p = jnp.broadcast_to(p_ref[...], (8, CMP_LEN * HEAD_DIM))
        bias = _dot(p, w1_ref[...])[0:1]
        pre = a + pltpu.roll(b, n_chunks - 1, axis=0) + bias
        hid = (pre * jax.nn.sigmoid(pre)).astype(BF16)
        o_ref[0, 0] = _dot(hid, w2_ref[...]).astype(BF16)

    one(k_ref, pk_ref, wk1_ref, wk2_ref, ok_ref)
    one(v_ref, pv_ref, wv1_ref, wv2_ref, ov_ref)


def _compress(proj, off, batch, seq, pk, pv, wk1, wk2, wv1, wv2):
    g = NSA_KV_HEADS
    nc = seq // CMP_STRIDE
    hid = wk1.shape[1]
    dk = wk2.shape[1]
    assert CMP_LEN == 2 * CMP_STRIDE and dk == HEAD_DIM
    kb, vb = off["kc"] // HEAD_DIM, off["vc"] // HEAD_DIM
    full = lambda shape: pl.BlockSpec(shape, lambda i, j: tuple(0 for _ in shape))
    out_spec = pl.BlockSpec((1, 1, nc, dk), lambda i, j: (i, j, 0, 0))
    return pl.pallas_call(
        _cmp_kernel,
        grid=(batch, g),
        in_specs=[pl.BlockSpec((1, seq, HEAD_DIM), lambda i, j: (i, 0, kb + j)),
                  pl.BlockSpec((1, seq, HEAD_DIM), lambda i, j: (i, 0, vb + j)),
                  full((1, CMP_LEN * dk)), full((1, CMP_LEN * dk)),
                  full((CMP_LEN * dk, hid)), full((hid, dk)), full((CMP_LEN * dk, hid)), full((hid, dk))],
        out_specs=[out_spec, out_spec],
        out_shape=[jax.ShapeDtypeStruct((batch, g, nc, dk), BF16)] * 2,
        scratch_shapes=[pltpu.VMEM((seq, HEAD_DIM), F32)],
        compiler_params=_params("parallel", "parallel"),
        name="nsa_compress",
    )(proj, proj, pk, pv, wk1, wk2, wv1, wv2)


ONES_ROWS = 16


def _transposed_values(v, ones_rows):
    vt = v.astype(F32).T
    if ones_rows:
        vt = jnp.concatenate([vt, jnp.ones((ones_rows, vt.shape[1]), F32)], axis=0)
    return vt.astype(BF16)


def _softmax_piece(st, vt, dv, elem_bias=None, blk_bias=None):
    nk, nc = st.shape
    if elem_bias is not None:
        st = st + elem_bias
    nb = 1 if blk_bias is None else blk_bias.shape[0]
    s4 = st.reshape(nb, nk // nb // 8, 8, nc)
    bm = jnp.max(s4, axis=1)
    if blk_bias is not None:
        bm = bm + blk_bias[:, None, :]
    m = jnp.max(jnp.max(bm, axis=0), axis=0, keepdims=True)
    m = jnp.maximum(m, MASK_FLOOR)
    shift = -m if blk_bias is None else blk_bias - m
    e = jnp.exp2(s4 + shift[:, None, None, :])
    o = _dot(vt, e.reshape(nk, nc).astype(BF16))
    if vt.shape[0] > dv:
        return m, o[dv:dv + 1], o[0:dv]
    return m, jnp.sum(jnp.sum(jnp.sum(e, axis=0), axis=0), axis=0, keepdims=True), o


def _merge_pieces(pieces):
    m = functools.reduce(jnp.maximum, [p[0] for p in pieces])
    ws = [jnp.exp2(p[0] - m) for p in pieces]
    l = sum(w * p[1] for w, p in zip(ws, pieces))
    o = sum(w * p[2] for w, p in zip(ws, pieces))
    return o / l


def _interleave(gens):
    results = [None] * len(gens)
    live = list(range(len(gens)))
    while live:
        for i in list(live):
            try:
                next(gens[i])
            except StopIteration as stop:
                results[i] = stop.value
                live.remove(i)
    return results


def _nsa_block(j, q, gates, ks_ref, kw_ref, kc_ref, vc_ref, ovl_ref, vst_ref, vwt_ref,
               *, rep, n_slc, n_sel, var_len):
    s0 = j * Q_BLOCK
    q4 = jnp.concatenate([q[:, r * HEAD_DIM:(r + 1) * HEAD_DIM] for r in range(rep)], axis=0)
    t_row = s0 + lax.broadcasted_iota(jnp.int32, (1, Q_BLOCK), 1)
    head = lambda a, r: a[:, r * Q_BLOCK:(r + 1) * Q_BLOCK]

    kk = lax.broadcasted_iota(jnp.int32, (Q_BLOCK, Q_BLOCK), 0)
    qq = lax.broadcasted_iota(jnp.int32, (Q_BLOCK, Q_BLOCK), 1)
    tri = jnp.concatenate([jnp.where(kk <= qq, 0.0, NEG)] * rep, axis=1)
    anti = jnp.concatenate([jnp.where(kk > qq, 0.0, NEG)] * rep, axis=1)
    n_back = WINDOW // Q_BLOCK
    rows_of = lambda ref, blk0, n: ref[0, blk0 * Q_BLOCK:(blk0 + n) * Q_BLOCK, :]
    tiles_of = lambda ref, blk0, n: jnp.concatenate([ref[blk0 + i] for i in range(n)], axis=1)
    if j >= n_back:
        far = j - n_back
        edge = (jnp.concatenate([rows_of(kw_ref, far, 1), rows_of(kw_ref, j, 1)], axis=0),
                lambda: jnp.concatenate([vwt_ref[far], vwt_ref[j]], axis=1), jnp.concatenate([anti, tri], axis=0))
    else:
        edge = (rows_of(kw_ref, j, 1), lambda: vwt_ref[j], tri)
    n_mid = min(n_back - 1, j)
    early = [edge]
    if n_mid:
        early.append((rows_of(kw_ref, j - n_mid, n_mid), lambda: tiles_of(vwt_ref, j - n_mid, n_mid), None))
    early.append((rows_of(ks_ref, j, 1), lambda: vst_ref[j], tri))
    early_scores = [_dot_nt(k, q4) for k, _, _ in early]
    q_per_var = var_len // Q_BLOCK
    first_parts = q_per_var if j >= q_per_var else j
    first_scores = _dot_nt(ks_ref[0, 0:first_parts * Q_BLOCK, :], q4) if first_parts else None
    v, h = divmod(j, q_per_var)
    later = [(ci, q_per_var) for ci in range(1, v)] + ([(v, h)] if v > 0 and h > 0 else [])
    later_scores = [_dot_nt(ks_ref[0, ci * var_len:ci * var_len + n_q * Q_BLOCK, :], q4) for ci, n_q in later]
    early_piece = lambda i: _softmax_piece(early_scores[i], early[i][1](), HEAD_DIM, early[i][2])

    kc = kc_ref[0, 0]
    n_rows = kc.shape[0]
    n_idx = lax.broadcasted_iota(jnp.int32, (n_rows, Q_BLOCK), 0)
    c_mask = (n_idx * CMP_STRIDE + (CMP_LEN - 1)) <= t_row
    st = _dot_nt(kc, q4)
    yield
    win_pieces = [early_piece(0)]
    p_sum = jnp.zeros((n_rows, Q_BLOCK), F32)
    ps = []
    for r in range(rep):
        t = jnp.where(c_mask, head(st, r), NEG)
        e = jnp.where(c_mask, jnp.exp2(t - jnp.max(t, axis=0, keepdims=True)), 0.0)
        l = jnp.sum(e, axis=0, keepdims=True)
        p = e / jnp.where(l > 0.0, l, 1.0)
        p_sum = p_sum + p
        ps.append(p.astype(BF16))
    o_cmp = _dot(vc_ref[0, 0].astype(F32).T.astype(BF16), jnp.concatenate(ps, axis=1))

    hi = p_sum.astype(BF16)
    lo = (p_sum - hi.astype(F32)).astype(BF16)
    ovl = ovl_ref[...]
    n_pad = -(-n_slc // 8) * 8
    imp = (_dot(ovl, hi) + _dot(ovl, lo))[0:n_pad]
    yield
    win_pieces += [early_piece(i) for i in range(1, len(early) - 1)]
    slc_diag = early_piece(len(early) - 1)
    blk = lax.broadcasted_iota(jnp.int32, (n_pad, Q_BLOCK), 0)
    cur = t_row // SLC_LEN
    forced = (blk == 0) | (blk == cur) | (blk == cur - 1)
    blk_causal = blk * SLC_LEN <= t_row
    imp = jnp.where(forced, 1e4, imp)
    imp = jnp.where(blk_causal, imp, -1.0)
    rank = jnp.zeros((n_pad, Q_BLOCK), F32)
    for sp in range(n_slc):
        row = imp[sp:sp + 1, :]
        ge = jnp.where(row >= imp, 1.0, 0.0)
        gt = jnp.where(row > imp, 1.0, 0.0)
        rank = rank + jnp.where(blk > sp, ge, gt)
    blocks_per_q = Q_BLOCK // SLC_LEN
    blk_bias = jnp.where((rank < float(n_sel)) & (blk < j * blocks_per_q), 0.0, NEG)
    blk_bias = jnp.concatenate([blk_bias] * rep, axis=1)
    yield

    gt = jax.nn.sigmoid(gates).T
    gate = lambda branch: jnp.concatenate([jnp.broadcast_to(gt[3 * r + branch:3 * r + branch + 1], (HEAD_DIM, Q_BLOCK))
                                           for r in range(rep)], axis=1)
    partial = gate(0) * o_cmp + gate(2) * _merge_pieces(win_pieces)
    blk_per_var = var_len // SLC_LEN
    blk_per_q = Q_BLOCK // SLC_LEN

    def chunk_piece(ci, n_q, st):
        b0 = ci * blk_per_var
        return _softmax_piece(st, tiles_of(vst_ref, ci * q_per_var, n_q), HEAD_DIM,
                              None, blk_bias[b0:b0 + n_q * blk_per_q])

    pieces = [slc_diag] + ([chunk_piece(0, first_parts, first_scores)] if first_parts else [])
    return dict(partial=partial, g_slc=gate(1), pieces=pieces, later=list(zip(later, later_scores)),
                chunk_piece=chunk_piece)


def _nsa_selected(ctx):
    pieces = list(ctx["pieces"])
    for (ci, n_q), st in ctx["later"]:
        pieces.append(ctx["chunk_piece"](ci, n_q, st))
        yield
    return ctx["partial"] + ctx["g_slc"] * _merge_pieces(pieces)


def _nsa_kernel(q_ref, ks_ref, vs_ref, kw_ref, vw_ref, kc_ref, vc_ref, gate_ref, ovl_ref,
                o_ref, vst_ref, vwt_ref, *, seq, rep, n_slc, n_sel, var_len, blocks):
    jj = pl.program_id(2)

    @pl.when(jj == 0)
    def _():
        for i in range(seq // Q_BLOCK):
            vst_ref[i] = _transposed_values(vs_ref[0, i * Q_BLOCK:(i + 1) * Q_BLOCK, :], ONES_ROWS)
            vwt_ref[i] = _transposed_values(vw_ref[0, i * Q_BLOCK:(i + 1) * Q_BLOCK, :], ONES_ROWS)

    rows = lambda h: slice(h * Q_BLOCK, (h + 1) * Q_BLOCK)
    assert blocks * Q_BLOCK == var_len
    for v in range(seq // var_len):
        @pl.when(jj == v)
        def _(v=v):
            ctxs = _interleave([
                _nsa_block(v * blocks + h, q_ref[0, rows(h), :], gate_ref[0, rows(h), :], ks_ref, kw_ref, kc_ref,
                           vc_ref, ovl_ref, vst_ref, vwt_ref, rep=rep, n_slc=n_slc, n_sel=n_sel, var_len=var_len)
                for h in range(blocks)])
            outs = _interleave([_nsa_selected(ctx) for ctx in ctxs])
            for h, out in enumerate(outs):
                for r in range(rep):
                    o_ref[0, rows(h), r * HEAD_DIM:(r + 1) * HEAD_DIM] = (
                        out[:, r * Q_BLOCK:(r + 1) * Q_BLOCK].T.astype(BF16))


def _block_overlap(n_rows, n_cmp, n_slc):
    c0 = np.arange(n_cmp) * CMP_STRIDE
    s0 = np.arange(n_slc) * SLC_LEN
    lo = np.maximum(c0[None, :], s0[:, None])
    hi = np.minimum(c0[None, :] + CMP_LEN, s0[:, None] + SLC_LEN)
    out = np.zeros((LANES, n_rows), np.float32)
    out[:n_slc, :n_cmp] = np.clip(hi - lo, 0, None) / CMP_LEN
    return out


def _nsa_attention(proj, gates, kc, vc, off, batch, seq, rep):
    g = NSA_KV_HEADS
    n_qb = seq // Q_BLOCK
    n_slc = seq // SLC_LEN
    n_sel = min(SLC_TOPK, n_slc)
    n_rows = kc.shape[2]
    n_cmp = (seq - CMP_LEN) // CMP_STRIDE + 1
    qw = rep * HEAD_DIM
    var_len = SLC_VARIANT_LEN if seq % SLC_VARIANT_LEN == 0 else seq
    blocks = var_len // Q_BLOCK
    ovl = jnp.asarray(_block_overlap(n_rows, n_cmp, n_slc), BF16)

    def kv_spec(name):
        base = off[name] // HEAD_DIM
        return pl.BlockSpec((1, seq, HEAD_DIM), lambda b, h, j: (b, 0, base + h))

    cmp_spec = pl.BlockSpec((1, 1, n_rows, HEAD_DIM), lambda b, h, j: (b, h, 0, 0))
    q_base = off["q_n"] // qw
    return pl.pallas_call(
        functools.partial(_nsa_kernel, seq=seq, rep=rep, n_slc=n_slc, n_sel=n_sel, var_len=var_len, blocks=blocks),
        grid=(batch, g, n_qb // blocks),
        scratch_shapes=[
            pltpu.VMEM((seq // Q_BLOCK, HEAD_DIM + ONES_ROWS, Q_BLOCK), BF16),
            pltpu.VMEM((seq // Q_BLOCK, HEAD_DIM + ONES_ROWS, Q_BLOCK), BF16),
        ],
        in_specs=[
            pl.BlockSpec((1, blocks * Q_BLOCK, qw), lambda b, h, j: (b, j, q_base + h)),
            kv_spec("ks"), kv_spec("vs"), kv_spec("kw"), kv_spec("vw"),
            cmp_spec, cmp_spec,
            pl.BlockSpec((1, blocks * Q_BLOCK, LANES), lambda b, h, j: (b, j, h)),
            pl.BlockSpec((LANES, n_rows), lambda b, h, j: (0, 0)),
        ],
        out_specs=pl.BlockSpec((1, blocks * Q_BLOCK, qw), lambda b, h, j: (b, j, h)),
        out_shape=jax.ShapeDtypeStruct((batch, seq, g * qw), BF16),
        compiler_params=_params("parallel", "parallel", "arbitrary"),
        name="nsa_attention",
    )(proj, proj, proj, proj, proj, kc, vc, gates, ovl)


def _diff_block(v, q, k_ref, vts, tri, lam, tq):
    qs = [q[:, c * HEAD_DIM:(c + 1) * HEAD_DIM] for c in range(2)]
    half = tq // 2
    chunks = [(v * tq, half, 0), (v * tq + half, half, half)] + [(kb * tq, tq, 0) for kb in range(v)]
    specs = [(c, ch) for ch in chunks for c in range(2)]
    lanes = lambda c: slice(c * HEAD_DIM, (c + 1) * HEAD_DIM)
    scores = lambda c, ch: _dot_nt(k_ref[0, ch[0]:ch[0] + ch[1], lanes(c)], qs[c][ch[2]:, :])
    pieces = ([], [])
    ahead = 2
    sts = [scores(*spec) for spec in specs[:ahead]]
    for i, (c, (k0, nk, q0)) in enumerate(specs):
        if i + ahead < len(specs):
            sts.append(scores(*specs[i + ahead]))
        vt = vts[k0 // tq][:, k0 % tq:k0 % tq + nk]
        diagonal = k0 >= v * tq
        bias = None if not diagonal else (tri if q0 else jnp.concatenate([tri, jnp.zeros_like(tri)], axis=1))
        m, l, o = _softmax_piece(sts[i], vt, 2 * HEAD_DIM, bias)
        if q0:
            m = jnp.concatenate([jnp.full((1, q0), MASK_FLOOR, F32), m], axis=1)
            l = jnp.concatenate([jnp.zeros((1, q0), F32), l], axis=1)
            o = jnp.concatenate([jnp.zeros((o.shape[0], q0), F32), o], axis=1)
        pieces[c].append((m, l, o))
        yield
    return _merge_pieces(pieces[0]) - lam * _merge_pieces(pieces[1])


def _diff_kernel(q_ref, k_ref, v_ref, lam_ref, sg_ref, o_ref, *, seq, tq, lambda_init):
    n_blocks = seq // tq
    rows = lambda i: slice(i * tq, (i + 1) * tq)
    vts = [_transposed_values(v_ref[0, rows(i), :], 0) for i in range(n_blocks)]
    lv = lam_ref[...]
    lam = (jnp.exp(jnp.sum(lv[0:1] * lv[1:2], axis=-1, keepdims=True))
           - jnp.exp(jnp.sum(lv[2:3] * lv[3:4], axis=-1, keepdims=True)) + lambda_init)
    half = tq // 2
    tri = jnp.where(lax.broadcasted_iota(jnp.int32, (half, half), 0) <= lax.broadcasted_iota(jnp.int32, (half, half), 1),
                    0.0, NEG)
    outs = _interleave([_diff_block(v, q_ref[0, rows(v), :], k_ref, vts, tri, lam, tq) for v in range(n_blocks)])
    for v, o in enumerate(outs):
        o_ref[0, rows(v), :] = (_rms(o.T, sg_ref[...]) * (1.0 - lambda_init)).astype(BF16)


def _diff_attention(proj, lam, subln, off, batch, seq, heads, lambda_init):
    vw = 2 * HEAD_DIM
    tq = _row_tile(seq, DIFF_Q_TILE)
    qb, kb, vb = off["q_d"] // vw, off["k_d"] // vw, off["v_d"] // vw
    return pl.pallas_call(
        functools.partial(_diff_kernel, seq=seq, tq=tq, lambda_init=lambda_init),
        grid=(batch, heads),
        in_specs=[
            pl.BlockSpec((1, seq, vw), lambda b, h: (b, 0, qb + h)),
            pl.BlockSpec((1, seq, vw), lambda b, h: (b, 0, kb + h)),
            pl.BlockSpec((1, seq, vw), lambda b, h: (b, 0, vb + h)),
            pl.BlockSpec((4, HEAD_DIM), lambda b, h: (0, 0)),
            pl.BlockSpec((1, vw), lambda b, h: (0, 0)),
        ],
        out_specs=pl.BlockSpec((1, seq, vw), lambda b, h: (b, 0, h)),
        out_shape=jax.ShapeDtypeStruct((batch, seq, heads * vw), BF16),
        compiler_params=_params("parallel", "parallel"),
        name="diff_attention",
    )(proj, proj, proj, lam, subln)


def _out_kernel(x_ref, a_ref, b_ref, wa_ref, wb_ref, o_ref):
    o_ref[...] = x_ref[...] + _dot(a_ref[...], wa_ref[...]) + _dot(b_ref[...], wb_ref[...])


def _out_project(x, a, b, wa, wb):
    n, d = x.shape
    ka, kb = a.shape[1], b.shape[1]
    tm = _row_tile(n, OUT_ROW_TILE)
    tn = LANES * _largest_divisor(d // LANES, OUT_GROUPS_PER_TILE)
    return pl.pallas_call(
        _out_kernel,
        grid=(n // tm, d // tn),
        in_specs=[
            pl.BlockSpec((tm, tn), lambda i, j: (i, j)),
            pl.BlockSpec((tm, ka), lambda i, j: (i, 0)),
            pl.BlockSpec((tm, kb), lambda i, j: (i, 0)),
            pl.BlockSpec((ka, tn), lambda i, j: (0, j)),
            pl.BlockSpec((kb, tn), lambda i, j: (0, j)),
        ],
        out_specs=pl.BlockSpec((tm, tn), lambda i, j: (i, j)),
        out_shape=jax.ShapeDtypeStruct((n, d), F32),
        compiler_params=_params("parallel", "arbitrary"),
        name="out_proj",
    )(x, a, b, wa, wb)


def _rope_tables(t):
    inv = 1.0 / (ROPE_THETA ** (jnp.arange(0, HEAD_DIM, 2, dtype=F32) / HEAD_DIM))
    ang = jnp.arange(t, dtype=F32)[:, None] * inv[None, :]
    ang = jnp.concatenate([ang, ang], axis=-1)
    sign = jnp.concatenate([-jnp.ones((HEAD_DIM // 2,), F32), jnp.ones((HEAD_DIM // 2,), F32)])
    return jnp.cos(ang), jnp.sin(ang) * sign[None, :]


def _layout(d_model):
    nsa_heads = d_model // (2 * HEAD_DIM)
    diff_heads = d_model // (4 * HEAD_DIM)
    kv = NSA_KV_HEADS * HEAD_DIM
    sizes = dict(q_n=nsa_heads * HEAD_DIM, kc=kv, vc=kv, ks=kv, vs=kv, kw=kv, vw=kv, gates=3 * nsa_heads,
                 q_d=2 * diff_heads * HEAD_DIM, k_d=2 * diff_heads * HEAD_DIM, v_d=diff_heads * 2 * HEAD_DIM)
    orig, off, o, p = {}, {}, 0, 0
    for name in ("q_n", "kc", "vc", "ks", "vs", "kw", "vw", "gates", "q_d", "k_d", "v_d"):
        orig[name] = o
        o += sizes[name]
        if name != "gates":
            off[name] = p
            p += sizes[name]
    groups = lambda names: tuple((off[nm] // LANES, (off[nm] + sizes[nm]) // LANES) for nm in names)
    return sizes, orig, off, groups(("q_n", "kc", "ks", "kw", "q_d", "k_d")), groups(("q_n", "q_d")), nsa_heads, diff_heads


def kernel(x, ffn1_norm, ffn1_w_gate, ffn1_w_up, ffn1_w_down, mix_norm, w_in, cmp_pos_k, cmp_pos_v, cmp_wk1, cmp_wk2, cmp_wv1, cmp_wv2, lam_q1, lam_k1, lam_q2, lam_k2, diff_subln, w_out, ffn2_norm, ffn2_w_gate, ffn2_w_up, ffn2_w_down, final_norm):
    batch, seq, d_model = x.shape
    depth = ffn1_norm.shape[0]
    n = batch * seq
    sizes, orig, off, rope_groups, query_groups, nsa_heads, diff_heads = _layout(d_model)
    g = NSA_KV_HEADS
    rep = nsa_heads // g
    cos, sin_signed = _rope_tables(seq)
    fg = final_norm.reshape(1, d_model)
    ffn1_stacked = (ffn1_w_gate, ffn1_w_up, ffn1_w_down)
    ffn2_stacked = (ffn2_w_gate, ffn2_w_up, ffn2_w_down)
    w_ffn = tuple(w[0].astype(BF16) for w in ffn1_stacked)

    xf = x.reshape(n, d_model)
    for l in range(depth):
        lambda_init = 0.8 - 0.6 * math.exp(-0.3 * l)
        xf, w_ffn = _ffn(xf, ffn1_norm[l].reshape(1, d_model), *w_ffn, fg, False, (ffn2_stacked, l))

        wl = w_in[l]
        gates_end = orig["gates"] + sizes["gates"]
        w_main = jnp.concatenate([wl[:, :orig["gates"]], wl[:, gates_end:]], axis=1).astype(BF16)
        gate_tiles = []
        for h in range(g):
            cols = wl[:, orig["gates"] + h * 3 * rep: orig["gates"] + (h + 1) * 3 * rep]
            gate_tiles.append(jnp.pad(cols, ((0, 0), (0, LANES - 3 * rep))))
        w_gate = jnp.concatenate(gate_tiles, axis=1).astype(BF16)
        proj, gates = _project(xf, mix_norm[l].reshape(1, d_model), w_main, w_gate, cos, sin_signed, rope_groups,
                               query_groups)
        proj3 = proj.reshape(batch, seq, -1)
        gates3 = gates.reshape(batch, seq, -1)

        kc, vc = _compress(proj3, off, batch, seq,
                           cmp_pos_k[l].reshape(1, -1).astype(BF16), cmp_pos_v[l].reshape(1, -1).astype(BF16),
                           cmp_wk1[l].astype(BF16), cmp_wk2[l].astype(BF16),
                           cmp_wv1[l].astype(BF16), cmp_wv2[l].astype(BF16))
        o_nsa = _nsa_attention(proj3, gates3, kc, vc, off, batch, seq, rep)
        lam = jnp.stack([lam_q1[l], lam_k1[l], lam_q2[l], lam_k2[l]])
        o_diff = _diff_attention(proj3, lam, diff_subln[l].reshape(1, -1), off, batch, seq, diff_heads, lambda_init)

        half = sizes["q_n"]
        wo = w_out[l].astype(BF16)
        xf = _out_project(xf, o_nsa.reshape(n, -1), o_diff.reshape(n, -1), wo[:half], wo[half:])

        last = l == depth - 1
        xf, w_ffn = _ffn(xf, ffn2_norm[l].reshape(1, d_model), *w_ffn, fg, last,
                         None if last else (ffn1_stacked, l + 1))
    return xf.reshape(batch, seq, d_model)
```

```python
import functools
import math

import numpy as np
import jax
import jax.numpy as jnp
from jax import lax
from jax.experimental import pallas as pl
from jax.experimental.pallas import tpu as pltpu

HEAD_DIM = 128
NSA_KV_HEADS = 2
CMP_LEN = 32
CMP_STRIDE = 16
SLC_LEN = 64
SLC_TOPK = 16
WINDOW = 512
Q_BLOCK = 128
ROPE_THETA = 10000.0
EPS = 1e-6
NEG = -1e30
MASK_FLOOR = -1e20
LOG2E = 1.4426950408889634
QK_SCALE = HEAD_DIM ** -0.5 * LOG2E
SLC_VARIANT_LEN = 512
LANES = 128
VMEM_LIMIT = 58 * 1024 * 1024
FFN_ROW_TILE = 1024
FFN_HIDDEN_GROUPS = 4
PROJ_ROW_TILE = 512
PROJ_GROUPS_PER_TILE = 22
OUT_ROW_TILE = 512
OUT_GROUPS_PER_TILE = 16
DIFF_Q_TILE = 512

F32 = jnp.float32
BF16 = jnp.bfloat16


def _largest_divisor(n, cap):
    for d in range(min(n, cap), 0, -1):
        if n % d == 0:
            return d
    return 1


def _row_tile(n, cap):
    for d in range(min(n, cap), 7, -1):
        if n % d == 0 and d % 8 == 0:
            return d
    return n


def _rms(x, g):
    return x * lax.rsqrt(jnp.mean(x * x, axis=-1, keepdims=True) + EPS) * g


def _dot(a, b):
    return jnp.dot(a, b, preferred_element_type=F32)


def _dot_nt(a, b):
    return lax.dot_general(a, b, (((1,), (1,)), ((), ())), preferred_element_type=F32)


def _params(*sem):
    return pltpu.CompilerParams(dimension_semantics=sem, vmem_limit_bytes=VMEM_LIMIT)


def _ffn_kernel(*refs, final_norm, cast_next):
    if cast_next:
        (x_ref, g_ref, wg_ref, wu_ref, wd_ref, fg_ref, ng_ref, nu_ref, nd_ref,
         o_ref, og_ref, ou_ref, od_ref, h_ref) = refs
        og_ref[...] = ng_ref[...].astype(BF16)
        ou_ref[...] = nu_ref[...].astype(BF16)
        od_ref[...] = nd_ref[...].astype(BF16)
    else:
        x_ref, g_ref, wg_ref, wu_ref, wd_ref, fg_ref, o_ref, h_ref = refs
    j = pl.program_id(1)

    def half_step_tile():
        h = h_ref[...]
        a = _dot(h, wg_ref[...])
        b = _dot(h, wu_ref[...])
        act = (a * jax.nn.sigmoid(a) * b * 0.5).astype(BF16)
        return _dot(act, wd_ref[...])

    @pl.when(j == 0)
    def _():
        h_ref[...] = _rms(x_ref[...], g_ref[...]).astype(BF16)
        o_ref[...] = x_ref[...] + half_step_tile()

    @pl.when(j > 0)
    def _():
        o_ref[...] += half_step_tile()

    if final_norm:
        @pl.when(j == pl.num_programs(1) - 1)
        def _():
            o_ref[...] = _rms(o_ref[...], fg_ref[...])


def _ffn(x, g, wg, wu, wd, fg, final_norm, next_weights=None):
    n, d = x.shape
    f = wg.shape[1]
    tm = _row_tile(n, FFN_ROW_TILE)
    tf = LANES * _largest_divisor(f // LANES, FFN_HIDDEN_GROUPS)
    n_i, n_j = n // tm, f // tf
    in_specs = [
        pl.BlockSpec((tm, d), lambda i, j: (i, 0)),
        pl.BlockSpec((1, d), lambda i, j: (0, 0)),
        pl.BlockSpec((d, tf), lambda i, j: (0, j)),
        pl.BlockSpec((d, tf), lambda i, j: (0, j)),
        pl.BlockSpec((tf, d), lambda i, j: (j, 0)),
        pl.BlockSpec((1, d), lambda i, j: (0, 0)),
    ]
    out_specs = [pl.BlockSpec((tm, d), lambda i, j: (i, 0))]
    out_shape = [jax.ShapeDtypeStruct((n, d), F32)]
    args = [x, g, wg, wu, wd, fg]
    if next_weights is not None:
        stacked, layer = next_weights
        dr, fr = d // n_i, tf // n_i
        assert dr * n_i == d and fr * n_i == tf and dr % 16 == 0 and fr % 16 == 0
        assert all(w.shape[1:] == s for w, s in zip(stacked, ((d, f), (d, f), (f, d))))
        in_specs += [
            pl.BlockSpec((None, dr, tf), lambda i, j: (layer, i, j)),
            pl.BlockSpec((None, dr, tf), lambda i, j: (layer, i, j)),
            pl.BlockSpec((None, fr, d), lambda i, j: (layer, j * n_i + i, 0)),
        ]
        out_specs += [
            pl.BlockSpec((dr, tf), lambda i, j: (i, j)),
            pl.BlockSpec((dr, tf), lambda i, j: (i, j)),
            pl.BlockSpec((fr, d), lambda i, j: (j * n_i + i, 0)),
        ]
        out_shape += [jax.ShapeDtypeStruct(w.shape[1:], BF16) for w in stacked]
        args += list(stacked)
    outs = pl.pallas_call(
        functools.partial(_ffn_kernel, final_norm=final_norm, cast_next=next_weights is not None),
        grid=(n_i, n_j),
        in_specs=in_specs,
        out_specs=out_specs,
        out_shape=out_shape,
        scratch_shapes=[pltpu.VMEM((tm, d), BF16)],
        compiler_params=_params("parallel", "arbitrary"),
        name="ffn",
    )(*args)
    return outs[0], tuple(outs[1:])


def _proj_kernel(x_ref, g_ref, w_ref, wgate_ref, cos_ref, sin_ref, o_ref, gate_ref, h_ref,
                 *, n_tiles, groups_per_tile, rope_groups, query_groups):
    j = pl.program_id(1)

    def column_tile(tile):
        acc = _dot(h_ref[...], w_ref[...])
        cos = cos_ref[...]
        sin = sin_ref[...]
        for gi in range(groups_per_tile):
            group = tile * groups_per_tile + gi
            in_any = lambda spans: any(lo <= group < hi for lo, hi in spans)
            y = acc[:, gi * LANES:(gi + 1) * LANES]
            if in_any(rope_groups):
                y = y * cos + pltpu.roll(y, HEAD_DIM // 2, axis=1) * sin
            if in_any(query_groups):
                y = y * QK_SCALE
            o_ref[:, gi * LANES:(gi + 1) * LANES] = y.astype(BF16)

    for tile in range(n_tiles):
        @pl.when(j == tile)
        def _(tile=tile):
            if tile == 0:
                h = _rms(x_ref[...], g_ref[...]).astype(BF16)
                h_ref[...] = h
                gate_ref[...] = _dot(h, wgate_ref[...])
            column_tile(tile)


def _project(x, g, w_main, w_gate, cos, sin_signed, rope_groups, query_groups):
    n, d = x.shape
    t = cos.shape[0]
    width = w_main.shape[1]
    gw = w_gate.shape[1]
    n_groups = width // LANES
    gpt = _largest_divisor(n_groups, PROJ_GROUPS_PER_TILE)
    tn = gpt * LANES
    tm = _row_tile(t, PROJ_ROW_TILE)
    t_blocks = t // tm
    return pl.pallas_call(
        functools.partial(_proj_kernel, n_tiles=width // tn, groups_per_tile=gpt, rope_groups=rope_groups,
                          query_groups=query_groups),
        grid=(n // tm, width // tn),
        in_specs=[
            pl.BlockSpec((tm, d), lambda i, j: (i, 0)),
            pl.BlockSpec((1, d), lambda i, j: (0, 0)),
            pl.BlockSpec((d, tn), lambda i, j: (0, j)),
            pl.BlockSpec((d, gw), lambda i, j: (0, 0)),
            pl.BlockSpec((tm, HEAD_DIM), lambda i, j: (i % t_blocks, 0)),
            pl.BlockSpec((tm, HEAD_DIM), lambda i, j: (i % t_blocks, 0)),
        ],
        out_specs=[
            pl.BlockSpec((tm, tn), lambda i, j: (i, j)),
            pl.BlockSpec((tm, gw), lambda i, j: (i, 0)),
        ],
        out_shape=[
            jax.ShapeDtypeStruct((n, width), BF16),
            jax.ShapeDtypeStruct((n, gw), F32),
        ],
        scratch_shapes=[pltpu.VMEM((tm, d), BF16)],
        compiler_params=_params("parallel", "arbitrary"),
        name="in_proj",
    )(x, g, w_main, w_gate, cos, sin_signed)


def _cmp_kernel(k_ref, v_ref, pk_ref, pv_ref, wk1_ref, wk2_ref, wv1_ref, wv2_ref, ok_ref, ov_ref, x_ref):
    seq = k_ref.shape[1]
    n_chunks = seq // CMP_STRIDE

    def one(kv_ref, p_ref, w1_ref, w2_ref, o_ref):
        x_ref[...] = kv_ref[0].astype(F32)
        a = jnp.zeros((n_chunks, w1_ref.shape[1]), F32)
        b = jnp.zeros((n_chunks, w1_ref.shape[1]), F32)
        for l in range(CMP_STRIDE):
            rows = x_ref[pl.ds(l, n_chunks, stride=CMP_STRIDE), :].astype(BF16)
            a = a + _dot(rows, w1_ref[l * HEAD_DIM:(l + 1) * HEAD_DIM, :])
            b = b + _dot(rows, w1_ref[(CMP_STRIDE + l) * HEAD_DIM:(CMP_STRIDE + l + 1) * HEAD_DIM, :])
        p = jnp.broadcast_to(p_ref[...], (8, CMP_LEN * HEAD_DIM))
        bias = _dot(p, w1_ref[...])[0:1]
        pre = a + pltpu.roll(b, n_chunks - 1, axis=0) + bias
        hid = (pre * jax.nn.sigmoid(pre)).astype(BF16)
        o_ref[0, 0] = _dot(hid, w2_ref[...]).astype(BF16)

    one(k_ref, pk_ref, wk1_ref, wk2_ref, ok_ref)
    one(v_ref, pv_ref, wv1_ref, wv2_ref, ov_ref)


def _compress(proj, off, batch, seq, pk, pv, wk1, wk2, wv1, wv2):
    g = NSA_KV_HEADS
    nc = seq // CMP_STRIDE
    hid = wk1.shape[1]
    dk = wk2.shape[1]
    assert CMP_LEN == 2 * CMP_STRIDE and dk == HEAD_DIM
    kb, vb = off["kc"] // HEAD_DIM, off["vc"] // HEAD_DIM
    full = lambda shape: pl.BlockSpec(shape, lambda i, j: tuple(0 for _ in shape))
    out_spec = pl.BlockSpec((1, 1, nc, dk), lambda i, j: (i, j, 0, 0))
    return pl.pallas_call(
        _cmp_kernel,
        grid=(batch, g),
        in_specs=[pl.BlockSpec((1, seq, HEAD_DIM), lambda i, j: (i, 0, kb + j)),
                  pl.BlockSpec((1, seq, HEAD_DIM), lambda i, j: (i, 0, vb + j)),
                  full((1, CMP_LEN * dk)), full((1, CMP_LEN * dk)),
                  full((CMP_LEN * dk, hid)), full((hid, dk)), full((CMP_LEN * dk, hid)), full((hid, dk))],
        out_specs=[out_spec, out_spec],
        out_shape=[jax.ShapeDtypeStruct((batch, g, nc, dk), BF16)] * 2,
        scratch_shapes=[pltpu.VMEM((seq, HEAD_DIM), F32)],
        compiler_params=_params("parallel", "parallel"),
        name="nsa_compress",
    )(proj, proj, pk, pv, wk1, wk2, wv1, wv2)


ONES_ROWS = 16


def _transposed_values(v, ones_rows):
    vt = v.astype(F32).T
    if ones_rows:
        vt = jnp.concatenate([vt, jnp.ones((ones_rows, vt.shape[1]), F32)], axis=0)
    return vt.astype(BF16)


def _softmax_piece(st, vt, dv, elem_bias=None, blk_bias=None):
    nk, nc = st.shape
    if elem_bias is not None:
        st = st + elem_bias
    nb = 1 if blk_bias is None else blk_bias.shape[0]
    s4 = st.reshape(nb, nk // nb // 8, 8, nc)
    bm = jnp.max(s4, axis=1)
    if blk_bias is not None:
        bm = bm + blk_bias[:, None, :]
    m = jnp.max(jnp.max(bm, axis=0), axis=0, keepdims=True)
    m = jnp.maximum(m, MASK_FLOOR)
    shift = -m if blk_bias is None else blk_bias - m
    e = jnp.exp2(s4 + shift[:, None, None, :])
    o = _dot(vt, e.reshape(nk, nc).astype(BF16))
    if vt.shape[0] > dv:
        return m, o[dv:dv + 1], o[0:dv]
    return m, jnp.sum(jnp.sum(jnp.sum(e, axis=0), axis=0), axis=0, keepdims=True), o


def _merge_pieces(pieces):
    m = functools.reduce(jnp.maximum, [p[0] for p in pieces])
    ws = [jnp.exp2(p[0] - m) for p in pieces]
    l = sum(w * p[1] for w, p in zip(ws, pieces))
    o = sum(w * p[2] for w, p in zip(ws, pieces))
    return o / l


def _interleave(gens):
    results = [None] * len(gens)
    live = list(range(len(gens)))
    while live:
        for i in list(live):
            try:
                next(gens[i])
            except StopIteration as stop:
                results[i] = stop.value
                live.remove(i)
    return results


def _nsa_block(j, q, gates, ks_ref, kw_ref, kc_ref, vc_ref, ovl_ref, vst_ref, vwt_ref,
               *, rep, n_slc, n_sel, var_len):
    s0 = j * Q_BLOCK
    q4 = jnp.concatenate([q[:, r * HEAD_DIM:(r + 1) * HEAD_DIM] for r in range(rep)], axis=0)
    t_row = s0 + lax.broadcasted_iota(jnp.int32, (1, Q_BLOCK), 1)
    head = lambda a, r: a[:, r * Q_BLOCK:(r + 1) * Q_BLOCK]

    kk = lax.broadcasted_iota(jnp.int32, (Q_BLOCK, Q_BLOCK), 0)
    qq = lax.broadcasted_iota(jnp.int32, (Q_BLOCK, Q_BLOCK), 1)
    tri = jnp.concatenate([jnp.where(kk <= qq, 0.0, NEG)] * rep, axis=1)
    anti = jnp.concatenate([jnp.where(kk > qq, 0.0, NEG)] * rep, axis=1)
    n_back = WINDOW // Q_BLOCK
    rows_of = lambda ref, blk0, n: ref[0, blk0 * Q_BLOCK:(blk0 + n) * Q_BLOCK, :]
    tiles_of = lambda ref, blk0, n: jnp.concatenate([ref[blk0 + i] for i in range(n)], axis=1)
    if j >= n_back:
        far = j - n_back
        edge = (jnp.concatenate([rows_of(kw_ref, far, 1), rows_of(kw_ref, j, 1)], axis=0),
                lambda: jnp.concatenate([vwt_ref[far], vwt_ref[j]], axis=1), jnp.concatenate([anti, tri], axis=0))
    else:
        edge = (rows_of(kw_ref, j, 1), lambda: vwt_ref[j], tri)
    n_mid = min(n_back - 1, j)
    early = [edge]
    if n_mid:
        early.append((rows_of(kw_ref, j - n_mid, n_mid), lambda: tiles_of(vwt_ref, j - n_mid, n_mid), None))
    early.append((rows_of(ks_ref, j, 1), lambda: vst_ref[j], tri))
    early_scores = [_dot_nt(k, q4) for k, _, _ in early]
    q_per_var = var_len // Q_BLOCK
    first_parts = q_per_var if j >= q_per_var else j
    first_scores = _dot_nt(ks_ref[0, 0:first_parts * Q_BLOCK, :], q4) if first_parts else None
    v, h = divmod(j, q_per_var)
    later = [(ci, q_per_var) for ci in range(1, v)] + ([(v, h)] if v > 0 and h > 0 else [])
    later_scores = [_dot_nt(ks_ref[0, ci * var_len:ci * var_len + n_q * Q_BLOCK, :], q4) for ci, n_q in later]
    early_piece = lambda i: _softmax_piece(early_scores[i], early[i][1](), HEAD_DIM, early[i][2])

    kc = kc_ref[0, 0]
    n_rows = kc.shape[0]
    n_idx = lax.broadcasted_iota(jnp.int32, (n_rows, Q_BLOCK), 0)
    c_mask = (n_idx * CMP_STRIDE + (CMP_LEN - 1)) <= t_row
    st = _dot_nt(kc, q4)
    yield
    win_pieces = [early_piece(0)]
    p_sum = jnp.zeros((n_rows, Q_BLOCK), F32)
    ps = []
    for r in range(rep):
        t = jnp.where(c_mask, head(st, r), NEG)
        e = jnp.where(c_mask, jnp.exp2(t - jnp.max(t, axis=0, keepdims=True)), 0.0)
        l = jnp.sum(e, axis=0, keepdims=True)
        p = e / jnp.where(l > 0.0, l, 1.0)
        p_sum = p_sum + p
        ps.append(p.astype(BF16))
    o_cmp = _dot(vc_ref[0, 0].astype(F32).T.astype(BF16), jnp.concatenate(ps, axis=1))

    hi = p_sum.astype(BF16)
    lo = (p_sum - hi.astype(F32)).astype(BF16)
    ovl = ovl_ref[...]
    n_pad = -(-n_slc // 8) * 8
    imp = (_dot(ovl, hi) + _dot(ovl, lo))[0:n_pad]
    yield
    win_pieces += [early_piece(i) for i in range(1, len(early) - 1)]
    slc_diag = early_piece(len(early) - 1)
    blk = lax.broadcasted_iota(jnp.int32, (n_pad, Q_BLOCK), 0)
    cur = t_row // SLC_LEN
    forced = (blk == 0) | (blk == cur) | (blk == cur - 1)
    blk_causal = blk * SLC_LEN <= t_row
    imp = jnp.where(forced, 1e4, imp)
    imp = jnp.where(blk_causal, imp, -1.0)
    rank = jnp.zeros((n_pad, Q_BLOCK), F32)
    for sp in range(n_slc):
        row = imp[sp:sp + 1, :]
        ge = jnp.where(row >= imp, 1.0, 0.0)
        gt = jnp.where(row > imp, 1.0, 0.0)
        rank = rank + jnp.where(blk > sp, ge, gt)
    blocks_per_q = Q_BLOCK // SLC_LEN
    blk_bias = jnp.where((rank < float(n_sel)) & (blk < j * blocks_per_q), 0.0, NEG)
    blk_bias = jnp.concatenate([blk_bias] * rep, axis=1)
    yield

    gt = jax.nn.sigmoid(gates).T
    gate = lambda branch: jnp.concatenate([jnp.broadcast_to(gt[3 * r + branch:3 * r + branch + 1], (HEAD_DIM, Q_BLOCK))
                                           for r in range(rep)], axis=1)
    partial = gate(0) * o_cmp + gate(2) * _merge_pieces(win_pieces)
    blk_per_var = var_len // SLC_LEN
    blk_per_q = Q_BLOCK // SLC_LEN

    def chunk_piece(ci, n_q, st):
        b0 = ci * blk_per_var
        return _softmax_piece(st, tiles_of(vst_ref, ci * q_per_var, n_q), HEAD_DIM,
                              None, blk_bias[b0:b0 + n_q * blk_per_q])

    pieces = [slc_diag] + ([chunk_piece(0, first_parts, first_scores)] if first_parts else [])
    return dict(partial=partial, g_slc=gate(1), pieces=pieces, later=list(zip(later, later_scores)),
                chunk_piece=chunk_piece)


def _nsa_selected(ctx):
    pieces = list(ctx["pieces"])
    for (ci, n_q), st in ctx["later"]:
        pieces.append(ctx["chunk_piece"](ci, n_q, st))
        yield
    return ctx["partial"] + ctx["g_slc"] * _merge_pieces(pieces)


def _nsa_kernel(q_ref, ks_ref, vs_ref, kw_ref, vw_ref, kc_ref, vc_ref, gate_ref, ovl_ref,
                o_ref, vst_ref, vwt_ref, *, seq, rep, n_slc, n_sel, var_len, blocks):
    jj = pl.program_id(2)

    @pl.when(jj == 0)
    def _():
        for i in range(seq // Q_BLOCK):
            vst_ref[i] = _transposed_values(vs_ref[0, i * Q_BLOCK:(i + 1) * Q_BLOCK, :], ONES_ROWS)
            vwt_ref[i] = _transposed_values(vw_ref[0, i * Q_BLOCK:(i + 1) * Q_BLOCK, :], ONES_ROWS)

    rows = lambda h: slice(h * Q_BLOCK, (h + 1) * Q_BLOCK)
    assert blocks * Q_BLOCK == var_len
    for v in range(seq // var_len):
        @pl.when(jj == v)
        def _(v=v):
            ctxs = _interleave([
                _nsa_block(v * blocks + h, q_ref[0, rows(h), :], gate_ref[0, rows(h), :], ks_ref, kw_ref, kc_ref,
                           vc_ref, ovl_ref, vst_ref, vwt_ref, rep=rep, n_slc=n_slc, n_sel=n_sel, var_len=var_len)
                for h in range(blocks)])
            outs = _interleave([_nsa_selected(ctx) for ctx in ctxs])
            for h, out in enumerate(outs):
                for r in range(rep):
                    o_ref[0, rows(h), r * HEAD_DIM:(r + 1) * HEAD_DIM] = (
                        out[:, r * Q_BLOCK:(r + 1) * Q_BLOCK].T.astype(BF16))


def _block_overlap(n_rows, n_cmp, n_slc):
    c0 = np.arange(n_cmp) * CMP_STRIDE
    s0 = np.arange(n_slc) * SLC_LEN
    lo = np.maximum(c0[None, :], s0[:, None])
    hi = np.minimum(c0[None, :] + CMP_LEN, s0[:, None] + SLC_LEN)
    out = np.zeros((LANES, n_rows), np.float32)
    out[:n_slc, :n_cmp] = np.clip(hi - lo, 0, None) / CMP_LEN
    return out


def _nsa_attention(proj, gates, kc, vc, off, batch, seq, rep):
    g = NSA_KV_HEADS
    n_qb = seq // Q_BLOCK
    n_slc = seq // SLC_LEN
    n_sel = min(SLC_TOPK, n_slc)
    n_rows = kc.shape[2]
    n_cmp = (seq - CMP_LEN) // CMP_STRIDE + 1
    qw = rep * HEAD_DIM
    var_len = SLC_VARIANT_LEN if seq % SLC_VARIANT_LEN == 0 else seq
    blocks = var_len // Q_BLOCK
    ovl = jnp.asarray(_block_overlap(n_rows, n_cmp, n_slc), BF16)

    def kv_spec(name):
        base = off[name] // HEAD_DIM
        return pl.BlockSpec((1, seq, HEAD_DIM), lambda b, h, j: (b, 0, base + h))

    cmp_spec = pl.BlockSpec((1, 1, n_rows, HEAD_DIM), lambda b, h, j: (b, h, 0, 0))
    q_base = off["q_n"] // qw
    return pl.pallas_call(
        functools.partial(_nsa_kernel, seq=seq, rep=rep, n_slc=n_slc, n_sel=n_sel, var_len=var_len, blocks=blocks),
        grid=(batch, g, n_qb // blocks),
        scratch_shapes=[
            pltpu.VMEM((seq // Q_BLOCK, HEAD_DIM + ONES_ROWS, Q_BLOCK), BF16),
            pltpu.VMEM((seq // Q_BLOCK, HEAD_DIM + ONES_ROWS, Q_BLOCK), BF16),
        ],
        in_specs=[
            pl.BlockSpec((1, blocks * Q_BLOCK, qw), lambda b, h, j: (b, j, q_base + h)),
            kv_spec("ks"), kv_spec("vs"), kv_spec("kw"), kv_spec("vw"),
            cmp_spec, cmp_spec,
            pl.BlockSpec((1, blocks * Q_BLOCK, LANES), lambda b, h, j: (b, j, h)),
            pl.BlockSpec((LANES, n_rows), lambda b, h, j: (0, 0)),
        ],
        out_specs=pl.BlockSpec((1, blocks * Q_BLOCK, qw), lambda b, h, j: (b, j, h)),
        out_shape=jax.ShapeDtypeStruct((batch, seq, g * qw), BF16),
        compiler_params=_params("parallel", "parallel", "arbitrary"),
        name="nsa_attention",
    )(proj, proj, proj, proj, proj, kc, vc, gates, ovl)


def _diff_block(v, q, k_ref, vts, tri, lam, tq):
    qs = [q[:, c * HEAD_DIM:(c + 1) * HEAD_DIM] for c in range(2)]
    half = tq // 2
    chunks = [(v * tq, half, 0), (v * tq + half, half, half)] + [(kb * tq, tq, 0) for kb in range(v)]
    specs = [(c, ch) for ch in chunks for c in range(2)]
    lanes = lambda c: slice(c * HEAD_DIM, (c + 1) * HEAD_DIM)
    scores = lambda c, ch: _dot_nt(k_ref[0, ch[0]:ch[0] + ch[1], lanes(c)], qs[c][ch[2]:, :])
    pieces = ([], [])
    sts = [scores(*spec) for spec in specs]
    yield
    for i, (c, (k0, nk, q0)) in enumerate(specs):
        vt = vts[k0 // tq][:, k0 % tq:k0 % tq + nk]
        diagonal = k0 >= v * tq
        bias = None if not diagonal else (tri if q0 else jnp.concatenate([tri, jnp.zeros_like(tri)], axis=1))
        m, l, o = _softmax_piece(sts[i], vt, 2 * HEAD_DIM, bias)
        if q0:
            m = jnp.concatenate([jnp.full((1, q0), MASK_FLOOR, F32), m], axis=1)
            l = jnp.concatenate([jnp.zeros((1, q0), F32), l], axis=1)
            o = jnp.concatenate([jnp.zeros((o.shape[0], q0), F32), o], axis=1)
        pieces[c].append((m, l, o))
        yield
    return _merge_pieces(pieces[0]) - lam * _merge_pieces(pieces[1])


def _diff_kernel(q_ref, k_ref, v_ref, lam_ref, sg_ref, o_ref, *, seq, tq, lambda_init):
    n_blocks = seq // tq
    rows = lambda i: slice(i * tq, (i + 1) * tq)
    vts = [_transposed_values(v_ref[0, rows(i), :], 0) for i in range(n_blocks)]
    lv = lam_ref[...]
    lam = (jnp.exp(jnp.sum(lv[0:1] * lv[1:2], axis=-1, keepdims=True))
           - jnp.exp(jnp.sum(lv[2:3] * lv[3:4], axis=-1, keepdims=True)) + lambda_init)
    half = tq // 2
    tri = jnp.where(lax.broadcasted_iota(jnp.int32, (half, half), 0) <= lax.broadcasted_iota(jnp.int32, (half, half), 1),
                    0.0, NEG)
    outs = _interleave([_diff_block(v, q_ref[0, rows(v), :], k_ref, vts, tri, lam, tq) for v in range(n_blocks)])
    for v, o in enumerate(outs):
        o_ref[0, rows(v), :] = (_rms(o.T, sg_ref[...]) * (1.0 - lambda_init)).astype(BF16)


def _diff_attention(proj, lam, subln, off, batch, seq, heads, lambda_init):
    vw = 2 * HEAD_DIM
    tq = _row_tile(seq, DIFF_Q_TILE)
    qb, kb, vb = off["q_d"] // vw, off["k_d"] // vw, off["v_d"] // vw
    return pl.pallas_call(
        functools.partial(_diff_kernel, seq=seq, tq=tq, lambda_init=lambda_init),
        grid=(batch, heads),
        in_specs=[
            pl.BlockSpec((1, seq, vw), lambda b, h: (b, 0, qb + h)),
            pl.BlockSpec((1, seq, vw), lambda b, h: (b, 0, kb + h)),
            pl.BlockSpec((1, seq, vw), lambda b, h: (b, 0, vb + h)),
            pl.BlockSpec((4, HEAD_DIM), lambda b, h: (0, 0)),
            pl.BlockSpec((1, vw), lambda b, h: (0, 0)),
        ],
        out_specs=pl.BlockSpec((1, seq, vw), lambda b, h: (b, 0, h)),
        out_shape=jax.ShapeDtypeStruct((batch, seq, heads * vw), BF16),
        compiler_params=_params("parallel", "parallel"),
        name="diff_attention",
    )(proj, proj, proj, lam, subln)


def _out_kernel(x_ref, a_ref, b_ref, wa_ref, wb_ref, o_ref):
    o_ref[...] = x_ref[...] + _dot(a_ref[...], wa_ref[...]) + _dot(b_ref[...], wb_ref[...])


def _out_project(x, a, b, wa, wb):
    n, d = x.shape
    ka, kb = a.shape[1], b.shape[1]
    tm = _row_tile(n, OUT_ROW_TILE)
    tn = LANES * _largest_divisor(d // LANES, OUT_GROUPS_PER_TILE)
    return pl.pallas_call(
        _out_kernel,
        grid=(n // tm, d // tn),
        in_specs=[
            pl.BlockSpec((tm, tn), lambda i, j: (i, j)),
            pl.BlockSpec((tm, ka), lambda i, j: (i, 0)),
            pl.BlockSpec((tm, kb), lambda i, j: (i, 0)),
            pl.BlockSpec((ka, tn), lambda i, j: (0, j)),
            pl.BlockSpec((kb, tn), lambda i, j: (0, j)),
        ],
        out_specs=pl.BlockSpec((tm, tn), lambda i, j: (i, j)),
        out_shape=jax.ShapeDtypeStruct((n, d), F32),
        compiler_params=_params("parallel", "arbitrary"),
        name="out_proj",
    )(x, a, b, wa, wb)


def _rope_tables(t):
    inv = 1.0 / (ROPE_THETA ** (jnp.arange(0, HEAD_DIM, 2, dtype=F32) / HEAD_DIM))
    ang = jnp.arange(t, dtype=F32)[:, None] * inv[None, :]
    ang = jnp.concatenate([ang, ang], axis=-1)
    sign = jnp.concatenate([-jnp.ones((HEAD_DIM // 2,), F32), jnp.ones((HEAD_DIM // 2,), F32)])
    return jnp.cos(ang), jnp.sin(ang) * sign[None, :]


def _layout(d_model):
    nsa_heads = d_model // (2 * HEAD_DIM)
    diff_heads = d_model // (4 * HEAD_DIM)
    kv = NSA_KV_HEADS * HEAD_DIM
    sizes = dict(q_n=nsa_heads * HEAD_DIM, kc=kv, vc=kv, ks=kv, vs=kv, kw=kv, vw=kv, gates=3 * nsa_heads,
                 q_d=2 * diff_heads * HEAD_DIM, k_d=2 * diff_heads * HEAD_DIM, v_d=diff_heads * 2 * HEAD_DIM)
    orig, off, o, p = {}, {}, 0, 0
    for name in ("q_n", "kc", "vc", "ks", "vs", "kw", "vw", "gates", "q_d", "k_d", "v_d"):
        orig[name] = o
        o += sizes[name]
        if name != "gates":
            off[name] = p
            p += sizes[name]
    groups = lambda names: tuple((off[nm] // LANES, (off[nm] + sizes[nm]) // LANES) for nm in names)
    return sizes, orig, off, groups(("q_n", "kc", "ks", "kw", "q_d", "k_d")), groups(("q_n", "q_d")), nsa_heads, diff_heads


def kernel(x, ffn1_norm, ffn1_w_gate, ffn1_w_up, ffn1_w_down, mix_norm, w_in, cmp_pos_k, cmp_pos_v, cmp_wk1, cmp_wk2, cmp_wv1, cmp_wv2, lam_q1, lam_k1, lam_q2, lam_k2, diff_subln, w_out, ffn2_norm, ffn2_w_gate, ffn2_w_up, ffn2_w_down, final_norm):
    batch, seq, d_model = x.shape
    depth = ffn1_norm.shape[0]
    n = batch * seq
    sizes, orig, off, rope_groups, query_groups, nsa_heads, diff_heads = _layout(d_model)
    g = NSA_KV_HEADS
    rep = nsa_heads // g
    cos, sin_signed = _rope_tables(seq)
    fg = final_norm.reshape(1, d_model)
    ffn1_stacked = (ffn1_w_gate, ffn1_w_up, ffn1_w_down)
    ffn2_stacked = (ffn2_w_gate, ffn2_w_up, ffn2_w_down)
    w_ffn = tuple(w[0].astype(BF16) for w in ffn1_stacked)

    xf = x.reshape(n, d_model)
    for l in range(depth):
        lambda_init = 0.8 - 0.6 * math.exp(-0.3 * l)
        xf, w_ffn = _ffn(xf, ffn1_norm[l].reshape(1, d_model), *w_ffn, fg, False, (ffn2_stacked, l))

        wl = w_in[l]
        gates_end = orig["gates"] + sizes["gates"]
        w_main = jnp.concatenate([wl[:, :orig["gates"]], wl[:, gates_end:]], axis=1).astype(BF16)
        gate_tiles = []
        for h in range(g):
            cols = wl[:, orig["gates"] + h * 3 * rep: orig["gates"] + (h + 1) * 3 * rep]
            gate_tiles.append(jnp.pad(cols, ((0, 0), (0, LANES - 3 * rep))))
        w_gate = jnp.concatenate(gate_tiles, axis=1).astype(BF16)
        proj, gates = _project(xf, mix_norm[l].reshape(1, d_model), w_main, w_gate, cos, sin_signed, rope_groups,
                               query_groups)
        proj3 = proj.reshape(batch, seq, -1)
        gates3 = gates.reshape(batch, seq, -1)

        kc, vc = _compress(proj3, off, batch, seq,
                           cmp_pos_k[l].reshape(1, -1).astype(BF16), cmp_pos_v[l].reshape(1, -1).astype(BF16),
                           cmp_wk1[l].astype(BF16), cmp_wk2[l].astype(BF16),
                           cmp_wv1[l].astype(BF16), cmp_wv2[l].astype(BF16))
        o_nsa = _nsa_attention(proj3, gates3, kc, vc, off, batch, seq, rep)
        lam = jnp.stack([lam_q1[l], lam_k1[l], lam_q2[l], lam_k2[l]])
        o_diff = _diff_attention(proj3, lam, diff_subln[l].reshape(1, -1), off, batch, seq, diff_heads, lambda_init)

        half = sizes["q_n"]
        wo = w_out[l].astype(BF16)
        xf = _out_project(xf, o_nsa.reshape(n, -1), o_diff.reshape(n, -1), wo[:half], wo[half:])

        last = l == depth - 1
        xf, w_ffn = _ffn(xf, ffn2_norm[l].reshape(1, d_model), *w_ffn, fg, last,
                         None if last else (ffn1_stacked, l + 1))
    return xf.reshape(batch, seq, d_model)
```

```python
import functools
import math

import numpy as np
import jax
import jax.numpy as jnp
from jax import lax
from jax.experimental import pallas as pl
from jax.experimental.pallas import tpu as pltpu

HEAD_DIM = 128
NSA_KV_HEADS = 2
CMP_LEN = 32
CMP_STRIDE = 16
SLC_LEN = 64
SLC_TOPK = 16
WINDOW = 512
Q_BLOCK = 128
ROPE_THETA = 10000.0
EPS = 1e-6
NEG = -1e30
MASK_FLOOR = -1e20
LOG2E = 1.4426950408889634
QK_SCALE = HEAD_DIM ** -0.5 * LOG2E
SLC_VARIANT_LEN = 512
LANES = 128
VMEM_LIMIT = 58 * 1024 * 1024
FFN_ROW_TILE = 1024
FFN_HIDDEN_GROUPS = 4
PROJ_ROW_TILE = 512
PROJ_GROUPS_PER_TILE = 22
OUT_ROW_TILE = 512
OUT_GROUPS_PER_TILE = 16
DIFF_Q_TILE = 512

F32 = jnp.float32
BF16 = jnp.bfloat16


def _largest_divisor(n, cap):
    for d in range(min(n, cap), 0, -1):
        if n % d == 0:
            return d
    return 1


def _row_tile(n, cap):
    for d in range(min(n, cap), 7, -1):
        if n % d == 0 and d % 8 == 0:
            return d
    return n


def _rms(x, g):
    return x * lax.rsqrt(jnp.mean(x * x, axis=-1, keepdims=True) + EPS) * g


def _dot(a, b):
    return jnp.dot(a, b, preferred_element_type=F32)


def _dot_nt(a, b):
    return lax.dot_general(a, b, (((1,), (1,)), ((), ())), preferred_element_type=F32)


def _params(*sem):
    return pltpu.CompilerParams(dimension_semantics=sem, vmem_limit_bytes=VMEM_LIMIT)


def _ffn_kernel(*refs, final_norm, cast_next):
    if cast_next:
        (x_ref, g_ref, wg_ref, wu_ref, wd_ref, fg_ref, ng_ref, nu_ref, nd_ref,
         o_ref, og_ref, ou_ref, od_ref, h_ref) = refs
        og_ref[...] = ng_ref[...].astype(BF16)
        ou_ref[...] = nu_ref[...].astype(BF16)
        od_ref[...] = nd_ref[...].astype(BF16)
    else:
        x_ref, g_ref, wg_ref, wu_ref, wd_ref, fg_ref, o_ref, h_ref = refs
    j = pl.program_id(1)

    def half_step_tile():
        h = h_ref[...]
        a = _dot(h, wg_ref[...])
        b = _dot(h, wu_ref[...])
        act = (a * jax.nn.sigmoid(a) * b * 0.5).astype(BF16)
        return _dot(act, wd_ref[...])

    @pl.when(j == 0)
    def _():
        h_ref[...] = _rms(x_ref[...], g_ref[...]).astype(BF16)
        o_ref[...] = x_ref[...] + half_step_tile()

    @pl.when(j > 0)
    def _():
        o_ref[...] += half_step_tile()

    if final_norm:
        @pl.when(j == pl.num_programs(1) - 1)
        def _():
            o_ref[...] = _rms(o_ref[...], fg_ref[...])


def _ffn(x, g, wg, wu, wd, fg, final_norm, next_weights=None):
    n, d = x.shape
    f = wg.shape[1]
    tm = _row_tile(n, FFN_ROW_TILE)
    tf = LANES * _largest_divisor(f // LANES, FFN_HIDDEN_GROUPS)
    n_i, n_j = n // tm, f // tf
    in_specs = [
        pl.BlockSpec((tm, d), lambda i, j: (i, 0)),
        pl.BlockSpec((1, d), lambda i, j: (0, 0)),
        pl.BlockSpec((d, tf), lambda i, j: (0, j)),
        pl.BlockSpec((d, tf), lambda i, j: (0, j)),
        pl.BlockSpec((tf, d), lambda i, j: (j, 0)),
        pl.BlockSpec((1, d), lambda i, j: (0, 0)),
    ]
    out_specs = [pl.BlockSpec((tm, d), lambda i, j: (i, 0))]
    out_shape = [jax.ShapeDtypeStruct((n, d), F32)]
    args = [x, g, wg, wu, wd, fg]
    if next_weights is not None:
        stacked, layer = next_weights
        dr, fr = d // n_i, tf // n_i
        assert dr * n_i == d and fr * n_i == tf and dr % 16 == 0 and fr % 16 == 0
        assert all(w.shape[1:] == s for w, s in zip(stacked, ((d, f), (d, f), (f, d))))
        in_specs += [
            pl.BlockSpec((None, dr, tf), lambda i, j: (layer, i, j)),
            pl.BlockSpec((None, dr, tf), lambda i, j: (layer, i, j)),
            pl.BlockSpec((None, fr, d), lambda i, j: (layer, j * n_i + i, 0)),
        ]
        out_specs += [
            pl.BlockSpec((dr, tf), lambda i, j: (i, j)),
            pl.BlockSpec((dr, tf), lambda i, j: (i, j)),
            pl.BlockSpec((fr, d), lambda i, j: (j * n_i + i, 0)),
        ]
        out_shape += [jax.ShapeDtypeStruct(w.shape[1:], BF16) for w in stacked]
        args += list(stacked)
    outs = pl.pallas_call(
        functools.partial(_ffn_kernel, final_norm=final_norm, cast_next=next_weights is not None),
        grid=(n_i, n_j),
        in_specs=in_specs,
        out_specs=out_specs,
        out_shape=out_shape,
        scratch_shapes=[pltpu.VMEM((tm, d), BF16)],
        compiler_params=_params("parallel", "arbitrary"),
        name="ffn",
    )(*args)
    return outs[0], tuple(outs[1:])


def _proj_kernel(x_ref, g_ref, w_ref, wgate_ref, cos_ref, sin_ref, o_ref, gate_ref, h_ref,
                 *, n_tiles, groups_per_tile, rope_groups, query_groups):
    j = pl.program_id(1)

    def column_tile(tile):
        acc = _dot(h_ref[...], w_ref[...])
        cos = cos_ref[...]
        sin = sin_ref[...]
        for gi in range(groups_per_tile):
            group = tile * groups_per_tile + gi
            in_any = lambda spans: any(lo <= group < hi for lo, hi in spans)
            y = acc[:, gi * LANES:(gi + 1) * LANES]
            if in_any(rope_groups):
                y = y * cos + pltpu.roll(y, HEAD_DIM // 2, axis=1) * sin
            if in_any(query_groups):
                y = y * QK_SCALE
            o_ref[:, gi * LANES:(gi + 1) * LANES] = y.astype(BF16)

    for tile in range(n_tiles):
        @pl.when(j == tile)
        def _(tile=tile):
            if tile == 0:
                h = _rms(x_ref[...], g_ref[...]).astype(BF16)
                h_ref[...] = h
                gate_ref[...] = _dot(h, wgate_ref[...])
            column_tile(tile)


def _project(x, g, w_main, w_gate, cos, sin_signed, rope_groups, query_groups):
    n, d = x.shape
    t = cos.shape[0]
    width = w_main.shape[1]
    gw = w_gate.shape[1]
    n_groups = width // LANES
    gpt = _largest_divisor(n_groups, PROJ_GROUPS_PER_TILE)
    tn = gpt * LANES
    tm = _row_tile(t, PROJ_ROW_TILE)
    t_blocks = t // tm
    return pl.pallas_call(
        functools.partial(_proj_kernel, n_tiles=width // tn, groups_per_tile=gpt, rope_groups=rope_groups,
                          query_groups=query_groups),
        grid=(n // tm, width // tn),
        in_specs=[
            pl.BlockSpec((tm, d), lambda i, j: (i, 0)),
            pl.BlockSpec((1, d), lambda i, j: (0, 0)),
            pl.BlockSpec((d, tn), lambda i, j: (0, j)),
            pl.BlockSpec((d, gw), lambda i, j: (0, 0)),
            pl.BlockSpec((tm, HEAD_DIM), lambda i, j: (i % t_blocks, 0)),
            pl.BlockSpec((tm, HEAD_DIM), lambda i, j: (i % t_blocks, 0)),
        ],
        out_specs=[
            pl.BlockSpec((tm, tn), lambda i, j: (i, j)),
            pl.BlockSpec((tm, gw), lambda i, j: (i, 0)),
        ],
        out_shape=[
            jax.ShapeDtypeStruct((n, width), BF16),
            jax.ShapeDtypeStruct((n, gw), F32),
        ],
        scratch_shapes=[pltpu.VMEM((tm, d), BF16)],
        compiler_params=_params("parallel", "arbitrary"),
        name="in_proj",
    )(x, g, w_main, w_gate, cos, sin_signed)


def _cmp_kernel(k_ref, v_ref, pk_ref, pv_ref, wk1_ref, wk2_ref, wv1_ref, wv2_ref, ok_ref, ov_ref, x_ref):
    seq = k_ref.shape[1]
    n_chunks = seq // CMP_STRIDE

    def one(kv_ref, p_ref, w1_ref, w2_ref, o_ref):
        x_ref[...] = kv_ref[0].astype(F32)
        a = jnp.zeros((n_chunks, w1_ref.shape[1]), F32)
        b = jnp.zeros((n_chunks, w1_ref.shape[1]), F32)
        for l in range(CMP_STRIDE):
            rows = x_ref[pl.ds(l, n_chunks, stride=CMP_STRIDE), :].astype(BF16)
            a = a + _dot(rows, w1_ref[l * HEAD_DIM:(l + 1) * HEAD_DIM, :])
            b = b + _dot(rows, w1_ref[(CMP_STRIDE + l) * HEAD_DIM:(CMP_STRIDE + l + 1) * HEAD_DIM, :])
        p = jnp.broadcast_to(p_ref[...], (8, CMP_LEN * HEAD_DIM))
        bias = _dot(p, w1_ref[...])[0:1]
        pre = a + pltpu.roll(b, n_chunks - 1, axis=0) + bias
        hid = (pre * jax.nn.sigmoid(pre)).astype(BF16)
        o_ref[0, 0] = _dot(hid, w2_ref[...]).astype(BF16)

    one(k_ref, pk_ref, wk1_ref, wk2_ref, ok_ref)
    one(v_ref, pv_ref, wv1_ref, wv2_ref, ov_ref)


def _compress(proj, off, batch, seq, pk, pv, wk1, wk2, wv1, wv2):
    g = NSA_KV_HEADS
    nc = seq // CMP_STRIDE
    hid = wk1.shape[1]
    dk = wk2.shape[1]
    assert CMP_LEN == 2 * CMP_STRIDE and dk == HEAD_DIM
    kb, vb = off["kc"] // HEAD_DIM, off["vc"] // HEAD_DIM
    full = lambda shape: pl.BlockSpec(shape, lambda i, j: tuple(0 for _ in shape))
    out_spec = pl.BlockSpec((1, 1, nc, dk), lambda i, j: (i, j, 0, 0))
    return pl.pallas_call(
        _cmp_kernel,
        grid=(batch, g),
        in_specs=[pl.BlockSpec((1, seq, HEAD_DIM), lambda i, j: (i, 0, kb + j)),
                  pl.BlockSpec((1, seq, HEAD_DIM), lambda i, j: (i, 0, vb + j)),
                  full((1, CMP_LEN * dk)), full((1, CMP_LEN * dk)),
                  full((CMP_LEN * dk, hid)), full((hid, dk)), full((CMP_LEN * dk, hid)), full((hid, dk))],
        out_specs=[out_spec, out_spec],
        out_shape=[jax.ShapeDtypeStruct((batch, g, nc, dk), BF16)] * 2,
        scratch_shapes=[pltpu.VMEM((seq, HEAD_DIM), F32)],
        compiler_params=_params("parallel", "parallel"),
        name="nsa_compress",
    )(proj, proj, pk, pv, wk1, wk2, wv1, wv2)


ONES_ROWS = 16


def _transposed_values(v, ones_rows):
    vt = v.astype(F32).T
    if ones_rows:
        vt = jnp.concatenate([vt, jnp.ones((ones_rows, vt.shape[1]), F32)], axis=0)
    return vt.astype(BF16)


def _softmax_piece(st, vt, dv, elem_bias=None, blk_bias=None):
    nk, nc = st.shape
    if elem_bias is not None:
        st = st + elem_bias
    nb = 1 if blk_bias is None else blk_bias.shape[0]
    s4 = st.reshape(nb, nk // nb // 8, 8, nc)
    bm = jnp.max(s4, axis=1)
    if blk_bias is not None:
        bm = bm + blk_bias[:, None, :]
    m = jnp.max(jnp.max(bm, axis=0), axis=0, keepdims=True)
    m = jnp.maximum(m, MASK_FLOOR)
    shift = -m if blk_bias is None else blk_bias - m
    e = jnp.exp2(s4 + shift[:, None, None, :])
    o = _dot(vt, e.reshape(nk, nc).astype(BF16))
    if vt.shape[0] > dv:
        return m, o[dv:dv + 1], o[0:dv]
    return m, jnp.sum(jnp.sum(jnp.sum(e, axis=0), axis=0), axis=0, keepdims=True), o


def _merge_pieces(pieces):
    m = functools.reduce(jnp.maximum, [p[0] for p in pieces])
    ws = [jnp.exp2(p[0] - m) for p in pieces]
    l = sum(w * p[1] for w, p in zip(ws, pieces))
    o = sum(w * p[2] for w, p in zip(ws, pieces))
    return o / l


def _interleave(gens):
    results = [None] * len(gens)
    live = list(range(len(gens)))
    while live:
        for i in list(live):
            try:
                next(gens[i])
            except StopIteration as stop:
                results[i] = stop.value
                live.remove(i)
    return results


def _nsa_block(j, q, gates, ks_ref, kw_ref, kc_ref, vc_ref, ovl_ref, vst_ref, vwt_ref,
               *, rep, n_slc, n_sel, var_len):
    s0 = j * Q_BLOCK
    q4 = jnp.concatenate([q[:, r * HEAD_DIM:(r + 1) * HEAD_DIM] for r in range(rep)], axis=0)
    t_row = s0 + lax.broadcasted_iota(jnp.int32, (1, Q_BLOCK), 1)
    head = lambda a, r: a[:, r * Q_BLOCK:(r + 1) * Q_BLOCK]

    kk = lax.broadcasted_iota(jnp.int32, (Q_BLOCK, Q_BLOCK), 0)
    qq = lax.broadcasted_iota(jnp.int32, (Q_BLOCK, Q_BLOCK), 1)
    tri = jnp.concatenate([jnp.where(kk <= qq, 0.0, NEG)] * rep, axis=1)
    anti = jnp.concatenate([jnp.where(kk > qq, 0.0, NEG)] * rep, axis=1)
    n_back = WINDOW // Q_BLOCK
    rows_of = lambda ref, blk0, n: ref[0, blk0 * Q_BLOCK:(blk0 + n) * Q_BLOCK, :]
    tiles_of = lambda ref, blk0, n: jnp.concatenate([ref[blk0 + i] for i in range(n)], axis=1)
    if j >= n_back:
        far = j - n_back
        edge = (jnp.concatenate([rows_of(kw_ref, far, 1), rows_of(kw_ref, j, 1)], axis=0),
                lambda: jnp.concatenate([vwt_ref[far], vwt_ref[j]], axis=1), jnp.concatenate([anti, tri], axis=0))
    else:
        edge = (rows_of(kw_ref, j, 1), lambda: vwt_ref[j], tri)
    n_mid = min(n_back - 1, j)
    early = [edge]
    if n_mid:
        early.append((rows_of(kw_ref, j - n_mid, n_mid), lambda: tiles_of(vwt_ref, j - n_mid, n_mid), None))
    early.append((rows_of(ks_ref, j, 1), lambda: vst_ref[j], tri))
    early_scores = [_dot_nt(k, q4) for k, _, _ in early]
    q_per_var = var_len // Q_BLOCK
    first_parts = q_per_var if j >= q_per_var else j
    first_scores = _dot_nt(ks_ref[0, 0:first_parts * Q_BLOCK, :], q4) if first_parts else None
    v, h = divmod(j, q_per_var)
    later = [(ci, q_per_var) for ci in range(1, v)] + ([(v, h)] if v > 0 and h > 0 else [])
    later_scores = [_dot_nt(ks_ref[0, ci * var_len:ci * var_len + n_q * Q_BLOCK, :], q4) for ci, n_q in later]
    early_piece = lambda i: _softmax_piece(early_scores[i], early[i][1](), HEAD_DIM, early[i][2])

    kc = kc_ref[0, 0]
    n_rows = kc.shape[0]
    n_idx = lax.broadcasted_iota(jnp.int32, (n_rows, Q_BLOCK), 0)
    c_mask = (n_idx * CMP_STRIDE + (CMP_LEN - 1)) <= t_row
    st = _dot_nt(kc, q4)
    yield
    win_pieces = [early_piece(0)]
    p_sum = jnp.zeros((n_rows, Q_BLOCK), F32)
    ps = []
    for r in range(rep):
        t = jnp.where(c_mask, head(st, r), NEG)
        e = jnp.where(c_mask, jnp.exp2(t - jnp.max(t, axis=0, keepdims=True)), 0.0)
        l = jnp.sum(e, axis=0, keepdims=True)
        p = e / jnp.where(l > 0.0, l, 1.0)
        p_sum = p_sum + p
        ps.append(p.astype(BF16))
    o_cmp = _dot(vc_ref[0, 0].astype(F32).T.astype(BF16), jnp.concatenate(ps, axis=1))

    hi = p_sum.astype(BF16)
    lo = (p_sum - hi.astype(F32)).astype(BF16)
    ovl = ovl_ref[...]
    n_pad = -(-n_slc // 8) * 8
    imp = (_dot(ovl, hi) + _dot(ovl, lo))[0:n_pad]
    yield
    win_pieces += [early_piece(i) for i in range(1, len(early) - 1)]
    slc_diag = early_piece(len(early) - 1)
    blk = lax.broadcasted_iota(jnp.int32, (n_pad, Q_BLOCK), 0)
    cur = t_row // SLC_LEN
    forced = (blk == 0) | (blk == cur) | (blk == cur - 1)
    blk_causal = blk * SLC_LEN <= t_row
    imp = jnp.where(forced, 1e4, imp)
    imp = jnp.where(blk_causal, imp, -1.0)
    rank = jnp.zeros((n_pad, Q_BLOCK), F32)
    for sp in range(n_slc):
        row = imp[sp:sp + 1, :]
        ge = jnp.where(row >= imp, 1.0, 0.0)
        gt = jnp.where(row > imp, 1.0, 0.0)
        rank = rank + jnp.where(blk > sp, ge, gt)
    blocks_per_q = Q_BLOCK // SLC_LEN
    blk_bias = jnp.where((rank < float(n_sel)) & (blk < j * blocks_per_q), 0.0, NEG)
    blk_bias = jnp.concatenate([blk_bias] * rep, axis=1)
    yield

    gt = jax.nn.sigmoid(gates).T
    gate = lambda branch: jnp.concatenate([jnp.broadcast_to(gt[3 * r + branch:3 * r + branch + 1], (HEAD_DIM, Q_BLOCK))
                                           for r in range(rep)], axis=1)
    partial = gate(0) * o_cmp + gate(2) * _merge_pieces(win_pieces)
    blk_per_var = var_len // SLC_LEN
    blk_per_q = Q_BLOCK // SLC_LEN

    def chunk_piece(ci, n_q, st):
        b0 = ci * blk_per_var
        return _softmax_piece(st, tiles_of(vst_ref, ci * q_per_var, n_q), HEAD_DIM,
                              None, blk_bias[b0:b0 + n_q * blk_per_q])

    pieces = [slc_diag] + ([chunk_piece(0, first_parts, first_scores)] if first_parts else [])
    return dict(partial=partial, g_slc=gate(1), pieces=pieces, later=list(zip(later, later_scores)),
                chunk_piece=chunk_piece)


def _nsa_selected(ctx):
    pieces = list(ctx["pieces"])
    for (ci, n_q), st in ctx["later"]:
        pieces.append(ctx["chunk_piece"](ci, n_q, st))
        yield
    return ctx["partial"] + ctx["g_slc"] * _merge_pieces(pieces)


def _nsa_kernel(q_ref, ks_ref, vs_ref, kw_ref, vw_ref, kc_ref, vc_ref, gate_ref, ovl_ref,
                o_ref, vst_ref, vwt_ref, *, seq, rep, n_slc, n_sel, var_len, blocks):
    jj = pl.program_id(2)

    @pl.when(jj == 0)
    def _():
        for i in range(seq // Q_BLOCK):
            vst_ref[i] = _transposed_values(vs_ref[0, i * Q_BLOCK:(i + 1) * Q_BLOCK, :], ONES_ROWS)
            vwt_ref[i] = _transposed_values(vw_ref[0, i * Q_BLOCK:(i + 1) * Q_BLOCK, :], ONES_ROWS)

    rows = lambda h: slice(h * Q_BLOCK, (h + 1) * Q_BLOCK)
    assert blocks * Q_BLOCK == var_len
    for v in range(seq // var_len):
        @pl.when(jj == v)
        def _(v=v):
            ctxs = _interleave([
                _nsa_block(v * blocks + h, q_ref[0, rows(h), :], gate_ref[0, rows(h), :], ks_ref, kw_ref, kc_ref,
                           vc_ref, ovl_ref, vst_ref, vwt_ref, rep=rep, n_slc=n_slc, n_sel=n_sel, var_len=var_len)
                for h in range(blocks)])
            outs = _interleave([_nsa_selected(ctx) for ctx in ctxs])
            for h, out in enumerate(outs):
                for r in range(rep):
                    o_ref[0, rows(h), r * HEAD_DIM:(r + 1) * HEAD_DIM] = (
                        out[:, r * Q_BLOCK:(r + 1) * Q_BLOCK].T.astype(BF16))


def _block_overlap(n_rows, n_cmp, n_slc):
    c0 = np.arange(n_cmp) * CMP_STRIDE
    s0 = np.arange(n_slc) * SLC_LEN
    lo = np.maximum(c0[None, :], s0[:, None])
    hi = np.minimum(c0[None, :] + CMP_LEN, s0[:, None] + SLC_LEN)
    out = np.zeros((LANES, n_rows), np.float32)
    out[:n_slc, :n_cmp] = np.clip(hi - lo, 0, None) / CMP_LEN
    return out


def _nsa_attention(proj, gates, kc, vc, off, batch, seq, rep):
    g = NSA_KV_HEADS
    n_qb = seq // Q_BLOCK
    n_slc = seq // SLC_LEN
    n_sel = min(SLC_TOPK, n_slc)
    n_rows = kc.shape[2]
    n_cmp = (seq - CMP_LEN) // CMP_STRIDE + 1
    qw = rep * HEAD_DIM
    var_len = SLC_VARIANT_LEN if seq % SLC_VARIANT_LEN == 0 else seq
    blocks = var_len // Q_BLOCK
    ovl = jnp.asarray(_block_overlap(n_rows, n_cmp, n_slc), BF16)

    def kv_spec(name):
        base = off[name] // HEAD_DIM
        return pl.BlockSpec((1, seq, HEAD_DIM), lambda b, h, j: (b, 0, base + h))

    cmp_spec = pl.BlockSpec((1, 1, n_rows, HEAD_DIM), lambda b, h, j: (b, h, 0, 0))
    q_base = off["q_n"] // qw
    return pl.pallas_call(
        functools.partial(_nsa_kernel, seq=seq, rep=rep, n_slc=n_slc, n_sel=n_sel, var_len=var_len, blocks=blocks),
        grid=(batch, g, n_qb // blocks),
        scratch_shapes=[
            pltpu.VMEM((seq // Q_BLOCK, HEAD_DIM + ONES_ROWS, Q_BLOCK), BF16),
            pltpu.VMEM((seq // Q_BLOCK, HEAD_DIM + ONES_ROWS, Q_BLOCK), BF16),
        ],
        in_specs=[
            pl.BlockSpec((1, blocks * Q_BLOCK, qw), lambda b, h, j: (b, j, q_base + h)),
            kv_spec("ks"), kv_spec("vs"), kv_spec("kw"), kv_spec("vw"),
            cmp_spec, cmp_spec,
            pl.BlockSpec((1, blocks * Q_BLOCK, LANES), lambda b, h, j: (b, j, h)),
            pl.BlockSpec((LANES, n_rows), lambda b, h, j: (0, 0)),
        ],
        out_specs=pl.BlockSpec((1, blocks * Q_BLOCK, qw), lambda b, h, j: (b, j, h)),
        out_shape=jax.ShapeDtypeStruct((batch, seq, g * qw), BF16),
        compiler_params=_params("parallel", "parallel", "arbitrary"),
        name="nsa_attention",
    )(proj, proj, proj, proj, proj, kc, vc, gates, ovl)


def _diff_block(v, q, k_ref, vts, tri, lam, tq):
    qs = [q[:, c * HEAD_DIM:(c + 1) * HEAD_DIM] for c in range(2)]
    half = tq // 2
    chunks = [(v * tq, half, 0), (v * tq + half, half, half)] + ([(0, v * tq, 0)] if v else [])
    specs = [(c, ch) for ch in chunks for c in range(2)]
    lanes = lambda c: slice(c * HEAD_DIM, (c + 1) * HEAD_DIM)
    scores = lambda c, ch: _dot_nt(k_ref[0, ch[0]:ch[0] + ch[1], lanes(c)], qs[c][ch[2]:, :])
    pieces = ([], [])
    sts = [scores(*spec) for spec in specs]
    yield
    for i, (c, (k0, nk, q0)) in enumerate(specs):
        diagonal = k0 >= v * tq
        vt = (vts[v][:, k0 - v * tq:k0 - v * tq + nk] if diagonal
              else jnp.concatenate([vts[kb] for kb in range(v)], axis=1))
        bias = None if not diagonal else (tri if q0 else jnp.concatenate([tri, jnp.zeros_like(tri)], axis=1))
        m, l, o = _softmax_piece(sts[i], vt, 2 * HEAD_DIM, bias)
        if q0:
            m = jnp.concatenate([jnp.full((1, q0), MASK_FLOOR, F32), m], axis=1)
            l = jnp.concatenate([jnp.zeros((1, q0), F32), l], axis=1)
            o = jnp.concatenate([jnp.zeros((o.shape[0], q0), F32), o], axis=1)
        pieces[c].append((m, l, o))
        yield
    return _merge_pieces(pieces[0]) - lam * _merge_pieces(pieces[1])


def _diff_kernel(q_ref, k_ref, v_ref, lam_ref, sg_ref, o_ref, *, seq, tq, lambda_init):
    n_blocks = seq // tq
    rows = lambda i: slice(i * tq, (i + 1) * tq)
    vts = [_transposed_values(v_ref[0, rows(i), :], 0) for i in range(n_blocks)]
    lv = lam_ref[...]
    lam = (jnp.exp(jnp.sum(lv[0:1] * lv[1:2], axis=-1, keepdims=True))
           - jnp.exp(jnp.sum(lv[2:3] * lv[3:4], axis=-1, keepdims=True)) + lambda_init)
    half = tq // 2
    tri = jnp.where(lax.broadcasted_iota(jnp.int32, (half, half), 0) <= lax.broadcasted_iota(jnp.int32, (half, half), 1),
                    0.0, NEG)
    outs = _interleave([_diff_block(v, q_ref[0, rows(v), :], k_ref, vts, tri, lam, tq) for v in range(n_blocks)])
    for v, o in enumerate(outs):
        o_ref[0, rows(v), :] = (_rms(o.T, sg_ref[...]) * (1.0 - lambda_init)).astype(BF16)


def _diff_attention(proj, lam, subln, off, batch, seq, heads, lambda_init):
    vw = 2 * HEAD_DIM
    tq = _row_tile(seq, DIFF_Q_TILE)
    qb, kb, vb = off["q_d"] // vw, off["k_d"] // vw, off["v_d"] // vw
    return pl.pallas_call(
        functools.partial(_diff_kernel, seq=seq, tq=tq, lambda_init=lambda_init),
        grid=(batch, heads),
        in_specs=[
            pl.BlockSpec((1, seq, vw), lambda b, h: (b, 0, qb + h)),
            pl.BlockSpec((1, seq, vw), lambda b, h: (b, 0, kb + h)),
            pl.BlockSpec((1, seq, vw), lambda b, h: (b, 0, vb + h)),
            pl.BlockSpec((4, HEAD_DIM), lambda b, h: (0, 0)),
            pl.BlockSpec((1, vw), lambda b, h: (0, 0)),
        ],
        out_specs=pl.BlockSpec((1, seq, vw), lambda b, h: (b, 0, h)),
        out_shape=jax.ShapeDtypeStruct((batch, seq, heads * vw), BF16),
        compiler_params=_params("parallel", "parallel"),
        name="diff_attention",
    )(proj, proj, proj, lam, subln)


def _out_kernel(x_ref, a_ref, b_ref, wa_ref, wb_ref, o_ref):
    o_ref[...] = x_ref[...] + _dot(a_ref[...], wa_ref[...]) + _dot(b_ref[...], wb_ref[...])


def _out_project(x, a, b, wa, wb):
    n, d = x.shape
    ka, kb = a.shape[1], b.shape[1]
    tm = _row_tile(n, OUT_ROW_TILE)
    tn = LANES * _largest_divisor(d // LANES, OUT_GROUPS_PER_TILE)
    return pl.pallas_call(
        _out_kernel,
        grid=(n // tm, d // tn),
        in_specs=[
            pl.BlockSpec((tm, tn), lambda i, j: (i, j)),
            pl.BlockSpec((tm, ka), lambda i, j: (i, 0)),
            pl.BlockSpec((tm, kb), lambda i, j: (i, 0)),
            pl.BlockSpec((ka, tn), lambda i, j: (0, j)),
            pl.BlockSpec((kb, tn), lambda i, j: (0, j)),
        ],
        out_specs=pl.BlockSpec((tm, tn), lambda i, j: (i, j)),
        out_shape=jax.ShapeDtypeStruct((n, d), F32),
        compiler_params=_params("parallel", "arbitrary"),
        name="out_proj",
    )(x, a, b, wa, wb)


def _rope_tables(t):
    inv = 1.0 / (ROPE_THETA ** (jnp.arange(0, HEAD_DIM, 2, dtype=F32) / HEAD_DIM))
    ang = jnp.arange(t, dtype=F32)[:, None] * inv[None, :]
    ang = jnp.concatenate([ang, ang], axis=-1)
    sign = jnp.concatenate([-jnp.ones((HEAD_DIM // 2,), F32), jnp.ones((HEAD_DIM // 2,), F32)])
    return jnp.cos(ang), jnp.sin(ang) * sign[None, :]


def _layout(d_model):
    nsa_heads = d_model // (2 * HEAD_DIM)
    diff_heads = d_model // (4 * HEAD_DIM)
    kv = NSA_KV_HEADS * HEAD_DIM
    sizes = dict(q_n=nsa_heads * HEAD_DIM, kc=kv, vc=kv, ks=kv, vs=kv, kw=kv, vw=kv, gates=3 * nsa_heads,
                 q_d=2 * diff_heads * HEAD_DIM, k_d=2 * diff_heads * HEAD_DIM, v_d=diff_heads * 2 * HEAD_DIM)
    orig, off, o, p = {}, {}, 0, 0
    for name in ("q_n", "kc", "vc", "ks", "vs", "kw", "vw", "gates", "q_d", "k_d", "v_d"):
        orig[name] = o
        o += sizes[name]
        if name != "gates":
            off[name] = p
            p += sizes[name]
    groups = lambda names: tuple((off[nm] // LANES, (off[nm] + sizes[nm]) // LANES) for nm in names)
    return sizes, orig, off, groups(("q_n", "kc", "ks", "kw", "q_d", "k_d")), groups(("q_n", "q_d")), nsa_heads, diff_heads


def kernel(x, ffn1_norm, ffn1_w_gate, ffn1_w_up, ffn1_w_down, mix_norm, w_in, cmp_pos_k, cmp_pos_v, cmp_wk1, cmp_wk2, cmp_wv1, cmp_wv2, lam_q1, lam_k1, lam_q2, lam_k2, diff_subln, w_out, ffn2_norm, ffn2_w_gate, ffn2_w_up, ffn2_w_down, final_norm):
    batch, seq, d_model = x.shape
    depth = ffn1_norm.shape[0]
    n = batch * seq
    sizes, orig, off, rope_groups, query_groups, nsa_heads, diff_heads = _layout(d_model)
    g = NSA_KV_HEADS
    rep = nsa_heads // g
    cos, sin_signed = _rope_tables(seq)
    fg = final_norm.reshape(1, d_model)
    ffn1_stacked = (ffn1_w_gate, ffn1_w_up, ffn1_w_down)
    ffn2_stacked = (ffn2_w_gate, ffn2_w_up, ffn2_w_down)
    w_ffn = tuple(w[0].astype(BF16) for w in ffn1_stacked)

    xf = x.reshape(n, d_model)
    for l in range(depth):
        lambda_init = 0.8 - 0.6 * math.exp(-0.3 * l)
        xf, w_ffn = _ffn(xf, ffn1_norm[l].reshape(1, d_model), *w_ffn, fg, False, (ffn2_stacked, l))

        wl = w_in[l]
        gates_end = orig["gates"] + sizes["gates"]
        w_main = jnp.concatenate([wl[:, :orig["gates"]], wl[:, gates_end:]], axis=1).astype(BF16)
        gate_tiles = []
        for h in range(g):
            cols = wl[:, orig["gates"] + h * 3 * rep: orig["gates"] + (h + 1) * 3 * rep]
            gate_tiles.append(jnp.pad(cols, ((0, 0), (0, LANES - 3 * rep))))
        w_gate = jnp.concatenate(gate_tiles, axis=1).astype(BF16)
        proj, gates = _project(xf, mix_norm[l].reshape(1, d_model), w_main, w_gate, cos, sin_signed, rope_groups,
                               query_groups)
        proj3 = proj.reshape(batch, seq, -1)
        gates3 = gates.reshape(batch, seq, -1)

        kc, vc = _compress(proj3, off, batch, seq,
                           cmp_pos_k[l].reshape(1, -1).astype(BF16), cmp_pos_v[l].reshape(1, -1).astype(BF16),
                           cmp_wk1[l].astype(BF16), cmp_wk2[l].astype(BF16),
                           cmp_wv1[l].astype(BF16), cmp_wv2[l].astype(BF16))
        o_nsa = _nsa_attention(proj3, gates3, kc, vc, off, batch, seq, rep)
        lam = jnp.stack([lam_q1[l], lam_k1[l], lam_q2[l], lam_k2[l]])
        o_diff = _diff_attention(proj3, lam, diff_subln[l].reshape(1, -1), off, batch, seq, diff_heads, lambda_init)

        half = sizes["q_n"]
        wo = w_out[l].astype(BF16)
        xf = _out_project(xf, o_nsa.reshape(n, -1), o_diff.reshape(n, -1), wo[:half], wo[half:])

        last = l == depth - 1
        xf, w_ffn = _ffn(xf, ffn2_norm[l].reshape(1, d_model), *w_ffn, fg, last,
                         None if last else (ffn1_stacked, l + 1))
    return xf.reshape(batch, seq, d_model)
```

```python
import functools
import math

import numpy as np
import jax
import jax.numpy as jnp
from jax import lax
from jax.experimental import pallas as pl
from jax.experimental.pallas import tpu as pltpu

HEAD_DIM = 128
NSA_KV_HEADS = 2
CMP_LEN = 32
CMP_STRIDE = 16
SLC_LEN = 64
SLC_TOPK = 16
WINDOW = 512
Q_BLOCK = 128
ROPE_THETA = 10000.0
EPS = 1e-6
NEG = -1e30
MASK_FLOOR = -1e20
LOG2E = 1.4426950408889634
QK_SCALE = HEAD_DIM ** -0.5 * LOG2E
SLC_VARIANT_LEN = 512
LANES = 128
VMEM_LIMIT = 58 * 1024 * 1024
FFN_ROW_TILE = 1024
FFN_HIDDEN_GROUPS = 4
PROJ_ROW_TILE = 512
PROJ_GROUPS_PER_TILE = 22
OUT_ROW_TILE = 512
OUT_GROUPS_PER_TILE = 16
DIFF_Q_TILE = 512

F32 = jnp.float32
BF16 = jnp.bfloat16


def _largest_divisor(n, cap):
    for d in range(min(n, cap), 0, -1):
        if n % d == 0:
            return d
    return 1


def _row_tile(n, cap):
    for d in range(min(n, cap), 7, -1):
        if n % d == 0 and d % 8 == 0:
            return d
    return n


def _rms(x, g):
    return x * lax.rsqrt(jnp.mean(x * x, axis=-1, keepdims=True) + EPS) * g


def _dot(a, b):
    return jnp.dot(a, b, preferred_element_type=F32)


def _dot_nt(a, b):
    return lax.dot_general(a, b, (((1,), (1,)), ((), ())), preferred_element_type=F32)


def _params(*sem):
    return pltpu.CompilerParams(dimension_semantics=sem, vmem_limit_bytes=VMEM_LIMIT)


def _ffn_kernel(*refs, final_norm, cast_next):
    if cast_next:
        (x_ref, g_ref, wg_ref, wu_ref, wd_ref, fg_ref, ng_ref, nu_ref, nd_ref,
         o_ref, og_ref, ou_ref, od_ref, h_ref) = refs
        og_ref[...] = ng_ref[...].astype(BF16)
        ou_ref[...] = nu_ref[...].astype(BF16)
        od_ref[...] = nd_ref[...].astype(BF16)
    else:
        x_ref, g_ref, wg_ref, wu_ref, wd_ref, fg_ref, o_ref, h_ref = refs
    j = pl.program_id(1)

    def half_step_tile():
        h = h_ref[...]
        a = _dot(h, wg_ref[...])
        b = _dot(h, wu_ref[...])
        act = (a * jax.nn.sigmoid(a) * b * 0.5).astype(BF16)
        return _dot(act, wd_ref[...])

    @pl.when(j == 0)
    def _():
        h_ref[...] = _rms(x_ref[...], g_ref[...]).astype(BF16)
        o_ref[...] = x_ref[...] + half_step_tile()

    @pl.when(j > 0)
    def _():
        o_ref[...] += half_step_tile()

    if final_norm:
        @pl.when(j == pl.num_programs(1) - 1)
        def _():
            o_ref[...] = _rms(o_ref[...], fg_ref[...])


def _ffn(x, g, wg, wu, wd, fg, final_norm, next_weights=None):
    n, d = x.shape
    f = wg.shape[1]
    tm = _row_tile(n, FFN_ROW_TILE)
    tf = LANES * _largest_divisor(f // LANES, FFN_HIDDEN_GROUPS)
    n_i, n_j = n // tm, f // tf
    in_specs = [
        pl.BlockSpec((tm, d), lambda i, j: (i, 0)),
        pl.BlockSpec((1, d), lambda i, j: (0, 0)),
        pl.BlockSpec((d, tf), lambda i, j: (0, j)),
        pl.BlockSpec((d, tf), lambda i, j: (0, j)),
        pl.BlockSpec((tf, d), lambda i, j: (j, 0)),
        pl.BlockSpec((1, d), lambda i, j: (0, 0)),
    ]
    out_specs = [pl.BlockSpec((tm, d), lambda i, j: (i, 0))]
    out_shape = [jax.ShapeDtypeStruct((n, d), F32)]
    args = [x, g, wg, wu, wd, fg]
    if next_weights is not None:
        stacked, layer = next_weights
        dr, fr = d // n_i, tf // n_i
        assert dr * n_i == d and fr * n_i == tf and dr % 16 == 0 and fr % 16 == 0
        assert all(w.shape[1:] == s for w, s in zip(stacked, ((d, f), (d, f), (f, d))))
        in_specs += [
            pl.BlockSpec((None, dr, tf), lambda i, j: (layer, i, j)),
            pl.BlockSpec((None, dr, tf), lambda i, j: (layer, i, j)),
            pl.BlockSpec((None, fr, d), lambda i, j: (layer, j * n_i + i, 0)),
        ]
        out_specs += [
            pl.BlockSpec((dr, tf), lambda i, j: (i, j)),
            pl.BlockSpec((dr, tf), lambda i, j: (i, j)),
            pl.BlockSpec((fr, d), lambda i, j: (j * n_i + i, 0)),
        ]
        out_shape += [jax.ShapeDtypeStruct(w.shape[1:], BF16) for w in stacked]
        args += list(stacked)
    outs = pl.pallas_call(
        functools.partial(_ffn_kernel, final_norm=final_norm, cast_next=next_weights is not None),
        grid=(n_i, n_j),
        in_specs=in_specs,
        out_specs=out_specs,
        out_shape=out_shape,
        scratch_shapes=[pltpu.VMEM((tm, d), BF16)],
        compiler_params=_params("parallel", "arbitrary"),
        name="ffn",
    )(*args)
    return outs[0], tuple(outs[1:])


def _proj_kernel(x_ref, g_ref, w_ref, wgate_ref, cos_ref, sin_ref, o_ref, gate_ref, h_ref,
                 *, n_tiles, groups_per_tile, rope_groups, query_groups):
    j = pl.program_id(1)

    def column_tile(tile):
        acc = _dot(h_ref[...], w_ref[...])
        cos = cos_ref[...]
        sin = sin_ref[...]
        for gi in range(groups_per_tile):
            group = tile * groups_per_tile + gi
            in_any = lambda spans: any(lo <= group < hi for lo, hi in spans)
            y = acc[:, gi * LANES:(gi + 1) * LANES]
            if in_any(rope_groups):
                y = y * cos + pltpu.roll(y, HEAD_DIM // 2, axis=1) * sin
            if in_any(query_groups):
                y = y * QK_SCALE
            o_ref[:, gi * LANES:(gi + 1) * LANES] = y.astype(BF16)

    for tile in range(n_tiles):
        @pl.when(j == tile)
        def _(tile=tile):
            if tile == 0:
                h = _rms(x_ref[...], g_ref[...]).astype(BF16)
                h_ref[...] = h
                gate_ref[...] = _dot(h, wgate_ref[...])
            column_tile(tile)


def _project(x, g, w_main, w_gate, cos, sin_signed, rope_groups, query_groups):
    n, d = x.shape
    t = cos.shape[0]
    width = w_main.shape[1]
    gw = w_gate.shape[1]
    n_groups = width // LANES
    gpt = _largest_divisor(n_groups, PROJ_GROUPS_PER_TILE)
    tn = gpt * LANES
    tm = _row_tile(t, PROJ_ROW_TILE)
    t_blocks = t // tm
    return pl.pallas_call(
        functools.partial(_proj_kernel, n_tiles=width // tn, groups_per_tile=gpt, rope_groups=rope_groups,
                          query_groups=query_groups),
        grid=(n // tm, width // tn),
        in_specs=[
            pl.BlockSpec((tm, d), lambda i, j: (i, 0)),
            pl.BlockSpec((1, d), lambda i, j: (0, 0)),
            pl.BlockSpec((d, tn), lambda i, j: (0, j)),
            pl.BlockSpec((d, gw), lambda i, j: (0, 0)),
            pl.BlockSpec((tm, HEAD_DIM), lambda i, j: (i % t_blocks, 0)),
            pl.BlockSpec((tm, HEAD_DIM), lambda i, j: (i % t_blocks, 0)),
        ],
        out_specs=[
            pl.BlockSpec((tm, tn), lambda i, j: (i, j)),
            pl.BlockSpec((tm, gw), lambda i, j: (i, 0)),
        ],
        out_shape=[
            jax.ShapeDtypeStruct((n, width), BF16),
            jax.ShapeDtypeStruct((n, gw), F32),
        ],
        scratch_shapes=[pltpu.VMEM((tm, d), BF16)],
        compiler_params=_params("parallel", "arbitrary"),
        name="in_proj",
    )(x, g, w_main, w_gate, cos, sin_signed)


def _cmp_kernel(k_ref, v_ref, pk_ref, pv_ref, wk1_ref, wk2_ref, wv1_ref, wv2_ref, ok_ref, ov_ref, x_ref):
    seq = k_ref.shape[1]
    n_chunks = seq // CMP_STRIDE

    def one(kv_ref, p_ref, w1_ref, w2_ref, o_ref):
        x_ref[...] = kv_ref[0].astype(F32)
        a = jnp.zeros((n_chunks, w1_ref.shape[1]), F32)
        b = jnp.zeros((n_chunks, w1_ref.shape[1]), F32)
        for l in range(0, CMP_STRIDE, 2):
            rows = jnp.concatenate([x_ref[pl.ds(l + t, n_chunks, stride=CMP_STRIDE), :].astype(BF16) for t in range(2)],
                                   axis=1)
            a = a + _dot(rows, w1_ref[l * HEAD_DIM:(l + 2) * HEAD_DIM, :])
            b = b + _dot(rows, w1_ref[(CMP_STRIDE + l) * HEAD_DIM:(CMP_STRIDE + l + 2) * HEAD_DIM, :])
        p = jnp.broadcast_to(p_ref[...], (8, CMP_LEN * HEAD_DIM))
        bias = _dot(p, w1_ref[...])[0:1]
        pre = a + pltpu.roll(b, n_chunks - 1, axis=0) + bias
        hid = (pre * jax.nn.sigmoid(pre)).astype(BF16)
        o_ref[0, 0] = _dot(hid, w2_ref[...]).astype(BF16)

    one(k_ref, pk_ref, wk1_ref, wk2_ref, ok_ref)
    one(v_ref, pv_ref, wv1_ref, wv2_ref, ov_ref)


def _compress(proj, off, batch, seq, pk, pv, wk1, wk2, wv1, wv2):
    g = NSA_KV_HEADS
    nc = seq // CMP_STRIDE
    hid = wk1.shape[1]
    dk = wk2.shape[1]
    assert CMP_LEN == 2 * CMP_STRIDE and dk == HEAD_DIM
    kb, vb = off["kc"] // HEAD_DIM, off["vc"] // HEAD_DIM
    full = lambda shape: pl.BlockSpec(shape, lambda i, j: tuple(0 for _ in shape))
    out_spec = pl.BlockSpec((1, 1, nc, dk), lambda i, j: (i, j, 0, 0))
    return pl.pallas_call(
        _cmp_kernel,
        grid=(batch, g),
        in_specs=[pl.BlockSpec((1, seq, HEAD_DIM), lambda i, j: (i, 0, kb + j)),
                  pl.BlockSpec((1, seq, HEAD_DIM), lambda i, j: (i, 0, vb + j)),
                  full((1, CMP_LEN * dk)), full((1, CMP_LEN * dk)),
                  full((CMP_LEN * dk, hid)), full((hid, dk)), full((CMP_LEN * dk, hid)), full((hid, dk))],
        out_specs=[out_spec, out_spec],
        out_shape=[jax.ShapeDtypeStruct((batch, g, nc, dk), BF16)] * 2,
        scratch_shapes=[pltpu.VMEM((seq, HEAD_DIM), F32)],
        compiler_params=_params("parallel", "parallel"),
        name="nsa_compress",
    )(proj, proj, pk, pv, wk1, wk2, wv1, wv2)


ONES_ROWS = 16


def _transposed_values(v, ones_rows):
    vt = v.astype(F32).T
    if ones_rows:
        vt = jnp.concatenate([vt, jnp.ones((ones_rows, vt.shape[1]), F32)], axis=0)
    return vt.astype(BF16)


def _softmax_piece(st, vt, dv, elem_bias=None, blk_bias=None):
    nk, nc = st.shape
    if elem_bias is not None:
        st = st + elem_bias
    nb = 1 if blk_bias is None else blk_bias.shape[0]
    s4 = st.reshape(nb, nk // nb // 8, 8, nc)
    bm = jnp.max(s4, axis=1)
    if blk_bias is not None:
        bm = bm + blk_bias[:, None, :]
    m = jnp.max(jnp.max(bm, axis=0), axis=0, keepdims=True)
    m = jnp.maximum(m, MASK_FLOOR)
    shift = -m if blk_bias is None else blk_bias - m
    e = jnp.exp2(s4 + shift[:, None, None, :])
    o = _dot(vt, e.reshape(nk, nc).astype(BF16))
    if vt.shape[0] > dv:
        return m, o[dv:dv + 1], o[0:dv]
    return m, jnp.sum(jnp.sum(jnp.sum(e, axis=0), axis=0), axis=0, keepdims=True), o


def _merge_pieces(pieces):
    m = functools.reduce(jnp.maximum, [p[0] for p in pieces])
    ws = [jnp.exp2(p[0] - m) for p in pieces]
    l = sum(w * p[1] for w, p in zip(ws, pieces))
    o = sum(w * p[2] for w, p in zip(ws, pieces))
    return o / l


def _interleave(gens):
    results = [None] * len(gens)
    live = list(range(len(gens)))
    while live:
        for i in list(live):
            try:
                next(gens[i])
            except StopIteration as stop:
                results[i] = stop.value
                live.remove(i)
    return results


def _nsa_block(j, q, gates, ks_ref, kw_ref, kc_ref, vc_ref, ovl_ref, vst_ref, vwt_ref,
               *, rep, n_slc, n_sel, var_len):
    s0 = j * Q_BLOCK
    q4 = jnp.concatenate([q[:, r * HEAD_DIM:(r + 1) * HEAD_DIM] for r in range(rep)], axis=0)
    t_row = s0 + lax.broadcasted_iota(jnp.int32, (1, Q_BLOCK), 1)
    head = lambda a, r: a[:, r * Q_BLOCK:(r + 1) * Q_BLOCK]

    kk = lax.broadcasted_iota(jnp.int32, (Q_BLOCK, Q_BLOCK), 0)
    qq = lax.broadcasted_iota(jnp.int32, (Q_BLOCK, Q_BLOCK), 1)
    tri = jnp.concatenate([jnp.where(kk <= qq, 0.0, NEG)] * rep, axis=1)
    anti = jnp.concatenate([jnp.where(kk > qq, 0.0, NEG)] * rep, axis=1)
    n_back = WINDOW // Q_BLOCK
    rows_of = lambda ref, blk0, n: ref[0, blk0 * Q_BLOCK:(blk0 + n) * Q_BLOCK, :]
    tiles_of = lambda ref, blk0, n: jnp.concatenate([ref[blk0 + i] for i in range(n)], axis=1)
    if j >= n_back:
        far = j - n_back
        edge = (jnp.concatenate([rows_of(kw_ref, far, 1), rows_of(kw_ref, j, 1)], axis=0),
                lambda: jnp.concatenate([vwt_ref[far], vwt_ref[j]], axis=1), jnp.concatenate([anti, tri], axis=0))
    else:
        edge = (rows_of(kw_ref, j, 1), lambda: vwt_ref[j], tri)
    n_mid = min(n_back - 1, j)
    early = [edge]
    if n_mid:
        early.append((rows_of(kw_ref, j - n_mid, n_mid), lambda: tiles_of(vwt_ref, j - n_mid, n_mid), None))
    early.append((rows_of(ks_ref, j, 1), lambda: vst_ref[j], tri))
    early_scores = [_dot_nt(k, q4) for k, _, _ in early]
    q_per_var = var_len // Q_BLOCK
    first_parts = q_per_var if j >= q_per_var else j
    first_scores = _dot_nt(ks_ref[0, 0:first_parts * Q_BLOCK, :], q4) if first_parts else None
    v, h = divmod(j, q_per_var)
    later = [(ci, q_per_var) for ci in range(1, v)] + ([(v, h)] if v > 0 and h > 0 else [])
    later_scores = [_dot_nt(ks_ref[0, ci * var_len:ci * var_len + n_q * Q_BLOCK, :], q4) for ci, n_q in later]
    early_piece = lambda i: _softmax_piece(early_scores[i], early[i][1](), HEAD_DIM, early[i][2])

    kc = kc_ref[0, 0]
    n_rows = kc.shape[0]
    n_idx = lax.broadcasted_iota(jnp.int32, (n_rows, Q_BLOCK), 0)
    c_mask = (n_idx * CMP_STRIDE + (CMP_LEN - 1)) <= t_row
    st = _dot_nt(kc, q4)
    yield
    win_pieces = [early_piece(0)]
    p_sum = jnp.zeros((n_rows, Q_BLOCK), F32)
    ps = []
    for r in range(rep):
        t = jnp.where(c_mask, head(st, r), NEG)
        e = jnp.where(c_mask, jnp.exp2(t - jnp.max(t, axis=0, keepdims=True)), 0.0)
        l = jnp.sum(e, axis=0, keepdims=True)
        p = e / jnp.where(l > 0.0, l, 1.0)
        p_sum = p_sum + p
        ps.append(p.astype(BF16))
    o_cmp = _dot(vc_ref[0, 0].astype(F32).T.astype(BF16), jnp.concatenate(ps, axis=1))

    hi = p_sum.astype(BF16)
    lo = (p_sum - hi.astype(F32)).astype(BF16)
    ovl = ovl_ref[...]
    n_pad = -(-n_slc // 8) * 8
    imp = (_dot(ovl, hi) + _dot(ovl, lo))[0:n_pad]
    yield
    win_pieces += [early_piece(i) for i in range(1, len(early) - 1)]
    slc_diag = early_piece(len(early) - 1)
    blk = lax.broadcasted_iota(jnp.int32, (n_pad, Q_BLOCK), 0)
    cur = t_row // SLC_LEN
    forced = (blk == 0) | (blk == cur) | (blk == cur - 1)
    blk_causal = blk * SLC_LEN <= t_row
    imp = jnp.where(forced, 1e4, imp)
    imp = jnp.where(blk_causal, imp, -1.0)
    rank = jnp.zeros((n_pad, Q_BLOCK), F32)
    for sp in range(n_slc):
        row = imp[sp:sp + 1, :]
        ge = jnp.where(row >= imp, 1.0, 0.0)
        gt = jnp.where(row > imp, 1.0, 0.0)
        rank = rank + jnp.where(blk > sp, ge, gt)
    blocks_per_q = Q_BLOCK // SLC_LEN
    blk_bias = jnp.where((rank < float(n_sel)) & (blk < j * blocks_per_q), 0.0, NEG)
    blk_bias = jnp.concatenate([blk_bias] * rep, axis=1)
    yield

    gt = jax.nn.sigmoid(gates).T
    gate = lambda branch: jnp.concatenate([jnp.broadcast_to(gt[3 * r + branch:3 * r + branch + 1], (HEAD_DIM, Q_BLOCK))
                                           for r in range(rep)], axis=1)
    partial = gate(0) * o_cmp + gate(2) * _merge_pieces(win_pieces)
    blk_per_var = var_len // SLC_LEN
    blk_per_q = Q_BLOCK // SLC_LEN

    def chunk_piece(ci, n_q, st):
        b0 = ci * blk_per_var
        return _softmax_piece(st, tiles_of(vst_ref, ci * q_per_var, n_q), HEAD_DIM,
                              None, blk_bias[b0:b0 + n_q * blk_per_q])

    pieces = [slc_diag] + ([chunk_piece(0, first_parts, first_scores)] if first_parts else [])
    return dict(partial=partial, g_slc=gate(1), pieces=pieces, later=list(zip(later, later_scores)),
                chunk_piece=chunk_piece)


def _nsa_selected(ctx):
    pieces = list(ctx["pieces"])
    for (ci, n_q), st in ctx["later"]:
        pieces.append(ctx["chunk_piece"](ci, n_q, st))
        yield
    return ctx["partial"] + ctx["g_slc"] * _merge_pieces(pieces)


def _nsa_kernel(q_ref, ks_ref, vs_ref, kw_ref, vw_ref, kc_ref, vc_ref, gate_ref, ovl_ref,
                o_ref, vst_ref, vwt_ref, *, seq, rep, n_slc, n_sel, var_len, blocks):
    jj = pl.program_id(2)

    @pl.when(jj == 0)
    def _():
        for i in range(seq // Q_BLOCK):
            vst_ref[i] = _transposed_values(vs_ref[0, i * Q_BLOCK:(i + 1) * Q_BLOCK, :], ONES_ROWS)
            vwt_ref[i] = _transposed_values(vw_ref[0, i * Q_BLOCK:(i + 1) * Q_BLOCK, :], ONES_ROWS)

    rows = lambda h: slice(h * Q_BLOCK, (h + 1) * Q_BLOCK)
    assert blocks * Q_BLOCK == var_len
    for v in range(seq // var_len):
        @pl.when(jj == v)
        def _(v=v):
            ctxs = _interleave([
                _nsa_block(v * blocks + h, q_ref[0, rows(h), :], gate_ref[0, rows(h), :], ks_ref, kw_ref, kc_ref,
                           vc_ref, ovl_ref, vst_ref, vwt_ref, rep=rep, n_slc=n_slc, n_sel=n_sel, var_len=var_len)
                for h in range(blocks)])
            outs = _interleave([_nsa_selected(ctx) for ctx in ctxs])
            for h, out in enumerate(outs):
                for r in range(rep):
                    o_ref[0, rows(h), r * HEAD_DIM:(r + 1) * HEAD_DIM] = (
                        out[:, r * Q_BLOCK:(r + 1) * Q_BLOCK].T.astype(BF16))


def _block_overlap(n_rows, n_cmp, n_slc):
    c0 = np.arange(n_cmp) * CMP_STRIDE
    s0 = np.arange(n_slc) * SLC_LEN
    lo = np.maximum(c0[None, :], s0[:, None])
    hi = np.minimum(c0[None, :] + CMP_LEN, s0[:, None] + SLC_LEN)
    out = np.zeros((LANES, n_rows), np.float32)
    out[:n_slc, :n_cmp] = np.clip(hi - lo, 0, None) / CMP_LEN
    return out


def _nsa_attention(proj, gates, kc, vc, off, batch, seq, rep):
    g = NSA_KV_HEADS
    n_qb = seq // Q_BLOCK
    n_slc = seq // SLC_LEN
    n_sel = min(SLC_TOPK, n_slc)
    n_rows = kc.shape[2]
    n_cmp = (seq - CMP_LEN) // CMP_STRIDE + 1
    qw = rep * HEAD_DIM
    var_len = SLC_VARIANT_LEN if seq % SLC_VARIANT_LEN == 0 else seq
    blocks = var_len // Q_BLOCK
    ovl = jnp.asarray(_block_overlap(n_rows, n_cmp, n_slc), BF16)

    def kv_spec(name):
        base = off[name] // HEAD_DIM
        return pl.BlockSpec((1, seq, HEAD_DIM), lambda b, h, j: (b, 0, base + h))

    cmp_spec = pl.BlockSpec((1, 1, n_rows, HEAD_DIM), lambda b, h, j: (b, h, 0, 0))
    q_base = off["q_n"] // qw
    return pl.pallas_call(
        functools.partial(_nsa_kernel, seq=seq, rep=rep, n_slc=n_slc, n_sel=n_sel, var_len=var_len, blocks=blocks),
        grid=(batch, g, n_qb // blocks),
        scratch_shapes=[
            pltpu.VMEM((seq // Q_BLOCK, HEAD_DIM + ONES_ROWS, Q_BLOCK), BF16),
            pltpu.VMEM((seq // Q_BLOCK, HEAD_DIM + ONES_ROWS, Q_BLOCK), BF16),
        ],
        in_specs=[
            pl.BlockSpec((1, blocks * Q_BLOCK, qw), lambda b, h, j: (b, j, q_base + h)),
            kv_spec("ks"), kv_spec("vs"), kv_spec("kw"), kv_spec("vw"),
            cmp_spec, cmp_spec,
            pl.BlockSpec((1, blocks * Q_BLOCK, LANES), lambda b, h, j: (b, j, h)),
            pl.BlockSpec((LANES, n_rows), lambda b, h, j: (0, 0)),
        ],
        out_specs=pl.BlockSpec((1, blocks * Q_BLOCK, qw), lambda b, h, j: (b, j, h)),
        out_shape=jax.ShapeDtypeStruct((batch, seq, g * qw), BF16),
        compiler_params=_params("parallel", "parallel", "arbitrary"),
        name="nsa_attention",
    )(proj, proj, proj, proj, proj, kc, vc, gates, ovl)


def _diff_block(v, q, k_ref, vts, tri, lam, tq):
    qs = [q[:, c * HEAD_DIM:(c + 1) * HEAD_DIM] for c in range(2)]
    half = tq // 2
    chunks = [(v * tq, half, 0), (v * tq + half, half, half)] + ([(0, v * tq, 0)] if v else [])
    specs = [(c, ch) for ch in chunks for c in range(2)]
    lanes = lambda c: slice(c * HEAD_DIM, (c + 1) * HEAD_DIM)
    scores = lambda c, ch: _dot_nt(k_ref[0, ch[0]:ch[0] + ch[1], lanes(c)], qs[c][ch[2]:, :])
    pieces = ([], [])
    sts = [scores(*spec) for spec in specs]
    yield
    for i, (c, (k0, nk, q0)) in enumerate(specs):
        diagonal = k0 >= v * tq
        vt = (vts[v][:, k0 - v * tq:k0 - v * tq + nk] if diagonal
              else jnp.concatenate([vts[kb] for kb in range(v)], axis=1))
        bias = None if not diagonal else (tri if q0 else jnp.concatenate([tri, jnp.zeros_like(tri)], axis=1))
        m, l, o = _softmax_piece(sts[i], vt, 2 * HEAD_DIM, bias)
        if q0:
            m = jnp.concatenate([jnp.full((1, q0), MASK_FLOOR, F32), m], axis=1)
            l = jnp.concatenate([jnp.zeros((1, q0), F32), l], axis=1)
            o = jnp.concatenate([jnp.zeros((o.shape[0], q0), F32), o], axis=1)
        pieces[c].append((m, l, o))
        yield
    return _merge_pieces(pieces[0]) - lam * _merge_pieces(pieces[1])


def _diff_kernel(q_ref, k_ref, v_ref, lam_ref, sg_ref, o_ref, *, seq, tq, lambda_init):
    n_blocks = seq // tq
    rows = lambda i: slice(i * tq, (i + 1) * tq)
    vts = [_transposed_values(v_ref[0, rows(i), :], 0) for i in range(n_blocks)]
    lv = lam_ref[...]
    lam = (jnp.exp(jnp.sum(lv[0:1] * lv[1:2], axis=-1, keepdims=True))
           - jnp.exp(jnp.sum(lv[2:3] * lv[3:4], axis=-1, keepdims=True)) + lambda_init)
    half = tq // 2
    tri = jnp.where(lax.broadcasted_iota(jnp.int32, (half, half), 0) <= lax.broadcasted_iota(jnp.int32, (half, half), 1),
                    0.0, NEG)
    outs = _interleave([_diff_block(v, q_ref[0, rows(v), :], k_ref, vts, tri, lam, tq) for v in range(n_blocks)])
    for v, o in enumerate(outs):
        o_ref[0, rows(v), :] = (_rms(o.T, sg_ref[...]) * (1.0 - lambda_init)).astype(BF16)


def _diff_attention(proj, lam, subln, off, batch, seq, heads, lambda_init):
    vw = 2 * HEAD_DIM
    tq = _row_tile(seq, DIFF_Q_TILE)
    qb, kb, vb = off["q_d"] // vw, off["k_d"] // vw, off["v_d"] // vw
    return pl.pallas_call(
        functools.partial(_diff_kernel, seq=seq, tq=tq, lambda_init=lambda_init),
        grid=(batch, heads),
        in_specs=[
            pl.BlockSpec((1, seq, vw), lambda b, h: (b, 0, qb + h)),
            pl.BlockSpec((1, seq, vw), lambda b, h: (b, 0, kb + h)),
            pl.BlockSpec((1, seq, vw), lambda b, h: (b, 0, vb + h)),
            pl.BlockSpec((4, HEAD_DIM), lambda b, h: (0, 0)),
            pl.BlockSpec((1, vw), lambda b, h: (0, 0)),
        ],
        out_specs=pl.BlockSpec((1, seq, vw), lambda b, h: (b, 0, h)),
        out_shape=jax.ShapeDtypeStruct((batch, seq, heads * vw), BF16),
        compiler_params=_params("parallel", "parallel"),
        name="diff_attention",
    )(proj, proj, proj, lam, subln)


def _out_kernel(x_ref, a_ref, b_ref, wa_ref, wb_ref, o_ref):
    o_ref[...] = x_ref[...] + _dot(a_ref[...], wa_ref[...]) + _dot(b_ref[...], wb_ref[...])


def _out_project(x, a, b, wa, wb):
    n, d = x.shape
    ka, kb = a.shape[1], b.shape[1]
    tm = _row_tile(n, OUT_ROW_TILE)
    tn = LANES * _largest_divisor(d // LANES, OUT_GROUPS_PER_TILE)
    return pl.pallas_call(
        _out_kernel,
        grid=(n // tm, d // tn),
        in_specs=[
            pl.BlockSpec((tm, tn), lambda i, j: (i, j)),
            pl.BlockSpec((tm, ka), lambda i, j: (i, 0)),
            pl.BlockSpec((tm, kb), lambda i, j: (i, 0)),
            pl.BlockSpec((ka, tn), lambda i, j: (0, j)),
            pl.BlockSpec((kb, tn), lambda i, j: (0, j)),
        ],
        out_specs=pl.BlockSpec((tm, tn), lambda i, j: (i, j)),
        out_shape=jax.ShapeDtypeStruct((n, d), F32),
        compiler_params=_params("parallel", "arbitrary"),
        name="out_proj",
    )(x, a, b, wa, wb)


def _rope_tables(t):
    inv = 1.0 / (ROPE_THETA ** (jnp.arange(0, HEAD_DIM, 2, dtype=F32) / HEAD_DIM))
    ang = jnp.arange(t, dtype=F32)[:, None] * inv[None, :]
    ang = jnp.concatenate([ang, ang], axis=-1)
    sign = jnp.concatenate([-jnp.ones((HEAD_DIM // 2,), F32), jnp.ones((HEAD_DIM // 2,), F32)])
    return jnp.cos(ang), jnp.sin(ang) * sign[None, :]


def _layout(d_model):
    nsa_heads = d_model // (2 * HEAD_DIM)
    diff_heads = d_model // (4 * HEAD_DIM)
    kv = NSA_KV_HEADS * HEAD_DIM
    sizes = dict(q_n=nsa_heads * HEAD_DIM, kc=kv, vc=kv, ks=kv, vs=kv, kw=kv, vw=kv, gates=3 * nsa_heads,
                 q_d=2 * diff_heads * HEAD_DIM, k_d=2 * diff_heads * HEAD_DIM, v_d=diff_heads * 2 * HEAD_DIM)
    orig, off, o, p = {}, {}, 0, 0
    for name in ("q_n", "kc", "vc", "ks", "vs", "kw", "vw", "gates", "q_d", "k_d", "v_d"):
        orig[name] = o
        o += sizes[name]
        if name != "gates":
            off[name] = p
            p += sizes[name]
    groups = lambda names: tuple((off[nm] // LANES, (off[nm] + sizes[nm]) // LANES) for nm in names)
    return sizes, orig, off, groups(("q_n", "kc", "ks", "kw", "q_d", "k_d")), groups(("q_n", "q_d")), nsa_heads, diff_heads


def kernel(x, ffn1_norm, ffn1_w_gate, ffn1_w_up, ffn1_w_down, mix_norm, w_in, cmp_pos_k, cmp_pos_v, cmp_wk1, cmp_wk2, cmp_wv1, cmp_wv2, lam_q1, lam_k1, lam_q2, lam_k2, diff_subln, w_out, ffn2_norm, ffn2_w_gate, ffn2_w_up, ffn2_w_down, final_norm):
    batch, seq, d_model = x.shape
    depth = ffn1_norm.shape[0]
    n = batch * seq
    sizes, orig, off, rope_groups, query_groups, nsa_heads, diff_heads = _layout(d_model)
    g = NSA_KV_HEADS
    rep = nsa_heads // g
    cos, sin_signed = _rope_tables(seq)
    fg = final_norm.reshape(1, d_model)
    ffn1_stacked = (ffn1_w_gate, ffn1_w_up, ffn1_w_down)
    ffn2_stacked = (ffn2_w_gate, ffn2_w_up, ffn2_w_down)
    w_ffn = tuple(w[0].astype(BF16) for w in ffn1_stacked)

    xf = x.reshape(n, d_model)
    for l in range(depth):
        lambda_init = 0.8 - 0.6 * math.exp(-0.3 * l)
        xf, w_ffn = _ffn(xf, ffn1_norm[l].reshape(1, d_model), *w_ffn, fg, False, (ffn2_stacked, l))

        wl = w_in[l]
        gates_end = orig["gates"] + sizes["gates"]
        w_main = jnp.concatenate([wl[:, :orig["gates"]], wl[:, gates_end:]], axis=1).astype(BF16)
        gate_tiles = []
        for h in range(g):
            cols = wl[:, orig["gates"] + h * 3 * rep: orig["gates"] + (h + 1) * 3 * rep]
            gate_tiles.append(jnp.pad(cols, ((0, 0), (0, LANES - 3 * rep))))
        w_gate = jnp.concatenate(gate_tiles, axis=1).astype(BF16)
        proj, gates = _project(xf, mix_norm[l].reshape(1, d_model), w_main, w_gate, cos, sin_signed, rope_groups,
                               query_groups)
        proj3 = proj.reshape(batch, seq, -1)
        gates3 = gates.reshape(batch, seq, -1)

        kc, vc = _compress(proj3, off, batch, seq,
                           cmp_pos_k[l].reshape(1, -1).astype(BF16), cmp_pos_v[l].reshape(1, -1).astype(BF16),
                           cmp_wk1[l].astype(BF16), cmp_wk2[l].astype(BF16),
                           cmp_wv1[l].astype(BF16), cmp_wv2[l].astype(BF16))
        o_nsa = _nsa_attention(proj3, gates3, kc, vc, off, batch, seq, rep)
        lam = jnp.stack([lam_q1[l], lam_k1[l], lam_q2[l], lam_k2[l]])
        o_diff = _diff_attention(proj3, lam, diff_subln[l].reshape(1, -1), off, batch, seq, diff_heads, lambda_init)

        half = sizes["q_n"]
        wo = w_out[l].astype(BF16)
        xf = _out_project(xf, o_nsa.reshape(n, -1), o_diff.reshape(n, -1), wo[:half], wo[half:])

        last = l == depth - 1
        xf, w_ffn = _ffn(xf, ffn2_norm[l].reshape(1, d_model), *w_ffn, fg, last,
                         None if last else (ffn1_stacked, l + 1))
    return xf.reshape(batch, seq, d_model)
```
